```python
import jax, jax.numpy as jnp
from jax import lax
import numpy as np


D_MODEL = 1024
BATCH = 16
SEQ = 4096
DEPTH = 2

GRID_W = 64
CTX_LEN = 256
ROPE_THETA = 10000.0
EPS = 1e-6
Q_BLOCK = 128

MLA_HEADS = 4
MLA_NOPE = 64
MLA_ROPE = 32
MLA_V = 64
MLA_Q_LORA = 256
MLA_KV_LORA = 128
MLA_SCALE = (MLA_NOPE + MLA_ROPE) ** -0.5
GQA_HEADS = 4
GQA_KV_HEADS = 2
GQA_HD = 64
NA_HEADS = 4
NA_HD = 64
NA_WIN_R = 8
NA_WIN_C = 16
RET_HEADS = 4
RET_DK = 64
RET_DV = 64
RET_CHUNK = 128
N_BRANCH = 4
BRANCH_W = 256
PEER_HEADS = 8
PEER_N_KEYS = 128
PEER_N_EXPERTS = PEER_N_KEYS * PEER_N_KEYS
PEER_TOPK = 16
PEER_DK = 128
PEER_BLOCK = 128

IN_SIZES = (MLA_Q_LORA, MLA_KV_LORA, MLA_ROPE,
            GQA_HEADS * GQA_HD, GQA_KV_HEADS * GQA_HD, GQA_KV_HEADS * GQA_HD,
            NA_HEADS * NA_HD, NA_HEADS * NA_HD, NA_HEADS * NA_HD,
            RET_HEADS * RET_DK, RET_HEADS * RET_DK, RET_HEADS * RET_DV,
            RET_HEADS * RET_DV, RET_HEADS * RET_DV,
            N_BRANCH * D_MODEL)
IN_COLS = sum(IN_SIZES)
IN_OFFSETS = tuple(int(v) for v in np.cumsum(IN_SIZES)[:-1])

kernel_name = 'hybrid_parallel_mla_gqa_natten_retention_peer'


def rmsnorm(x, w):
    xf = x.astype(jnp.float32)
    y = xf * lax.rsqrt(jnp.mean(xf * xf, axis=-1, keepdims=True) + EPS)
    return y.astype(x.dtype) * w


def rope_1d(x, pos):
    half = x.shape[-1] // 2
    freqs = ROPE_THETA ** (-jnp.arange(half, dtype=jnp.float32) / half)
    ang = pos[:, None] * freqs[None, :]
    cos = jnp.cos(ang)[:, None, :].astype(x.dtype)
    sin = jnp.sin(ang)[:, None, :].astype(x.dtype)
    x1, x2 = x[..., :half], x[..., half:]
    return jnp.concatenate([x1 * cos - x2 * sin, x1 * sin + x2 * cos], axis=-1)


def axial_rope(x, row, col):
    half = x.shape[-1] // 2
    return jnp.concatenate([rope_1d(x[..., :half], row), rope_1d(x[..., half:], col)], axis=-1)


def attend(q, k, v, scale):
    s = jnp.einsum('bqhgd,bkhd->bhgqk', q, k).astype(jnp.float32) * scale
    p = jax.nn.softmax(s, axis=-1).astype(v.dtype)
    return jnp.einsum('bhgqk,bkhe->bqhge', p, v)


def block_attend(q, k, v, scale):
    B, S = q.shape[:2]
    nb = S // Q_BLOCK
    qb = jnp.moveaxis(q.reshape((B, nb, Q_BLOCK) + q.shape[2:]), 1, 0)
    o = lax.map(lambda qi: attend(qi, k, v, scale), qb)
    return jnp.moveaxis(o, 0, 1).reshape((B, S) + o.shape[3:])


def neighbourhood_attend(q, k, v, k_ctx, v_ctx, bias):
    B, S, H, d = q.shape
    rows = S // GRID_W
    wr = min(NA_WIN_R, rows)
    wc = NA_WIN_C
    scale = d ** -0.5
    qg = q.reshape(B, rows, GRID_W, H, d)
    kg = k.reshape(B, rows, GRID_W, H, d)
    vg = v.reshape(B, rows, GRID_W, H, d)
    cols = np.arange(GRID_W)
    col_start = np.clip(cols - wc // 2, 0, GRID_W - wc)
    col_idx = col_start[:, None] + np.arange(wc)[None, :]
    rel_c = col_idx - cols[:, None] + (NA_WIN_C - 1)

    def one_row(args):
        r, q_r = args
        rs = jnp.clip(r - wr // 2, 0, rows - wr)
        k_rows = lax.dynamic_slice_in_dim(kg, rs, wr, axis=1)
        v_rows = lax.dynamic_slice_in_dim(vg, rs, wr, axis=1)
        k_win = k_rows[:, :, col_idx]
        v_win = v_rows[:, :, col_idx]
        rel_r = rs + jnp.arange(wr) - r + (NA_WIN_R - 1)
        b = bias[:, rel_r[None, :, None], rel_c[:, None, :]]
        s_win = jnp.einsum('bchd,brcwhd->bhcrw', q_r, k_win).astype(jnp.float32) * scale + b.astype(jnp.float32)
        s_ctx = jnp.einsum('bchd,bkhd->bhck', q_r, k_ctx).astype(jnp.float32) * scale
        s = jnp.concatenate([s_win.reshape(B, H, GRID_W, wr * wc), s_ctx], axis=-1)
        p = jax.nn.softmax(s, axis=-1).astype(v.dtype)
        p_win = p[..., :wr * wc].reshape(B, H, GRID_W, wr, wc)
        p_ctx = p[..., wr * wc:]
        return (jnp.einsum('bhcrw,brcwhd->bchd', p_win, v_win)
                + jnp.einsum('bhck,bkhd->bchd', p_ctx, v_ctx))

    o = lax.map(one_row, (jnp.arange(rows), jnp.moveaxis(qg, 1, 0)))
    return jnp.moveaxis(o, 0, 1).reshape(B, S, H * d)


def retention_scan(q, k, v, log_gamma, s0):
    B, H, L, _ = q.shape
    C = RET_CHUNK
    nc = L // C
    idx = jnp.arange(C, dtype=jnp.float32)
    diff = idx[:, None] - idx[None, :]
    decay = jnp.where(diff >= 0, jnp.exp(log_gamma[:, None, None] * jnp.maximum(diff, 0.0)), 0.0)
    q_decay = jnp.exp(log_gamma[:, None] * (idx + 1.0))[None, :, :, None]
    k_decay = jnp.exp(log_gamma[:, None] * (C - 1.0 - idx))[None, :, :, None]
    chunk_decay = jnp.exp(log_gamma * C)[None, :, None, None]

    def chunks(t):
        return jnp.moveaxis(t.reshape(B, H, nc, C, t.shape[-1]), 2, 0)

    def step(state, qkv):
        qc, kc, vc = qkv
        inner = jnp.einsum('bhnd,bhmd->bhnm', qc, kc) * decay
        o = (jnp.einsum('bhnm,bhme->bhne', inner, vc)
             + jnp.einsum('bhnd,bhde->bhne', qc, state) * q_decay)
        state = state * chunk_decay + jnp.einsum('bhmd,bhme->bhde', kc * k_decay, vc)
        return state, o

    state, o = lax.scan(step, s0, (chunks(q), chunks(k), chunks(v)))
    return jnp.moveaxis(o, 0, 2).reshape(B, H, L, -1), state


def head_groupnorm(o, w):
    B, H, L, dv = o.shape
    mu = jnp.mean(o, axis=-1, keepdims=True)
    var = jnp.mean(jnp.square(o - mu), axis=-1, keepdims=True)
    y = (o - mu) * lax.rsqrt(var + EPS)
    return jnp.moveaxis(y, 1, 2).reshape(B, L, H * dv) * w.astype(jnp.float32)


def retention_mixer(ret_x, ret_c, decay_logit, gn_w, need_ctx):
    def heads(t, d):
        B, L, _ = t.shape
        return jnp.moveaxis(t.reshape(B, L, RET_HEADS, d), 1, 2).astype(jnp.float32)

    def prep(r):
        q, k, v, gf, gb = r
        return heads(q, RET_DK), heads(k, RET_DK) * (RET_DK ** -0.5), heads(v, RET_DV), gf, gb

    qx, kx, vx, gfx, gbx = prep(ret_x)
    qc, kc, vc, gfc, gbc = prep(ret_c)
    log_g = jax.nn.log_sigmoid(decay_logit.astype(jnp.float32))
    flip = lambda t: jnp.flip(t, axis=2)
    s0 = jnp.zeros((qc.shape[0], RET_HEADS, RET_DK, RET_DV), jnp.float32)
    oc_f, st_f = retention_scan(qc, kc, vc, log_g[0], s0)
    oc_b, st_b = retention_scan(flip(qc), flip(kc), flip(vc), log_g[1], s0)
    ox_f, _ = retention_scan(qx, kx, vx, log_g[0], st_f)
    ox_b, _ = retention_scan(flip(qx), flip(kx), flip(vx), log_g[1], st_b)

    def combine(of, ob, gf, gb):
        y = (head_groupnorm(of, gn_w) * jax.nn.silu(gf.astype(jnp.float32))
             + head_groupnorm(flip(ob), gn_w) * jax.nn.silu(gb.astype(jnp.float32)))
        return y.astype(gf.dtype)

    yx = combine(ox_f, ox_b, gfx, gbx)
    yc = combine(oc_f, oc_b, gfc, gbc) if need_ctx else None
    return yx, yc


def stream_heads(proj, pos, mla_qn, mla_wuq, mla_kvn, mla_wukv, gqa_qn, gqa_kn):
    (cq, ckv, kpe, gq, gk, gv, nq, nk, nv, rq, rk, rv, rgf, rgb, gates) = jnp.split(proj, IN_OFFSETS, axis=-1)
    B, L = proj.shape[:2]
    qa = (rmsnorm(cq, mla_qn) @ mla_wuq).reshape(B, L, MLA_HEADS, MLA_NOPE + MLA_ROPE)
    kva = (rmsnorm(ckv, mla_kvn) @ mla_wukv).reshape(B, L, MLA_HEADS, MLA_NOPE + MLA_V)
    q_nope, q_pe = qa[..., :MLA_NOPE], qa[..., MLA_NOPE:]
    k_nope, v_a = kva[..., :MLA_NOPE], kva[..., MLA_NOPE:]
    k_pe = kpe[:, :, None, :]
    gq = rmsnorm(gq.reshape(B, L, GQA_HEADS, GQA_HD), gqa_qn)
    gk = rmsnorm(gk.reshape(B, L, GQA_KV_HEADS, GQA_HD), gqa_kn)
    if pos is not None:
        row, col = pos
        q_pe = axial_rope(q_pe, row, col)
        k_pe = axial_rope(k_pe, row, col)
        gq = axial_rope(gq, row, col)
        gk = axial_rope(gk, row, col)
    q_a = jnp.concatenate([q_nope, q_pe], axis=-1)[:, :, :, None, :]
    k_a = jnp.concatenate([k_nope, jnp.broadcast_to(k_pe, (B, L, MLA_HEADS, MLA_ROPE))], axis=-1)
    mla = (q_a, k_a, v_a)
    gqa = (gq.reshape(B, L, GQA_KV_HEADS, GQA_HEADS // GQA_KV_HEADS, GQA_HD), gk,
           gv.reshape(B, L, GQA_KV_HEADS, GQA_HD))
    na = (nq.reshape(B, L, NA_HEADS, NA_HD), nk.reshape(B, L, NA_HEADS, NA_HD),
          nv.reshape(B, L, NA_HEADS, NA_HD))
    ret = (rq, rk, rv, rgf, rgb)
    return mla, gqa, na, ret, gates


def merge_branches(outs, gates, w_branch, w_out):
    g = jnp.split(gates, N_BRANCH, axis=-1)
    acc = jax.nn.sigmoid(g[0]) * (outs[0] @ w_branch[0])
    for i in range(1, N_BRANCH):
        acc = acc + jax.nn.sigmoid(g[i]) * (outs[i] @ w_branch[i])
    return acc @ w_out


def hybrid_mixer(hx, hc, pos, w_in, mla_qn, mla_wuq, mla_kvn, mla_wukv, gqa_qn, gqa_kn,
                 na_bias, ret_decay_logit, ret_gn_w, w_branch, w_out, need_ctx):
    wts = (mla_qn, mla_wuq, mla_kvn, mla_wukv, gqa_qn, gqa_kn)
    mla_x, gqa_x, na_x, ret_x, gates_x = stream_heads(hx @ w_in, pos, *wts)
    mla_c, gqa_c, na_c, ret_c, gates_c = stream_heads(hc @ w_in, None, *wts)
    B, S = hx.shape[:2]
    cat = lambda a, b: jnp.concatenate([a, b], axis=1)
    oa = block_attend(mla_x[0], cat(mla_c[1], mla_x[1]), cat(mla_c[2], mla_x[2]), MLA_SCALE)
    ob = block_attend(gqa_x[0], cat(gqa_c[1], gqa_x[1]), cat(gqa_c[2], gqa_x[2]), GQA_HD ** -0.5)
    oc = neighbourhood_attend(na_x[0], na_x[1], na_x[2], na_c[1], na_c[2], na_bias)
    od, od_c = retention_mixer(ret_x, ret_c, ret_decay_logit, ret_gn_w, need_ctx)
    yx = merge_branches((oa.reshape(B, S, -1), ob.reshape(B, S, -1), oc, od), gates_x, w_branch, w_out)
    yc = None
    if need_ctx:
        L = hc.shape[1]
        ca = block_attend(mla_c[0], mla_c[1], mla_c[2], MLA_SCALE).reshape(B, L, -1)
        cb = block_attend(gqa_c[0], gqa_c[1], gqa_c[2], GQA_HD ** -0.5).reshape(B, L, -1)
        cc = block_attend(na_c[0][:, :, :, None, :], na_c[1], na_c[2], NA_HD ** -0.5).reshape(B, L, -1)
        yc = merge_branches((ca, cb, cc, od_c), gates_c, w_branch, w_out)
    return yx, yc


def peer_ffn(h, w_q, keys, u, v):
    B, L, D = h.shape
    nb = (B * L) // PEER_BLOCK
    xb = h.reshape(nb, PEER_BLOCK, D)

    def one_block(xt):
        q = (xt @ w_q).reshape(PEER_BLOCK, PEER_HEADS, 2, PEER_DK // 2)
        s = jnp.einsum('thpd,hpkd->thpk', q, keys).astype(jnp.float32)
        s1, i1 = lax.top_k(s[:, :, 0], PEER_TOPK)
        s2, i2 = lax.top_k(s[:, :, 1], PEER_TOPK)
        cand = (s1[..., :, None] + s2[..., None, :]).reshape(PEER_BLOCK, PEER_HEADS, PEER_TOPK * PEER_TOPK)
        cand_idx = (i1[..., :, None] * PEER_N_KEYS + i2[..., None, :]).reshape(PEER_BLOCK, PEER_HEADS, PEER_TOPK * PEER_TOPK)
        top_s, top_pos = lax.top_k(cand, PEER_TOPK)
        eidx = jnp.take_along_axis(cand_idx, top_pos, axis=-1)
        g = jax.nn.softmax(top_s, axis=-1)
        act = jax.nn.gelu(jnp.einsum('td,thkd->thk', xt, u[eidx]).astype(jnp.float32), approximate=False)
        return jnp.einsum('thk,thkd->td', (g * act).astype(xt.dtype), v[eidx])

    return lax.map(one_block, xb).reshape(B, L, D)


def setup_inputs(seed: int = 0) -> dict:
    key = jax.random.key(seed)
    ks = jax.random.split(key, 26)
    f32 = jnp.float32
    D = D_MODEL

    def nrm(k, shape, scale):
        return jax.random.normal(k, shape, f32) * scale

    gam = 1.0 - 2.0 ** (-5.0 - jnp.arange(RET_HEADS, dtype=f32))
    base_logit = jnp.log(gam) - jnp.log1p(-gam)
    return {
        'x': nrm(ks[0], (BATCH, SEQ, D), 1.0),
        'c': nrm(ks[1], (BATCH, D), 1.0),
        'ctx': nrm(ks[2], (BATCH, CTX_LEN, D), 1.0),
        'c_ctx': nrm(ks[3], (D,), 1.0),
        'mod_w': nrm(ks[4], (DEPTH, D, 6 * D), 0.5 * D ** -0.5),
        'mod_b': nrm(ks[5], (DEPTH, 6 * D), 0.01),
        'norm1_w': 1.0 + nrm(ks[6], (DEPTH, D), 0.02),
        'norm2_w': 1.0 + nrm(ks[7], (DEPTH, D), 0.02),
        'w_in': nrm(ks[8], (DEPTH, D, IN_COLS), D ** -0.5),
        'mla_q_norm': 1.0 + nrm(ks[9], (DEPTH, MLA_Q_LORA), 0.02),
        'mla_w_uq': nrm(ks[10], (DEPTH, MLA_Q_LORA, MLA_HEADS * (MLA_NOPE + MLA_ROPE)), MLA_Q_LORA ** -0.5),
        'mla_kv_norm': 1.0 + nrm(ks[11], (DEPTH, MLA_KV_LORA), 0.02),
        'mla_w_ukv': nrm(ks[12], (DEPTH, MLA_KV_LORA, MLA_HEADS * (MLA_NOPE + MLA_V)), MLA_KV_LORA ** -0.5),
        'gqa_q_norm': 1.0 + nrm(ks[13], (DEPTH, GQA_HD), 0.02),
        'gqa_k_norm': 1.0 + nrm(ks[14], (DEPTH, GQA_HD), 0.02),
        'na_bias': nrm(ks[15], (DEPTH, NA_HEADS, 2 * NA_WIN_R - 1, 2 * NA_WIN_C - 1), 0.1),
        'ret_decay_logit': base_logit[None, None, :] + nrm(ks[16], (DEPTH, 2, RET_HEADS), 0.1),
        'ret_gn_w': 1.0 + nrm(ks[17], (DEPTH, RET_HEADS * RET_DV), 0.02),
        'w_branch': nrm(ks[18], (DEPTH, N_BRANCH, BRANCH_W, D), BRANCH_W ** -0.5),
        'w_out': nrm(ks[19], (DEPTH, D, D), D ** -0.5),
        'peer_w_q': nrm(ks[20], (DEPTH, D, PEER_HEADS * PEER_DK), D ** -0.5),
        'peer_keys': nrm(ks[21], (DEPTH, PEER_HEADS, 2, PEER_N_KEYS, PEER_DK // 2), (PEER_DK // 2) ** -0.5),
        'peer_u': nrm(ks[22], (DEPTH, PEER_N_EXPERTS, D), D ** -0.5),
        'peer_v': nrm(ks[23], (DEPTH, PEER_N_EXPERTS, D), PEER_HEADS ** -0.5),
        'final_norm_w': 1.0 + nrm(ks[24], (D,), 0.02),
    }


def reference(x, c, ctx, c_ctx, mod_w, mod_b, norm1_w, norm2_w, w_in, mla_q_norm, mla_w_uq,
              mla_kv_norm, mla_w_ukv, gqa_q_norm, gqa_k_norm, na_bias, ret_decay_logit, ret_gn_w,
              w_branch, w_out, peer_w_q, peer_keys, peer_u, peer_v, final_norm_w):
    S = x.shape[1]
    t = jnp.arange(S)
    pos = ((t // GRID_W).astype(jnp.float32), (t % GRID_W).astype(jnp.float32))
    for l in range(DEPTH):
        need_ctx = l < DEPTH - 1
        mx = (jax.nn.silu(c) @ mod_w[l] + mod_b[l])[:, None, :]
        mc = (jax.nn.silu(c_ctx) @ mod_w[l] + mod_b[l])[None, None, :]
        sh1x, sc1x, g1x, sh2x, sc2x, g2x = jnp.split(mx, 6, axis=-1)
        sh1c, sc1c, g1c, sh2c, sc2c, g2c = jnp.split(mc, 6, axis=-1)
        hx = rmsnorm(x, norm1_w[l]) * (1.0 + sc1x) + sh1x
        hc = rmsnorm(ctx, norm1_w[l]) * (1.0 + sc1c) + sh1c
        yx, yc = hybrid_mixer(hx, hc, pos, w_in[l], mla_q_norm[l], mla_w_uq[l], mla_kv_norm[l],
                              mla_w_ukv[l], gqa_q_norm[l], gqa_k_norm[l], na_bias[l],
                              ret_decay_logit[l], ret_gn_w[l], w_branch[l], w_out[l], need_ctx)
        x = x + g1x * yx
        hx2 = rmsnorm(x, norm2_w[l]) * (1.0 + sc2x) + sh2x
        x = x + g2x * peer_ffn(hx2, peer_w_q[l], peer_keys[l], peer_u[l], peer_v[l])
        if need_ctx:
            ctx = ctx + g1c * yc
            hc2 = rmsnorm(ctx, norm2_w[l]) * (1.0 + sc2c) + sh2c
            ctx = ctx + g2c * peer_ffn(hc2, peer_w_q[l], peer_keys[l], peer_u[l], peer_v[l])
    return rmsnorm(x, final_norm_w)
```

```python
import functools

import numpy as np
import jax
import jax.numpy as jnp
from jax import lax
from jax.experimental import pallas as pl
from jax.experimental.pallas import tpu as pltpu

F32 = jnp.float32
BF16 = jnp.bfloat16
HIGHEST = lax.Precision.HIGHEST

D_MODEL = 1024
GRID_W = 64
ROPE_THETA = 10000.0
EPS = 1e-6
N_HEADS = 4
HEAD_W = 64
BRANCH_W = N_HEADS * HEAD_W
MLA_NOPE, MLA_ROPE, MLA_V = 64, 32, 64
MLA_Q_LORA, MLA_KV_LORA = 256, 128
MLA_SCALE = (MLA_NOPE + MLA_ROPE) ** -0.5
MLA_HEAD_PAD = 128
GQA_KV_HEADS = 2
NA_WIN_R, NA_WIN_C = 8, 16
NA_Q_ROWS = 4
RET_CHUNK = 128
N_BRANCH = 4
PEER_HEADS, PEER_N_KEYS, PEER_TOPK, PEER_DK = 8, 128, 16, 128
PEER_SLOTS = PEER_HEADS * PEER_TOPK
PEER_W_PITCH = PEER_N_KEYS + 8
SQRT_HALF = 0.7071067811865476
NEG_BIG = -1e30

IN_SIZES = (256, 128, 32, 256, 128, 128, 256, 256, 256, 256, 256, 256, 256, 256, 4096)
IN_OFFSETS = tuple(int(v) for v in np.cumsum(IN_SIZES)[:-1])
PROJ_COLS = 7168
COL_NQ, COL_NK, COL_NV = 1024, 1280, 1536
COL_RET = 1792
COL_GATES = 3072

VMEM_LIMIT_V7X = 56 * 1024 * 1024


def _cparams(sem, vmem=None):
    return pltpu.CompilerParams(dimension_semantics=sem, vmem_limit_bytes=vmem)


def _dot(a, b):
    return jnp.dot(a, b, preferred_element_type=F32)


def _dot_hi(a, b):
    return jnp.dot(a, b, preferred_element_type=F32, precision=HIGHEST)


def _dot_nt(a, b):
    return lax.dot_general(a, b, (((1,), (1,)), ((), ())), preferred_element_type=F32)


def _dot_nt_hi(a, b):
    return lax.dot_general(a, b, (((1,), (1,)), ((), ())), preferred_element_type=F32, precision=HIGHEST)


def _rms(x):
    return x * lax.rsqrt(jnp.mean(x * x, axis=-1, keepdims=True) + EPS)


def _silu(x):
    return x * jax.nn.sigmoid(x)


def _head_mask(shape, h, width=HEAD_W):
    lane = lax.broadcasted_iota(jnp.int32, shape, len(shape) - 1)
    lo = h * width
    return (lane >= lo) & (lane < lo + width)


def _mod_kernel(c_ref, w_ref, b_ref, o_ref):
    o_ref[0] = _dot_hi(_silu(c_ref[...]), w_ref[0]) + b_ref[0]


def _modulation(cc, mod_w, mod_b):
    depth, d, n = mod_w.shape
    rows = cc.shape[0]
    tn = 1536
    return pl.pallas_call(
        _mod_kernel,
        grid=(depth, n // tn),
        in_specs=[pl.BlockSpec((rows, d), lambda l, j: (0, 0)),
                  pl.BlockSpec((1, d, tn), lambda l, j: (l, 0, j)),
                  pl.BlockSpec((1, 1, tn), lambda l, j: (l, 0, j))],
        out_specs=pl.BlockSpec((1, rows, tn), lambda l, j: (l, 0, j)),
        out_shape=jax.ShapeDtypeStruct((depth, rows, n), F32),
        compiler_params=_cparams(("parallel", "parallel"), 40 * 1024 * 1024),
        name="modulation",
    )(cc, mod_w, mod_b.reshape(depth, 1, n))


def _inproj_kernel(x_ref, nw_ref, sc_ref, sh_ref, w_ref, o_ref, h_scr):
    @pl.when(pl.program_id(2) == 0)
    def _():
        h = _rms(x_ref[0]) * nw_ref[...] * (1.0 + sc_ref[0]) + sh_ref[0]
        h_scr[...] = h.astype(BF16)

    o_ref[0] = _dot(h_scr[...], w_ref[...]).astype(o_ref.dtype)


def _inproj(x, nw, sc, sh, w):
    b, l, d = x.shape
    n = w.shape[1]
    tm = min(512, l)
    tn = 1024
    return pl.pallas_call(
        _inproj_kernel,
        grid=(b, l // tm, n // tn),
        in_specs=[pl.BlockSpec((1, tm, d), lambda bi, i, j: (bi, i, 0)),
                  pl.BlockSpec((1, d), lambda bi, i, j: (0, 0)),
                  pl.BlockSpec((1, 1, d), lambda bi, i, j: (bi, 0, 0)),
                  pl.BlockSpec((1, 1, d), lambda bi, i, j: (bi, 0, 0)),
                  pl.BlockSpec((d, tn), lambda bi, i, j: (0, j))],
        out_specs=pl.BlockSpec((1, tm, tn), lambda bi, i, j: (bi, i, j)),
        out_shape=jax.ShapeDtypeStruct((b, l, n), BF16),
        scratch_shapes=[pltpu.VMEM((tm, d), BF16)],
        compiler_params=_cparams(("parallel", "parallel", "arbitrary"), 40 * 1024 * 1024),
        name="inproj",
    )(x, nw, sc, sh, w)


def _prep_kernel(*refs, use_rope):
    (p_ref, qn_ref, wuq_ref, kvn_ref, wk_ref, wv_ref, gqn_ref, gkn_ref, gseg_ref, pm_ref, pg_ref,
     e_ref) = refs[:12]
    if use_rope:
        cm_ref, sm_ref, cg_ref, sg_ref = refs[12:16]
        outs = refs[16:]
    else:
        outs = refs[12:]
    qm_o, km_o, vm_o, qg_o, kg_o, vg_o = outs

    pb = p_ref[0]
    cq = pb[:, 0:256].astype(F32)
    ckv = pb[:, 256:384].astype(F32)
    kpe = pb[:, 384:512].astype(F32)
    gq = pb[:, 512:768].astype(F32)
    gk = pb[:, 768:896].astype(F32)
    gv = pb[:, 896:1024]

    cqn = (_rms(cq) * qn_ref[...]).astype(BF16)
    qa = _dot(cqn, wuq_ref[...])
    ckn = (_rms(ckv) * kvn_ref[...]).astype(BF16)
    kn = _dot(ckn, wk_ref[...])
    vm = _dot(ckn, wv_ref[...])
    if use_rope:
        cm, sm = cm_ref[...], sm_ref[...]
        pm = pm_ref[...]

        def rope_m(t):
            return t * cm + _dot_hi(t, pm) * sm

        qa = jnp.concatenate([rope_m(qa[:, h * 128:(h + 1) * 128]) for h in range(N_HEADS)], axis=1)
        kpe = rope_m(kpe)
    km = kn + jnp.concatenate([kpe] * N_HEADS, axis=1)
    qm_o[0] = (qa * MLA_SCALE).astype(BF16)
    km_o[0] = km.astype(BF16)
    vm_o[0] = vm.astype(BF16)

    gseg = gseg_ref[...]
    gqn = gq * lax.rsqrt(_dot_hi(gq * gq, gseg) * (1.0 / HEAD_W) + EPS) * gqn_ref[...]
    gkn = gk * lax.rsqrt(_dot_hi(gk * gk, gseg[:128, :128]) * (1.0 / HEAD_W) + EPS) * gkn_ref[...]
    if use_rope:
        cg, sg = cg_ref[...], sg_ref[...]
        pg = pg_ref[...]
        gqn = gqn * cg + _dot_hi(gqn, pg) * sg
        gkn = gkn * cg[:, :128] + _dot_hi(gkn, pg[:128, :128]) * sg[:, :128]
    qg_o[0] = (gqn * (HEAD_W ** -0.5)).astype(BF16)
    e = e_ref[...]
    kg_o[0] = _dot(gkn.astype(BF16), e).astype(BF16)
    vg_o[0] = _dot(gv, e).astype(BF16)


def _prep(proj, consts, tables):
    b, l, _ = proj.shape
    tm = min(512, l)
    use_rope = tables is not None
    full = lambda a: pl.BlockSpec(a.shape, lambda bi, i: (0,) * a.ndim)
    in_specs = [pl.BlockSpec((1, tm, 1024), lambda bi, i: (bi, i, 0))] + [full(a) for a in consts]
    args = [proj] + list(consts)
    if use_rope:
        in_specs += [pl.BlockSpec((tm, t.shape[1]), lambda bi, i: (i, 0)) for t in tables]
        args += list(tables)
    widths = (512, 512, 256, 256, 256, 256)
    return pl.pallas_call(
        functools.partial(_prep_kernel, use_rope=use_rope),
        grid=(b, l // tm),
        in_specs=in_specs,
        out_specs=[pl.BlockSpec((1, tm, w), lambda bi, i: (bi, i, 0)) for w in widths],
        out_shape=[jax.ShapeDtypeStruct((b, l, w), BF16) for w in widths],
        compiler_params=_cparams(("parallel", "parallel"), 40 * 1024 * 1024),
        name="prep_rope" if use_rope else "prep",
    )(*args)


def _attn_kernel(*refs, nseg, dq, tk, qscale):
    q_ref = refs[0]
    segs = [(refs[1 + 2 * i], refs[2 + 2 * i]) for i in range(nseg)]
    o_ref = refs[1 + 2 * nseg]
    tq = q_ref.shape[1]
    q = q_ref[0]
    assert qscale is None or dq % 128 != 0
    if qscale is not None:
        q = q * jnp.asarray(qscale, BF16)
    out = jnp.zeros((tq, BRANCH_W), F32)
    for h in range(N_HEADS):
        if dq % 128 == 0:
            qh = q_ref[0, :, h * dq:(h + 1) * dq]
            lanes = slice(h * dq, (h + 1) * dq)
        else:
            qh = jnp.where(_head_mask(q.shape, h, dq), q, jnp.zeros_like(q))
            lanes = slice(None)
        carry = (jnp.full((tq, 1), -jnp.inf, F32), jnp.zeros((tq, 1), F32), jnp.zeros((tq, BRANCH_W), F32))
        for k_ref, v_ref in segs:
            lk = k_ref.shape[1]
            tkk = min(tk, lk)

            def body(c, carry, k_ref=k_ref, v_ref=v_ref, tkk=tkk, qh=qh, lanes=lanes):
                m, l, acc = carry
                off = c * tkk if isinstance(c, int) else pl.multiple_of(c * tkk, tkk)
                kc = k_ref[0, pl.ds(off, tkk), lanes]
                vc = v_ref[0, pl.ds(off, tkk), :]
                s = _dot_nt(qh, kc)
                mn = jnp.maximum(m, jnp.max(s, axis=-1, keepdims=True))
                alpha = jnp.exp(m - mn)
                p = jnp.exp(s - mn)
                l = alpha * l + jnp.sum(p, axis=-1, keepdims=True)
                acc = alpha * acc + _dot(p.astype(BF16), vc)
                return mn, l, acc

            if lk // tkk == 1:
                carry = body(0, carry)
            else:
                carry = lax.fori_loop(0, lk // tkk, body, carry)
        _, l, acc = carry
        out = out + jnp.where(_head_mask(out.shape, h), acc * (1.0 / l), 0.0)
    o_ref[0] = out.astype(o_ref.dtype)


def _attention(q, segs, dq, name, qscale=None):
    (qa, qcol) = q
    b, lq, _ = qa.shape
    wq = N_HEADS * dq
    tq = min(256, lq)
    assert qcol % wq == 0
    in_specs = [pl.BlockSpec((1, tq, wq), lambda bi, i: (bi, i, qcol // wq))]
    args = [qa]
    for (ka, kcol), (va, vcol) in segs:
        assert kcol % wq == 0 and vcol % BRANCH_W == 0
        in_specs.append(pl.BlockSpec((1, ka.shape[1], wq), lambda bi, i, kcol=kcol: (bi, 0, kcol // wq)))
        in_specs.append(pl.BlockSpec((1, va.shape[1], BRANCH_W), lambda bi, i, vcol=vcol: (bi, 0, vcol // BRANCH_W)))
        args += [ka, va]
    return pl.pallas_call(
        functools.partial(_attn_kernel, nseg=len(segs), dq=dq, tk=512, qscale=qscale),
        grid=(b, lq // tq),
        in_specs=in_specs,
        out_specs=pl.BlockSpec((1, tq, BRANCH_W), lambda bi, i: (bi, i, 0)),
        out_shape=jax.ShapeDtypeStruct((b, lq, BRANCH_W), BF16),
        compiler_params=_cparams(("parallel", "arbitrary"), 48 * 1024 * 1024),
        name=name,
    )(*args)


def _proj_cols(arr, col, width, rows):
    assert col % width == 0
    return pl.BlockSpec((1, rows, width), lambda *idx: (idx[0], 0, col // width))


def _na_kernel(pat_ref, ks_ref, q_ref, k_ref, v_ref, kc_ref, vc_ref, m_ref, o_ref, *, kw):
    del pat_ref
    g = pl.program_id(1)
    off = pl.multiple_of(ks_ref[g] * GRID_W, GRID_W)
    q = q_ref[0] * jnp.asarray(HEAD_W ** -0.5, BF16)
    kwin = k_ref[0, pl.ds(off, kw), :]
    vwin = v_ref[0, pl.ds(off, kw), :]
    kc = kc_ref[0]
    vc = vc_ref[0]
    out = jnp.zeros((q.shape[0], BRANCH_W), F32)
    for h in range(N_HEADS):
        qh = jnp.where(_head_mask(q.shape, h), q, jnp.zeros_like(q))
        sw = _dot_nt(qh, kwin) + m_ref[0, h]
        sc = _dot_nt(qh, kc)
        mx = jnp.maximum(jnp.max(sw, axis=-1, keepdims=True), jnp.max(sc, axis=-1, keepdims=True))
        pw = jnp.exp(sw - mx)
        pc = jnp.exp(sc - mx)
        l = jnp.sum(pw, axis=-1, keepdims=True) + jnp.sum(pc, axis=-1, keepdims=True)
        o = _dot(pw.astype(BF16), vwin) + _dot(pc.astype(BF16), vc)
        out = out + jnp.where(_head_mask(out.shape, h), o * (1.0 / l), 0.0)
    o_ref[0] = out.astype(o_ref.dtype)


def _na_plan(s):
    rows = s // GRID_W
    wr = min(NA_WIN_R, rows)
    wc = NA_WIN_C
    qr = min(NA_Q_ROWS, rows)
    kwr = min(qr + wr - 1 + (1 if qr + wr - 1 < rows else 0), rows)
    ngrp = rows // qr
    pats, pat_ids, ks_rows = [], [], []
    qi = np.arange(qr * GRID_W)
    kj = np.arange(kwr * GRID_W)
    for g in range(ngrp):
        r0 = g * qr
        ks = int(np.clip(r0 - wr // 2, 0, rows - kwr))
        r = r0 + qi // GRID_W
        qc = qi % GRID_W
        rs = np.clip(r - wr // 2, 0, rows - wr)
        cs = np.clip(qc - wc // 2, 0, GRID_W - wc)
        kr = ks + kj // GRID_W
        kc = kj % GRID_W
        valid = ((kr[None, :] >= rs[:, None]) & (kr[None, :] < rs[:, None] + wr)
                 & (kc[None, :] >= cs[:, None]) & (kc[None, :] < cs[:, None] + wc))
        assert (valid.sum(1) == wr * wc).all()
        rel_r = np.where(valid, kr[None, :] - r[:, None] + (NA_WIN_R - 1), 0)
        rel_c = np.where(valid, kc[None, :] - qc[:, None] + (NA_WIN_C - 1), 0)
        key = (valid.tobytes(), rel_r.tobytes(), rel_c.tobytes())
        for pi, (pk, *_rest) in enumerate(pats):
            if pk == key:
                pat_ids.append(pi)
                break
        else:
            pat_ids.append(len(pats))
            pats.append((key, valid, rel_r, rel_c))
        ks_rows.append(ks)
    valid = np.stack([p[1] for p in pats])
    rel_r = np.stack([p[2] for p in pats])
    rel_c = np.stack([p[3] for p in pats])
    return qr, kwr, np.asarray(pat_ids, np.int32), np.asarray(ks_rows, np.int32), valid, rel_r, rel_c


def _na_attention(projx, projc, na_bias):
    b, s, _ = projx.shape
    lc = projc.shape[1]
    qr, kwr, pat_ids, ks_rows, valid, rel_r, rel_c = _na_plan(s)
    qb, kw = qr * GRID_W, kwr * GRID_W
    mb = jnp.where(valid[None], na_bias[:, rel_r, rel_c].astype(F32), NEG_BIG)
    mb = jnp.transpose(mb, (1, 0, 2, 3))
    grid_spec = pltpu.PrefetchScalarGridSpec(
        num_scalar_prefetch=2,
        grid=(b, s // qb),
        in_specs=[pl.BlockSpec((1, qb, 256), lambda bi, g, pat, ks: (bi, g, COL_NQ // 256)),
                  pl.BlockSpec((1, s, 256), lambda bi, g, pat, ks: (bi, 0, COL_NK // 256)),
                  pl.BlockSpec((1, s, 256), lambda bi, g, pat, ks: (bi, 0, COL_NV // 256)),
                  pl.BlockSpec((1, lc, 256), lambda bi, g, pat, ks: (bi, 0, COL_NK // 256)),
                  pl.BlockSpec((1, lc, 256), lambda bi, g, pat, ks: (bi, 0, COL_NV // 256)),
                  pl.BlockSpec((1, N_HEADS, qb, kw), lambda bi, g, pat, ks: (pat[g], 0, 0, 0))],
        out_specs=pl.BlockSpec((1, qb, BRANCH_W), lambda bi, g, pat, ks: (bi, g, 0)),
    )
    return pl.pallas_call(
        functools.partial(_na_kernel, kw=kw),
        grid_spec=grid_spec,
        out_shape=jax.ShapeDtypeStruct((b, s, BRANCH_W), BF16),
        compiler_params=_cparams(("parallel", "arbitrary"), 48 * 1024 * 1024),
        name="na_attention",
    )(jnp.asarray(pat_ids), jnp.asarray(ks_rows), projx, projx, projx, projc, projc, mb)


def _ret_kernel(lgs_ref, lgl_ref, gnw_ref, gseg_ref,
                qx, kx, vx, gfx, gbx, qc, kc, vc, gfc, gbc,
                yx_o, yc_o, of_s, ob_s, st_s, dec_s, qk_s, *, need_ctx):
    c = RET_CHUNK
    lc = qc.shape[1]
    sx = qx.shape[1]
    n_col = lax.broadcasted_iota(jnp.int32, (c, c), 0).astype(F32)
    m_row = lax.broadcasted_iota(jnp.int32, (c, c), 1).astype(F32)
    diff = n_col - m_row
    for h in range(N_HEADS):
        dec_s[h] = jnp.where(diff >= 0, jnp.exp(lgs_ref[h] * jnp.maximum(diff, 0.0)), 0.0)
        dec_s[N_HEADS + h] = jnp.where(diff <= 0, jnp.exp(lgs_ref[N_HEADS + h] * jnp.maximum(-diff, 0.0)), 0.0)
    pos = lax.broadcasted_iota(jnp.int32, (c, BRANCH_W), 0).astype(F32)
    lgf, lgb = lgl_ref[0], lgl_ref[1]
    qk_s[0] = jnp.exp(lgf * (pos + 1.0))
    qk_s[1] = jnp.exp(lgf * (c - 1.0 - pos))
    qk_s[2] = jnp.exp(lgb * (c - pos))
    qk_s[3] = jnp.exp(lgb * pos)
    cd_f = jnp.exp(lgf * float(c))
    cd_b = jnp.exp(lgb * float(c))
    st_s[...] = jnp.zeros_like(st_s)
    rowb = lax.broadcasted_iota(jnp.int32, (BRANCH_W, BRANCH_W), 0) // HEAD_W
    colb = lax.broadcasted_iota(jnp.int32, (BRANCH_W, BRANCH_W), 1) // HEAD_W
    bd_mask = rowb == colb

    def chunk_step(q, k, v, d, dec_off, cd):
        kk = k * jnp.asarray(HEAD_W ** -0.5, BF16)
        state = st_s[d]
        o = _dot(q, state.astype(BF16)) * qk_s[2 * d]
        for h in range(N_HEADS):
            km = jnp.where(_head_mask(kk.shape, h), kk, jnp.zeros_like(kk))
            inner = _dot_nt(q, km) * dec_s[dec_off + h]
            r = _dot(inner.astype(BF16), v)
            o = o + jnp.where(_head_mask(r.shape, h), r, 0.0)
        kd = (kk.astype(F32) * qk_s[2 * d + 1]).astype(BF16)
        upd = lax.dot_general(kd, v, (((0,), (0,)), ((), ())), preferred_element_type=F32)
        st_s[d] = state * cd + jnp.where(bd_mask, upd, 0.0)
        return o

    def scan(q_ref, k_ref, v_ref, base, n):
        def body(i, _):
            fo = pl.multiple_of(i * c, c)
            bo = pl.multiple_of((n - 1 - i) * c, c)
            of_s[pl.ds(base + fo, c), :] = chunk_step(
                q_ref[0, pl.ds(fo, c), :], k_ref[0, pl.ds(fo, c), :], v_ref[0, pl.ds(fo, c), :], 0, 0, cd_f)
            ob_s[pl.ds(base + bo, c), :] = chunk_step(
                q_ref[0, pl.ds(bo, c), :], k_ref[0, pl.ds(bo, c), :], v_ref[0, pl.ds(bo, c), :], 1, N_HEADS, cd_b)
            return 0
        lax.fori_loop(0, n, body, 0)

    scan(qc, kc, vc, 0, lc // c)
    scan(qx, kx, vx, lc, sx // c)

    gseg = gseg_ref[...]
    gnw = gnw_ref[...]

    def gnorm(o):
        mu = _dot_hi(o, gseg) * (1.0 / HEAD_W)
        dlt = o - mu
        var = _dot_hi(dlt * dlt, gseg) * (1.0 / HEAD_W)
        return dlt * lax.rsqrt(var + EPS) * gnw

    def combine(gf_ref, gb_ref, y_ref, base, n):
        def body(i, _):
            ro = pl.multiple_of(i * c, c)
            y = (gnorm(of_s[pl.ds(base + ro, c), :]) * _silu(gf_ref[0, pl.ds(ro, c), :].astype(F32))
                 + gnorm(ob_s[pl.ds(base + ro, c), :]) * _silu(gb_ref[0, pl.ds(ro, c), :].astype(F32)))
            y_ref[0, pl.ds(ro, c), :] = y.astype(y_ref.dtype)
            return 0
        lax.fori_loop(0, n, body, 0)

    combine(gfx, gbx, yx_o, lc, sx // c)
    if need_ctx:
        combine(gfc, gbc, yc_o, 0, lc // c)
    else:
        yc_o[...] = jnp.zeros_like(yc_o)


def _retention(projx, projc, log_g, gn_w, gseg, need_ctx):
    b, s, _ = projx.shape
    lc = projc.shape[1]
    lgs = log_g.reshape(2 * N_HEADS)
    lgl = jnp.repeat(log_g, HEAD_W, axis=1).reshape(2, 1, BRANCH_W)
    xs = [_proj_cols(projx, COL_RET + 256 * i, 256, s) for i in range(5)]
    cs = [_proj_cols(projc, COL_RET + 256 * i, 256, lc) for i in range(5)]
    c = RET_CHUNK
    yx, yc = pl.pallas_call(
        functools.partial(_ret_kernel, need_ctx=need_ctx),
        grid=(b,),
        in_specs=[pl.BlockSpec(memory_space=pltpu.SMEM),
                  pl.BlockSpec((2, 1, BRANCH_W), lambda bi: (0, 0, 0)),
                  pl.BlockSpec((1, BRANCH_W), lambda bi: (0, 0)),
                  pl.BlockSpec((BRANCH_W, BRANCH_W), lambda bi: (0, 0))] + xs + cs,
        out_specs=[pl.BlockSpec((1, s, BRANCH_W), lambda bi: (bi, 0, 0)),
                   pl.BlockSpec((1, lc, BRANCH_W), lambda bi: (bi, 0, 0))],
        out_shape=[jax.ShapeDtypeStruct((b, s, BRANCH_W), BF16),
                   jax.ShapeDtypeStruct((b, lc, BRANCH_W), BF16)],
        scratch_shapes=[pltpu.VMEM((lc + s, BRANCH_W), F32),
                        pltpu.VMEM((lc + s, BRANCH_W), F32),
                        pltpu.VMEM((2, BRANCH_W, BRANCH_W), F32),
                        pltpu.VMEM((2 * N_HEADS, c, c), F32),
                        pltpu.VMEM((4, c, BRANCH_W), F32)],
        compiler_params=_cparams(("parallel",), 48 * 1024 * 1024),
        name="retention",
    )(lgs, lgl, gn_w.reshape(1, BRANCH_W), gseg, *([projx] * 5), *([projc] * 5))
    return yx, yc


def _merge_kernel(oa, ob, oc, od, g0, g1, g2, g3, x_ref, gate_ref, sc_ref, sh_ref, nw_ref, wb_ref, wo_ref,
                  xn_o, h2_o):
    acc = None
    for i, (o, g) in enumerate(((oa, g0), (ob, g1), (oc, g2), (od, g3))):
        t = jax.nn.sigmoid(g[0].astype(F32)) * _dot(o[0], wb_ref[i])
        acc = t if acc is None else acc + t
    y = _dot(acc.astype(BF16), wo_ref[...])
    xn = x_ref[0] + gate_ref[0] * y
    xn_o[0] = xn
    h2_o[0] = _rms(xn) * nw_ref[...] * (1.0 + sc_ref[0]) + sh_ref[0]


def _merge(outs, proj, x, gate, sc2, sh2, n2w, wb, wo):
    b, l, d = x.shape
    tm = min(512, l)
    tok = lambda w: pl.BlockSpec((1, tm, w), lambda bi, i: (bi, i, 0))
    vec = pl.BlockSpec((1, 1, d), lambda bi, i: (bi, 0, 0))
    gates = [pl.BlockSpec((1, tm, d), lambda bi, i, k=k: (bi, i, COL_GATES // d + k)) for k in range(N_BRANCH)]
    return pl.pallas_call(
        _merge_kernel,
        grid=(b, l // tm),
        in_specs=[tok(BRANCH_W)] * 4 + gates + [tok(d), vec, vec, vec,
                                                pl.BlockSpec((1, d), lambda bi, i: (0, 0)),
                                                pl.BlockSpec(wb.shape, lambda bi, i: (0, 0, 0)),
                                                pl.BlockSpec(wo.shape, lambda bi, i: (0, 0))],
        out_specs=[tok(d), tok(d)],
        out_shape=[jax.ShapeDtypeStruct((b, l, d), F32), jax.ShapeDtypeStruct((b, l, d), F32)],
        compiler_params=_cparams(("parallel", "parallel"), 48 * 1024 * 1024),
        name="merge",
    )(*outs, proj, proj, proj, proj, x, gate, sc2, sh2, n2w, wb, wo)


def _topk_rows(s, k):
    r, t = s.shape
    iota = lax.broadcasted_iota(jnp.int32, s.shape, 0)
    out_row = lax.broadcasted_iota(jnp.int32, (k, t), 0)
    vals = jnp.zeros((k, t), F32)
    idxs = jnp.zeros((k, t), jnp.int32)
    for i in range(k):
        m = jnp.max(s, axis=0, keepdims=True)
        idx = jnp.min(jnp.where(s == m, iota, r), axis=0, keepdims=True)
        vals = jnp.where(out_row == i, m, vals)
        idxs = jnp.where(out_row == i, idx, idxs)
        s = jnp.where(iota == idx, -jnp.inf, s)
    return vals, idxs


def _select_rows(table, sel, k):
    out = jnp.zeros_like(table)
    for r in range(k):
        out = jnp.where(sel == r, table[r:r + 1, :], out)
    return out


def _peer_route_kernel(h_ref, wq_ref, keys_ref, a_o, b_o, g_o, q_s, a_s, b_s, g_s):
    k = PEER_TOPK
    q_s[...] = _dot_hi(h_ref[...], wq_ref[...])

    def head(h, _):
        lo = pl.multiple_of(h * PEER_DK, PEER_DK)
        qh = q_s[:, pl.ds(lo, PEER_DK)]
        s = _dot_nt_hi(keys_ref[h], qh)
        s1, i1 = _topk_rows(s[:PEER_N_KEYS], k)
        s2, i2 = _topk_rows(s[PEER_N_KEYS:], k)
        t = s.shape[1]
        cand = (s1[:, None, :] + s2[None, :, :]).reshape(k * k, t)
        ts, tpos = _topk_rows(cand, k)
        e = jnp.exp(ts - ts[0:1, :])
        gate = e / jnp.sum(e, axis=0, keepdims=True)
        ro = pl.multiple_of(h * k, k)
        a_s[pl.ds(ro, k), :] = _select_rows(i1, tpos >> 4, k)
        b_s[pl.ds(ro, k), :] = _select_rows(i2, tpos & (k - 1), k)
        g_s[pl.ds(ro, k), :] = gate
        return 0

    lax.fori_loop(0, PEER_HEADS, head, 0)
    a_o[...] = a_s[...].T
    b_o[...] = b_s[...].T
    g_o[...] = g_s[...].T


def _peer_route(h2, wq, keys_bd):
    n, d = h2.shape
    t = 256
    return pl.pallas_call(
        _peer_route_kernel,
        grid=(n // t,),
        in_specs=[pl.BlockSpec((t, d), lambda i: (i, 0)),
                  pl.BlockSpec(wq.shape, lambda i: (0, 0)),
                  pl.BlockSpec(keys_bd.shape, lambda i: (0, 0, 0))],
        out_specs=[pl.BlockSpec((t, PEER_SLOTS), lambda i: (i, 0))] * 3,
        out_shape=[jax.ShapeDtypeStruct((n, PEER_SLOTS), jnp.int32),
                   jax.ShapeDtypeStruct((n, PEER_SLOTS), jnp.int32),
                   jax.ShapeDtypeStruct((n, PEER_SLOTS), F32)],
        scratch_shapes=[pltpu.VMEM((t, PEER_HEADS * PEER_DK), F32),
                        pltpu.VMEM((PEER_SLOTS, t), jnp.int32),
                        pltpu.VMEM((PEER_SLOTS, t), jnp.int32),
                        pltpu.VMEM((PEER_SLOTS, t), F32)],
        compiler_params=_cparams(("parallel",), 48 * 1024 * 1024),
        name="peer_route",
    )(h2, wq, keys_bd)


def _peer_ffn_kernel(h_ref, a_ref, b_ref, g_ref, x_ref, gate_ref, u_ref, v_ref, o_ref, hb_s, w_s, acc_s, *, ec):
    e = pl.program_id(1)
    t = h_ref.shape[0]
    nk = PEER_N_KEYS

    @pl.when(e == 0)
    def _():
        hb_s[...] = h_ref[...].astype(BF16)
        acc_s[...] = jnp.zeros_like(acc_s)
        jio = lax.broadcasted_iota(jnp.int32, (nk, PEER_SLOTS), 0)

        def build(t8, _):
            for u in range(8):
                tt = t8 * 8 + u
                arow = jnp.broadcast_to(a_ref[pl.ds(tt, 1), :], (nk, PEER_SLOTS))
                brow = jnp.broadcast_to(b_ref[pl.ds(tt, 1), :], (nk, PEER_SLOTS))
                grow = jnp.broadcast_to(g_ref[pl.ds(tt, 1), :], (nk, PEER_SLOTS))
                cm = jnp.where(jio == arow, grow, 0.0).astype(BF16)
                bm = jnp.where(jio == brow, 1.0, 0.0).astype(BF16)
                w_s[pl.ds(pl.multiple_of(tt * PEER_W_PITCH, 8), nk), :] = _dot_nt(cm, bm)
            return 0

        lax.fori_loop(0, t // 8, build, 0)

    hid = _dot(hb_s[...], u_ref[...])
    j0 = e * (ec // nk)
    wc = jnp.concatenate([w_s[pl.ds(j0 + j, t, stride=PEER_W_PITCH), :] for j in range(ec // nk)], axis=1)
    act = 0.5 * hid * (1.0 + lax.erf(hid * SQRT_HALF))
    acc_s[...] += _dot((wc * act).astype(BF16), v_ref[...])

    @pl.when(e == pl.num_programs(1) - 1)
    def _():
        o_ref[...] = x_ref[...] + gate_ref[0] * acc_s[...]


def _peer_ffn(h2, a, b_idx, g, x, gate, u_t, v, l):
    n, d = h2.shape
    ne = v.shape[0]
    t = 256
    ec = 512
    assert l % t == 0 and ne == PEER_N_KEYS * PEER_N_KEYS
    tok = lambda w: pl.BlockSpec((t, w), lambda i, e: (i, 0))
    return pl.pallas_call(
        functools.partial(_peer_ffn_kernel, ec=ec),
        grid=(n // t, ne // ec),
        in_specs=[tok(d), tok(PEER_SLOTS), tok(PEER_SLOTS), tok(PEER_SLOTS), tok(d),
                  pl.BlockSpec((1, 1, d), lambda i, e: ((i * t) // l, 0, 0)),
                  pl.BlockSpec((d, ec), lambda i, e: (0, e)),
                  pl.BlockSpec((ec, d), lambda i, e: (e, 0))],
        out_specs=tok(d),
        out_shape=jax.ShapeDtypeStruct((n, d), F32),
        scratch_shapes=[pltpu.VMEM((t, d), BF16),
                        pltpu.VMEM((t * PEER_W_PITCH, PEER_N_KEYS), F32),
                        pltpu.VMEM((t, d), F32)],
        compiler_params=_cparams(("parallel", "arbitrary"), VMEM_LIMIT_V7X),
        name="peer_ffn",
    )(h2, a, b_idx, g, x, gate, u_t, v)


def _final_norm_kernel(x_ref, w_ref, o_ref):
    o_ref[...] = _rms(x_ref[...]) * w_ref[...]


def _final_norm(x, w):
    n, d = x.shape
    tm = min(1024, n)
    return pl.pallas_call(
        _final_norm_kernel,
        grid=(n // tm,),
        in_specs=[pl.BlockSpec((tm, d), lambda i: (i, 0)), pl.BlockSpec((1, d), lambda i: (0, 0))],
        out_specs=pl.BlockSpec((tm, d), lambda i: (i, 0)),
        out_shape=jax.ShapeDtypeStruct((n, d), F32),
        compiler_params=_cparams(("parallel",), 40 * 1024 * 1024),
        name="final_norm",
    )(x, w.reshape(1, d))


def _layout_w_in(w):
    parts = jnp.split(w, IN_OFFSETS, axis=1)
    z = lambda n: jnp.zeros((w.shape[0], n), w.dtype)
    kpe_blk = jnp.concatenate([z(MLA_NOPE), parts[2], z(MLA_HEAD_PAD - MLA_NOPE - MLA_ROPE)], axis=1)
    return jnp.concatenate([parts[0], parts[1], kpe_blk] + list(parts[3:]), axis=1).astype(BF16)


def _layout_mla(w_uq, w_ukv):
    qh = w_uq.reshape(MLA_Q_LORA, N_HEADS, MLA_NOPE + MLA_ROPE)
    qh = jnp.pad(qh, ((0, 0), (0, 0), (0, MLA_HEAD_PAD - MLA_NOPE - MLA_ROPE)))
    kv = w_ukv.reshape(MLA_KV_LORA, N_HEADS, MLA_NOPE + MLA_V)
    kh = jnp.pad(kv[:, :, :MLA_NOPE], ((0, 0), (0, 0), (0, MLA_HEAD_PAD - MLA_NOPE)))
    vh = kv[:, :, MLA_NOPE:]
    return (qh.reshape(MLA_Q_LORA, -1).astype(BF16), kh.reshape(MLA_KV_LORA, -1).astype(BF16),
            vh.reshape(MLA_KV_LORA, -1).astype(BF16))


def _static_mats():
    gseg = np.kron(np.eye(N_HEADS), np.ones((HEAD_W, HEAD_W))).astype(np.float32)
    pm = np.zeros((MLA_HEAD_PAD, MLA_HEAD_PAD), np.float32)
    for dd in range(MLA_ROPE):
        blk, j = dd // 16, dd % 16
        pm[MLA_NOPE + blk * 16 + (j + 8) % 16, MLA_NOPE + dd] = 1.0
    pg = np.zeros((BRANCH_W, BRANCH_W), np.float32)
    for i in range(BRANCH_W):
        off, dd = (i // HEAD_W) * HEAD_W, i % HEAD_W
        blk, j = dd // 32, dd % 32
        pg[off + blk * 32 + (j + 16) % 32, i] = 1.0
    ex = np.zeros((GQA_KV_HEADS * HEAD_W, BRANCH_W), np.float32)
    for i in range(BRANCH_W):
        ex[((i // HEAD_W) // (N_HEADS // GQA_KV_HEADS)) * HEAD_W + i % HEAD_W, i] = 1.0
    return jnp.asarray(gseg), jnp.asarray(pm), jnp.asarray(pg), jnp.asarray(ex, dtype=BF16)


def _rope_half_tables(pos, hf):
    freqs = ROPE_THETA ** (-jnp.arange(hf, dtype=F32) / hf)
    ang = pos[:, None] * freqs[None, :]
    c, s = jnp.cos(ang), jnp.sin(ang)
    return jnp.concatenate([c, c], axis=1), jnp.concatenate([-s, s], axis=1)


def _axial_tables(row, col, dims):
    cr, sr = _rope_half_tables(row, dims // 4)
    cc, sc = _rope_half_tables(col, dims // 4)
    return jnp.concatenate([cr, cc], axis=1), jnp.concatenate([sr, sc], axis=1)


def _rope_tables(s):
    t = jnp.arange(s)
    row, col = (t // GRID_W).astype(F32), (t % GRID_W).astype(F32)
    c32, s32 = _axial_tables(row, col, MLA_ROPE)
    pad = MLA_HEAD_PAD - MLA_NOPE - MLA_ROPE
    cm = jnp.concatenate([jnp.ones((s, MLA_NOPE), F32), c32, jnp.ones((s, pad), F32)], axis=1)
    sm = jnp.concatenate([jnp.zeros((s, MLA_NOPE), F32), s32, jnp.zeros((s, pad), F32)], axis=1)
    c64, s64 = _axial_tables(row, col, HEAD_W)
    return cm, sm, jnp.tile(c64, (1, N_HEADS)), jnp.tile(s64, (1, N_HEADS))


def _layout_peer_keys(keys):
    h, _, nk, dh = keys.shape
    z = jnp.zeros((h, nk, dh), keys.dtype)
    top = jnp.concatenate([keys[:, 0], z], axis=2)
    bot = jnp.concatenate([z, keys[:, 1]], axis=2)
    return jnp.concatenate([top, bot], axis=1)


def kernel(x, c, ctx, c_ctx, mod_w, mod_b, norm1_w, norm2_w, w_in, mla_q_norm, mla_w_uq, mla_kv_norm, mla_w_ukv, gqa_q_norm, gqa_k_norm, na_bias, ret_decay_logit, ret_gn_w, w_branch, w_out, peer_w_q, peer_keys, peer_u, peer_v, final_norm_w):
    b, s, d = x.shape
    lc = ctx.shape[1]
    depth = mod_w.shape[0]
    assert d == D_MODEL and s % (GRID_W * NA_Q_ROWS) == 0 and s % 256 == 0 and lc % 256 == 0

    rows = -(-(b + 1) // 8) * 8
    cc = jnp.zeros((rows, d), F32).at[:b].set(c).at[b].set(c_ctx)
    mod = _modulation(cc, mod_w, mod_b)

    gseg, pm, pg, ex = _static_mats()
    tables = _rope_tables(s)

    for l in range(depth):
        need_ctx = l < depth - 1
        mx = mod[l, :b].reshape(b, 1, 6, d)
        mc = jnp.broadcast_to(mod[l, b].reshape(1, 1, 6, d), (b, 1, 6, d))
        sh1x, sc1x, g1x, sh2x, sc2x, g2x = (mx[:, :, i] for i in range(6))
        sh1c, sc1c, g1c, sh2c, sc2c, g2c = (mc[:, :, i] for i in range(6))

        w_in_l = _layout_w_in(w_in[l])
        wuq, wk, wv = _layout_mla(mla_w_uq[l], mla_w_ukv[l])
        consts = (mla_q_norm[l].reshape(1, -1), wuq, mla_kv_norm[l].reshape(1, -1), wk, wv,
                  jnp.tile(gqa_q_norm[l], N_HEADS).reshape(1, -1),
                  jnp.tile(gqa_k_norm[l], GQA_KV_HEADS).reshape(1, -1), gseg, pm, pg, ex)
        n1w = norm1_w[l].reshape(1, d)
        n2w = norm2_w[l].reshape(1, d)
        wb = w_branch[l].astype(BF16)
        wo = w_out[l].astype(BF16)
        keys_bd = _layout_peer_keys(peer_keys[l])
        u_t = peer_u[l].astype(BF16).T
        v_b = peer_v[l].astype(BF16)
        log_g = jax.nn.log_sigmoid(ret_decay_logit[l].astype(F32))

        projx = _inproj(x, n1w, sc1x, sh1x, w_in_l)
        projc = _inproj(ctx, n1w, sc1c, sh1c, w_in_l)
        qmx, kmx, vmx, qgx, kgx, vgx = _prep(projx, consts, tables)
        qmc, kmc, vmc, qgc, kgc, vgc = _prep(projc, consts, None)

        oa = _attention((qmx, 0), [((kmc, 0), (vmc, 0)), ((kmx, 0), (vmx, 0))], MLA_HEAD_PAD, "attn_mla")
        ob = _attention((qgx, 0), [((kgc, 0), (vgc, 0)), ((kgx, 0), (vgx, 0))], HEAD_W, "attn_gqa")
        oc = _na_attention(projx, projc, na_bias[l])
        od, od_c = _retention(projx, projc, log_g, ret_gn_w[l], gseg, need_ctx)

        x, h2x = _merge((oa, ob, oc, od), projx, x, g1x, sc2x, sh2x, n2w, wb, wo)
        ax, bx, gx = _peer_route(h2x.reshape(b * s, d), peer_w_q[l], keys_bd)
        x = _peer_ffn(h2x.reshape(b * s, d), ax, bx, gx, x.reshape(b * s, d), g2x, u_t, v_b, s).reshape(b, s, d)

        if need_ctx:
            ca = _attention((qmc, 0), [((kmc, 0), (vmc, 0))], MLA_HEAD_PAD, "attn_mla_ctx")
            cb = _attention((qgc, 0), [((kgc, 0), (vgc, 0))], HEAD_W, "attn_gqa_ctx")
            ccx = _attention((projc, COL_NQ), [((projc, COL_NK), (projc, COL_NV))], HEAD_W, "attn_na_ctx",
                             qscale=HEAD_W ** -0.5)
            ctx, h2c = _merge((ca, cb, ccx, od_c), projc, ctx, g1c, sc2c, sh2c, n2w, wb, wo)
            ac, bc, gc = _peer_route(h2c.reshape(b * lc, d), peer_w_q[l], keys_bd)
            ctx = _peer_ffn(h2c.reshape(b * lc, d), ac, bc, gc, ctx.reshape(b * lc, d), g2c, u_t, v_b,
                            lc).reshape(b, lc, d)

    return _final_norm(x.reshape(b * s, d), final_norm_w).reshape(b, s, d)
```

```python
import functools

import numpy as np
import jax
import jax.numpy as jnp
from jax import lax
from jax.experimental import pallas as pl
from jax.experimental.pallas import tpu as pltpu

F32 = jnp.float32
BF16 = jnp.bfloat16
HIGHEST = lax.Precision.HIGHEST

D_MODEL = 1024
GRID_W = 64
ROPE_THETA = 10000.0
EPS = 1e-6
N_HEADS = 4
HEAD_W = 64
BRANCH_W = N_HEADS * HEAD_W
MLA_NOPE, MLA_ROPE, MLA_V = 64, 32, 64
MLA_Q_LORA, MLA_KV_LORA = 256, 128
MLA_SCALE = (MLA_NOPE + MLA_ROPE) ** -0.5
MLA_HEAD_PAD = 128
GQA_KV_HEADS = 2
NA_WIN_R, NA_WIN_C = 8, 16
NA_Q_ROWS = 4
RET_CHUNK = 128
N_BRANCH = 4
PEER_HEADS, PEER_N_KEYS, PEER_TOPK, PEER_DK = 8, 128, 16, 128
PEER_SLOTS = PEER_HEADS * PEER_TOPK
PEER_W_PITCH = PEER_N_KEYS + 8
PEER_EXPERT_CHUNK = 512
SQRT_HALF = 0.7071067811865476
NEG_BIG = -1e30

IN_SIZES = (256, 128, 32, 256, 128, 128, 256, 256, 256, 256, 256, 256, 256, 256, 4096)
IN_OFFSETS = tuple(int(v) for v in np.cumsum(IN_SIZES)[:-1])
PROJ_COLS = 7168
COL_NQ, COL_NK, COL_NV = 1024, 1280, 1536
COL_RET = 1792
COL_GATES = 3072

VMEM_LIMIT_V7X = 56 * 1024 * 1024


def _cparams(sem, vmem=None):
    return pltpu.CompilerParams(dimension_semantics=sem, vmem_limit_bytes=vmem)


def _dot(a, b):
    return jnp.dot(a, b, preferred_element_type=F32)


def _dot_hi(a, b):
    return jnp.dot(a, b, preferred_element_type=F32, precision=HIGHEST)


def _dot_nt(a, b):
    return lax.dot_general(a, b, (((1,), (1,)), ((), ())), preferred_element_type=F32)


def _dot_nt_hi(a, b):
    return lax.dot_general(a, b, (((1,), (1,)), ((), ())), preferred_element_type=F32, precision=HIGHEST)


def _rms(x):
    return x * lax.rsqrt(jnp.mean(x * x, axis=-1, keepdims=True) + EPS)


def _silu(x):
    return x * jax.nn.sigmoid(x)


def _head_mask(shape, h, width=HEAD_W):
    lane = lax.broadcasted_iota(jnp.int32, shape, len(shape) - 1)
    lo = h * width
    return (lane >= lo) & (lane < lo + width)


def _mod_kernel(c_ref, w_ref, b_ref, o_ref):
    o_ref[0] = _dot_hi(_silu(c_ref[...]), w_ref[0]) + b_ref[0]


def _modulation(cc, mod_w, mod_b):
    depth, d, n = mod_w.shape
    rows = cc.shape[0]
    tn = 1536
    return pl.pallas_call(
        _mod_kernel,
        grid=(depth, n // tn),
        in_specs=[pl.BlockSpec((rows, d), lambda l, j: (0, 0)),
                  pl.BlockSpec((1, d, tn), lambda l, j: (l, 0, j)),
                  pl.BlockSpec((1, 1, tn), lambda l, j: (l, 0, j))],
        out_specs=pl.BlockSpec((1, rows, tn), lambda l, j: (l, 0, j)),
        out_shape=jax.ShapeDtypeStruct((depth, rows, n), F32),
        compiler_params=_cparams(("parallel", "parallel"), 40 * 1024 * 1024),
        name="modulation",
    )(cc, mod_w, mod_b.reshape(depth, 1, n))


def _inproj_kernel(x_ref, nw_ref, sc_ref, sh_ref, w_ref, o_ref, h_scr):
    @pl.when(pl.program_id(2) == 0)
    def _():
        h = _rms(x_ref[0]) * nw_ref[...] * (1.0 + sc_ref[0]) + sh_ref[0]
        h_scr[...] = h.astype(BF16)

    o_ref[0] = _dot(h_scr[...], w_ref[...]).astype(o_ref.dtype)


def _inproj(x, nw, sc, sh, w):
    b, l, d = x.shape
    n = w.shape[1]
    tm = min(512, l)
    tn = 1024
    return pl.pallas_call(
        _inproj_kernel,
        grid=(b, l // tm, n // tn),
        in_specs=[pl.BlockSpec((1, tm, d), lambda bi, i, j: (bi, i, 0)),
                  pl.BlockSpec((1, d), lambda bi, i, j: (0, 0)),
                  pl.BlockSpec((1, 1, d), lambda bi, i, j: (bi, 0, 0)),
                  pl.BlockSpec((1, 1, d), lambda bi, i, j: (bi, 0, 0)),
                  pl.BlockSpec((d, tn), lambda bi, i, j: (0, j))],
        out_specs=pl.BlockSpec((1, tm, tn), lambda bi, i, j: (bi, i, j)),
        out_shape=jax.ShapeDtypeStruct((b, l, n), BF16),
        scratch_shapes=[pltpu.VMEM((tm, d), BF16)],
        compiler_params=_cparams(("parallel", "parallel", "arbitrary"), 40 * 1024 * 1024),
        name="inproj",
    )(x, nw, sc, sh, w)


def _prep_kernel(*refs, use_rope):
    (p_ref, qn_ref, wuq_ref, kvn_ref, wk_ref, wv_ref, gqn_ref, gkn_ref, gseg_ref, pm_ref, pg_ref,
     e_ref, et_ref) = refs[:13]
    if use_rope:
        cm_ref, sm_ref, cg_ref, sg_ref = refs[13:17]
        outs = refs[17:]
    else:
        outs = refs[13:]
    qm_o, km_o, vm_o, qg_o, kg_o, vg_o = outs

    pb = p_ref[0]
    cq = pb[:, 0:256].astype(F32)
    ckv = pb[:, 256:384].astype(F32)
    kpe = pb[:, 384:512].astype(F32)
    gq = pb[:, 512:768].astype(F32)
    gk = pb[:, 768:896].astype(F32)
    gv = pb[:, 896:1024]

    cqn = (_rms(cq) * qn_ref[...]).astype(BF16)
    qa = _dot(cqn, wuq_ref[...])
    ckn = (_rms(ckv) * kvn_ref[...]).astype(BF16)
    kn = _dot(ckn, wk_ref[...])
    vm_t = _dot_nt(wv_ref[...], ckn)
    if use_rope:
        cm, sm = cm_ref[...], sm_ref[...]
        pm = pm_ref[...]

        def rope_m(t):
            return t * cm + _dot_hi(t, pm) * sm

        qa = jnp.concatenate([rope_m(qa[:, h * 128:(h + 1) * 128]) for h in range(N_HEADS)], axis=1)
        kpe = rope_m(kpe)
    km = kn + jnp.concatenate([kpe] * N_HEADS, axis=1)
    qm_o[0] = (qa * MLA_SCALE).astype(BF16)
    km_o[0] = km.astype(BF16)
    vm_o[0] = vm_t.astype(BF16)

    gseg = gseg_ref[...]
    gqn = gq * lax.rsqrt(_dot_hi(gq * gq, gseg) * (1.0 / HEAD_W) + EPS) * gqn_ref[...]
    gkn = gk * lax.rsqrt(_dot_hi(gk * gk, gseg[:128, :128]) * (1.0 / HEAD_W) + EPS) * gkn_ref[...]
    if use_rope:
        cg, sg = cg_ref[...], sg_ref[...]
        pg = pg_ref[...]
        gqn = gqn * cg + _dot_hi(gqn, pg) * sg
        gkn = gkn * cg[:, :128] + _dot_hi(gkn, pg[:128, :128]) * sg[:, :128]
    qg_o[0] = (gqn * (HEAD_W ** -0.5)).astype(BF16)
    e = e_ref[...]
    kg_o[0] = _dot(gkn.astype(BF16), e).astype(BF16)
    vg_o[0] = _dot_nt(et_ref[...], gv).astype(BF16)


def _prep(proj, consts, tables):
    b, l, _ = proj.shape
    tm = min(512, l)
    use_rope = tables is not None
    full = lambda a: pl.BlockSpec(a.shape, lambda bi, i: (0,) * a.ndim)
    in_specs = [pl.BlockSpec((1, tm, 1024), lambda bi, i: (bi, i, 0))] + [full(a) for a in consts]
    args = [proj] + list(consts)
    if use_rope:
        in_specs += [pl.BlockSpec((tm, t.shape[1]), lambda bi, i: (i, 0)) for t in tables]
        args += list(tables)
    tok = lambda w: (pl.BlockSpec((1, tm, w), lambda bi, i: (bi, i, 0)), jax.ShapeDtypeStruct((b, l, w), BF16))
    tr = (pl.BlockSpec((1, BRANCH_W, tm), lambda bi, i: (bi, 0, i)), jax.ShapeDtypeStruct((b, BRANCH_W, l), BF16))
    outs = (tok(512), tok(512), tr, tok(256), tok(256), tr)
    return pl.pallas_call(
        functools.partial(_prep_kernel, use_rope=use_rope),
        grid=(b, l // tm),
        in_specs=in_specs,
        out_specs=[o[0] for o in outs],
        out_shape=[o[1] for o in outs],
        compiler_params=_cparams(("parallel", "parallel"), 40 * 1024 * 1024),
        name="prep_rope" if use_rope else "prep",
    )(*args)


def _attn_kernel(*refs, nseg, dq, tk, qscale):
    q_ref = refs[0]
    segs = [(refs[1 + 2 * i], refs[2 + 2 * i]) for i in range(nseg)]
    o_ref = refs[1 + 2 * nseg]
    tq = q_ref.shape[1]
    assert qscale is None or dq % 128 != 0
    if dq % 128 == 0:
        qs = [q_ref[0, :, h * dq:(h + 1) * dq] for h in range(N_HEADS)]
        klanes = [slice(h * dq, (h + 1) * dq) for h in range(N_HEADS)]
    else:
        q = q_ref[0]
        if qscale is not None:
            q = q * jnp.asarray(qscale, BF16)
        qs = [jnp.where(_head_mask(q.shape, h, dq), q, jnp.zeros_like(q)) for h in range(N_HEADS)]
        klanes = [slice(None)] * N_HEADS
    carry = tuple((jnp.full((1, tq), -jnp.inf, F32), jnp.zeros((1, tq), F32), jnp.zeros((HEAD_W, tq), F32))
                  for _ in range(N_HEADS))
    for k_ref, vt_ref in segs:
        lk = k_ref.shape[1]
        tkk = min(tk, lk)

        def body(c, carry, k_ref=k_ref, vt_ref=vt_ref, tkk=tkk):
            off = c * tkk if isinstance(c, int) else pl.multiple_of(c * tkk, tkk)
            new = []
            for h in range(N_HEADS):
                m, l, acc = carry[h]
                kc = k_ref[0, pl.ds(off, tkk), klanes[h]]
                vt = vt_ref[0, h * HEAD_W:(h + 1) * HEAD_W, pl.ds(off, tkk)]
                st = _dot_nt(kc, qs[h])
                mn = jnp.maximum(m, jnp.max(st, axis=0, keepdims=True))
                alpha = jnp.exp(m - mn)
                p = jnp.exp(st - mn)
                l = alpha * l + jnp.sum(p, axis=0, keepdims=True)
                acc = alpha * acc + _dot(vt, p.astype(BF16))
                new.append((mn, l, acc))
            return tuple(new)

        if lk // tkk == 1:
            carry = body(0, carry)
        else:
            carry = lax.fori_loop(0, lk // tkk, body, carry)
    out_t = jnp.concatenate([acc * (1.0 / l) for _, l, acc in carry], axis=0)
    o_ref[0] = out_t.T.astype(o_ref.dtype)


def _attention(q, segs, dq, name, qscale=None):
    (qa, qcol) = q
    b, lq, _ = qa.shape
    wq = N_HEADS * dq
    tq = min(256, lq)
    assert qcol % wq == 0
    in_specs = [pl.BlockSpec((1, tq, wq), lambda bi, i: (bi, i, qcol // wq))]
    args = [qa]
    for (ka, kcol), vt in segs:
        assert kcol % wq == 0 and vt.shape[1] == BRANCH_W and vt.shape[2] == ka.shape[1]
        in_specs.append(pl.BlockSpec((1, ka.shape[1], wq), lambda bi, i, kcol=kcol: (bi, 0, kcol // wq)))
        in_specs.append(pl.BlockSpec((1, BRANCH_W, vt.shape[2]), lambda bi, i: (bi, 0, 0)))
        args += [ka, vt]
    return pl.pallas_call(
        functools.partial(_attn_kernel, nseg=len(segs), dq=dq, tk=512, qscale=qscale),
        grid=(b, lq // tq),
        in_specs=in_specs,
        out_specs=pl.BlockSpec((1, tq, BRANCH_W), lambda bi, i: (bi, i, 0)),
        out_shape=jax.ShapeDtypeStruct((b, lq, BRANCH_W), BF16),
        compiler_params=_cparams(("parallel", "arbitrary"), 48 * 1024 * 1024),
        name=name,
    )(*args)


def _proj_cols(arr, col, width, rows):
    assert col % width == 0
    return pl.BlockSpec((1, rows, width), lambda *idx: (idx[0], 0, col // width))


def _na_kernel(pat_ref, ks_ref, q_ref, k_ref, v_ref, kc_ref, vc_ref, m_ref, o_ref, *, kw):
    del pat_ref
    g = pl.program_id(1)
    off = pl.multiple_of(ks_ref[g] * GRID_W, GRID_W)
    q = q_ref[0] * jnp.asarray(HEAD_W ** -0.5, BF16)
    kwin = k_ref[0, pl.ds(off, kw), :]
    vwin = v_ref[0, pl.ds(off, kw), :]
    kc = kc_ref[0]
    vc = vc_ref[0]
    out = jnp.zeros((q.shape[0], BRANCH_W), F32)
    for h in range(N_HEADS):
        qh = jnp.where(_head_mask(q.shape, h), q, jnp.zeros_like(q))
        sw = _dot_nt(qh, kwin) + m_ref[0, h]
        sc = _dot_nt(qh, kc)
        mx = jnp.maximum(jnp.max(sw, axis=-1, keepdims=True), jnp.max(sc, axis=-1, keepdims=True))
        pw = jnp.exp(sw - mx)
        pc = jnp.exp(sc - mx)
        l = jnp.sum(pw, axis=-1, keepdims=True) + jnp.sum(pc, axis=-1, keepdims=True)
        o = _dot(pw.astype(BF16), vwin) + _dot(pc.astype(BF16), vc)
        out = out + jnp.where(_head_mask(out.shape, h), o * (1.0 / l), 0.0)
    o_ref[0] = out.astype(o_ref.dtype)


def _na_plan(s):
    rows = s // GRID_W
    wr = min(NA_WIN_R, rows)
    wc = NA_WIN_C
    qr = min(NA_Q_ROWS, rows)
    kwr = min(qr + wr - 1 + (1 if qr + wr - 1 < rows else 0), rows)
    ngrp = rows // qr
    qc = np.arange(GRID_W)[:, None]
    kc = np.arange(GRID_W)[None, :]
    cs = np.clip(qc - wc // 2, 0, GRID_W - wc)
    valid_c = (kc >= cs) & (kc < cs + wc)
    rel_c = np.where(valid_c, kc - qc + (NA_WIN_C - 1), 0)
    assert (valid_c.sum(1) == wc).all()
    pats, pat_ids, ks_rows = [], [], []
    for g in range(ngrp):
        r0 = g * qr
        ks = int(np.clip(r0 - wr // 2, 0, rows - kwr))
        r = (r0 + np.arange(qr))[:, None]
        kr = (ks + np.arange(kwr))[None, :]
        rs = np.clip(r - wr // 2, 0, rows - wr)
        valid_r = (kr >= rs) & (kr < rs + wr)
        assert (valid_r.sum(1) == wr).all()
        rel_r = np.where(valid_r, kr - r + (NA_WIN_R - 1), 0)
        key = (valid_r.tobytes(), rel_r.tobytes())
        for pi, (pk, *_rest) in enumerate(pats):
            if pk == key:
                pat_ids.append(pi)
                break
        else:
            pat_ids.append(len(pats))
            pats.append((key, valid_r, rel_r))
        ks_rows.append(ks)
    valid_r = np.stack([p[1] for p in pats])
    rel_r = np.stack([p[2] for p in pats])
    return (qr, kwr, np.asarray(pat_ids, np.int32), np.asarray(ks_rows, np.int32), valid_r, rel_r, valid_c, rel_c)


def _na_bias_masks(na_bias, valid_r, rel_r, valid_c, rel_c):
    h = na_bias.shape[0]
    npat, qr, kwr = valid_r.shape
    ncol = 2 * NA_WIN_C - 1
    brow = na_bias[:, rel_r, :].astype(F32)
    onehot_c = ((rel_c[None] == np.arange(ncol)[:, None, None]) & valid_c[None]).astype(np.float32)
    m = jnp.einsum('hpqkc,cxy->phqxky', brow, jnp.asarray(onehot_c), precision=HIGHEST)
    valid = valid_r[:, None, :, None, :, None] & valid_c[None, None, None, :, None, :]
    m = jnp.where(valid, m, NEG_BIG)
    return m.reshape(npat, h, qr * GRID_W, kwr * GRID_W)


def _na_attention(projx, projc, na_bias):
    b, s, _ = projx.shape
    lc = projc.shape[1]
    qr, kwr, pat_ids, ks_rows, valid_r, rel_r, valid_c, rel_c = _na_plan(s)
    qb, kw = qr * GRID_W, kwr * GRID_W
    mb = _na_bias_masks(na_bias, valid_r, rel_r, valid_c, rel_c)
    grid_spec = pltpu.PrefetchScalarGridSpec(
        num_scalar_prefetch=2,
        grid=(b, s // qb),
        in_specs=[pl.BlockSpec((1, qb, 256), lambda bi, g, pat, ks: (bi, g, COL_NQ // 256)),
                  pl.BlockSpec((1, s, 256), lambda bi, g, pat, ks: (bi, 0, COL_NK // 256)),
                  pl.BlockSpec((1, s, 256), lambda bi, g, pat, ks: (bi, 0, COL_NV // 256)),
                  pl.BlockSpec((1, lc, 256), lambda bi, g, pat, ks: (bi, 0, COL_NK // 256)),
                  pl.BlockSpec((1, lc, 256), lambda bi, g, pat, ks: (bi, 0, COL_NV // 256)),
                  pl.BlockSpec((1, N_HEADS, qb, kw), lambda bi, g, pat, ks: (pat[g], 0, 0, 0))],
        out_specs=pl.BlockSpec((1, qb, BRANCH_W), lambda bi, g, pat, ks: (bi, g, 0)),
    )
    return pl.pallas_call(
        functools.partial(_na_kernel, kw=kw),
        grid_spec=grid_spec,
        out_shape=jax.ShapeDtypeStruct((b, s, BRANCH_W), BF16),
        compiler_params=_cparams(("parallel", "arbitrary"), 48 * 1024 * 1024),
        name="na_attention",
    )(jnp.asarray(pat_ids), jnp.asarray(ks_rows), projx, projx, projx, projc, projc, mb)


def _ret_kernel(lgs_ref, lgl_ref, gnw_ref, gseg_ref,
                qx, kx, vx, gfx, gbx, qc, kc, vc, gfc, gbc,
                yx_o, yc_o, of_s, ob_s, st_s, dec_s, qk_s, *, need_ctx):
    c = RET_CHUNK
    lc = qc.shape[1]
    sx = qx.shape[1]
    n_col = lax.broadcasted_iota(jnp.int32, (c, c), 0).astype(F32)
    m_row = lax.broadcasted_iota(jnp.int32, (c, c), 1).astype(F32)
    diff = n_col - m_row
    for h in range(N_HEADS):
        dec_s[h] = jnp.where(diff >= 0, jnp.exp(lgs_ref[h] * jnp.maximum(diff, 0.0)), 0.0)
        dec_s[N_HEADS + h] = jnp.where(diff <= 0, jnp.exp(lgs_ref[N_HEADS + h] * jnp.maximum(-diff, 0.0)), 0.0)
    pos = lax.broadcasted_iota(jnp.int32, (c, BRANCH_W), 0).astype(F32)
    lgf, lgb = lgl_ref[0], lgl_ref[1]
    qk_s[0] = jnp.exp(lgf * (pos + 1.0))
    qk_s[1] = jnp.exp(lgf * (c - 1.0 - pos))
    qk_s[2] = jnp.exp(lgb * (c - pos))
    qk_s[3] = jnp.exp(lgb * pos)
    cd_f = jnp.exp(lgf * float(c))
    cd_b = jnp.exp(lgb * float(c))
    st_s[...] = jnp.zeros_like(st_s)
    rowb = lax.broadcasted_iota(jnp.int32, (BRANCH_W, BRANCH_W), 0) // HEAD_W
    colb = lax.broadcasted_iota(jnp.int32, (BRANCH_W, BRANCH_W), 1) // HEAD_W
    bd_mask = rowb == colb

    def chunk_step(q, k, v, d, dec_off, cd):
        kk = k * jnp.asarray(HEAD_W ** -0.5, BF16)
        state = st_s[d]
        o = _dot(q, state.astype(BF16)) * qk_s[2 * d]
        for h in range(N_HEADS):
            km = jnp.where(_head_mask(kk.shape, h), kk, jnp.zeros_like(kk))
            inner = _dot_nt(q, km) * dec_s[dec_off + h]
            r = _dot(inner.astype(BF16), v)
            o = o + jnp.where(_head_mask(r.shape, h), r, 0.0)
        kd = (kk.astype(F32) * qk_s[2 * d + 1]).astype(BF16)
        upd = lax.dot_general(kd, v, (((0,), (0,)), ((), ())), preferred_element_type=F32)
        st_s[d] = state * cd + jnp.where(bd_mask, upd, 0.0)
        return o

    def scan(q_ref, k_ref, v_ref, base, n):
        def body(i, _):
            fo = pl.multiple_of(i * c, c)
            bo = pl.multiple_of((n - 1 - i) * c, c)
            of_s[pl.ds(base + fo, c), :] = chunk_step(
                q_ref[0, pl.ds(fo, c), :], k_ref[0, pl.ds(fo, c), :], v_ref[0, pl.ds(fo, c), :], 0, 0, cd_f)
            ob_s[pl.ds(base + bo, c), :] = chunk_step(
                q_ref[0, pl.ds(bo, c), :], k_ref[0, pl.ds(bo, c), :], v_ref[0, pl.ds(bo, c), :], 1, N_HEADS, cd_b)
            return 0
        lax.fori_loop(0, n, body, 0)

    scan(qc, kc, vc, 0, lc // c)
    scan(qx, kx, vx, lc, sx // c)

    gseg = gseg_ref[...]
    gnw = gnw_ref[...]

    def gnorm(o):
        mu = _dot_hi(o, gseg) * (1.0 / HEAD_W)
        dlt = o - mu
        var = _dot_hi(dlt * dlt, gseg) * (1.0 / HEAD_W)
        return dlt * lax.rsqrt(var + EPS) * gnw

    def combine(gf_ref, gb_ref, y_ref, base, n):
        def body(i, _):
            ro = pl.multiple_of(i * c, c)
            y = (gnorm(of_s[pl.ds(base + ro, c), :]) * _silu(gf_ref[0, pl.ds(ro, c), :].astype(F32))
                 + gnorm(ob_s[pl.ds(base + ro, c), :]) * _silu(gb_ref[0, pl.ds(ro, c), :].astype(F32)))
            y_ref[0, pl.ds(ro, c), :] = y.astype(y_ref.dtype)
            return 0
        lax.fori_loop(0, n, body, 0)

    combine(gfx, gbx, yx_o, lc, sx // c)
    if need_ctx:
        combine(gfc, gbc, yc_o, 0, lc // c)
    else:
        yc_o[...] = jnp.zeros_like(yc_o)


def _retention(projx, projc, log_g, gn_w, gseg, need_ctx):
    b, s, _ = projx.shape
    lc = projc.shape[1]
    lgs = log_g.reshape(2 * N_HEADS)
    lgl = jnp.repeat(log_g, HEAD_W, axis=1).reshape(2, 1, BRANCH_W)
    xs = [_proj_cols(projx, COL_RET + 256 * i, 256, s) for i in range(5)]
    cs = [_proj_cols(projc, COL_RET + 256 * i, 256, lc) for i in range(5)]
    c = RET_CHUNK
    yx, yc = pl.pallas_call(
        functools.partial(_ret_kernel, need_ctx=need_ctx),
        grid=(b,),
        in_specs=[pl.BlockSpec(memory_space=pltpu.SMEM),
                  pl.BlockSpec((2, 1, BRANCH_W), lambda bi: (0, 0, 0)),
                  pl.BlockSpec((1, BRANCH_W), lambda bi: (0, 0)),
                  pl.BlockSpec((BRANCH_W, BRANCH_W), lambda bi: (0, 0))] + xs + cs,
        out_specs=[pl.BlockSpec((1, s, BRANCH_W), lambda bi: (bi, 0, 0)),
                   pl.BlockSpec((1, lc, BRANCH_W), lambda bi: (bi, 0, 0))],
        out_shape=[jax.ShapeDtypeStruct((b, s, BRANCH_W), BF16),
                   jax.ShapeDtypeStruct((b, lc, BRANCH_W), BF16)],
        scratch_shapes=[pltpu.VMEM((lc + s, BRANCH_W), F32),
                        pltpu.VMEM((lc + s, BRANCH_W), F32),
                        pltpu.VMEM((2, BRANCH_W, BRANCH_W), F32),
                        pltpu.VMEM((2 * N_HEADS, c, c), F32),
                        pltpu.VMEM((4, c, BRANCH_W), F32)],
        compiler_params=_cparams(("parallel",), 48 * 1024 * 1024),
        name="retention",
    )(lgs, lgl, gn_w.reshape(1, BRANCH_W), gseg, *([projx] * 5), *([projc] * 5))
    return yx, yc


def _merge_kernel(oa, ob, oc, od, g0, g1, g2, g3, x_ref, gate_ref, sc_ref, sh_ref, nw_ref, wb_ref, wo_ref,
                  xn_o, h2_o):
    acc = None
    for i, (o, g) in enumerate(((oa, g0), (ob, g1), (oc, g2), (od, g3))):
        t = jax.nn.sigmoid(g[0].astype(F32)) * _dot(o[0], wb_ref[i])
        acc = t if acc is None else acc + t
    y = _dot(acc.astype(BF16), wo_ref[...])
    xn = x_ref[0] + gate_ref[0] * y
    xn_o[0] = xn
    h2_o[0] = _rms(xn) * nw_ref[...] * (1.0 + sc_ref[0]) + sh_ref[0]


def _merge(outs, proj, x, gate, sc2, sh2, n2w, wb, wo):
    b, l, d = x.shape
    tm = min(512, l)
    tok = lambda w: pl.BlockSpec((1, tm, w), lambda bi, i: (bi, i, 0))
    vec = pl.BlockSpec((1, 1, d), lambda bi, i: (bi, 0, 0))
    gates = [pl.BlockSpec((1, tm, d), lambda bi, i, k=k: (bi, i, COL_GATES // d + k)) for k in range(N_BRANCH)]
    return pl.pallas_call(
        _merge_kernel,
        grid=(b, l // tm),
        in_specs=[tok(BRANCH_W)] * 4 + gates + [tok(d), vec, vec, vec,
                                                pl.BlockSpec((1, d), lambda bi, i: (0, 0)),
                                                pl.BlockSpec(wb.shape, lambda bi, i: (0, 0, 0)),
                                                pl.BlockSpec(wo.shape, lambda bi, i: (0, 0))],
        out_specs=[tok(d), tok(d)],
        out_shape=[jax.ShapeDtypeStruct((b, l, d), F32), jax.ShapeDtypeStruct((b, l, d), F32)],
        compiler_params=_cparams(("parallel", "parallel"), 48 * 1024 * 1024),
        name="merge",
    )(*outs, proj, proj, proj, proj, x, gate, sc2, sh2, n2w, wb, wo)


def _topk_rows(s, k):
    r, t = s.shape
    iota = lax.broadcasted_iota(jnp.int32, s.shape, 0)
    out_row = lax.broadcasted_iota(jnp.int32, (k, t), 0)
    vals = jnp.zeros((k, t), F32)
    idxs = jnp.zeros((k, t), jnp.int32)
    for i in range(k):
        m = jnp.max(s, axis=0, keepdims=True)
        idx = jnp.min(jnp.where(s == m, iota, r), axis=0, keepdims=True)
        vals = jnp.where(out_row == i, m, vals)
        idxs = jnp.where(out_row == i, idx, idxs)
        s = jnp.where(iota == idx, -jnp.inf, s)
    return vals, idxs


def _select_rows(table, sel, k):
    out = jnp.zeros_like(table)
    for r in range(k):
        out = jnp.where(sel == r, table[r:r + 1, :], out)
    return out


def _split_bf16(x):
    hi = x.astype(BF16)
    return hi, (x - hi.astype(F32)).astype(BF16)


_CAND_MID = tuple(PEER_TOPK // (i + 1) for i in range(1, 8))
_CAND_ROWS = PEER_TOPK + 8 * len(_CAND_MID) + 8


def _peer_candidates(s1, s2):
    t = s1.shape[1]
    sub = lax.broadcasted_iota(jnp.int32, (8, t), 0)
    blocks = [s1[0:1] + s2]
    for i, nj in enumerate(_CAND_MID, start=1):
        blocks.append(jnp.where(sub < nj, s1[i:i + 1] + s2[0:8], -jnp.inf))
    blocks.append(s1[8:16] + s2[0:1])
    return jnp.concatenate(blocks, axis=0)


def _peer_candidate_ranks(pos):
    mid = pos - PEER_TOPK
    i = jnp.where(pos < PEER_TOPK, 0, jnp.where(pos < _CAND_ROWS - 8, (mid >> 3) + 1, pos - (_CAND_ROWS - 16)))
    j = jnp.where(pos < PEER_TOPK, pos, jnp.where(pos < _CAND_ROWS - 8, mid & 7, 0))
    return i, j


def _peer_route_kernel(h_ref, wqh_ref, wql_ref, kh_ref, kl_ref, a_o, b_o, g_o, q_s, a_s, b_s, g_s):
    k = PEER_TOPK
    assert k == 16
    hh, hl = _split_bf16(h_ref[...])
    q_s[...] = _dot(hh, wqh_ref[...]) + (_dot(hl, wqh_ref[...]) + _dot(hh, wql_ref[...]))

    def head(h, _):
        lo = pl.multiple_of(h * PEER_DK, PEER_DK)
        qh, ql = _split_bf16(q_s[:, pl.ds(lo, PEER_DK)])
        kh, kl = kh_ref[h], kl_ref[h]
        s = _dot_nt(kh, qh) + (_dot_nt(kh, ql) + _dot_nt(kl, qh))
        s1, i1 = _topk_rows(s[:PEER_N_KEYS], k)
        s2, i2 = _topk_rows(s[PEER_N_KEYS:], k)
        ts, tpos = _topk_rows(_peer_candidates(s1, s2), k)
        e = jnp.exp(ts - ts[0:1, :])
        gate = e / jnp.sum(e, axis=0, keepdims=True)
        ri, rj = _peer_candidate_ranks(tpos)
        ro = pl.multiple_of(h * k, k)
        a_s[pl.ds(ro, k), :] = _select_rows(i1, ri, k)
        b_s[pl.ds(ro, k), :] = _select_rows(i2, rj, k)
        g_s[pl.ds(ro, k), :] = gate
        return 0

    lax.fori_loop(0, PEER_HEADS, head, 0)
    a_o[...] = a_s[...].T
    b_o[...] = b_s[...].T
    g_o[...] = g_s[...].T


def _peer_route(h2, wq_hl, keys_hl):
    n, d = h2.shape
    t = 256
    wq_hi, wq_lo = wq_hl
    k_hi, k_lo = keys_hl
    return pl.pallas_call(
        _peer_route_kernel,
        grid=(n // t,),
        in_specs=[pl.BlockSpec((t, d), lambda i: (i, 0)),
                  pl.BlockSpec(wq_hi.shape, lambda i: (0, 0)),
                  pl.BlockSpec(wq_lo.shape, lambda i: (0, 0)),
                  pl.BlockSpec(k_hi.shape, lambda i: (0, 0, 0)),
                  pl.BlockSpec(k_lo.shape, lambda i: (0, 0, 0))],
        out_specs=[pl.BlockSpec((t, PEER_SLOTS), lambda i: (i, 0))] * 3,
        out_shape=[jax.ShapeDtypeStruct((n, PEER_SLOTS), jnp.int32),
                   jax.ShapeDtypeStruct((n, PEER_SLOTS), jnp.int32),
                   jax.ShapeDtypeStruct((n, PEER_SLOTS), F32)],
        scratch_shapes=[pltpu.VMEM((t, PEER_HEADS * PEER_DK), F32),
                        pltpu.VMEM((PEER_SLOTS, t), jnp.int32),
                        pltpu.VMEM((PEER_SLOTS, t), jnp.int32),
                        pltpu.VMEM((PEER_SLOTS, t), F32)],
        compiler_params=_cparams(("parallel",), 48 * 1024 * 1024),
        name="peer_route",
    )(h2, wq_hi, wq_lo, k_hi, k_lo)


_HI16 = -65536


def _bf16_bits(w):
    r = lax.bitcast_convert_type(w, jnp.int32)
    r = r + 0x7FFF + (lax.shift_right_logical(r, 16) & 1)
    return r & _HI16


def _peer_ffn_kernel(h_ref, a_ref, b_ref, g_ref, x_ref, gate_ref, u_ref, v_ref, o_ref, hb_s, w_s, *, ec, unroll):
    e = pl.program_id(1)
    t = h_ref.shape[0]
    half = t // 2
    nk = PEER_N_KEYS

    @pl.when(e == 0)
    def _():
        hb_s[...] = h_ref[...].astype(BF16)
        o_ref[...] = jnp.zeros_like(o_ref)
        jio = lax.broadcasted_iota(jnp.int32, (nk, PEER_SLOTS), 0)

        def tile(tt):
            arow = jnp.broadcast_to(a_ref[pl.ds(tt, 1), :], (nk, PEER_SLOTS))
            brow = jnp.broadcast_to(b_ref[pl.ds(tt, 1), :], (nk, PEER_SLOTS))
            grow = jnp.broadcast_to(g_ref[pl.ds(tt, 1), :], (nk, PEER_SLOTS))
            cm = jnp.where(jio == arow, grow, 0.0).astype(BF16)
            bm = jnp.where(jio == brow, 1.0, 0.0).astype(BF16)
            return _dot_nt(cm, bm)

        def build(tb, _):
            for u in range(unroll):
                tt = tb * unroll + u
                word = _bf16_bits(tile(tt + half)) | lax.shift_right_logical(_bf16_bits(tile(tt)), 16)
                w_s[pl.ds(pl.multiple_of(tt * PEER_W_PITCH, 8), nk), :] = word
            return 0

        lax.fori_loop(0, half // unroll, build, 0)

    hid = _dot(hb_s[...], u_ref[0])
    j0 = e * (ec // nk)
    words = jnp.concatenate([w_s[pl.ds(j0 + j, half, stride=PEER_W_PITCH), :] for j in range(ec // nk)], axis=1)
    w_lo = lax.bitcast_convert_type(lax.shift_left(words, 16), F32)
    w_hi = lax.bitcast_convert_type(words & _HI16, F32)
    wc = jnp.concatenate([w_lo, w_hi], axis=0)
    act = 0.5 * hid * (1.0 + lax.erf(hid * SQRT_HALF))
    o_ref[...] += _dot((wc * act).astype(BF16), v_ref[...])

    @pl.when(e == pl.num_programs(1) - 1)
    def _():
        o_ref[...] = x_ref[...] + gate_ref[0] * o_ref[...]


def _peer_ffn(h2, a, b_idx, g, x, gate, u_blk, v, l):
    n, d = h2.shape
    ne = v.shape[0]
    neb, _, ec = u_blk.shape
    t = min(512, l)
    unroll = 16
    assert l % t == 0 and ne == PEER_N_KEYS * PEER_N_KEYS and neb * ec == ne and (t // 2) % unroll == 0
    tok = lambda w: pl.BlockSpec((t, w), lambda i, e: (i, 0))
    return pl.pallas_call(
        functools.partial(_peer_ffn_kernel, ec=ec, unroll=unroll),
        grid=(n // t, neb),
        in_specs=[tok(d), tok(PEER_SLOTS), tok(PEER_SLOTS), tok(PEER_SLOTS), tok(d),
                  pl.BlockSpec((1, 1, d), lambda i, e: ((i * t) // l, 0, 0)),
                  pl.BlockSpec((1, d, ec), lambda i, e: (e, 0, 0)),
                  pl.BlockSpec((ec, d), lambda i, e: (e, 0))],
        out_specs=tok(d),
        out_shape=jax.ShapeDtypeStruct((n, d), F32),
        scratch_shapes=[pltpu.VMEM((t, d), BF16),
                        pltpu.VMEM((t // 2 * PEER_W_PITCH, PEER_N_KEYS), jnp.int32)],
        compiler_params=_cparams(("parallel", "arbitrary"), VMEM_LIMIT_V7X),
        name="peer_ffn",
    )(h2, a, b_idx, g, x, gate, u_blk, v)


def _final_norm_kernel(x_ref, w_ref, o_ref):
    o_ref[...] = _rms(x_ref[...]) * w_ref[...]


def _final_norm(x, w):
    n, d = x.shape
    tm = min(1024, n)
    return pl.pallas_call(
        _final_norm_kernel,
        grid=(n // tm,),
        in_specs=[pl.BlockSpec((tm, d), lambda i: (i, 0)), pl.BlockSpec((1, d), lambda i: (0, 0))],
        out_specs=pl.BlockSpec((tm, d), lambda i: (i, 0)),
        out_shape=jax.ShapeDtypeStruct((n, d), F32),
        compiler_params=_cparams(("parallel",), 40 * 1024 * 1024),
        name="final_norm",
    )(x, w.reshape(1, d))


def _layout_w_in(w):
    parts = jnp.split(w, IN_OFFSETS, axis=1)
    z = lambda n: jnp.zeros((w.shape[0], n), w.dtype)
    kpe_blk = jnp.concatenate([z(MLA_NOPE), parts[2], z(MLA_HEAD_PAD - MLA_NOPE - MLA_ROPE)], axis=1)
    return jnp.concatenate([parts[0], parts[1], kpe_blk] + list(parts[3:]), axis=1).astype(BF16)


def _layout_mla(w_uq, w_ukv):
    qh = w_uq.reshape(MLA_Q_LORA, N_HEADS, MLA_NOPE + MLA_ROPE)
    qh = jnp.pad(qh, ((0, 0), (0, 0), (0, MLA_HEAD_PAD - MLA_NOPE - MLA_ROPE)))
    kv = w_ukv.reshape(MLA_KV_LORA, N_HEADS, MLA_NOPE + MLA_V)
    kh = jnp.pad(kv[:, :, :MLA_NOPE], ((0, 0), (0, 0), (0, MLA_HEAD_PAD - MLA_NOPE)))
    vh = kv[:, :, MLA_NOPE:]
    return (qh.reshape(MLA_Q_LORA, -1).astype(BF16), kh.reshape(MLA_KV_LORA, -1).astype(BF16),
            vh.reshape(MLA_KV_LORA, -1).T.astype(BF16))


def _static_mats():
    gseg = np.kron(np.eye(N_HEADS), np.ones((HEAD_W, HEAD_W))).astype(np.float32)
    pm = np.zeros((MLA_HEAD_PAD, MLA_HEAD_PAD), np.float32)
    for dd in range(MLA_ROPE):
        blk, j = dd // 16, dd % 16
        pm[MLA_NOPE + blk * 16 + (j + 8) % 16, MLA_NOPE + dd] = 1.0
    pg = np.zeros((BRANCH_W, BRANCH_W), np.float32)
    for i in range(BRANCH_W):
        off, dd = (i // HEAD_W) * HEAD_W, i % HEAD_W
        blk, j = dd // 32, dd % 32
        pg[off + blk * 32 + (j + 16) % 32, i] = 1.0
    ex = np.zeros((GQA_KV_HEADS * HEAD_W, BRANCH_W), np.float32)
    for i in range(BRANCH_W):
        ex[((i // HEAD_W) // (N_HEADS // GQA_KV_HEADS)) * HEAD_W + i % HEAD_W, i] = 1.0
    return jnp.asarray(gseg), jnp.asarray(pm), jnp.asarray(pg), jnp.asarray(ex, dtype=BF16), jnp.asarray(ex.T, dtype=BF16)


def _rope_half_tables(pos, hf):
    freqs = ROPE_THETA ** (-jnp.arange(hf, dtype=F32) / hf)
    ang = pos[:, None] * freqs[None, :]
    c, s = jnp.cos(ang), jnp.sin(ang)
    return jnp.concatenate([c, c], axis=1), jnp.concatenate([-s, s], axis=1)


def _axial_tables(row, col, dims):
    cr, sr = _rope_half_tables(row, dims // 4)
    cc, sc = _rope_half_tables(col, dims // 4)
    return jnp.concatenate([cr, cc], axis=1), jnp.concatenate([sr, sc], axis=1)


def _rope_tables(s):
    t = jnp.arange(s)
    row, col = (t // GRID_W).astype(F32), (t % GRID_W).astype(F32)
    c32, s32 = _axial_tables(row, col, MLA_ROPE)
    pad = MLA_HEAD_PAD - MLA_NOPE - MLA_ROPE
    cm = jnp.concatenate([jnp.ones((s, MLA_NOPE), F32), c32, jnp.ones((s, pad), F32)], axis=1)
    sm = jnp.concatenate([jnp.zeros((s, MLA_NOPE), F32), s32, jnp.zeros((s, pad), F32)], axis=1)
    c64, s64 = _axial_tables(row, col, HEAD_W)
    return cm, sm, jnp.tile(c64, (1, N_HEADS)), jnp.tile(s64, (1, N_HEADS))


def _split_f32(w):
    hi = w.astype(BF16)
    return hi, (w - hi.astype(F32)).astype(BF16)


def _layout_peer_keys(keys):
    h, _, nk, dh = keys.shape
    z = jnp.zeros((h, nk, dh), keys.dtype)
    top = jnp.concatenate([keys[:, 0], z], axis=2)
    bot = jnp.concatenate([z, keys[:, 1]], axis=2)
    return jnp.concatenate([top, bot], axis=1)


def kernel(x, c, ctx, c_ctx, mod_w, mod_b, norm1_w, norm2_w, w_in, mla_q_norm, mla_w_uq, mla_kv_norm, mla_w_ukv, gqa_q_norm, gqa_k_norm, na_bias, ret_decay_logit, ret_gn_w, w_branch, w_out, peer_w_q, peer_keys, peer_u, peer_v, final_norm_w):
    b, s, d = x.shape
    lc = ctx.shape[1]
    depth = mod_w.shape[0]
    assert d == D_MODEL and s % (GRID_W * NA_Q_ROWS) == 0 and s % 256 == 0 and lc % 256 == 0

    rows = -(-(b + 1) // 8) * 8
    cc = jnp.zeros((rows, d), F32).at[:b].set(c).at[b].set(c_ctx)
    mod = _modulation(cc, mod_w, mod_b)

    gseg, pm, pg, ex, ex_t = _static_mats()
    tables = _rope_tables(s)

    for l in range(depth):
        need_ctx = l < depth - 1
        mx = mod[l, :b].reshape(b, 1, 6, d)
        mc = jnp.broadcast_to(mod[l, b].reshape(1, 1, 6, d), (b, 1, 6, d))
        sh1x, sc1x, g1x, sh2x, sc2x, g2x = (mx[:, :, i] for i in range(6))
        sh1c, sc1c, g1c, sh2c, sc2c, g2c = (mc[:, :, i] for i in range(6))

        w_in_l = _layout_w_in(w_in[l])
        wuq, wk, wv = _layout_mla(mla_w_uq[l], mla_w_ukv[l])
        consts = (mla_q_norm[l].reshape(1, -1), wuq, mla_kv_norm[l].reshape(1, -1), wk, wv,
                  jnp.tile(gqa_q_norm[l], N_HEADS).reshape(1, -1),
                  jnp.tile(gqa_k_norm[l], GQA_KV_HEADS).reshape(1, -1), gseg, pm, pg, ex, ex_t)
        n1w = norm1_w[l].reshape(1, d)
        n2w = norm2_w[l].reshape(1, d)
        wb = w_branch[l].astype(BF16)
        wo = w_out[l].astype(BF16)
        keys_hl = _split_f32(_layout_peer_keys(peer_keys[l]))
        wq_hl = _split_f32(peer_w_q[l])
        ne = peer_u.shape[1]
        u_blk = jnp.swapaxes(peer_u[l].astype(BF16).reshape(ne // PEER_EXPERT_CHUNK, PEER_EXPERT_CHUNK, d), 1, 2)
        v_b = peer_v[l].astype(BF16)
        log_g = jax.nn.log_sigmoid(ret_decay_logit[l].astype(F32))

        projx = _inproj(x, n1w, sc1x, sh1x, w_in_l)
        projc = _inproj(ctx, n1w, sc1c, sh1c, w_in_l)
        qmx, kmx, vmx, qgx, kgx, vgx = _prep(projx, consts, tables)
        qmc, kmc, vmc, qgc, kgc, vgc = _prep(projc, consts, None)

        oa = _attention((qmx, 0), [((kmc, 0), vmc), ((kmx, 0), vmx)], MLA_HEAD_PAD, "attn_mla")
        ob = _attention((qgx, 0), [((kgc, 0), vgc), ((kgx, 0), vgx)], HEAD_W, "attn_gqa")
        oc = _na_attention(projx, projc, na_bias[l])
        od, od_c = _retention(projx, projc, log_g, ret_gn_w[l], gseg, need_ctx)

        x, h2x = _merge((oa, ob, oc, od), projx, x, g1x, sc2x, sh2x, n2w, wb, wo)
        ax, bx, gx = _peer_route(h2x.reshape(b * s, d), wq_hl, keys_hl)
        x = _peer_ffn(h2x.reshape(b * s, d), ax, bx, gx, x.reshape(b * s, d), g2x, u_blk, v_b, s).reshape(b, s, d)

        if need_ctx:
            ca = _attention((qmc, 0), [((kmc, 0), vmc)], MLA_HEAD_PAD, "attn_mla_ctx")
            cb = _attention((qgc, 0), [((kgc, 0), vgc)], HEAD_W, "attn_gqa_ctx")
            nv_t = jnp.swapaxes(projc[:, :, COL_NV:COL_NV + BRANCH_W], 1, 2)
            ccx = _attention((projc, COL_NQ), [((projc, COL_NK), nv_t)], HEAD_W, "attn_na_ctx",
                             qscale=HEAD_W ** -0.5)
            ctx, h2c = _merge((ca, cb, ccx, od_c), projc, ctx, g1c, sc2c, sh2c, n2w, wb, wo)
            ac, bc, gc = _peer_route(h2c.reshape(b * lc, d), wq_hl, keys_hl)
            ctx = _peer_ffn(h2c.reshape(b * lc, d), ac, bc, gc, ctx.reshape(b * lc, d), g2c, u_blk, v_b,
                            lc).reshape(b, lc, d)

    return _final_norm(x.reshape(b * s, d), final_norm_w).reshape(b, s, d)
```

```python
import functools

import numpy as np
import jax
import jax.numpy as jnp
from jax import lax
from jax.experimental import pallas as pl
from jax.experimental.pallas import tpu as pltpu

F32 = jnp.float32
BF16 = jnp.bfloat16
HIGHEST = lax.Precision.HIGHEST

D_MODEL = 1024
GRID_W = 64
ROPE_THETA = 10000.0
EPS = 1e-6
N_HEADS = 4
HEAD_W = 64
BRANCH_W = N_HEADS * HEAD_W
MLA_NOPE, MLA_ROPE, MLA_V = 64, 32, 64
MLA_Q_LORA, MLA_KV_LORA = 256, 128
MLA_SCALE = (MLA_NOPE + MLA_ROPE) ** -0.5
MLA_HEAD_PAD = 128
GQA_KV_HEADS = 2
NA_WIN_R, NA_WIN_C = 8, 16
NA_Q_ROWS = 4
ATTN_KEY_CHUNK = 1024
RET_CHUNK = 128
N_BRANCH = 4
PEER_HEADS, PEER_N_KEYS, PEER_TOPK, PEER_DK = 8, 128, 16, 128
PEER_SLOTS = PEER_HEADS * PEER_TOPK
PEER_W_PITCH = PEER_N_KEYS + 8
PEER_EXPERT_CHUNK = 1024
SQRT_HALF = 0.7071067811865476
NEG_BIG = -1e30

IN_SIZES = (256, 128, 32, 256, 128, 128, 256, 256, 256, 256, 256, 256, 256, 256, 4096)
IN_OFFSETS = tuple(int(v) for v in np.cumsum(IN_SIZES)[:-1])
PROJ_COLS = 7168
COL_NQ, COL_NK, COL_NV = 1024, 1280, 1536
COL_RET = 1792
COL_GATES = 3072

VMEM_LIMIT_V7X = 56 * 1024 * 1024


def _cparams(sem, vmem=None):
    return pltpu.CompilerParams(dimension_semantics=sem, vmem_limit_bytes=vmem)


def _dot(a, b):
    return jnp.dot(a, b, preferred_element_type=F32)


def _dot_hi(a, b):
    return jnp.dot(a, b, preferred_element_type=F32, precision=HIGHEST)


def _dot_nt(a, b):
    return lax.dot_general(a, b, (((1,), (1,)), ((), ())), preferred_element_type=F32)


def _dot_nt_hi(a, b):
    return lax.dot_general(a, b, (((1,), (1,)), ((), ())), preferred_element_type=F32, precision=HIGHEST)


def _rms(x):
    return x * lax.rsqrt(jnp.mean(x * x, axis=-1, keepdims=True) + EPS)


def _silu(x):
    return x * jax.nn.sigmoid(x)


def _head_mask(shape, h, width=HEAD_W):
    lane = lax.broadcasted_iota(jnp.int32, shape, len(shape) - 1)
    lo = h * width
    return (lane >= lo) & (lane < lo + width)


def _mod_kernel(c_ref, w_ref, b_ref, o_ref):
    o_ref[0] = _dot_hi(_silu(c_ref[...]), w_ref[0]) + b_ref[0]


def _modulation(cc, mod_w, mod_b):
    depth, d, n = mod_w.shape
    rows = cc.shape[0]
    tn = 1536
    return pl.pallas_call(
        _mod_kernel,
        grid=(depth, n // tn),
        in_specs=[pl.BlockSpec((rows, d), lambda l, j: (0, 0)),
                  pl.BlockSpec((1, d, tn), lambda l, j: (l, 0, j)),
                  pl.BlockSpec((1, 1, tn), lambda l, j: (l, 0, j))],
        out_specs=pl.BlockSpec((1, rows, tn), lambda l, j: (l, 0, j)),
        out_shape=jax.ShapeDtypeStruct((depth, rows, n), F32),
        compiler_params=_cparams(("parallel", "parallel"), 40 * 1024 * 1024),
        name="modulation",
    )(cc, mod_w, mod_b.reshape(depth, 1, n))


def _inproj_kernel(x_ref, nw_ref, sc_ref, sh_ref, w_ref, o_ref, h_scr):
    @pl.when(pl.program_id(2) == 0)
    def _():
        h = _rms(x_ref[0]) * nw_ref[...] * (1.0 + sc_ref[0]) + sh_ref[0]
        h_scr[...] = h.astype(BF16)

    o_ref[0] = _dot(h_scr[...], w_ref[...]).astype(o_ref.dtype)


def _inproj(x, nw, sc, sh, w):
    b, l, d = x.shape
    n = w.shape[1]
    tm = min(512, l)
    tn = 1024
    return pl.pallas_call(
        _inproj_kernel,
        grid=(b, l // tm, n // tn),
        in_specs=[pl.BlockSpec((1, tm, d), lambda bi, i, j: (bi, i, 0)),
                  pl.BlockSpec((1, d), lambda bi, i, j: (0, 0)),
                  pl.BlockSpec((1, 1, d), lambda bi, i, j: (bi, 0, 0)),
                  pl.BlockSpec((1, 1, d), lambda bi, i, j: (bi, 0, 0)),
                  pl.BlockSpec((d, tn), lambda bi, i, j: (0, j))],
        out_specs=pl.BlockSpec((1, tm, tn), lambda bi, i, j: (bi, i, j)),
        out_shape=jax.ShapeDtypeStruct((b, l, n), BF16),
        scratch_shapes=[pltpu.VMEM((tm, d), BF16)],
        compiler_params=_cparams(("parallel", "parallel", "arbitrary"), 40 * 1024 * 1024),
        name="inproj",
    )(x, nw, sc, sh, w)


def _prep_kernel(*refs, use_rope):
    (p_ref, qn_ref, wuq_ref, kvn_ref, wk_ref, wv_ref, gqn_ref, gkn_ref, gseg_ref, pm_ref, pg_ref,
     e_ref, et_ref) = refs[:13]
    if use_rope:
        cm_ref, sm_ref, cg_ref, sg_ref = refs[13:17]
        outs = refs[17:]
    else:
        outs = refs[13:]
    qm_o, km_o, vm_o, qg_o, kg_o, vg_o = outs

    pb = p_ref[0]
    cq = pb[:, 0:256].astype(F32)
    ckv = pb[:, 256:384].astype(F32)
    kpe = pb[:, 384:512].astype(F32)
    gq = pb[:, 512:768].astype(F32)
    gk = pb[:, 768:896].astype(F32)
    gv = pb[:, 896:1024]

    cqn = (_rms(cq) * qn_ref[...]).astype(BF16)
    qa = _dot(cqn, wuq_ref[...])
    ckn = (_rms(ckv) * kvn_ref[...]).astype(BF16)
    kn = _dot(ckn, wk_ref[...])
    vm_t = _dot_nt(wv_ref[...], ckn)
    if use_rope:
        cm, sm = cm_ref[...], sm_ref[...]
        pm = pm_ref[...]

        def rope_m(t):
            return t * cm + _dot_hi(t, pm) * sm

        qa = jnp.concatenate([rope_m(qa[:, h * 128:(h + 1) * 128]) for h in range(N_HEADS)], axis=1)
        kpe = rope_m(kpe)
    km = kn + jnp.concatenate([kpe] * N_HEADS, axis=1)
    qm_o[0] = (qa * MLA_SCALE).astype(BF16)
    km_o[0] = km.astype(BF16)
    vm_o[0] = vm_t.astype(BF16)

    gseg = gseg_ref[...]
    gqn = gq * lax.rsqrt(_dot_hi(gq * gq, gseg) * (1.0 / HEAD_W) + EPS) * gqn_ref[...]
    gkn = gk * lax.rsqrt(_dot_hi(gk * gk, gseg[:128, :128]) * (1.0 / HEAD_W) + EPS) * gkn_ref[...]
    if use_rope:
        cg, sg = cg_ref[...], sg_ref[...]
        pg = pg_ref[...]
        gqn = gqn * cg + _dot_hi(gqn, pg) * sg
        gkn = gkn * cg[:, :128] + _dot_hi(gkn, pg[:128, :128]) * sg[:, :128]
    qg_o[0] = (gqn * (HEAD_W ** -0.5)).astype(BF16)
    e = e_ref[...]
    kg_o[0] = _dot(gkn.astype(BF16), e).astype(BF16)
    vg_o[0] = _dot_nt(et_ref[...], gv).astype(BF16)


def _prep(proj, consts, tables):
    b, l, _ = proj.shape
    tm = min(512, l)
    use_rope = tables is not None
    full = lambda a: pl.BlockSpec(a.shape, lambda bi, i: (0,) * a.ndim)
    in_specs = [pl.BlockSpec((1, tm, 1024), lambda bi, i: (bi, i, 0))] + [full(a) for a in consts]
    args = [proj] + list(consts)
    if use_rope:
        in_specs += [pl.BlockSpec((tm, t.shape[1]), lambda bi, i: (i, 0)) for t in tables]
        args += list(tables)
    tok = lambda w: (pl.BlockSpec((1, tm, w), lambda bi, i: (bi, i, 0)), jax.ShapeDtypeStruct((b, l, w), BF16))
    tr = (pl.BlockSpec((1, BRANCH_W, tm), lambda bi, i: (bi, 0, i)), jax.ShapeDtypeStruct((b, BRANCH_W, l), BF16))
    outs = (tok(512), tok(512), tr, tok(256), tok(256), tr)
    return pl.pallas_call(
        functools.partial(_prep_kernel, use_rope=use_rope),
        grid=(b, l // tm),
        in_specs=in_specs,
        out_specs=[o[0] for o in outs],
        out_shape=[o[1] for o in outs],
        compiler_params=_cparams(("parallel", "parallel"), 40 * 1024 * 1024),
        name="prep_rope" if use_rope else "prep",
    )(*args)


def _attn_kernel(*refs, nseg, dq, tk, qscale):
    q_ref = refs[0]
    segs = [(refs[1 + 2 * i], refs[2 + 2 * i]) for i in range(nseg)]
    o_ref = refs[1 + 2 * nseg]
    tq = q_ref.shape[1]
    gw = 256
    hpg = gw // dq
    qstacks = []
    for g in range(N_HEADS // hpg):
        qg = q_ref[0, :, g * gw:(g + 1) * gw]
        if qscale is not None:
            qg = qg * jnp.asarray(qscale, BF16)
        qstacks.append(jnp.concatenate([jnp.where(_head_mask(qg.shape, j, dq), qg, jnp.zeros_like(qg))
                                        for j in range(hpg)], axis=0))
    carry = tuple((jnp.full((1, tq), -jnp.inf, F32), jnp.zeros((1, tq), F32), jnp.zeros((HEAD_W, tq), F32))
                  for _ in range(N_HEADS))
    for k_ref, vt_ref in segs:
        lk = k_ref.shape[1]
        tkk = min(tk, lk)

        def body(c, carry, k_ref=k_ref, vt_ref=vt_ref, tkk=tkk):
            off = c * tkk if isinstance(c, int) else pl.multiple_of(c * tkk, tkk)
            new = []
            st_g = [_dot_nt(k_ref[0, pl.ds(off, tkk), g * gw:(g + 1) * gw], qstacks[g])
                    for g in range(N_HEADS // hpg)]
            for h in range(N_HEADS):
                m, l, acc = carry[h]
                vt = vt_ref[0, h * HEAD_W:(h + 1) * HEAD_W, pl.ds(off, tkk)]
                st = st_g[h // hpg][:, (h % hpg) * tq:(h % hpg + 1) * tq]
                mn = jnp.maximum(m, jnp.max(st, axis=0, keepdims=True))
                alpha = jnp.exp(m - mn)
                p = jnp.exp(st - mn)
                l = alpha * l + jnp.sum(p, axis=0, keepdims=True)
                acc = alpha * acc + _dot(vt, p.astype(BF16))
                new.append((mn, l, acc))
            return tuple(new)

        if lk // tkk == 1:
            carry = body(0, carry)
        else:
            carry = lax.fori_loop(0, lk // tkk, body, carry)
    out_t = jnp.concatenate([acc * (1.0 / l) for _, l, acc in carry], axis=0)
    o_ref[0] = out_t.T.astype(o_ref.dtype)


def _attention(q, segs, dq, name, qscale=None):
    (qa, qcol) = q
    b, lq, _ = qa.shape
    wq = N_HEADS * dq
    tq = min(256, lq)
    assert qcol % wq == 0
    in_specs = [pl.BlockSpec((1, tq, wq), lambda bi, i: (bi, i, qcol // wq))]
    args = [qa]
    for (ka, kcol), vt in segs:
        assert kcol % wq == 0 and vt.shape[1] == BRANCH_W and vt.shape[2] == ka.shape[1]
        in_specs.append(pl.BlockSpec((1, ka.shape[1], wq), lambda bi, i, kcol=kcol: (bi, 0, kcol // wq)))
        in_specs.append(pl.BlockSpec((1, BRANCH_W, vt.shape[2]), lambda bi, i: (bi, 0, 0)))
        args += [ka, vt]
    return pl.pallas_call(
        functools.partial(_attn_kernel, nseg=len(segs), dq=dq, tk=ATTN_KEY_CHUNK, qscale=qscale),
        grid=(b, lq // tq),
        in_specs=in_specs,
        out_specs=pl.BlockSpec((1, tq, BRANCH_W), lambda bi, i: (bi, i, 0)),
        out_shape=jax.ShapeDtypeStruct((b, lq, BRANCH_W), BF16),
        compiler_params=_cparams(("parallel", "arbitrary"), 48 * 1024 * 1024),
        name=name,
    )(*args)


def _proj_cols(arr, col, width, rows):
    assert col % width == 0
    return pl.BlockSpec((1, rows, width), lambda *idx: (idx[0], 0, col // width))


def _na_kernel(pat_ref, ks_ref, q_ref, k_ref, v_ref, kc_ref, vc_ref, m_ref, o_ref, *, kw):
    del pat_ref
    g = pl.program_id(1)
    off = pl.multiple_of(ks_ref[g] * GRID_W, GRID_W)
    q = q_ref[0] * jnp.asarray(HEAD_W ** -0.5, BF16)
    kwin = k_ref[0, pl.ds(off, kw), :]
    vwin = v_ref[0, pl.ds(off, kw), :]
    kc = kc_ref[0]
    vc = vc_ref[0]
    out = jnp.zeros((q.shape[0], BRANCH_W), F32)
    for h in range(N_HEADS):
        qh = jnp.where(_head_mask(q.shape, h), q, jnp.zeros_like(q))
        sw = _dot_nt(qh, kwin) + m_ref[0, h]
        sc = _dot_nt(qh, kc)
        mx = jnp.maximum(jnp.max(sw, axis=-1, keepdims=True), jnp.max(sc, axis=-1, keepdims=True))
        pw = jnp.exp(sw - mx)
        pc = jnp.exp(sc - mx)
        l = jnp.sum(pw, axis=-1, keepdims=True) + jnp.sum(pc, axis=-1, keepdims=True)
        o = _dot(pw.astype(BF16), vwin) + _dot(pc.astype(BF16), vc)
        out = out + jnp.where(_head_mask(out.shape, h), o * (1.0 / l), 0.0)
    o_ref[0] = out.astype(o_ref.dtype)


def _na_plan(s):
    rows = s // GRID_W
    wr = min(NA_WIN_R, rows)
    wc = NA_WIN_C
    qr = min(NA_Q_ROWS, rows)
    kwr = min(qr + wr - 1 + (1 if qr + wr - 1 < rows else 0), rows)
    ngrp = rows // qr
    qc = np.arange(GRID_W)[:, None]
    kc = np.arange(GRID_W)[None, :]
    cs = np.clip(qc - wc // 2, 0, GRID_W - wc)
    valid_c = (kc >= cs) & (kc < cs + wc)
    rel_c = np.where(valid_c, kc - qc + (NA_WIN_C - 1), 0)
    assert (valid_c.sum(1) == wc).all()
    pats, pat_ids, ks_rows = [], [], []
    for g in range(ngrp):
        r0 = g * qr
        ks = int(np.clip(r0 - wr // 2, 0, rows - kwr))
        r = (r0 + np.arange(qr))[:, None]
        kr = (ks + np.arange(kwr))[None, :]
        rs = np.clip(r - wr // 2, 0, rows - wr)
        valid_r = (kr >= rs) & (kr < rs + wr)
        assert (valid_r.sum(1) == wr).all()
        rel_r = np.where(valid_r, kr - r + (NA_WIN_R - 1), 0)
        key = (valid_r.tobytes(), rel_r.tobytes())
        for pi, (pk, *_rest) in enumerate(pats):
            if pk == key:
                pat_ids.append(pi)
                break
        else:
            pat_ids.append(len(pats))
            pats.append((key, valid_r, rel_r))
        ks_rows.append(ks)
    valid_r = np.stack([p[1] for p in pats])
    rel_r = np.stack([p[2] for p in pats])
    return (qr, kwr, np.asarray(pat_ids, np.int32), np.asarray(ks_rows, np.int32), valid_r, rel_r, valid_c, rel_c)


def _na_bias_masks(na_bias, valid_r, rel_r, valid_c, rel_c):
    h = na_bias.shape[0]
    npat, qr, kwr = valid_r.shape
    ncol = 2 * NA_WIN_C - 1
    brow = na_bias[:, rel_r, :].astype(F32)
    onehot_c = ((rel_c[None] == np.arange(ncol)[:, None, None]) & valid_c[None]).astype(np.float32)
    m = jnp.einsum('hpqkc,cxy->phqxky', brow, jnp.asarray(onehot_c), precision=HIGHEST)
    valid = valid_r[:, None, :, None, :, None] & valid_c[None, None, None, :, None, :]
    m = jnp.where(valid, m, NEG_BIG)
    return m.reshape(npat, h, qr * GRID_W, kwr * GRID_W)


def _na_attention(projx, projc, na_bias):
    b, s, _ = projx.shape
    lc = projc.shape[1]
    qr, kwr, pat_ids, ks_rows, valid_r, rel_r, valid_c, rel_c = _na_plan(s)
    qb, kw = qr * GRID_W, kwr * GRID_W
    mb = _na_bias_masks(na_bias, valid_r, rel_r, valid_c, rel_c)
    grid_spec = pltpu.PrefetchScalarGridSpec(
        num_scalar_prefetch=2,
        grid=(b, s // qb),
        in_specs=[pl.BlockSpec((1, qb, 256), lambda bi, g, pat, ks: (bi, g, COL_NQ // 256)),
                  pl.BlockSpec((1, s, 256), lambda bi, g, pat, ks: (bi, 0, COL_NK // 256)),
                  pl.BlockSpec((1, s, 256), lambda bi, g, pat, ks: (bi, 0, COL_NV // 256)),
                  pl.BlockSpec((1, lc, 256), lambda bi, g, pat, ks: (bi, 0, COL_NK // 256)),
                  pl.BlockSpec((1, lc, 256), lambda bi, g, pat, ks: (bi, 0, COL_NV // 256)),
                  pl.BlockSpec((1, N_HEADS, qb, kw), lambda bi, g, pat, ks: (pat[g], 0, 0, 0))],
        out_specs=pl.BlockSpec((1, qb, BRANCH_W), lambda bi, g, pat, ks: (bi, g, 0)),
    )
    return pl.pallas_call(
        functools.partial(_na_kernel, kw=kw),
        grid_spec=grid_spec,
        out_shape=jax.ShapeDtypeStruct((b, s, BRANCH_W), BF16),
        compiler_params=_cparams(("parallel", "arbitrary"), 48 * 1024 * 1024),
        name="na_attention",
    )(jnp.asarray(pat_ids), jnp.asarray(ks_rows), projx, projx, projx, projc, projc, mb)


def _ret_kernel(lgs_ref, lgl_ref, gnw_ref, gseg_ref,
                qx, kx, vx, gfx, gbx, qc, kc, vc, gfc, gbc,
                yx_o, yc_o, of_s, ob_s, st_s, dec_s, qk_s, *, need_ctx):
    c = RET_CHUNK
    lc = qc.shape[1]
    sx = qx.shape[1]
    n_col = lax.broadcasted_iota(jnp.int32, (c, c), 0).astype(F32)
    m_row = lax.broadcasted_iota(jnp.int32, (c, c), 1).astype(F32)
    diff = n_col - m_row
    for h in range(N_HEADS):
        dec_s[h] = jnp.where(diff >= 0, jnp.exp(lgs_ref[h] * jnp.maximum(diff, 0.0)), 0.0)
        dec_s[N_HEADS + h] = jnp.where(diff <= 0, jnp.exp(lgs_ref[N_HEADS + h] * jnp.maximum(-diff, 0.0)), 0.0)
    pos = lax.broadcasted_iota(jnp.int32, (c, BRANCH_W), 0).astype(F32)
    lgf, lgb = lgl_ref[0], lgl_ref[1]
    qk_s[0] = jnp.exp(lgf * (pos + 1.0))
    qk_s[1] = jnp.exp(lgf * (c - 1.0 - pos))
    qk_s[2] = jnp.exp(lgb * (c - pos))
    qk_s[3] = jnp.exp(lgb * pos)
    cd_f = jnp.exp(lgf * float(c))
    cd_b = jnp.exp(lgb * float(c))
    st_s[...] = jnp.zeros_like(st_s)
    rowb = lax.broadcasted_iota(jnp.int32, (BRANCH_W, BRANCH_W), 0) // HEAD_W
    colb = lax.broadcasted_iota(jnp.int32, (BRANCH_W, BRANCH_W), 1) // HEAD_W
    bd_mask = rowb == colb

    def chunk_step(q, k, v, d, dec_off, cd):
        kk = k * jnp.asarray(HEAD_W ** -0.5, BF16)
        state = st_s[d]
        o = _dot(q, state.astype(BF16)) * qk_s[2 * d]
        for h in range(N_HEADS):
            km = jnp.where(_head_mask(kk.shape, h), kk, jnp.zeros_like(kk))
            inner = _dot_nt(q, km) * dec_s[dec_off + h]
            r = _dot(inner.astype(BF16), v)
            o = o + jnp.where(_head_mask(r.shape, h), r, 0.0)
        kd = (kk.astype(F32) * qk_s[2 * d + 1]).astype(BF16)
        upd = lax.dot_general(kd, v, (((0,), (0,)), ((), ())), preferred_element_type=F32)
        st_s[d] = state * cd + jnp.where(bd_mask, upd, 0.0)
        return o

    def scan(q_ref, k_ref, v_ref, base, n):
        def body(i, _):
            fo = pl.multiple_of(i * c, c)
            bo = pl.multiple_of((n - 1 - i) * c, c)
            of_s[pl.ds(base + fo, c), :] = chunk_step(
                q_ref[0, pl.ds(fo, c), :], k_ref[0, pl.ds(fo, c), :], v_ref[0, pl.ds(fo, c), :], 0, 0, cd_f)
            ob_s[pl.ds(base + bo, c), :] = chunk_step(
                q_ref[0, pl.ds(bo, c), :], k_ref[0, pl.ds(bo, c), :], v_ref[0, pl.ds(bo, c), :], 1, N_HEADS, cd_b)
            return 0
        lax.fori_loop(0, n, body, 0)

    scan(qc, kc, vc, 0, lc // c)
    scan(qx, kx, vx, lc, sx // c)

    gseg = gseg_ref[...]
    gnw = gnw_ref[...]

    def gnorm(o):
        mu = _dot_hi(o, gseg) * (1.0 / HEAD_W)
        dlt = o - mu
        var = _dot_hi(dlt * dlt, gseg) * (1.0 / HEAD_W)
        return dlt * lax.rsqrt(var + EPS) * gnw

    def combine(gf_ref, gb_ref, y_ref, base, n):
        def body(i, _):
            ro = pl.multiple_of(i * c, c)
            y = (gnorm(of_s[pl.ds(base + ro, c), :]) * _silu(gf_ref[0, pl.ds(ro, c), :].astype(F32))
                 + gnorm(ob_s[pl.ds(base + ro, c), :]) * _silu(gb_ref[0, pl.ds(ro, c), :].astype(F32)))
            y_ref[0, pl.ds(ro, c), :] = y.astype(y_ref.dtype)
            return 0
        lax.fori_loop(0, n, body, 0)

    combine(gfx, gbx, yx_o, lc, sx // c)
    if need_ctx:
        combine(gfc, gbc, yc_o, 0, lc // c)
    else:
        yc_o[...] = jnp.zeros_like(yc_o)


def _retention(projx, projc, log_g, gn_w, gseg, need_ctx):
    b, s, _ = projx.shape
    lc = projc.shape[1]
    lgs = log_g.reshape(2 * N_HEADS)
    lgl = jnp.repeat(log_g, HEAD_W, axis=1).reshape(2, 1, BRANCH_W)
    xs = [_proj_cols(projx, COL_RET + 256 * i, 256, s) for i in range(5)]
    cs = [_proj_cols(projc, COL_RET + 256 * i, 256, lc) for i in range(5)]
    c = RET_CHUNK
    yx, yc = pl.pallas_call(
        functools.partial(_ret_kernel, need_ctx=need_ctx),
        grid=(b,),
        in_specs=[pl.BlockSpec(memory_space=pltpu.SMEM),
                  pl.BlockSpec((2, 1, BRANCH_W), lambda bi: (0, 0, 0)),
                  pl.BlockSpec((1, BRANCH_W), lambda bi: (0, 0)),
                  pl.BlockSpec((BRANCH_W, BRANCH_W), lambda bi: (0, 0))] + xs + cs,
        out_specs=[pl.BlockSpec((1, s, BRANCH_W), lambda bi: (bi, 0, 0)),
                   pl.BlockSpec((1, lc, BRANCH_W), lambda bi: (bi, 0, 0))],
        out_shape=[jax.ShapeDtypeStruct((b, s, BRANCH_W), BF16),
                   jax.ShapeDtypeStruct((b, lc, BRANCH_W), BF16)],
        scratch_shapes=[pltpu.VMEM((lc + s, BRANCH_W), F32),
                        pltpu.VMEM((lc + s, BRANCH_W), F32),
                        pltpu.VMEM((2, BRANCH_W, BRANCH_W), F32),
                        pltpu.VMEM((2 * N_HEADS, c, c), F32),
                        pltpu.VMEM((4, c, BRANCH_W), F32)],
        compiler_params=_cparams(("parallel",), 48 * 1024 * 1024),
        name="retention",
    )(lgs, lgl, gn_w.reshape(1, BRANCH_W), gseg, *([projx] * 5), *([projc] * 5))
    return yx, yc


def _merge_kernel(oa, ob, oc, od, g0, g1, g2, g3, x_ref, gate_ref, sc_ref, sh_ref, nw_ref, wb_ref, wo_ref,
                  xn_o, h2_o):
    acc = None
    for i, (o, g) in enumerate(((oa, g0), (ob, g1), (oc, g2), (od, g3))):
        t = jax.nn.sigmoid(g[0].astype(F32)) * _dot(o[0], wb_ref[i])
        acc = t if acc is None else acc + t
    y = _dot(acc.astype(BF16), wo_ref[...])
    xn = x_ref[0] + gate_ref[0] * y
    xn_o[0] = xn
    h2_o[0] = _rms(xn) * nw_ref[...] * (1.0 + sc_ref[0]) + sh_ref[0]


def _merge(outs, proj, x, gate, sc2, sh2, n2w, wb, wo):
    b, l, d = x.shape
    tm = min(512, l)
    tok = lambda w: pl.BlockSpec((1, tm, w), lambda bi, i: (bi, i, 0))
    vec = pl.BlockSpec((1, 1, d), lambda bi, i: (bi, 0, 0))
    gates = [pl.BlockSpec((1, tm, d), lambda bi, i, k=k: (bi, i, COL_GATES // d + k)) for k in range(N_BRANCH)]
    return pl.pallas_call(
        _merge_kernel,
        grid=(b, l // tm),
        in_specs=[tok(BRANCH_W)] * 4 + gates + [tok(d), vec, vec, vec,
                                                pl.BlockSpec((1, d), lambda bi, i: (0, 0)),
                                                pl.BlockSpec(wb.shape, lambda bi, i: (0, 0, 0)),
                                                pl.BlockSpec(wo.shape, lambda bi, i: (0, 0))],
        out_specs=[tok(d), tok(d)],
        out_shape=[jax.ShapeDtypeStruct((b, l, d), F32), jax.ShapeDtypeStruct((b, l, d), F32)],
        compiler_params=_cparams(("parallel", "parallel"), 48 * 1024 * 1024),
        name="merge",
    )(*outs, proj, proj, proj, proj, x, gate, sc2, sh2, n2w, wb, wo)


def _topk_rows(s, k):
    r, t = s.shape
    iota = lax.broadcasted_iota(jnp.int32, s.shape, 0)
    out_row = lax.broadcasted_iota(jnp.int32, (k, t), 0)
    vals = jnp.zeros((k, t), F32)
    idxs = jnp.zeros((k, t), jnp.int32)
    for i in range(k):
        m = jnp.max(s, axis=0, keepdims=True)
        idx = jnp.min(jnp.where(s == m, iota, r), axis=0, keepdims=True)
        vals = jnp.where(out_row == i, m, vals)
        idxs = jnp.where(out_row == i, idx, idxs)
        s = jnp.where(iota == idx, -jnp.inf, s)
    return vals, idxs


def _select_rows(table, sel, k):
    out = jnp.zeros_like(table)
    for r in range(k):
        out = jnp.where(sel == r, table[r:r + 1, :], out)
    return out


def _split_bf16(x):
    hi = x.astype(BF16)
    return hi, (x - hi.astype(F32)).astype(BF16)


_CAND_MID = tuple(PEER_TOPK // (i + 1) for i in range(1, 8))
_CAND_ROWS = PEER_TOPK + 8 * len(_CAND_MID) + 8


def _peer_candidates(s1, s2):
    t = s1.shape[1]
    sub = lax.broadcasted_iota(jnp.int32, (8, t), 0)
    blocks = [s1[0:1] + s2]
    for i, nj in enumerate(_CAND_MID, start=1):
        blocks.append(jnp.where(sub < nj, s1[i:i + 1] + s2[0:8], -jnp.inf))
    blocks.append(s1[8:16] + s2[0:1])
    return jnp.concatenate(blocks, axis=0)


def _peer_candidate_ranks(pos):
    mid = pos - PEER_TOPK
    i = jnp.where(pos < PEER_TOPK, 0, jnp.where(pos < _CAND_ROWS - 8, (mid >> 3) + 1, pos - (_CAND_ROWS - 16)))
    j = jnp.where(pos < PEER_TOPK, pos, jnp.where(pos < _CAND_ROWS - 8, mid & 7, 0))
    return i, j


def _peer_route_kernel(h_ref, wqh_ref, wql_ref, kh_ref, kl_ref, a_o, b_o, g_o, q_s, a_s, b_s, g_s):
    k = PEER_TOPK
    assert k == 16
    hh, hl = _split_bf16(h_ref[...])
    q_s[...] = _dot(hh, wqh_ref[...]) + (_dot(hl, wqh_ref[...]) + _dot(hh, wql_ref[...]))

    def head(h, _):
        lo = pl.multiple_of(h * PEER_DK, PEER_DK)
        qh, ql = _split_bf16(q_s[:, pl.ds(lo, PEER_DK)])
        kh, kl = kh_ref[h], kl_ref[h]
        s = _dot_nt(kh, qh) + (_dot_nt(kh, ql) + _dot_nt(kl, qh))
        s1, i1 = _topk_rows(s[:PEER_N_KEYS], k)
        s2, i2 = _topk_rows(s[PEER_N_KEYS:], k)
        ts, tpos = _topk_rows(_peer_candidates(s1, s2), k)
        e = jnp.exp(ts - ts[0:1, :])
        gate = e / jnp.sum(e, axis=0, keepdims=True)
        ri, rj = _peer_candidate_ranks(tpos)
        ro = pl.multiple_of(h * k, k)
        a_s[pl.ds(ro, k), :] = _select_rows(i1, ri, k)
        b_s[pl.ds(ro, k), :] = _select_rows(i2, rj, k)
        g_s[pl.ds(ro, k), :] = gate
        return 0

    lax.fori_loop(0, PEER_HEADS, head, 0)
    a_o[...] = a_s[...].T
    b_o[...] = b_s[...].T
    g_o[...] = g_s[...].T


def _peer_route(h2, wq_hl, keys_hl):
    n, d = h2.shape
    t = 256
    wq_hi, wq_lo = wq_hl
    k_hi, k_lo = keys_hl
    return pl.pallas_call(
        _peer_route_kernel,
        grid=(n // t,),
        in_specs=[pl.BlockSpec((t, d), lambda i: (i, 0)),
                  pl.BlockSpec(wq_hi.shape, lambda i: (0, 0)),
                  pl.BlockSpec(wq_lo.shape, lambda i: (0, 0)),
                  pl.BlockSpec(k_hi.shape, lambda i: (0, 0, 0)),
                  pl.BlockSpec(k_lo.shape, lambda i: (0, 0, 0))],
        out_specs=[pl.BlockSpec((t, PEER_SLOTS), lambda i: (i, 0))] * 3,
        out_shape=[jax.ShapeDtypeStruct((n, PEER_SLOTS), jnp.int32),
                   jax.ShapeDtypeStruct((n, PEER_SLOTS), jnp.int32),
                   jax.ShapeDtypeStruct((n, PEER_SLOTS), F32)],
        scratch_shapes=[pltpu.VMEM((t, PEER_HEADS * PEER_DK), F32),
                        pltpu.VMEM((PEER_SLOTS, t), jnp.int32),
                        pltpu.VMEM((PEER_SLOTS, t), jnp.int32),
                        pltpu.VMEM((PEER_SLOTS, t), F32)],
        compiler_params=_cparams(("parallel",), 48 * 1024 * 1024),
        name="peer_route",
    )(h2, wq_hi, wq_lo, k_hi, k_lo)


_HI16 = -65536


def _bf16_bits(w):
    return lax.bitcast_convert_type(w, jnp.int32) & _HI16


def _peer_ffn_kernel(h_ref, a_ref, b_ref, g_ref, x_ref, gate_ref, u_ref, v_ref, o_ref, hb_s, w_s, *, ec, unroll):
    e = pl.program_id(1)
    t = h_ref.shape[0]
    half = t // 2
    nk = PEER_N_KEYS

    @pl.when(e == 0)
    def _():
        hb_s[...] = h_ref[...].astype(BF16)
        o_ref[...] = jnp.zeros_like(o_ref)
        jio = lax.broadcasted_iota(jnp.int32, (nk, PEER_SLOTS), 0)

        def tile(tt):
            arow = jnp.broadcast_to(a_ref[pl.ds(tt, 1), :], (nk, PEER_SLOTS))
            brow = jnp.broadcast_to(b_ref[pl.ds(tt, 1), :], (nk, PEER_SLOTS))
            grow = jnp.broadcast_to(g_ref[pl.ds(tt, 1), :], (nk, PEER_SLOTS))
            cm = jnp.where(jio == arow, grow, 0.0).astype(BF16)
            bm_t = jnp.where(jio == brow, 1.0, 0.0).T.astype(BF16)
            return _dot(cm, bm_t)

        def build(tb, _):
            for u in range(unroll):
                tt = tb * unroll + u
                word = _bf16_bits(tile(tt + half)) | lax.shift_right_logical(_bf16_bits(tile(tt)), 16)
                w_s[pl.ds(pl.multiple_of(tt * PEER_W_PITCH, 8), nk), :] = word
            return 0

        lax.fori_loop(0, half // unroll, build, 0)

    hid = _dot(hb_s[...], u_ref[0])
    j0 = e * (ec // nk)
    words = jnp.concatenate([w_s[pl.ds(j0 + j, half, stride=PEER_W_PITCH), :] for j in range(ec // nk)], axis=1)
    w_lo = lax.bitcast_convert_type(lax.shift_left(words, 16), F32)
    w_hi = lax.bitcast_convert_type(words & _HI16, F32)
    wc = jnp.concatenate([w_lo, w_hi], axis=0)
    act = 0.5 * hid * (1.0 + lax.erf(hid * SQRT_HALF))
    o_ref[...] += _dot((wc * act).astype(BF16), v_ref[...])

    @pl.when(e == pl.num_programs(1) - 1)
    def _():
        o_ref[...] = x_ref[...] + gate_ref[0] * o_ref[...]


def _peer_ffn(h2, a, b_idx, g, x, gate, u_blk, v, l):
    n, d = h2.shape
    ne = v.shape[0]
    neb, _, ec = u_blk.shape
    t = min(512, l)
    unroll = 16
    assert l % t == 0 and ne == PEER_N_KEYS * PEER_N_KEYS and neb * ec == ne and (t // 2) % unroll == 0
    tok = lambda w: pl.BlockSpec((t, w), lambda i, e: (i, 0))
    return pl.pallas_call(
        functools.partial(_peer_ffn_kernel, ec=ec, unroll=unroll),
        grid=(n // t, neb),
        in_specs=[tok(d), tok(PEER_SLOTS), tok(PEER_SLOTS), tok(PEER_SLOTS), tok(d),
                  pl.BlockSpec((1, 1, d), lambda i, e: ((i * t) // l, 0, 0)),
                  pl.BlockSpec((1, d, ec), lambda i, e: (e, 0, 0)),
                  pl.BlockSpec((ec, d), lambda i, e: (e, 0))],
        out_specs=tok(d),
        out_shape=jax.ShapeDtypeStruct((n, d), F32),
        scratch_shapes=[pltpu.VMEM((t, d), BF16),
                        pltpu.VMEM((t // 2 * PEER_W_PITCH, PEER_N_KEYS), jnp.int32)],
        compiler_params=_cparams(("parallel", "arbitrary"), VMEM_LIMIT_V7X),
        name="peer_ffn",
    )(h2, a, b_idx, g, x, gate, u_blk, v)


def _final_norm_kernel(x_ref, w_ref, o_ref):
    o_ref[...] = _rms(x_ref[...]) * w_ref[...]


def _final_norm(x, w):
    n, d = x.shape
    tm = min(1024, n)
    return pl.pallas_call(
        _final_norm_kernel,
        grid=(n // tm,),
        in_specs=[pl.BlockSpec((tm, d), lambda i: (i, 0)), pl.BlockSpec((1, d), lambda i: (0, 0))],
        out_specs=pl.BlockSpec((tm, d), lambda i: (i, 0)),
        out_shape=jax.ShapeDtypeStruct((n, d), F32),
        compiler_params=_cparams(("parallel",), 40 * 1024 * 1024),
        name="final_norm",
    )(x, w.reshape(1, d))


def _layout_w_in(w):
    parts = jnp.split(w, IN_OFFSETS, axis=1)
    z = lambda n: jnp.zeros((w.shape[0], n), w.dtype)
    kpe_blk = jnp.concatenate([z(MLA_NOPE), parts[2], z(MLA_HEAD_PAD - MLA_NOPE - MLA_ROPE)], axis=1)
    return jnp.concatenate([parts[0], parts[1], kpe_blk] + list(parts[3:]), axis=1).astype(BF16)


def _layout_mla(w_uq, w_ukv):
    qh = w_uq.reshape(MLA_Q_LORA, N_HEADS, MLA_NOPE + MLA_ROPE)
    qh = jnp.pad(qh, ((0, 0), (0, 0), (0, MLA_HEAD_PAD - MLA_NOPE - MLA_ROPE)))
    kv = w_ukv.reshape(MLA_KV_LORA, N_HEADS, MLA_NOPE + MLA_V)
    kh = jnp.pad(kv[:, :, :MLA_NOPE], ((0, 0), (0, 0), (0, MLA_HEAD_PAD - MLA_NOPE)))
    vh = kv[:, :, MLA_NOPE:]
    return (qh.reshape(MLA_Q_LORA, -1).astype(BF16), kh.reshape(MLA_KV_LORA, -1).astype(BF16),
            vh.reshape(MLA_KV_LORA, -1).T.astype(BF16))


def _static_mats():
    gseg = np.kron(np.eye(N_HEADS), np.ones((HEAD_W, HEAD_W))).astype(np.float32)
    pm = np.zeros((MLA_HEAD_PAD, MLA_HEAD_PAD), np.float32)
    for dd in range(MLA_ROPE):
        blk, j = dd // 16, dd % 16
        pm[MLA_NOPE + blk * 16 + (j + 8) % 16, MLA_NOPE + dd] = 1.0
    pg = np.zeros((BRANCH_W, BRANCH_W), np.float32)
    for i in range(BRANCH_W):
        off, dd = (i // HEAD_W) * HEAD_W, i % HEAD_W
        blk, j = dd // 32, dd % 32
        pg[off + blk * 32 + (j + 16) % 32, i] = 1.0
    ex = np.zeros((GQA_KV_HEADS * HEAD_W, BRANCH_W), np.float32)
    for i in range(BRANCH_W):
        ex[((i // HEAD_W) // (N_HEADS // GQA_KV_HEADS)) * HEAD_W + i % HEAD_W, i] = 1.0
    return jnp.asarray(gseg), jnp.asarray(pm), jnp.asarray(pg), jnp.asarray(ex, dtype=BF16), jnp.asarray(ex.T, dtype=BF16)


def _rope_half_tables(pos, hf):
    freqs = ROPE_THETA ** (-jnp.arange(hf, dtype=F32) / hf)
    ang = pos[:, None] * freqs[None, :]
    c, s = jnp.cos(ang), jnp.sin(ang)
    return jnp.concatenate([c, c], axis=1), jnp.concatenate([-s, s], axis=1)


def _axial_tables(row, col, dims):
    cr, sr = _rope_half_tables(row, dims // 4)
    cc, sc = _rope_half_tables(col, dims // 4)
    return jnp.concatenate([cr, cc], axis=1), jnp.concatenate([sr, sc], axis=1)


def _rope_tables(s):
    t = jnp.arange(s)
    row, col = (t // GRID_W).astype(F32), (t % GRID_W).astype(F32)
    c32, s32 = _axial_tables(row, col, MLA_ROPE)
    pad = MLA_HEAD_PAD - MLA_NOPE - MLA_ROPE
    cm = jnp.concatenate([jnp.ones((s, MLA_NOPE), F32), c32, jnp.ones((s, pad), F32)], axis=1)
    sm = jnp.concatenate([jnp.zeros((s, MLA_NOPE), F32), s32, jnp.zeros((s, pad), F32)], axis=1)
    c64, s64 = _axial_tables(row, col, HEAD_W)
    return cm, sm, jnp.tile(c64, (1, N_HEADS)), jnp.tile(s64, (1, N_HEADS))


def _split_f32(w):
    hi = w.astype(BF16)
    return hi, (w - hi.astype(F32)).astype(BF16)


def _layout_peer_keys(keys):
    h, _, nk, dh = keys.shape
    z = jnp.zeros((h, nk, dh), keys.dtype)
    top = jnp.concatenate([keys[:, 0], z], axis=2)
    bot = jnp.concatenate([z, keys[:, 1]], axis=2)
    return jnp.concatenate([top, bot], axis=1)


def kernel(x, c, ctx, c_ctx, mod_w, mod_b, norm1_w, norm2_w, w_in, mla_q_norm, mla_w_uq, mla_kv_norm, mla_w_ukv, gqa_q_norm, gqa_k_norm, na_bias, ret_decay_logit, ret_gn_w, w_branch, w_out, peer_w_q, peer_keys, peer_u, peer_v, final_norm_w):
    b, s, d = x.shape
    lc = ctx.shape[1]
    depth = mod_w.shape[0]
    assert d == D_MODEL and s % (GRID_W * NA_Q_ROWS) == 0 and s % 256 == 0 and lc % 256 == 0

    rows = -(-(b + 1) // 8) * 8
    cc = jnp.zeros((rows, d), F32).at[:b].set(c).at[b].set(c_ctx)
    mod = _modulation(cc, mod_w, mod_b)

    gseg, pm, pg, ex, ex_t = _static_mats()
    tables = _rope_tables(s)

    for l in range(depth):
        need_ctx = l < depth - 1
        mx = mod[l, :b].reshape(b, 1, 6, d)
        mc = jnp.broadcast_to(mod[l, b].reshape(1, 1, 6, d), (b, 1, 6, d))
        sh1x, sc1x, g1x, sh2x, sc2x, g2x = (mx[:, :, i] for i in range(6))
        sh1c, sc1c, g1c, sh2c, sc2c, g2c = (mc[:, :, i] for i in range(6))

        w_in_l = _layout_w_in(w_in[l])
        wuq, wk, wv = _layout_mla(mla_w_uq[l], mla_w_ukv[l])
        consts = (mla_q_norm[l].reshape(1, -1), wuq, mla_kv_norm[l].reshape(1, -1), wk, wv,
                  jnp.tile(gqa_q_norm[l], N_HEADS).reshape(1, -1),
                  jnp.tile(gqa_k_norm[l], GQA_KV_HEADS).reshape(1, -1), gseg, pm, pg, ex, ex_t)
        n1w = norm1_w[l].reshape(1, d)
        n2w = norm2_w[l].reshape(1, d)
        wb = w_branch[l].astype(BF16)
        wo = w_out[l].astype(BF16)
        keys_hl = _split_f32(_layout_peer_keys(peer_keys[l]))
        wq_hl = _split_f32(peer_w_q[l])
        ne = peer_u.shape[1]
        u_blk = jnp.swapaxes(peer_u[l].astype(BF16).reshape(ne // PEER_EXPERT_CHUNK, PEER_EXPERT_CHUNK, d), 1, 2)
        v_b = peer_v[l].astype(BF16)
        log_g = jax.nn.log_sigmoid(ret_decay_logit[l].astype(F32))

        projx = _inproj(x, n1w, sc1x, sh1x, w_in_l)
        projc = _inproj(ctx, n1w, sc1c, sh1c, w_in_l)
        qmx, kmx, vmx, qgx, kgx, vgx = _prep(projx, consts, tables)
        qmc, kmc, vmc, qgc, kgc, vgc = _prep(projc, consts, None)

        oa = _attention((qmx, 0), [((kmc, 0), vmc), ((kmx, 0), vmx)], MLA_HEAD_PAD, "attn_mla")
        ob = _attention((qgx, 0), [((kgc, 0), vgc), ((kgx, 0), vgx)], HEAD_W, "attn_gqa")
        oc = _na_attention(projx, projc, na_bias[l])
        od, od_c = _retention(projx, projc, log_g, ret_gn_w[l], gseg, need_ctx)

        x, h2x = _merge((oa, ob, oc, od), projx, x, g1x, sc2x, sh2x, n2w, wb, wo)
        ax, bx, gx = _peer_route(h2x.reshape(b * s, d), wq_hl, keys_hl)
        x = _peer_ffn(h2x.reshape(b * s, d), ax, bx, gx, x.reshape(b * s, d), g2x, u_blk, v_b, s).reshape(b, s, d)

        if need_ctx:
            ca = _attention((qmc, 0), [((kmc, 0), vmc)], MLA_HEAD_PAD, "attn_mla_ctx")
            cb = _attention((qgc, 0), [((kgc, 0), vgc)], HEAD_W, "attn_gqa_ctx")
            nv_t = jnp.swapaxes(projc[:, :, COL_NV:COL_NV + BRANCH_W], 1, 2)
            ccx = _attention((projc, COL_NQ), [((projc, COL_NK), nv_t)], HEAD_W, "attn_na_ctx",
                             qscale=HEAD_W ** -0.5)
            ctx, h2c = _merge((ca, cb, ccx, od_c), projc, ctx, g1c, sc2c, sh2c, n2w, wb, wo)
            ac, bc, gc = _peer_route(h2c.reshape(b * lc, d), wq_hl, keys_hl)
            ctx = _peer_ffn(h2c.reshape(b * lc, d), ac, bc, gc, ctx.reshape(b * lc, d), g2c, u_blk, v_b,
                            lc).reshape(b, lc, d)

    return _final_norm(x.reshape(b * s, d), final_norm_w).reshape(b, s, d)
```

```python
import functools

import numpy as np
import jax
import jax.numpy as jnp
from jax import lax
from jax.experimental import pallas as pl
from jax.experimental.pallas import tpu as pltpu

F32 = jnp.float32
BF16 = jnp.bfloat16
HIGHEST = lax.Precision.HIGHEST

D_MODEL = 1024
GRID_W = 64
ROPE_THETA = 10000.0
EPS = 1e-6
N_HEADS = 4
HEAD_W = 64
BRANCH_W = N_HEADS * HEAD_W
MLA_NOPE, MLA_ROPE, MLA_V = 64, 32, 64
MLA_Q_LORA, MLA_KV_LORA = 256, 128
MLA_SCALE = (MLA_NOPE + MLA_ROPE) ** -0.5
MLA_HEAD_PAD = 128
GQA_KV_HEADS = 2
NA_WIN_R, NA_WIN_C = 8, 16
NA_Q_ROWS = 4
ATTN_KEY_CHUNK = 1024
RET_CHUNK = 128
N_BRANCH = 4
PEER_HEADS, PEER_N_KEYS, PEER_TOPK, PEER_DK = 8, 128, 16, 128
PEER_SLOTS = PEER_HEADS * PEER_TOPK
PEER_W_PITCH = PEER_N_KEYS + 8
PEER_EXPERT_CHUNK = 1024
SQRT_HALF = 0.7071067811865476
LOG2E = 1.4426950408889634
NEG_BIG = -1e30

IN_SIZES = (256, 128, 32, 256, 128, 128, 256, 256, 256, 256, 256, 256, 256, 256, 4096)
IN_OFFSETS = tuple(int(v) for v in np.cumsum(IN_SIZES)[:-1])
PROJ_COLS = 7168
COL_NQ, COL_NK, COL_NV = 1024, 1280, 1536
COL_RET = 1792
COL_GATES = 3072

VMEM_LIMIT_V7X = 56 * 1024 * 1024


def _cparams(sem, vmem=None):
    return pltpu.CompilerParams(dimension_semantics=sem, vmem_limit_bytes=vmem)


def _dot(a, b):
    return jnp.dot(a, b, preferred_element_type=F32)


def _dot_hi(a, b):
    return jnp.dot(a, b, preferred_element_type=F32, precision=HIGHEST)


def _dot_nt(a, b):
    return lax.dot_general(a, b, (((1,), (1,)), ((), ())), preferred_element_type=F32)


def _dot_sel(x, sel):
    hi = x.astype(BF16)
    r1 = x - hi.astype(F32)
    mid = r1.astype(BF16)
    lo = (r1 - mid.astype(F32)).astype(BF16)
    return _dot(hi, sel) + (_dot(mid, sel) + _dot(lo, sel))


def _rms(x):
    return x * lax.rsqrt(jnp.mean(x * x, axis=-1, keepdims=True) + EPS)


def _silu(x):
    return x * jax.nn.sigmoid(x)


def _head_mask(shape, h, width=HEAD_W):
    lane = lax.broadcasted_iota(jnp.int32, shape, len(shape) - 1)
    lo = h * width
    return (lane >= lo) & (lane < lo + width)


def _mod_kernel(c_ref, w_ref, b_ref, o_ref):
    o_ref[0] = _dot_hi(_silu(c_ref[...]), w_ref[0]) + b_ref[0]


def _modulation(cc, mod_w, mod_b):
    depth, d, n = mod_w.shape
    rows = cc.shape[0]
    tn = 1536
    return pl.pallas_call(
        _mod_kernel,
        grid=(depth, n // tn),
        in_specs=[pl.BlockSpec((rows, d), lambda l, j: (0, 0)),
                  pl.BlockSpec((1, d, tn), lambda l, j: (l, 0, j)),
                  pl.BlockSpec((1, 1, tn), lambda l, j: (l, 0, j))],
        out_specs=pl.BlockSpec((1, rows, tn), lambda l, j: (l, 0, j)),
        out_shape=jax.ShapeDtypeStruct((depth, rows, n), F32),
        compiler_params=_cparams(("parallel", "parallel"), 40 * 1024 * 1024),
        name="modulation",
    )(cc, mod_w, mod_b.reshape(depth, 1, n))


def _inproj_kernel(x_ref, nw_ref, sc_ref, sh_ref, w_ref, o_ref, h_scr):
    @pl.when(pl.program_id(2) == 0)
    def _():
        h = _rms(x_ref[0]) * nw_ref[...] * (1.0 + sc_ref[0]) + sh_ref[0]
        h_scr[...] = h.astype(BF16)

    o_ref[0] = _dot(h_scr[...], w_ref[...]).astype(o_ref.dtype)


def _inproj(x, nw, sc, sh, w):
    b, l, d = x.shape
    n = w.shape[1]
    tm = min(512, l)
    tn = 1792
    return pl.pallas_call(
        _inproj_kernel,
        grid=(b, l // tm, n // tn),
        in_specs=[pl.BlockSpec((1, tm, d), lambda bi, i, j: (bi, i, 0)),
                  pl.BlockSpec((1, d), lambda bi, i, j: (0, 0)),
                  pl.BlockSpec((1, 1, d), lambda bi, i, j: (bi, 0, 0)),
                  pl.BlockSpec((1, 1, d), lambda bi, i, j: (bi, 0, 0)),
                  pl.BlockSpec((d, tn), lambda bi, i, j: (0, j))],
        out_specs=pl.BlockSpec((1, tm, tn), lambda bi, i, j: (bi, i, j)),
        out_shape=jax.ShapeDtypeStruct((b, l, n), BF16),
        scratch_shapes=[pltpu.VMEM((tm, d), BF16)],
        compiler_params=_cparams(("parallel", "parallel", "arbitrary"), 40 * 1024 * 1024),
        name="inproj",
    )(x, nw, sc, sh, w)


def _prep_kernel(*refs, use_rope):
    (p_ref, qn_ref, wuq_ref, kvn_ref, wk_ref, wv_ref, gqn_ref, gkn_ref, gseg_ref, pm_ref, pg_ref,
     e_ref, et_ref) = refs[:13]
    if use_rope:
        cm_ref, sm_ref, cg_ref, sg_ref = refs[13:17]
        outs = refs[17:]
    else:
        outs = refs[13:]
    qm_o, km_o, vm_o, qg_o, kg_o, vg_o = outs

    pb = p_ref[0]
    cq = pb[:, 0:256].astype(F32)
    ckv = pb[:, 256:384].astype(F32)
    kpe = pb[:, 384:512].astype(F32)
    gq = pb[:, 512:768].astype(F32)
    gk = pb[:, 768:896].astype(F32)
    gv = pb[:, 896:1024]

    cqn = (_rms(cq) * qn_ref[...]).astype(BF16)
    qa = _dot(cqn, wuq_ref[...])
    ckn = (_rms(ckv) * kvn_ref[...]).astype(BF16)
    kn = _dot(ckn, wk_ref[...])
    vm_t = _dot_nt(wv_ref[...], ckn)
    if use_rope:
        cm, sm = cm_ref[...], sm_ref[...]
        pm = pm_ref[...]

        def rope_m(t):
            return t * cm + _dot_sel(t, pm) * sm

        qa = jnp.concatenate([rope_m(qa[:, h * 128:(h + 1) * 128]) for h in range(N_HEADS)], axis=1)
        kpe = rope_m(kpe)
    km = kn + jnp.concatenate([kpe] * N_HEADS, axis=1)
    qm_o[0] = (qa * (MLA_SCALE * LOG2E)).astype(BF16)
    km_o[0] = km.astype(BF16)
    vm_o[0] = vm_t.astype(BF16)

    gseg = gseg_ref[...]
    gqn = gq * lax.rsqrt(_dot_sel(gq * gq, gseg) * (1.0 / HEAD_W) + EPS) * gqn_ref[...]
    gkn = gk * lax.rsqrt(_dot_sel(gk * gk, gseg[:128, :128]) * (1.0 / HEAD_W) + EPS) * gkn_ref[...]
    if use_rope:
        cg, sg = cg_ref[...], sg_ref[...]
        pg = pg_ref[...]
        gqn = gqn * cg + _dot_sel(gqn, pg) * sg
        gkn = gkn * cg[:, :128] + _dot_sel(gkn, pg[:128, :128]) * sg[:, :128]
    qg_o[0] = (gqn * (HEAD_W ** -0.5 * LOG2E)).astype(BF16)
    e = e_ref[...]
    kg_o[0] = _dot(gkn.astype(BF16), e).astype(BF16)
    vg_o[0] = _dot_nt(et_ref[...], gv).astype(BF16)


def _prep(proj, consts, tables):
    b, l, _ = proj.shape
    tm = min(512, l)
    use_rope = tables is not None
    full = lambda a: pl.BlockSpec(a.shape, lambda bi, i: (0,) * a.ndim)
    in_specs = [pl.BlockSpec((1, tm, 1024), lambda bi, i: (bi, i, 0))] + [full(a) for a in consts]
    args = [proj] + list(consts)
    if use_rope:
        in_specs += [pl.BlockSpec((tm, t.shape[1]), lambda bi, i: (i, 0)) for t in tables]
        args += list(tables)
    tok = lambda w: (pl.BlockSpec((1, tm, w), lambda bi, i: (bi, i, 0)), jax.ShapeDtypeStruct((b, l, w), BF16))
    tr = (pl.BlockSpec((1, BRANCH_W, tm), lambda bi, i: (bi, 0, i)), jax.ShapeDtypeStruct((b, BRANCH_W, l), BF16))
    outs = (tok(512), tok(512), tr, tok(256), tok(256), tr)
    return pl.pallas_call(
        functools.partial(_prep_kernel, use_rope=use_rope),
        grid=(b, l // tm),
        in_specs=in_specs,
        out_specs=[o[0] for o in outs],
        out_shape=[o[1] for o in outs],
        compiler_params=_cparams(("parallel", "parallel"), 40 * 1024 * 1024),
        name="prep_rope" if use_rope else "prep",
    )(*args)


def _attn_kernel(*refs, nseg, dq, tk, qscale, log2_scores):
    exp = jnp.exp2 if log2_scores else jnp.exp
    q_ref = refs[0]
    segs = [(refs[1 + 2 * i], refs[2 + 2 * i]) for i in range(nseg)]
    o_ref = refs[1 + 2 * nseg]
    tq = q_ref.shape[1]
    gw = 256
    hpg = gw // dq
    qstacks = []
    for g in range(N_HEADS // hpg):
        qg = q_ref[0, :, g * gw:(g + 1) * gw]
        if qscale is not None:
            qg = qg * jnp.asarray(qscale, BF16)
        qstacks.append(jnp.concatenate([jnp.where(_head_mask(qg.shape, j, dq), qg, jnp.zeros_like(qg))
                                        for j in range(hpg)], axis=0))
    carry = tuple((jnp.full((1, tq), -jnp.inf, F32), jnp.zeros((1, tq), F32), jnp.zeros((HEAD_W, tq), F32))
                  for _ in range(N_HEADS))

    def scores(k_ref, off, tkk):
        return [_dot_nt(k_ref[0, pl.ds(off, tkk), g * gw:(g + 1) * gw], qstacks[g])
                for g in range(N_HEADS // hpg)]

    def softmax_step(carry, head_scores, vt_ref, off, tkk):
        new = []
        for h in range(N_HEADS):
            m, l, acc = carry[h]
            vt = vt_ref[0, h * HEAD_W:(h + 1) * HEAD_W, pl.ds(off, tkk)]
            st = head_scores(h)
            mn = jnp.maximum(m, jnp.max(st, axis=0, keepdims=True))
            alpha = exp(m - mn)
            p = exp(st - mn)
            l = alpha * l + jnp.sum(p, axis=0, keepdims=True)
            acc = alpha * acc + _dot(vt, p.astype(BF16))
            new.append((mn, l, acc))
        return tuple(new)

    for k_ref, vt_ref in segs:
        lk = k_ref.shape[1]
        tkk = min(tk, lk)
        n = lk // tkk

        def body(c, carry, k_ref=k_ref, vt_ref=vt_ref, tkk=tkk):
            off = c * tkk if isinstance(c, int) else pl.multiple_of(c * tkk, tkk)
            st_g = scores(k_ref, off, tkk)
            return softmax_step(carry, lambda h: st_g[h // hpg][:, (h % hpg) * tq:(h % hpg + 1) * tq],
                                vt_ref, off, tkk)

        carry = body(0, carry) if n == 1 else lax.fori_loop(0, n, body, carry)
    out_t = jnp.concatenate([acc * (1.0 / l) for _, l, acc in carry], axis=0)
    o_ref[0] = out_t.T.astype(o_ref.dtype)


def _attention(q, segs, dq, name, qscale=None, log2_scores=True):
    (qa, qcol) = q
    b, lq, _ = qa.shape
    wq = N_HEADS * dq
    tq = min(256, lq)
    assert qcol % wq == 0
    in_specs = [pl.BlockSpec((1, tq, wq), lambda bi, i: (bi, i, qcol // wq))]
    args = [qa]
    for (ka, kcol), vt in segs:
        assert kcol % wq == 0 and vt.shape[1] == BRANCH_W and vt.shape[2] == ka.shape[1]
        in_specs.append(pl.BlockSpec((1, ka.shape[1], wq), lambda bi, i, kcol=kcol: (bi, 0, kcol // wq)))
        in_specs.append(pl.BlockSpec((1, BRANCH_W, vt.shape[2]), lambda bi, i: (bi, 0, 0)))
        args += [ka, vt]
    return pl.pallas_call(
        functools.partial(_attn_kernel, nseg=len(segs), dq=dq, tk=ATTN_KEY_CHUNK, qscale=qscale,
                          log2_scores=log2_scores),
        grid=(b, lq // tq),
        in_specs=in_specs,
        out_specs=pl.BlockSpec((1, tq, BRANCH_W), lambda bi, i: (bi, i, 0)),
        out_shape=jax.ShapeDtypeStruct((b, lq, BRANCH_W), BF16),
        compiler_params=_cparams(("parallel", "arbitrary"), 48 * 1024 * 1024),
        name=name,
    )(*args)


def _proj_cols(arr, col, width, rows):
    assert col % width == 0
    return pl.BlockSpec((1, rows, width), lambda *idx: (idx[0], 0, col // width))


def _na_kernel(pat_ref, ks_ref, q_ref, k_ref, v_ref, kc_ref, vc_ref, m_ref, o_ref, *, kw):
    del pat_ref
    g = pl.program_id(1)
    off = pl.multiple_of(ks_ref[g] * GRID_W, GRID_W)
    q = q_ref[0] * jnp.asarray(HEAD_W ** -0.5, BF16)
    kwin = k_ref[0, pl.ds(off, kw), :]
    vwin = v_ref[0, pl.ds(off, kw), :]
    kc = kc_ref[0]
    vc = vc_ref[0]
    out = jnp.zeros((q.shape[0], BRANCH_W), F32)
    for h in range(N_HEADS):
        qh = jnp.where(_head_mask(q.shape, h), q, jnp.zeros_like(q))
        sw = _dot_nt(qh, kwin) + m_ref[0, h]
        sc = _dot_nt(qh, kc)
        mx = jnp.maximum(jnp.max(sw, axis=-1, keepdims=True), jnp.max(sc, axis=-1, keepdims=True))
        pw = jnp.exp(sw - mx)
        pc = jnp.exp(sc - mx)
        l = jnp.sum(pw, axis=-1, keepdims=True) + jnp.sum(pc, axis=-1, keepdims=True)
        o = _dot(pw.astype(BF16), vwin) + _dot(pc.astype(BF16), vc)
        out = out + jnp.where(_head_mask(out.shape, h), o * (1.0 / l), 0.0)
    o_ref[0] = out.astype(o_ref.dtype)


def _na_plan(s):
    rows = s // GRID_W
    wr = min(NA_WIN_R, rows)
    wc = NA_WIN_C
    qr = min(NA_Q_ROWS, rows)
    kwr = min(qr + wr - 1 + (1 if qr + wr - 1 < rows else 0), rows)
    ngrp = rows // qr
    qc = np.arange(GRID_W)[:, None]
    kc = np.arange(GRID_W)[None, :]
    cs = np.clip(qc - wc // 2, 0, GRID_W - wc)
    valid_c = (kc >= cs) & (kc < cs + wc)
    rel_c = np.where(valid_c, kc - qc + (NA_WIN_C - 1), 0)
    assert (valid_c.sum(1) == wc).all()
    pats, pat_ids, ks_rows = [], [], []
    for g in range(ngrp):
        r0 = g * qr
        ks = int(np.clip(r0 - wr // 2, 0, rows - kwr))
        r = (r0 + np.arange(qr))[:, None]
        kr = (ks + np.arange(kwr))[None, :]
        rs = np.clip(r - wr // 2, 0, rows - wr)
        valid_r = (kr >= rs) & (kr < rs + wr)
        assert (valid_r.sum(1) == wr).all()
        rel_r = np.where(valid_r, kr - r + (NA_WIN_R - 1), 0)
        key = (valid_r.tobytes(), rel_r.tobytes())
        for pi, (pk, *_rest) in enumerate(pats):
            if pk == key:
                pat_ids.append(pi)
                break
        else:
            pat_ids.append(len(pats))
            pats.append((key, valid_r, rel_r))
        ks_rows.append(ks)
    valid_r = np.stack([p[1] for p in pats])
    rel_r = np.stack([p[2] for p in pats])
    return (qr, kwr, np.asarray(pat_ids, np.int32), np.asarray(ks_rows, np.int32), valid_r, rel_r, valid_c, rel_c)


def _na_bias_masks(na_bias, valid_r, rel_r, valid_c, rel_c):
    h = na_bias.shape[0]
    npat, qr, kwr = valid_r.shape
    ncol = 2 * NA_WIN_C - 1
    brow = na_bias[:, rel_r, :].astype(F32)
    onehot_c = ((rel_c[None] == np.arange(ncol)[:, None, None]) & valid_c[None]).astype(np.float32)
    m = jnp.einsum('hpqkc,cxy->phqxky', brow, jnp.asarray(onehot_c), precision=HIGHEST)
    valid = valid_r[:, None, :, None, :, None] & valid_c[None, None, None, :, None, :]
    m = jnp.where(valid, m, NEG_BIG)
    return m.reshape(npat, h, qr * GRID_W, kwr * GRID_W)


def _na_attention(projx, projc, na_bias):
    b, s, _ = projx.shape
    lc = projc.shape[1]
    qr, kwr, pat_ids, ks_rows, valid_r, rel_r, valid_c, rel_c = _na_plan(s)
    qb, kw = qr * GRID_W, kwr * GRID_W
    mb = _na_bias_masks(na_bias, valid_r, rel_r, valid_c, rel_c)
    grid_spec = pltpu.PrefetchScalarGridSpec(
        num_scalar_prefetch=2,
        grid=(b, s // qb),
        in_specs=[pl.BlockSpec((1, qb, 256), lambda bi, g, pat, ks: (bi, g, COL_NQ // 256)),
                  pl.BlockSpec((1, s, 256), lambda bi, g, pat, ks: (bi, 0, COL_NK // 256)),
                  pl.BlockSpec((1, s, 256), lambda bi, g, pat, ks: (bi, 0, COL_NV // 256)),
                  pl.BlockSpec((1, lc, 256), lambda bi, g, pat, ks: (bi, 0, COL_NK // 256)),
                  pl.BlockSpec((1, lc, 256), lambda bi, g, pat, ks: (bi, 0, COL_NV // 256)),
                  pl.BlockSpec((1, N_HEADS, qb, kw), lambda bi, g, pat, ks: (pat[g], 0, 0, 0))],
        out_specs=pl.BlockSpec((1, qb, BRANCH_W), lambda bi, g, pat, ks: (bi, g, 0)),
    )
    return pl.pallas_call(
        functools.partial(_na_kernel, kw=kw),
        grid_spec=grid_spec,
        out_shape=jax.ShapeDtypeStruct((b, s, BRANCH_W), BF16),
        compiler_params=_cparams(("parallel", "arbitrary"), 48 * 1024 * 1024),
        name="na_attention",
    )(jnp.asarray(pat_ids), jnp.asarray(ks_rows), projx, projx, projx, projc, projc, mb)


def _ret_kernel(lgs_ref, lgl_ref, gnw_ref, gseg_ref,
                qx, kx, vx, gfx, gbx, qc, kc, vc, gfc, gbc,
                yx_o, yc_o, of_s, ob_s, st_s, dec_s, qk_s, *, need_ctx):
    c = RET_CHUNK
    lc = qc.shape[1]
    sx = qx.shape[1]
    n_col = lax.broadcasted_iota(jnp.int32, (c, c), 0).astype(F32)
    m_row = lax.broadcasted_iota(jnp.int32, (c, c), 1).astype(F32)
    diff = n_col - m_row
    for h in range(N_HEADS):
        dec_s[h] = jnp.where(diff >= 0, jnp.exp(lgs_ref[h] * jnp.maximum(diff, 0.0)), 0.0)
        dec_s[N_HEADS + h] = jnp.where(diff <= 0, jnp.exp(lgs_ref[N_HEADS + h] * jnp.maximum(-diff, 0.0)), 0.0)
    pos = lax.broadcasted_iota(jnp.int32, (c, BRANCH_W), 0).astype(F32)
    lgf, lgb = lgl_ref[0], lgl_ref[1]
    qk_s[0] = jnp.exp(lgf * (pos + 1.0))
    qk_s[1] = jnp.exp(lgf * (c - 1.0 - pos))
    qk_s[2] = jnp.exp(lgb * (c - pos))
    qk_s[3] = jnp.exp(lgb * pos)
    cd_f = jnp.exp(lgf * float(c))
    cd_b = jnp.exp(lgb * float(c))
    st_s[...] = jnp.zeros_like(st_s)
    rowb = lax.broadcasted_iota(jnp.int32, (BRANCH_W, BRANCH_W), 0) // HEAD_W
    colb = lax.broadcasted_iota(jnp.int32, (BRANCH_W, BRANCH_W), 1) // HEAD_W
    bd_mask = rowb == colb

    def chunk_step(q, k, v, d, dec_off, cd):
        kk = k * jnp.asarray(HEAD_W ** -0.5, BF16)
        state = st_s[d]
        o = _dot(q, state.astype(BF16)) * qk_s[2 * d]
        for h in range(N_HEADS):
            km = jnp.where(_head_mask(kk.shape, h), kk, jnp.zeros_like(kk))
            inner = _dot_nt(q, km) * dec_s[dec_off + h]
            r = _dot(inner.astype(BF16), v)
            o = o + jnp.where(_head_mask(r.shape, h), r, 0.0)
        kd = (kk.astype(F32) * qk_s[2 * d + 1]).astype(BF16)
        upd = lax.dot_general(kd, v, (((0,), (0,)), ((), ())), preferred_element_type=F32)
        st_s[d] = state * cd + jnp.where(bd_mask, upd, 0.0)
        return o

    def scan(q_ref, k_ref, v_ref, base, n):
        def body(i, _):
            fo = pl.multiple_of(i * c, c)
            bo = pl.multiple_of((n - 1 - i) * c, c)
            of_s[pl.ds(base + fo, c), :] = chunk_step(
                q_ref[0, pl.ds(fo, c), :], k_ref[0, pl.ds(fo, c), :], v_ref[0, pl.ds(fo, c), :], 0, 0, cd_f)
            ob_s[pl.ds(base + bo, c), :] = chunk_step(
                q_ref[0, pl.ds(bo, c), :], k_ref[0, pl.ds(bo, c), :], v_ref[0, pl.ds(bo, c), :], 1, N_HEADS, cd_b)
            return 0
        lax.fori_loop(0, n, body, 0)

    scan(qc, kc, vc, 0, lc // c)
    scan(qx, kx, vx, lc, sx // c)

    gseg = gseg_ref[...]
    gnw = gnw_ref[...]

    def gnorm(o):
        mu = _dot_sel(o, gseg) * (1.0 / HEAD_W)
        dlt = o - mu
        var = _dot_sel(dlt * dlt, gseg) * (1.0 / HEAD_W)
        return dlt * lax.rsqrt(var + EPS) * gnw

    def combine(gf_ref, gb_ref, y_ref, base, n):
        def body(i, _):
            ro = pl.multiple_of(i * c, c)
            y = (gnorm(of_s[pl.ds(base + ro, c), :]) * _silu(gf_ref[0, pl.ds(ro, c), :].astype(F32))
                 + gnorm(ob_s[pl.ds(base + ro, c), :]) * _silu(gb_ref[0, pl.ds(ro, c), :].astype(F32)))
            y_ref[0, pl.ds(ro, c), :] = y.astype(y_ref.dtype)
            return 0
        lax.fori_loop(0, n, body, 0)

    combine(gfx, gbx, yx_o, lc, sx // c)
    if need_ctx:
        combine(gfc, gbc, yc_o, 0, lc // c)
    else:
        yc_o[...] = jnp.zeros_like(yc_o)


def _retention(projx, projc, log_g, gn_w, gseg, need_ctx):
    b, s, _ = projx.shape
    lc = projc.shape[1]
    lgs = log_g.reshape(2 * N_HEADS)
    lgl = jnp.repeat(log_g, HEAD_W, axis=1).reshape(2, 1, BRANCH_W)
    xs = [_proj_cols(projx, COL_RET + 256 * i, 256, s) for i in range(5)]
    cs = [_proj_cols(projc, COL_RET + 256 * i, 256, lc) for i in range(5)]
    c = RET_CHUNK
    yx, yc = pl.pallas_call(
        functools.partial(_ret_kernel, need_ctx=need_ctx),
        grid=(b,),
        in_specs=[pl.BlockSpec(memory_space=pltpu.SMEM),
                  pl.BlockSpec((2, 1, BRANCH_W), lambda bi: (0, 0, 0)),
                  pl.BlockSpec((1, BRANCH_W), lambda bi: (0, 0)),
                  pl.BlockSpec((BRANCH_W, BRANCH_W), lambda bi: (0, 0))] + xs + cs,
        out_specs=[pl.BlockSpec((1, s, BRANCH_W), lambda bi: (bi, 0, 0)),
                   pl.BlockSpec((1, lc, BRANCH_W), lambda bi: (bi, 0, 0))],
        out_shape=[jax.ShapeDtypeStruct((b, s, BRANCH_W), BF16),
                   jax.ShapeDtypeStruct((b, lc, BRANCH_W), BF16)],
        scratch_shapes=[pltpu.VMEM((lc + s, BRANCH_W), F32),
                        pltpu.VMEM((lc + s, BRANCH_W), F32),
                        pltpu.VMEM((2, BRANCH_W, BRANCH_W), F32),
                        pltpu.VMEM((2 * N_HEADS, c, c), F32),
                        pltpu.VMEM((4, c, BRANCH_W), F32)],
        compiler_params=_cparams(("parallel",), 48 * 1024 * 1024),
        name="retention",
    )(lgs, lgl, gn_w.reshape(1, BRANCH_W), gseg, *([projx] * 5), *([projc] * 5))
    return yx, yc


def _merge_kernel(oa, ob, oc, od, g0, g1, g2, g3, x_ref, gate_ref, sc_ref, sh_ref, nw_ref, wb_ref, wo_ref,
                  xn_o, h2_o):
    acc = None
    for i, (o, g) in enumerate(((oa, g0), (ob, g1), (oc, g2), (od, g3))):
        t = jax.nn.sigmoid(g[0].astype(F32)) * _dot(o[0], wb_ref[i])
        acc = t if acc is None else acc + t
    y = _dot(acc.astype(BF16), wo_ref[...])
    xn = x_ref[0] + gate_ref[0] * y
    xn_o[0] = xn
    h2_o[0] = _rms(xn) * nw_ref[...] * (1.0 + sc_ref[0]) + sh_ref[0]


def _merge(outs, proj, x, gate, sc2, sh2, n2w, wb, wo):
    b, l, d = x.shape
    tm = min(512, l)
    tok = lambda w: pl.BlockSpec((1, tm, w), lambda bi, i: (bi, i, 0))
    vec = pl.BlockSpec((1, 1, d), lambda bi, i: (bi, 0, 0))
    gates = [pl.BlockSpec((1, tm, d), lambda bi, i, k=k: (bi, i, COL_GATES // d + k)) for k in range(N_BRANCH)]
    return pl.pallas_call(
        _merge_kernel,
        grid=(b, l // tm),
        in_specs=[tok(BRANCH_W)] * 4 + gates + [tok(d), vec, vec, vec,
                                                pl.BlockSpec((1, d), lambda bi, i: (0, 0)),
                                                pl.BlockSpec(wb.shape, lambda bi, i: (0, 0, 0)),
                                                pl.BlockSpec(wo.shape, lambda bi, i: (0, 0))],
        out_specs=[tok(d), tok(d)],
        out_shape=[jax.ShapeDtypeStruct((b, l, d), F32), jax.ShapeDtypeStruct((b, l, d), F32)],
        compiler_params=_cparams(("parallel", "parallel"), 48 * 1024 * 1024),
        name="merge",
    )(*outs, proj, proj, proj, proj, x, gate, sc2, sh2, n2w, wb, wo)


def _topk_rows(s, k):
    r, t = s.shape
    iota = lax.broadcasted_iota(jnp.int32, s.shape, 0).astype(F32)
    out_row = lax.broadcasted_iota(jnp.int32, (k, t), 0)
    vals = jnp.zeros((k, t), F32)
    idxs = jnp.zeros((k, t), F32)
    for i in range(k):
        m = jnp.max(s, axis=0, keepdims=True)
        idx = jnp.min(jnp.where(s == m, iota, float(r)), axis=0, keepdims=True)
        vals = jnp.where(out_row == i, m, vals)
        idxs = jnp.where(out_row == i, idx, idxs)
        s = jnp.where(iota == idx, -jnp.inf, s)
    return vals, idxs.astype(jnp.int32)


def _select_rows(table, sel, k):
    out = jnp.zeros_like(table)
    for r in range(k):
        out = jnp.where(sel == r, table[r:r + 1, :], out)
    return out


def _split_bf16(x):
    hi = x.astype(BF16)
    return hi, (x - hi.astype(F32)).astype(BF16)


_CAND_MID = tuple(PEER_TOPK // (i + 1) for i in range(1, 8))
_CAND_ROWS = PEER_TOPK + 8 * len(_CAND_MID) + 8


def _peer_candidates(s1, s2):
    t = s1.shape[1]
    sub = lax.broadcasted_iota(jnp.int32, (8, t), 0)
    blocks = [s1[0:1] + s2]
    for i, nj in enumerate(_CAND_MID, start=1):
        blocks.append(jnp.where(sub < nj, s1[i:i + 1] + s2[0:8], -jnp.inf))
    blocks.append(s1[8:16] + s2[0:1])
    return jnp.concatenate(blocks, axis=0)


def _peer_candidate_ranks(pos):
    mid = pos - PEER_TOPK
    i = jnp.where(pos < PEER_TOPK, 0, jnp.where(pos < _CAND_ROWS - 8, (mid >> 3) + 1, pos - (_CAND_ROWS - 16)))
    j = jnp.where(pos < PEER_TOPK, pos, jnp.where(pos < _CAND_ROWS - 8, mid & 7, 0))
    return i, j


def _peer_route_kernel(h_ref, wqh_ref, wql_ref, kh_ref, kl_ref, a_o, b_o, g_o, q_s, a_s, b_s, g_s):
    k = PEER_TOPK
    assert k == 16
    hh, hl = _split_bf16(h_ref[...])
    q_s[...] = _dot(hh, wqh_ref[...]) + (_dot(hl, wqh_ref[...]) + _dot(hh, wql_ref[...]))

    def head(h):
        lo = pl.multiple_of(h * PEER_DK, PEER_DK)
        qh, ql = _split_bf16(q_s[:, pl.ds(lo, PEER_DK)])
        kh, kl = kh_ref[h], kl_ref[h]
        s = _dot_nt(kh, qh) + (_dot_nt(kh, ql) + _dot_nt(kl, qh))
        s1, i1 = _topk_rows(s[:PEER_N_KEYS], k)
        s2, i2 = _topk_rows(s[PEER_N_KEYS:], k)
        ts, tpos = _topk_rows(_peer_candidates(s1, s2), k)
        e = jnp.exp(ts - ts[0:1, :])
        gate = e / jnp.sum(e, axis=0, keepdims=True)
        ri, rj = _peer_candidate_ranks(tpos)
        ro = pl.multiple_of(h * k, k)
        a_s[pl.ds(ro, k), :] = _select_rows(i1, ri, k)
        b_s[pl.ds(ro, k), :] = _select_rows(i2, rj, k)
        g_s[pl.ds(ro, k), :] = gate

    def head_pair(i, _):
        head(2 * i)
        head(2 * i + 1)
        return 0

    lax.fori_loop(0, PEER_HEADS // 2, head_pair, 0)
    a_o[...] = a_s[...].T
    b_o[...] = b_s[...].T
    g_o[...] = g_s[...].T


def _peer_route(h2, wq_hl, keys_hl):
    n, d = h2.shape
    t = 256
    wq_hi, wq_lo = wq_hl
    k_hi, k_lo = keys_hl
    return pl.pallas_call(
        _peer_route_kernel,
        grid=(n // t,),
        in_specs=[pl.BlockSpec((t, d), lambda i: (i, 0)),
                  pl.BlockSpec(wq_hi.shape, lambda i: (0, 0)),
                  pl.BlockSpec(wq_lo.shape, lambda i: (0, 0)),
                  pl.BlockSpec(k_hi.shape, lambda i: (0, 0, 0)),
                  pl.BlockSpec(k_lo.shape, lambda i: (0, 0, 0))],
        out_specs=[pl.BlockSpec((t, PEER_SLOTS), lambda i: (i, 0))] * 3,
        out_shape=[jax.ShapeDtypeStruct((n, PEER_SLOTS), jnp.int32),
                   jax.ShapeDtypeStruct((n, PEER_SLOTS), jnp.int32),
                   jax.ShapeDtypeStruct((n, PEER_SLOTS), F32)],
        scratch_shapes=[pltpu.VMEM((t, PEER_HEADS * PEER_DK), F32),
                        pltpu.VMEM((PEER_SLOTS, t), jnp.int32),
                        pltpu.VMEM((PEER_SLOTS, t), jnp.int32),
                        pltpu.VMEM((PEER_SLOTS, t), F32)],
        compiler_params=_cparams(("parallel",), 48 * 1024 * 1024),
        name="peer_route",
    )(h2, wq_hi, wq_lo, k_hi, k_lo)


_HI16 = -65536


def _bf16_bits(w):
    return lax.bitcast_convert_type(w, jnp.int32) & _HI16


def _peer_ffn_kernel(h_ref, a_ref, b_ref, g_ref, x_ref, gate_ref, u_ref, v_ref, o_ref, hb_s, w_s, *, ec, unroll):
    e = pl.program_id(1)
    t = h_ref.shape[0]
    half = t // 2
    nk = PEER_N_KEYS

    @pl.when(e == 0)
    def _():
        hb_s[...] = h_ref[...].astype(BF16)
        o_ref[...] = jnp.zeros_like(o_ref)
        jio = lax.broadcasted_iota(jnp.int32, (nk, PEER_SLOTS), 0)

        def tile(tt):
            arow = jnp.broadcast_to(a_ref[pl.ds(tt, 1), :], (nk, PEER_SLOTS))
            brow = jnp.broadcast_to(b_ref[pl.ds(tt, 1), :], (nk, PEER_SLOTS))
            grow = jnp.broadcast_to(g_ref[pl.ds(tt, 1), :], (nk, PEER_SLOTS))
            cm = jnp.where(jio == arow, grow, 0.0).astype(BF16)
            bm_t = jnp.where(jio == brow, 1.0, 0.0).T.astype(BF16)
            return _dot(cm, bm_t)

        def build(tb, _):
            for u in range(unroll):
                tt = tb * unroll + u
                word = _bf16_bits(tile(tt + half)) | lax.shift_right_logical(_bf16_bits(tile(tt)), 16)
                w_s[pl.ds(pl.multiple_of(tt * PEER_W_PITCH, 8), nk), :] = word
            return 0

        lax.fori_loop(0, half // unroll, build, 0)

    hid = _dot(hb_s[...], u_ref[0])
    j0 = e * (ec // nk)
    words = jnp.concatenate([w_s[pl.ds(j0 + j, half, stride=PEER_W_PITCH), :] for j in range(ec // nk)], axis=1)
    w_lo = lax.bitcast_convert_type(lax.shift_left(words, 16), F32)
    w_hi = lax.bitcast_convert_type(words & _HI16, F32)
    wc = jnp.concatenate([w_lo, w_hi], axis=0)
    act = 0.5 * hid * (1.0 + lax.erf(hid * SQRT_HALF))
    o_ref[...] += _dot((wc * act).astype(BF16), v_ref[...])

    @pl.when(e == pl.num_programs(1) - 1)
    def _():
        o_ref[...] = x_ref[...] + gate_ref[0] * o_ref[...]


def _peer_ffn(h2, a, b_idx, g, x, gate, u_blk, v, l):
    n, d = h2.shape
    ne = v.shape[0]
    neb, _, ec = u_blk.shape
    t = min(512, l)
    unroll = 16
    assert l % t == 0 and ne == PEER_N_KEYS * PEER_N_KEYS and neb * ec == ne and (t // 2) % unroll == 0
    tok = lambda w: pl.BlockSpec((t, w), lambda i, e: (i, 0))
    return pl.pallas_call(
        functools.partial(_peer_ffn_kernel, ec=ec, unroll=unroll),
        grid=(n // t, neb),
        in_specs=[tok(d), tok(PEER_SLOTS), tok(PEER_SLOTS), tok(PEER_SLOTS), tok(d),
                  pl.BlockSpec((1, 1, d), lambda i, e: ((i * t) // l, 0, 0)),
                  pl.BlockSpec((1, d, ec), lambda i, e: (e, 0, 0)),
                  pl.BlockSpec((ec, d), lambda i, e: (e, 0))],
        out_specs=tok(d),
        out_shape=jax.ShapeDtypeStruct((n, d), F32),
        scratch_shapes=[pltpu.VMEM((t, d), BF16),
                        pltpu.VMEM((t // 2 * PEER_W_PITCH, PEER_N_KEYS), jnp.int32)],
        compiler_params=_cparams(("parallel", "arbitrary"), VMEM_LIMIT_V7X),
        name="peer_ffn",
    )(h2, a, b_idx, g, x, gate, u_blk, v)


def _final_norm_kernel(x_ref, w_ref, o_ref):
    o_ref[...] = _rms(x_ref[...]) * w_ref[...]


def _final_norm(x, w):
    n, d = x.shape
    tm = min(1024, n)
    return pl.pallas_call(
        _final_norm_kernel,
        grid=(n // tm,),
        in_specs=[pl.BlockSpec((tm, d), lambda i: (i, 0)), pl.BlockSpec((1, d), lambda i: (0, 0))],
        out_specs=pl.BlockSpec((tm, d), lambda i: (i, 0)),
        out_shape=jax.ShapeDtypeStruct((n, d), F32),
        compiler_params=_cparams(("parallel",), 40 * 1024 * 1024),
        name="final_norm",
    )(x, w.reshape(1, d))


def _layout_w_in(w):
    parts = jnp.split(w, IN_OFFSETS, axis=1)
    z = lambda n: jnp.zeros((w.shape[0], n), w.dtype)
    kpe_blk = jnp.concatenate([z(MLA_NOPE), parts[2], z(MLA_HEAD_PAD - MLA_NOPE - MLA_ROPE)], axis=1)
    return jnp.concatenate([parts[0], parts[1], kpe_blk] + list(parts[3:]), axis=1).astype(BF16)


def _layout_mla(w_uq, w_ukv):
    qh = w_uq.reshape(MLA_Q_LORA, N_HEADS, MLA_NOPE + MLA_ROPE)
    qh = jnp.pad(qh, ((0, 0), (0, 0), (0, MLA_HEAD_PAD - MLA_NOPE - MLA_ROPE)))
    kv = w_ukv.reshape(MLA_KV_LORA, N_HEADS, MLA_NOPE + MLA_V)
    kh = jnp.pad(kv[:, :, :MLA_NOPE], ((0, 0), (0, 0), (0, MLA_HEAD_PAD - MLA_NOPE)))
    vh = kv[:, :, MLA_NOPE:]
    return (qh.reshape(MLA_Q_LORA, -1).astype(BF16), kh.reshape(MLA_KV_LORA, -1).astype(BF16),
            vh.reshape(MLA_KV_LORA, -1).T.astype(BF16))


def _static_mats():
    gseg = np.kron(np.eye(N_HEADS), np.ones((HEAD_W, HEAD_W))).astype(np.float32)
    pm = np.zeros((MLA_HEAD_PAD, MLA_HEAD_PAD), np.float32)
    for dd in range(MLA_ROPE):
        blk, j = dd // 16, dd % 16
        pm[MLA_NOPE + blk * 16 + (j + 8) % 16, MLA_NOPE + dd] = 1.0
    pg = np.zeros((BRANCH_W, BRANCH_W), np.float32)
    for i in range(BRANCH_W):
        off, dd = (i // HEAD_W) * HEAD_W, i % HEAD_W
        blk, j = dd // 32, dd % 32
        pg[off + blk * 32 + (j + 16) % 32, i] = 1.0
    ex = np.zeros((GQA_KV_HEADS * HEAD_W, BRANCH_W), np.float32)
    for i in range(BRANCH_W):
        ex[((i // HEAD_W) // (N_HEADS // GQA_KV_HEADS)) * HEAD_W + i % HEAD_W, i] = 1.0
    return tuple(jnp.asarray(m, dtype=BF16) for m in (gseg, pm, pg, ex, ex.T))


def _rope_half_tables(pos, hf):
    freqs = ROPE_THETA ** (-jnp.arange(hf, dtype=F32) / hf)
    ang = pos[:, None] * freqs[None, :]
    c, s = jnp.cos(ang), jnp.sin(ang)
    return jnp.concatenate([c, c], axis=1), jnp.concatenate([-s, s], axis=1)


def _axial_tables(row, col, dims):
    cr, sr = _rope_half_tables(row, dims // 4)
    cc, sc = _rope_half_tables(col, dims // 4)
    return jnp.concatenate([cr, cc], axis=1), jnp.concatenate([sr, sc], axis=1)


def _rope_tables(s):
    t = jnp.arange(s)
    row, col = (t // GRID_W).astype(F32), (t % GRID_W).astype(F32)
    c32, s32 = _axial_tables(row, col, MLA_ROPE)
    pad = MLA_HEAD_PAD - MLA_NOPE - MLA_ROPE
    cm = jnp.concatenate([jnp.ones((s, MLA_NOPE), F32), c32, jnp.ones((s, pad), F32)], axis=1)
    sm = jnp.concatenate([jnp.zeros((s, MLA_NOPE), F32), s32, jnp.zeros((s, pad), F32)], axis=1)
    c64, s64 = _axial_tables(row, col, HEAD_W)
    return cm, sm, jnp.tile(c64, (1, N_HEADS)), jnp.tile(s64, (1, N_HEADS))


def _split_f32(w):
    hi = w.astype(BF16)
    return hi, (w - hi.astype(F32)).astype(BF16)


def _layout_peer_keys(keys):
    h, _, nk, dh = keys.shape
    z = jnp.zeros((h, nk, dh), keys.dtype)
    top = jnp.concatenate([keys[:, 0], z], axis=2)
    bot = jnp.concatenate([z, keys[:, 1]], axis=2)
    return jnp.concatenate([top, bot], axis=1)


def kernel(x, c, ctx, c_ctx, mod_w, mod_b, norm1_w, norm2_w, w_in, mla_q_norm, mla_w_uq, mla_kv_norm, mla_w_ukv, gqa_q_norm, gqa_k_norm, na_bias, ret_decay_logit, ret_gn_w, w_branch, w_out, peer_w_q, peer_keys, peer_u, peer_v, final_norm_w):
    b, s, d = x.shape
    lc = ctx.shape[1]
    depth = mod_w.shape[0]
    assert d == D_MODEL and s % (GRID_W * NA_Q_ROWS) == 0 and s % 256 == 0 and lc % 256 == 0

    rows = -(-(b + 1) // 8) * 8
    cc = jnp.zeros((rows, d), F32).at[:b].set(c).at[b].set(c_ctx)
    mod = _modulation(cc, mod_w, mod_b)

    gseg, pm, pg, ex, ex_t = _static_mats()
    tables = _rope_tables(s)

    for l in range(depth):
        need_ctx = l < depth - 1
        mx = mod[l, :b].reshape(b, 1, 6, d)
        mc = jnp.broadcast_to(mod[l, b].reshape(1, 1, 6, d), (b, 1, 6, d))
        sh1x, sc1x, g1x, sh2x, sc2x, g2x = (mx[:, :, i] for i in range(6))
        sh1c, sc1c, g1c, sh2c, sc2c, g2c = (mc[:, :, i] for i in range(6))

        w_in_l = _layout_w_in(w_in[l])
        wuq, wk, wv = _layout_mla(mla_w_uq[l], mla_w_ukv[l])
        consts = (mla_q_norm[l].reshape(1, -1), wuq, mla_kv_norm[l].reshape(1, -1), wk, wv,
                  jnp.tile(gqa_q_norm[l], N_HEADS).reshape(1, -1),
                  jnp.tile(gqa_k_norm[l], GQA_KV_HEADS).reshape(1, -1), gseg, pm, pg, ex, ex_t)
        n1w = norm1_w[l].reshape(1, d)
        n2w = norm2_w[l].reshape(1, d)
        wb = w_branch[l].astype(BF16)
        wo = w_out[l].astype(BF16)
        keys_hl = _split_f32(_layout_peer_keys(peer_keys[l]))
        wq_hl = _split_f32(peer_w_q[l])
        ne = peer_u.shape[1]
        u_blk = jnp.swapaxes(peer_u[l].astype(BF16).reshape(ne // PEER_EXPERT_CHUNK, PEER_EXPERT_CHUNK, d), 1, 2)
        v_b = peer_v[l].astype(BF16)
        log_g = jax.nn.log_sigmoid(ret_decay_logit[l].astype(F32))

        projx = _inproj(x, n1w, sc1x, sh1x, w_in_l)
        projc = _inproj(ctx, n1w, sc1c, sh1c, w_in_l)
        qmx, kmx, vmx, qgx, kgx, vgx = _prep(projx, consts, tables)
        qmc, kmc, vmc, qgc, kgc, vgc = _prep(projc, consts, None)

        oa = _attention((qmx, 0), [((kmc, 0), vmc), ((kmx, 0), vmx)], MLA_HEAD_PAD, "attn_mla")
        ob = _attention((qgx, 0), [((kgc, 0), vgc), ((kgx, 0), vgx)], HEAD_W, "attn_gqa")
        oc = _na_attention(projx, projc, na_bias[l])
        od, od_c = _retention(projx, projc, log_g, ret_gn_w[l], gseg, need_ctx)

        x, h2x = _merge((oa, ob, oc, od), projx, x, g1x, sc2x, sh2x, n2w, wb, wo)
        ax, bx, gx = _peer_route(h2x.reshape(b * s, d), wq_hl, keys_hl)
        x = _peer_ffn(h2x.reshape(b * s, d), ax, bx, gx, x.reshape(b * s, d), g2x, u_blk, v_b, s).reshape(b, s, d)

        if need_ctx:
            ca = _attention((qmc, 0), [((kmc, 0), vmc)], MLA_HEAD_PAD, "attn_mla_ctx")
            cb = _attention((qgc, 0), [((kgc, 0), vgc)], HEAD_W, "attn_gqa_ctx")
            nv_t = jnp.swapaxes(projc[:, :, COL_NV:COL_NV + BRANCH_W], 1, 2)
            ccx = _attention((projc, COL_NQ), [((projc, COL_NK), nv_t)], HEAD_W, "attn_na_ctx",
                             qscale=HEAD_W ** -0.5, log2_scores=False)
            ctx, h2c = _merge((ca, cb, ccx, od_c), projc, ctx, g1c, sc2c, sh2c, n2w, wb, wo)
            ac, bc, gc = _peer_route(h2c.reshape(b * lc, d), wq_hl, keys_hl)
            ctx = _peer_ffn(h2c.reshape(b * lc, d), ac, bc, gc, ctx.reshape(b * lc, d), g2c, u_blk, v_b,
                            lc).reshape(b, lc, d)

    return _final_norm(x.reshape(b * s, d), final_norm_w).reshape(b, s, d)
```

```python
import functools

import numpy as np
import jax
import jax.numpy as jnp
from jax import lax
from jax.experimental import pallas as pl
from jax.experimental.pallas import tpu as pltpu

F32 = jnp.float32
BF16 = jnp.bfloat16
HIGHEST = lax.Precision.HIGHEST

D_MODEL = 1024
GRID_W = 64
ROPE_THETA = 10000.0
EPS = 1e-6
N_HEADS = 4
HEAD_W = 64
BRANCH_W = N_HEADS * HEAD_W
MLA_NOPE, MLA_ROPE, MLA_V = 64, 32, 64
MLA_Q_LORA, MLA_KV_LORA = 256, 128
MLA_SCALE = (MLA_NOPE + MLA_ROPE) ** -0.5
MLA_HEAD_PAD = 128
GQA_KV_HEADS = 2
NA_WIN_R, NA_WIN_C = 8, 16
NA_Q_ROWS = 4
ATTN_KEY_CHUNK = 1024
RET_CHUNK = 128
N_BRANCH = 4
PEER_HEADS, PEER_N_KEYS, PEER_TOPK, PEER_DK = 8, 128, 16, 128
PEER_SLOTS = PEER_HEADS * PEER_TOPK
PEER_W_PITCH = PEER_N_KEYS + 8
PEER_EXPERT_CHUNK = 1024
SQRT_HALF = 0.7071067811865476
LOG2E = 1.4426950408889634
NEG_BIG = -1e30

IN_SIZES = (256, 128, 32, 256, 128, 128, 256, 256, 256, 256, 256, 256, 256, 256, 4096)
IN_OFFSETS = tuple(int(v) for v in np.cumsum(IN_SIZES)[:-1])
PROJ_COLS = 7168
COL_NQ, COL_NK, COL_NV = 1024, 1280, 1536
COL_RET = 1792
COL_GATES = 3072

VMEM_LIMIT_V7X = 56 * 1024 * 1024


def _cparams(sem, vmem=None):
    return pltpu.CompilerParams(dimension_semantics=sem, vmem_limit_bytes=vmem)


def _dot(a, b):
    return jnp.dot(a, b, preferred_element_type=F32)


def _dot_hi(a, b):
    return jnp.dot(a, b, preferred_element_type=F32, precision=HIGHEST)


def _dot_nt(a, b):
    return lax.dot_general(a, b, (((1,), (1,)), ((), ())), preferred_element_type=F32)


def _dot_sel(x, sel):
    hi = x.astype(BF16)
    r1 = x - hi.astype(F32)
    mid = r1.astype(BF16)
    lo = (r1 - mid.astype(F32)).astype(BF16)
    return _dot(hi, sel) + (_dot(mid, sel) + _dot(lo, sel))


def _rms(x):
    return x * lax.rsqrt(jnp.mean(x * x, axis=-1, keepdims=True) + EPS)


def _silu(x):
    return x * jax.nn.sigmoid(x)


def _head_mask(shape, h, width=HEAD_W):
    lane = lax.broadcasted_iota(jnp.int32, shape, len(shape) - 1)
    lo = h * width
    return (lane >= lo) & (lane < lo + width)


def _mod_kernel(c_ref, w_ref, b_ref, o_ref):
    o_ref[0] = _dot_hi(_silu(c_ref[...]), w_ref[0]) + b_ref[0]


def _modulation(cc, mod_w, mod_b):
    depth, d, n = mod_w.shape
    rows = cc.shape[0]
    tn = 1536
    return pl.pallas_call(
        _mod_kernel,
        grid=(depth, n // tn),
        in_specs=[pl.BlockSpec((rows, d), lambda l, j: (0, 0)),
                  pl.BlockSpec((1, d, tn), lambda l, j: (l, 0, j)),
                  pl.BlockSpec((1, 1, tn), lambda l, j: (l, 0, j))],
        out_specs=pl.BlockSpec((1, rows, tn), lambda l, j: (l, 0, j)),
        out_shape=jax.ShapeDtypeStruct((depth, rows, n), F32),
        compiler_params=_cparams(("parallel", "parallel"), 40 * 1024 * 1024),
        name="modulation",
    )(cc, mod_w, mod_b.reshape(depth, 1, n))


def _inproj_kernel(x_ref, nw_ref, sc_ref, sh_ref, w_ref, o_ref, h_scr):
    @pl.when(pl.program_id(2) == 0)
    def _():
        h = _rms(x_ref[0]) * nw_ref[...] * (1.0 + sc_ref[0]) + sh_ref[0]
        h_scr[...] = h.astype(BF16)

    o_ref[0] = _dot(h_scr[...], w_ref[...]).astype(o_ref.dtype)


def _inproj(x, nw, sc, sh, w):
    b, l, d = x.shape
    n = w.shape[1]
    tm = min(512, l)
    tn = 1792
    return pl.pallas_call(
        _inproj_kernel,
        grid=(b, l // tm, n // tn),
        in_specs=[pl.BlockSpec((1, tm, d), lambda bi, i, j: (bi, i, 0)),
                  pl.BlockSpec((1, d), lambda bi, i, j: (0, 0)),
                  pl.BlockSpec((1, 1, d), lambda bi, i, j: (bi, 0, 0)),
                  pl.BlockSpec((1, 1, d), lambda bi, i, j: (bi, 0, 0)),
                  pl.BlockSpec((d, tn), lambda bi, i, j: (0, j))],
        out_specs=pl.BlockSpec((1, tm, tn), lambda bi, i, j: (bi, i, j)),
        out_shape=jax.ShapeDtypeStruct((b, l, n), BF16),
        scratch_shapes=[pltpu.VMEM((tm, d), BF16)],
        compiler_params=_cparams(("parallel", "parallel", "arbitrary"), 40 * 1024 * 1024),
        name="inproj",
    )(x, nw, sc, sh, w)


def _prep_kernel(*refs, use_rope):
    (p_ref, qn_ref, wuq_ref, kvn_ref, wk_ref, wv_ref, gqn_ref, gkn_ref, gseg_ref, pm_ref, pg_ref,
     e_ref, et_ref) = refs[:13]
    if use_rope:
        cm_ref, sm_ref, cg_ref, sg_ref = refs[13:17]
        outs = refs[17:]
    else:
        outs = refs[13:]
    qm_o, km_o, vm_o, qg_o, kg_o, vg_o = outs

    pb = p_ref[0]
    cq = pb[:, 0:256].astype(F32)
    ckv = pb[:, 256:384].astype(F32)
    kpe = pb[:, 384:512].astype(F32)
    gq = pb[:, 512:768].astype(F32)
    gk = pb[:, 768:896].astype(F32)
    gv = pb[:, 896:1024]

    cqn = (_rms(cq) * qn_ref[...]).astype(BF16)
    qa = _dot(cqn, wuq_ref[...])
    ckn = (_rms(ckv) * kvn_ref[...]).astype(BF16)
    kn = _dot(ckn, wk_ref[...])
    vm_t = _dot_nt(wv_ref[...], ckn)
    if use_rope:
        cm, sm = cm_ref[...], sm_ref[...]
        pm = pm_ref[...]

        def rope_m(t):
            return t * cm + _dot_sel(t, pm) * sm

        qa = jnp.concatenate([rope_m(qa[:, h * 128:(h + 1) * 128]) for h in range(N_HEADS)], axis=1)
        kpe = rope_m(kpe)
    km = kn + jnp.concatenate([kpe] * N_HEADS, axis=1)
    qm_o[0] = (qa * (MLA_SCALE * LOG2E)).astype(BF16)
    km_o[0] = km.astype(BF16)
    vm_o[0] = vm_t.astype(BF16)

    gseg = gseg_ref[...]
    gqn = gq * lax.rsqrt(_dot_sel(gq * gq, gseg) * (1.0 / HEAD_W) + EPS) * gqn_ref[...]
    gkn = gk * lax.rsqrt(_dot_sel(gk * gk, gseg[:128, :128]) * (1.0 / HEAD_W) + EPS) * gkn_ref[...]
    if use_rope:
        cg, sg = cg_ref[...], sg_ref[...]
        pg = pg_ref[...]
        gqn = gqn * cg + _dot_sel(gqn, pg) * sg
        gkn = gkn * cg[:, :128] + _dot_sel(gkn, pg[:128, :128]) * sg[:, :128]
    qg_o[0] = (gqn * (HEAD_W ** -0.5 * LOG2E)).astype(BF16)
    e = e_ref[...]
    kg_o[0] = _dot(gkn.astype(BF16), e).astype(BF16)
    vg_o[0] = _dot_nt(et_ref[...], gv).astype(BF16)


def _prep(proj, consts, tables):
    b, l, _ = proj.shape
    tm = min(512, l)
    use_rope = tables is not None
    full = lambda a: pl.BlockSpec(a.shape, lambda bi, i: (0,) * a.ndim)
    in_specs = [pl.BlockSpec((1, tm, 1024), lambda bi, i: (bi, i, 0))] + [full(a) for a in consts]
    args = [proj] + list(consts)
    if use_rope:
        in_specs += [pl.BlockSpec((tm, t.shape[1]), lambda bi, i: (i, 0)) for t in tables]
        args += list(tables)
    tok = lambda w: (pl.BlockSpec((1, tm, w), lambda bi, i: (bi, i, 0)), jax.ShapeDtypeStruct((b, l, w), BF16))
    tr = (pl.BlockSpec((1, BRANCH_W, tm), lambda bi, i: (bi, 0, i)), jax.ShapeDtypeStruct((b, BRANCH_W, l), BF16))
    outs = (tok(512), tok(512), tr, tok(256), tok(256), tr)
    return pl.pallas_call(
        functools.partial(_prep_kernel, use_rope=use_rope),
        grid=(b, l // tm),
        in_specs=in_specs,
        out_specs=[o[0] for o in outs],
        out_shape=[o[1] for o in outs],
        compiler_params=_cparams(("parallel", "parallel"), 40 * 1024 * 1024),
        name="prep_rope" if use_rope else "prep",
    )(*args)


def _attn_kernel(*refs, nseg, dq, tk, qscale, log2_scores):
    exp = jnp.exp2 if log2_scores else jnp.exp
    q_ref = refs[0]
    segs = [(refs[1 + 2 * i], refs[2 + 2 * i]) for i in range(nseg)]
    o_ref = refs[1 + 2 * nseg]
    tq = q_ref.shape[1]
    gw = 256
    hpg = gw // dq
    qstacks = []
    for g in range(N_HEADS // hpg):
        qg = q_ref[0, :, g * gw:(g + 1) * gw]
        if qscale is not None:
            qg = qg * jnp.asarray(qscale, BF16)
        qstacks.append(jnp.concatenate([jnp.where(_head_mask(qg.shape, j, dq), qg, jnp.zeros_like(qg))
                                        for j in range(hpg)], axis=0))
    carry = tuple((jnp.full((1, tq), -jnp.inf, F32), jnp.zeros((1, tq), F32), jnp.zeros((HEAD_W, tq), F32))
                  for _ in range(N_HEADS))

    def scores(k_ref, off, tkk):
        return [_dot_nt(k_ref[0, pl.ds(off, tkk), g * gw:(g + 1) * gw], qstacks[g])
                for g in range(N_HEADS // hpg)]

    def softmax_step(carry, head_scores, vt_ref, off, tkk):
        new = []
        for h in range(N_HEADS):
            m, l, acc = carry[h]
            vt = vt_ref[0, h * HEAD_W:(h + 1) * HEAD_W, pl.ds(off, tkk)]
            st = head_scores(h)
            mn = jnp.maximum(m, jnp.max(st, axis=0, keepdims=True))
            alpha = exp(m - mn)
            p = exp(st - mn)
            l = alpha * l + jnp.sum(p, axis=0, keepdims=True)
            acc = alpha * acc + _dot(vt, p.astype(BF16))
            new.append((mn, l, acc))
        return tuple(new)

    for k_ref, vt_ref in segs:
        lk = k_ref.shape[1]
        tkk = min(tk, lk)
        n = lk // tkk

        def body(c, carry, k_ref=k_ref, vt_ref=vt_ref, tkk=tkk):
            off = c * tkk if isinstance(c, int) else pl.multiple_of(c * tkk, tkk)
            st_g = scores(k_ref, off, tkk)
            return softmax_step(carry, lambda h: st_g[h // hpg][:, (h % hpg) * tq:(h % hpg + 1) * tq],
                                vt_ref, off, tkk)

        carry = body(0, carry) if n == 1 else lax.fori_loop(0, n, body, carry)
    out_t = jnp.concatenate([acc * (1.0 / l) for _, l, acc in carry], axis=0)
    o_ref[0] = out_t.T.astype(o_ref.dtype)


def _attention(q, segs, dq, name, qscale=None, log2_scores=True):
    (qa, qcol) = q
    b, lq, _ = qa.shape
    wq = N_HEADS * dq
    tq = min(512, lq)
    assert qcol % wq == 0
    in_specs = [pl.BlockSpec((1, tq, wq), lambda bi, i: (bi, i, qcol // wq))]
    args = [qa]
    for (ka, kcol), vt in segs:
        assert kcol % wq == 0 and vt.shape[1] == BRANCH_W and vt.shape[2] == ka.shape[1]
        in_specs.append(pl.BlockSpec((1, ka.shape[1], wq), lambda bi, i, kcol=kcol: (bi, 0, kcol // wq)))
        in_specs.append(pl.BlockSpec((1, BRANCH_W, vt.shape[2]), lambda bi, i: (bi, 0, 0)))
        args += [ka, vt]
    return pl.pallas_call(
        functools.partial(_attn_kernel, nseg=len(segs), dq=dq, tk=ATTN_KEY_CHUNK, qscale=qscale,
                          log2_scores=log2_scores),
        grid=(b, lq // tq),
        in_specs=in_specs,
        out_specs=pl.BlockSpec((1, tq, BRANCH_W), lambda bi, i: (bi, i, 0)),
        out_shape=jax.ShapeDtypeStruct((b, lq, BRANCH_W), BF16),
        compiler_params=_cparams(("parallel", "arbitrary"), 48 * 1024 * 1024),
        name=name,
    )(*args)


def _proj_cols(arr, col, width, rows):
    assert col % width == 0
    return pl.BlockSpec((1, rows, width), lambda *idx: (idx[0], 0, col // width))


def _na_kernel(pat_ref, ks_ref, q_ref, k_ref, v_ref, kc_ref, vc_ref, m_ref, o_ref, *, kw):
    del pat_ref
    g = pl.program_id(1)
    off = pl.multiple_of(ks_ref[g] * GRID_W, GRID_W)
    q = q_ref[0] * jnp.asarray(HEAD_W ** -0.5, BF16)
    kwin = k_ref[0, pl.ds(off, kw), :]
    vwin = v_ref[0, pl.ds(off, kw), :]
    kc = kc_ref[0]
    vc = vc_ref[0]
    out = jnp.zeros((q.shape[0], BRANCH_W), F32)
    for h in range(N_HEADS):
        qh = jnp.where(_head_mask(q.shape, h), q, jnp.zeros_like(q))
        sw = _dot_nt(qh, kwin) + m_ref[0, h]
        sc = _dot_nt(qh, kc)
        mx = jnp.maximum(jnp.max(sw, axis=-1, keepdims=True), jnp.max(sc, axis=-1, keepdims=True))
        pw = jnp.exp(sw - mx)
        pc = jnp.exp(sc - mx)
        l = jnp.sum(pw, axis=-1, keepdims=True) + jnp.sum(pc, axis=-1, keepdims=True)
        o = _dot(pw.astype(BF16), vwin) + _dot(pc.astype(BF16), vc)
        out = out + jnp.where(_head_mask(out.shape, h), o * (1.0 / l), 0.0)
    o_ref[0] = out.astype(o_ref.dtype)


def _na_plan(s):
    rows = s // GRID_W
    wr = min(NA_WIN_R, rows)
    wc = NA_WIN_C
    qr = min(NA_Q_ROWS, rows)
    kwr = min(qr + wr - 1 + (1 if qr + wr - 1 < rows else 0), rows)
    ngrp = rows // qr
    qc = np.arange(GRID_W)[:, None]
    kc = np.arange(GRID_W)[None, :]
    cs = np.clip(qc - wc // 2, 0, GRID_W - wc)
    valid_c = (kc >= cs) & (kc < cs + wc)
    rel_c = np.where(valid_c, kc - qc + (NA_WIN_C - 1), 0)
    assert (valid_c.sum(1) == wc).all()
    pats, pat_ids, ks_rows = [], [], []
    for g in range(ngrp):
        r0 = g * qr
        ks = int(np.clip(r0 - wr // 2, 0, rows - kwr))
        r = (r0 + np.arange(qr))[:, None]
        kr = (ks + np.arange(kwr))[None, :]
        rs = np.clip(r - wr // 2, 0, rows - wr)
        valid_r = (kr >= rs) & (kr < rs + wr)
        assert (valid_r.sum(1) == wr).all()
        rel_r = np.where(valid_r, kr - r + (NA_WIN_R - 1), 0)
        key = (valid_r.tobytes(), rel_r.tobytes())
        for pi, (pk, *_rest) in enumerate(pats):
            if pk == key:
                pat_ids.append(pi)
                break
        else:
            pat_ids.append(len(pats))
            pats.append((key, valid_r, rel_r))
        ks_rows.append(ks)
    valid_r = np.stack([p[1] for p in pats])
    rel_r = np.stack([p[2] for p in pats])
    return (qr, kwr, np.asarray(pat_ids, np.int32), np.asarray(ks_rows, np.int32), valid_r, rel_r, valid_c, rel_c)


def _na_bias_masks(na_bias, valid_r, rel_r, valid_c, rel_c):
    h = na_bias.shape[0]
    npat, qr, kwr = valid_r.shape
    ncol = 2 * NA_WIN_C - 1
    brow = na_bias[:, rel_r, :].astype(F32)
    onehot_c = ((rel_c[None] == np.arange(ncol)[:, None, None]) & valid_c[None]).astype(np.float32)
    m = jnp.einsum('hpqkc,cxy->phqxky', brow, jnp.asarray(onehot_c), precision=HIGHEST)
    valid = valid_r[:, None, :, None, :, None] & valid_c[None, None, None, :, None, :]
    m = jnp.where(valid, m, NEG_BIG)
    return m.reshape(npat, h, qr * GRID_W, kwr * GRID_W)


def _na_attention(projx, projc, na_bias):
    b, s, _ = projx.shape
    lc = projc.shape[1]
    qr, kwr, pat_ids, ks_rows, valid_r, rel_r, valid_c, rel_c = _na_plan(s)
    qb, kw = qr * GRID_W, kwr * GRID_W
    mb = _na_bias_masks(na_bias, valid_r, rel_r, valid_c, rel_c)
    grid_spec = pltpu.PrefetchScalarGridSpec(
        num_scalar_prefetch=2,
        grid=(b, s // qb),
        in_specs=[pl.BlockSpec((1, qb, 256), lambda bi, g, pat, ks: (bi, g, COL_NQ // 256)),
                  pl.BlockSpec((1, s, 256), lambda bi, g, pat, ks: (bi, 0, COL_NK // 256)),
                  pl.BlockSpec((1, s, 256), lambda bi, g, pat, ks: (bi, 0, COL_NV // 256)),
                  pl.BlockSpec((1, lc, 256), lambda bi, g, pat, ks: (bi, 0, COL_NK // 256)),
                  pl.BlockSpec((1, lc, 256), lambda bi, g, pat, ks: (bi, 0, COL_NV // 256)),
                  pl.BlockSpec((1, N_HEADS, qb, kw), lambda bi, g, pat, ks: (pat[g], 0, 0, 0))],
        out_specs=pl.BlockSpec((1, qb, BRANCH_W), lambda bi, g, pat, ks: (bi, g, 0)),
    )
    return pl.pallas_call(
        functools.partial(_na_kernel, kw=kw),
        grid_spec=grid_spec,
        out_shape=jax.ShapeDtypeStruct((b, s, BRANCH_W), BF16),
        compiler_params=_cparams(("parallel", "arbitrary"), 48 * 1024 * 1024),
        name="na_attention",
    )(jnp.asarray(pat_ids), jnp.asarray(ks_rows), projx, projx, projx, projc, projc, mb)


def _ret_kernel(lgs_ref, lgl_ref, gnw_ref, gseg_ref,
                qx, kx, vx, gfx, gbx, qc, kc, vc, gfc, gbc,
                yx_o, yc_o, of_s, ob_s, st_s, dec_s, qk_s, *, need_ctx):
    c = RET_CHUNK
    lc = qc.shape[1]
    sx = qx.shape[1]
    n_col = lax.broadcasted_iota(jnp.int32, (c, c), 0).astype(F32)
    m_row = lax.broadcasted_iota(jnp.int32, (c, c), 1).astype(F32)
    diff = n_col - m_row
    for h in range(N_HEADS):
        dec_s[0, :, h * c:(h + 1) * c] = jnp.where(diff >= 0, jnp.exp(lgs_ref[h] * jnp.maximum(diff, 0.0)), 0.0)
        dec_s[1, :, h * c:(h + 1) * c] = jnp.where(diff <= 0,
                                                   jnp.exp(lgs_ref[N_HEADS + h] * jnp.maximum(-diff, 0.0)), 0.0)
    pos = lax.broadcasted_iota(jnp.int32, (c, BRANCH_W), 0).astype(F32)
    lgf, lgb = lgl_ref[0], lgl_ref[1]
    qk_s[0] = jnp.exp(lgf * (pos + 1.0))
    qk_s[1] = jnp.exp(lgf * (c - 1.0 - pos))
    qk_s[2] = jnp.exp(lgb * (c - pos))
    qk_s[3] = jnp.exp(lgb * pos)
    cd_f = jnp.exp(lgf * float(c))
    cd_b = jnp.exp(lgb * float(c))
    st_s[...] = jnp.zeros_like(st_s)
    rowb = lax.broadcasted_iota(jnp.int32, (BRANCH_W, BRANCH_W), 0) // HEAD_W
    colb = lax.broadcasted_iota(jnp.int32, (BRANCH_W, BRANCH_W), 1) // HEAD_W
    bd_mask = rowb == colb

    def chunk_step(q, k, v, d, cd):
        kk = k * jnp.asarray(HEAD_W ** -0.5, BF16)
        state = st_s[d]
        o = _dot(q, state.astype(BF16)) * qk_s[2 * d]
        kstack = jnp.concatenate([jnp.where(_head_mask(kk.shape, h), kk, jnp.zeros_like(kk))
                                  for h in range(N_HEADS)], axis=0)
        vstack = jnp.concatenate([jnp.where(_head_mask(v.shape, h), v, jnp.zeros_like(v))
                                  for h in range(N_HEADS)], axis=0)
        inner = _dot_nt(q, kstack) * dec_s[d]
        o = o + _dot(inner.astype(BF16), vstack)
        kd = (kk.astype(F32) * qk_s[2 * d + 1]).astype(BF16)
        upd = lax.dot_general(kd, v, (((0,), (0,)), ((), ())), preferred_element_type=F32)
        st_s[d] = state * cd + jnp.where(bd_mask, upd, 0.0)
        return o

    def scan(q_ref, k_ref, v_ref, base, n):
        def body(i, _):
            fo = pl.multiple_of(i * c, c)
            bo = pl.multiple_of((n - 1 - i) * c, c)
            of_s[pl.ds(base + fo, c), :] = chunk_step(
                q_ref[0, pl.ds(fo, c), :], k_ref[0, pl.ds(fo, c), :], v_ref[0, pl.ds(fo, c), :], 0, cd_f)
            ob_s[pl.ds(base + bo, c), :] = chunk_step(
                q_ref[0, pl.ds(bo, c), :], k_ref[0, pl.ds(bo, c), :], v_ref[0, pl.ds(bo, c), :], 1, cd_b)
            return 0
        lax.fori_loop(0, n, body, 0)

    scan(qc, kc, vc, 0, lc // c)
    scan(qx, kx, vx, lc, sx // c)

    gseg = gseg_ref[...]
    gnw = gnw_ref[...]

    def gnorm(o):
        mu = _dot_sel(o, gseg) * (1.0 / HEAD_W)
        dlt = o - mu
        var = _dot_sel(dlt * dlt, gseg) * (1.0 / HEAD_W)
        return dlt * lax.rsqrt(var + EPS) * gnw

    def combine(gf_ref, gb_ref, y_ref, base, n):
        def body(i, _):
            ro = pl.multiple_of(i * c, c)
            y = (gnorm(of_s[pl.ds(base + ro, c), :]) * _silu(gf_ref[0, pl.ds(ro, c), :].astype(F32))
                 + gnorm(ob_s[pl.ds(base + ro, c), :]) * _silu(gb_ref[0, pl.ds(ro, c), :].astype(F32)))
            y_ref[0, pl.ds(ro, c), :] = y.astype(y_ref.dtype)
            return 0
        lax.fori_loop(0, n, body, 0)

    combine(gfx, gbx, yx_o, lc, sx // c)
    if need_ctx:
        combine(gfc, gbc, yc_o, 0, lc // c)
    else:
        yc_o[...] = jnp.zeros_like(yc_o)


def _retention(projx, projc, log_g, gn_w, gseg, need_ctx):
    b, s, _ = projx.shape
    lc = projc.shape[1]
    lgs = log_g.reshape(2 * N_HEADS)
    lgl = jnp.repeat(log_g, HEAD_W, axis=1).reshape(2, 1, BRANCH_W)
    xs = [_proj_cols(projx, COL_RET + 256 * i, 256, s) for i in range(5)]
    cs = [_proj_cols(projc, COL_RET + 256 * i, 256, lc) for i in range(5)]
    c = RET_CHUNK
    yx, yc = pl.pallas_call(
        functools.partial(_ret_kernel, need_ctx=need_ctx),
        grid=(b,),
        in_specs=[pl.BlockSpec(memory_space=pltpu.SMEM),
                  pl.BlockSpec((2, 1, BRANCH_W), lambda bi: (0, 0, 0)),
                  pl.BlockSpec((1, BRANCH_W), lambda bi: (0, 0)),
                  pl.BlockSpec((BRANCH_W, BRANCH_W), lambda bi: (0, 0))] + xs + cs,
        out_specs=[pl.BlockSpec((1, s, BRANCH_W), lambda bi: (bi, 0, 0)),
                   pl.BlockSpec((1, lc, BRANCH_W), lambda bi: (bi, 0, 0))],
        out_shape=[jax.ShapeDtypeStruct((b, s, BRANCH_W), BF16),
                   jax.ShapeDtypeStruct((b, lc, BRANCH_W), BF16)],
        scratch_shapes=[pltpu.VMEM((lc + s, BRANCH_W), F32),
                        pltpu.VMEM((lc + s, BRANCH_W), F32),
                        pltpu.VMEM((2, BRANCH_W, BRANCH_W), F32),
                        pltpu.VMEM((2, c, N_HEADS * c), F32),
                        pltpu.VMEM((4, c, BRANCH_W), F32)],
        compiler_params=_cparams(("parallel",), 48 * 1024 * 1024),
        name="retention",
    )(lgs, lgl, gn_w.reshape(1, BRANCH_W), gseg, *([projx] * 5), *([projc] * 5))
    return yx, yc


def _merge_kernel(oa, ob, oc, od, g0, g1, g2, g3, x_ref, gate_ref, sc_ref, sh_ref, nw_ref, wb_ref, wo_ref,
                  xn_o, h2_o):
    acc = None
    for i, (o, g) in enumerate(((oa, g0), (ob, g1), (oc, g2), (od, g3))):
        t = jax.nn.sigmoid(g[0].astype(F32)) * _dot(o[0], wb_ref[i])
        acc = t if acc is None else acc + t
    y = _dot(acc.astype(BF16), wo_ref[...])
    xn = x_ref[0] + gate_ref[0] * y
    xn_o[0] = xn
    h2_o[0] = _rms(xn) * nw_ref[...] * (1.0 + sc_ref[0]) + sh_ref[0]


def _merge(outs, proj, x, gate, sc2, sh2, n2w, wb, wo):
    b, l, d = x.shape
    tm = min(512, l)
    tok = lambda w: pl.BlockSpec((1, tm, w), lambda bi, i: (bi, i, 0))
    vec = pl.BlockSpec((1, 1, d), lambda bi, i: (bi, 0, 0))
    gates = [pl.BlockSpec((1, tm, d), lambda bi, i, k=k: (bi, i, COL_GATES // d + k)) for k in range(N_BRANCH)]
    return pl.pallas_call(
        _merge_kernel,
        grid=(b, l // tm),
        in_specs=[tok(BRANCH_W)] * 4 + gates + [tok(d), vec, vec, vec,
                                                pl.BlockSpec((1, d), lambda bi, i: (0, 0)),
                                                pl.BlockSpec(wb.shape, lambda bi, i: (0, 0, 0)),
                                                pl.BlockSpec(wo.shape, lambda bi, i: (0, 0))],
        out_specs=[tok(d), tok(d)],
        out_shape=[jax.ShapeDtypeStruct((b, l, d), F32), jax.ShapeDtypeStruct((b, l, d), F32)],
        compiler_params=_cparams(("parallel", "parallel"), 48 * 1024 * 1024),
        name="merge",
    )(*outs, proj, proj, proj, proj, x, gate, sc2, sh2, n2w, wb, wo)


def _topk_rows(s, k):
    r, t = s.shape
    iota = lax.broadcasted_iota(jnp.int32, s.shape, 0).astype(F32)
    out_row = lax.broadcasted_iota(jnp.int32, (k, t), 0)
    vals = jnp.zeros((k, t), F32)
    idxs = jnp.zeros((k, t), F32)
    for i in range(k):
        m = jnp.max(s, axis=0, keepdims=True)
        idx = jnp.min(jnp.where(s == m, iota, float(r)), axis=0, keepdims=True)
        vals = jnp.where(out_row == i, m, vals)
        idxs = jnp.where(out_row == i, idx, idxs)
        s = jnp.where(iota == idx, -jnp.inf, s)
    return vals, idxs.astype(jnp.int32)


def _select_rows(table, sel, k):
    out = jnp.zeros_like(table)
    for r in range(k):
        out = jnp.where(sel == r, table[r:r + 1, :], out)
    return out


def _split_bf16(x):
    hi = x.astype(BF16)
    return hi, (x - hi.astype(F32)).astype(BF16)


_CAND_MID = tuple(PEER_TOPK // (i + 1) for i in range(1, 8))
_CAND_ROWS = PEER_TOPK + 8 * len(_CAND_MID) + 8


def _peer_candidates(s1, s2):
    t = s1.shape[1]
    sub = lax.broadcasted_iota(jnp.int32, (8, t), 0)
    blocks = [s1[0:1] + s2]
    for i, nj in enumerate(_CAND_MID, start=1):
        blocks.append(jnp.where(sub < nj, s1[i:i + 1] + s2[0:8], -jnp.inf))
    blocks.append(s1[8:16] + s2[0:1])
    return jnp.concatenate(blocks, axis=0)


def _peer_candidate_ranks(pos):
    mid = pos - PEER_TOPK
    i = jnp.where(pos < PEER_TOPK, 0, jnp.where(pos < _CAND_ROWS - 8, (mid >> 3) + 1, pos - (_CAND_ROWS - 16)))
    j = jnp.where(pos < PEER_TOPK, pos, jnp.where(pos < _CAND_ROWS - 8, mid & 7, 0))
    return i, j


def _peer_route_kernel(h_ref, wqh_ref, wql_ref, kh_ref, kl_ref, a_o, b_o, g_o, q_s, a_s, b_s, g_s):
    k = PEER_TOPK
    assert k == 16
    hh, hl = _split_bf16(h_ref[...])
    q_s[...] = _dot(hh, wqh_ref[...]) + (_dot(hl, wqh_ref[...]) + _dot(hh, wql_ref[...]))

    def head(h):
        lo = pl.multiple_of(h * PEER_DK, PEER_DK)
        qh, ql = _split_bf16(q_s[:, pl.ds(lo, PEER_DK)])
        kh, kl = kh_ref[h], kl_ref[h]
        s = _dot_nt(kh, qh) + (_dot_nt(kh, ql) + _dot_nt(kl, qh))
        s1, i1 = _topk_rows(s[:PEER_N_KEYS], k)
        s2, i2 = _topk_rows(s[PEER_N_KEYS:], k)
        ts, tpos = _topk_rows(_peer_candidates(s1, s2), k)
        e = jnp.exp(ts - ts[0:1, :])
        gate = e / jnp.sum(e, axis=0, keepdims=True)
        ri, rj = _peer_candidate_ranks(tpos)
        ro = pl.multiple_of(h * k, k)
        a_s[pl.ds(ro, k), :] = _select_rows(i1, ri, k)
        b_s[pl.ds(ro, k), :] = _select_rows(i2, rj, k)
        g_s[pl.ds(ro, k), :] = gate

    def head_pair(i, _):
        head(2 * i)
        head(2 * i + 1)
        return 0

    lax.fori_loop(0, PEER_HEADS // 2, head_pair, 0)
    a_o[...] = a_s[...].T
    b_o[...] = b_s[...].T
    g_o[...] = g_s[...].T


def _peer_route(h2, wq_hl, keys_hl):
    n, d = h2.shape
    t = 256
    wq_hi, wq_lo = wq_hl
    k_hi, k_lo = keys_hl
    return pl.pallas_call(
        _peer_route_kernel,
        grid=(n // t,),
        in_specs=[pl.BlockSpec((t, d), lambda i: (i, 0)),
                  pl.BlockSpec(wq_hi.shape, lambda i: (0, 0)),
                  pl.BlockSpec(wq_lo.shape, lambda i: (0, 0)),
                  pl.BlockSpec(k_hi.shape, lambda i: (0, 0, 0)),
                  pl.BlockSpec(k_lo.shape, lambda i: (0, 0, 0))],
        out_specs=[pl.BlockSpec((t, PEER_SLOTS), lambda i: (i, 0))] * 3,
        out_shape=[jax.ShapeDtypeStruct((n, PEER_SLOTS), jnp.int32),
                   jax.ShapeDtypeStruct((n, PEER_SLOTS), jnp.int32),
                   jax.ShapeDtypeStruct((n, PEER_SLOTS), F32)],
        scratch_shapes=[pltpu.VMEM((t, PEER_HEADS * PEER_DK), F32),
                        pltpu.VMEM((PEER_SLOTS, t), jnp.int32),
                        pltpu.VMEM((PEER_SLOTS, t), jnp.int32),
                        pltpu.VMEM((PEER_SLOTS, t), F32)],
        compiler_params=_cparams(("parallel",), 48 * 1024 * 1024),
        name="peer_route",
    )(h2, wq_hi, wq_lo, k_hi, k_lo)


_HI16 = -65536


def _bf16_bits(w):
    return lax.bitcast_convert_type(w, jnp.int32) & _HI16


def _peer_ffn_kernel(h_ref, a_ref, b_ref, g_ref, x_ref, gate_ref, u_ref, v_ref, fw_ref, o_ref, hb_s, w_s, *,
                     ec, unroll, final_norm):
    e = pl.program_id(1)
    t = h_ref.shape[0]
    half = t // 2
    nk = PEER_N_KEYS

    @pl.when(e == 0)
    def _():
        hb_s[...] = h_ref[...].astype(BF16)
        o_ref[...] = jnp.zeros_like(o_ref)
        jio = lax.broadcasted_iota(jnp.int32, (nk, PEER_SLOTS), 0)

        def tile(tt):
            arow = jnp.broadcast_to(a_ref[pl.ds(tt, 1), :], (nk, PEER_SLOTS))
            brow = jnp.broadcast_to(b_ref[pl.ds(tt, 1), :], (nk, PEER_SLOTS))
            grow = jnp.broadcast_to(g_ref[pl.ds(tt, 1), :], (nk, PEER_SLOTS))
            cm = jnp.where(jio == arow, grow, 0.0).astype(BF16)
            bm_t = jnp.where(jio == brow, 1.0, 0.0).T.astype(BF16)
            return _dot(cm, bm_t)

        def build(tb, _):
            for u in range(unroll):
                tt = tb * unroll + u
                word = _bf16_bits(tile(tt + half)) | lax.shift_right_logical(_bf16_bits(tile(tt)), 16)
                w_s[pl.ds(pl.multiple_of(tt * PEER_W_PITCH, 8), nk), :] = word
            return 0

        lax.fori_loop(0, half // unroll, build, 0)

    hid = _dot(hb_s[...], u_ref[0])
    j0 = e * (ec // nk)
    words = jnp.concatenate([w_s[pl.ds(j0 + j, half, stride=PEER_W_PITCH), :] for j in range(ec // nk)], axis=1)
    w_lo = lax.bitcast_convert_type(lax.shift_left(words, 16), F32)
    w_hi = lax.bitcast_convert_type(words & _HI16, F32)
    wc = jnp.concatenate([w_lo, w_hi], axis=0)
    act = 0.5 * hid * (1.0 + lax.erf(hid * SQRT_HALF))
    o_ref[...] += _dot((wc * act).astype(BF16), v_ref[...])

    @pl.when(e == pl.num_programs(1) - 1)
    def _():
        y = x_ref[...] + gate_ref[0] * o_ref[...]
        o_ref[...] = _rms(y) * fw_ref[...] if final_norm else y


def _peer_ffn(h2, a, b_idx, g, x, gate, u_blk, v, l, final_w, final_norm):
    n, d = h2.shape
    ne = v.shape[0]
    neb, _, ec = u_blk.shape
    t = min(512, l)
    unroll = 16
    assert l % t == 0 and ne == PEER_N_KEYS * PEER_N_KEYS and neb * ec == ne and (t // 2) % unroll == 0
    tok = lambda w: pl.BlockSpec((t, w), lambda i, e: (i, 0))
    return pl.pallas_call(
        functools.partial(_peer_ffn_kernel, ec=ec, unroll=unroll, final_norm=final_norm),
        grid=(n // t, neb),
        in_specs=[tok(d), tok(PEER_SLOTS), tok(PEER_SLOTS), tok(PEER_SLOTS), tok(d),
                  pl.BlockSpec((1, 1, d), lambda i, e: ((i * t) // l, 0, 0)),
                  pl.BlockSpec((1, d, ec), lambda i, e: (e, 0, 0)),
                  pl.BlockSpec((ec, d), lambda i, e: (e, 0)),
                  pl.BlockSpec((1, d), lambda i, e: (0, 0))],
        out_specs=tok(d),
        out_shape=jax.ShapeDtypeStruct((n, d), F32),
        scratch_shapes=[pltpu.VMEM((t, d), BF16),
                        pltpu.VMEM((t // 2 * PEER_W_PITCH, PEER_N_KEYS), jnp.int32)],
        compiler_params=_cparams(("parallel", "arbitrary"), VMEM_LIMIT_V7X),
        name="peer_ffn",
    )(h2, a, b_idx, g, x, gate, u_blk, v, final_w.reshape(1, d))


def _layout_w_in(w):
    parts = jnp.split(w, IN_OFFSETS, axis=1)
    z = lambda n: jnp.zeros((w.shape[0], n), w.dtype)
    kpe_blk = jnp.concatenate([z(MLA_NOPE), parts[2], z(MLA_HEAD_PAD - MLA_NOPE - MLA_ROPE)], axis=1)
    return jnp.concatenate([parts[0], parts[1], kpe_blk] + list(parts[3:]), axis=1).astype(BF16)


def _layout_mla(w_uq, w_ukv):
    qh = w_uq.reshape(MLA_Q_LORA, N_HEADS, MLA_NOPE + MLA_ROPE)
    qh = jnp.pad(qh, ((0, 0), (0, 0), (0, MLA_HEAD_PAD - MLA_NOPE - MLA_ROPE)))
    kv = w_ukv.reshape(MLA_KV_LORA, N_HEADS, MLA_NOPE + MLA_V)
    kh = jnp.pad(kv[:, :, :MLA_NOPE], ((0, 0), (0, 0), (0, MLA_HEAD_PAD - MLA_NOPE)))
    vh = kv[:, :, MLA_NOPE:]
    return (qh.reshape(MLA_Q_LORA, -1).astype(BF16), kh.reshape(MLA_KV_LORA, -1).astype(BF16),
            vh.reshape(MLA_KV_LORA, -1).T.astype(BF16))


def _static_mats():
    gseg = np.kron(np.eye(N_HEADS), np.ones((HEAD_W, HEAD_W))).astype(np.float32)
    pm = np.zeros((MLA_HEAD_PAD, MLA_HEAD_PAD), np.float32)
    for dd in range(MLA_ROPE):
        blk, j = dd // 16, dd % 16
        pm[MLA_NOPE + blk * 16 + (j + 8) % 16, MLA_NOPE + dd] = 1.0
    pg = np.zeros((BRANCH_W, BRANCH_W), np.float32)
    for i in range(BRANCH_W):
        off, dd = (i // HEAD_W) * HEAD_W, i % HEAD_W
        blk, j = dd // 32, dd % 32
        pg[off + blk * 32 + (j + 16) % 32, i] = 1.0
    ex = np.zeros((GQA_KV_HEADS * HEAD_W, BRANCH_W), np.float32)
    for i in range(BRANCH_W):
        ex[((i // HEAD_W) // (N_HEADS // GQA_KV_HEADS)) * HEAD_W + i % HEAD_W, i] = 1.0
    return tuple(jnp.asarray(m, dtype=BF16) for m in (gseg, pm, pg, ex, ex.T))


def _rope_half_tables(pos, hf):
    freqs = ROPE_THETA ** (-jnp.arange(hf, dtype=F32) / hf)
    ang = pos[:, None] * freqs[None, :]
    c, s = jnp.cos(ang), jnp.sin(ang)
    return jnp.concatenate([c, c], axis=1), jnp.concatenate([-s, s], axis=1)


def _axial_tables(row, col, dims):
    cr, sr = _rope_half_tables(row, dims // 4)
    cc, sc = _rope_half_tables(col, dims // 4)
    return jnp.concatenate([cr, cc], axis=1), jnp.concatenate([sr, sc], axis=1)


def _rope_tables(s):
    t = jnp.arange(s)
    row, col = (t // GRID_W).astype(F32), (t % GRID_W).astype(F32)
    c32, s32 = _axial_tables(row, col, MLA_ROPE)
    pad = MLA_HEAD_PAD - MLA_NOPE - MLA_ROPE
    cm = jnp.concatenate([jnp.ones((s, MLA_NOPE), F32), c32, jnp.ones((s, pad), F32)], axis=1)
    sm = jnp.concatenate([jnp.zeros((s, MLA_NOPE), F32), s32, jnp.zeros((s, pad), F32)], axis=1)
    c64, s64 = _axial_tables(row, col, HEAD_W)
    return cm, sm, jnp.tile(c64, (1, N_HEADS)), jnp.tile(s64, (1, N_HEADS))


def _split_f32(w):
    hi = w.astype(BF16)
    return hi, (w - hi.astype(F32)).astype(BF16)


def _layout_peer_keys(keys):
    h, _, nk, dh = keys.shape
    z = jnp.zeros((h, nk, dh), keys.dtype)
    top = jnp.concatenate([keys[:, 0], z], axis=2)
    bot = jnp.concatenate([z, keys[:, 1]], axis=2)
    return jnp.concatenate([top, bot], axis=1)


def kernel(x, c, ctx, c_ctx, mod_w, mod_b, norm1_w, norm2_w, w_in, mla_q_norm, mla_w_uq, mla_kv_norm, mla_w_ukv, gqa_q_norm, gqa_k_norm, na_bias, ret_decay_logit, ret_gn_w, w_branch, w_out, peer_w_q, peer_keys, peer_u, peer_v, final_norm_w):
    b, s, d = x.shape
    lc = ctx.shape[1]
    depth = mod_w.shape[0]
    assert d == D_MODEL and s % (GRID_W * NA_Q_ROWS) == 0 and s % 256 == 0 and lc % 256 == 0

    rows = -(-(b + 1) // 8) * 8
    cc = jnp.zeros((rows, d), F32).at[:b].set(c).at[b].set(c_ctx)
    mod = _modulation(cc, mod_w, mod_b)

    gseg, pm, pg, ex, ex_t = _static_mats()
    tables = _rope_tables(s)

    for l in range(depth):
        need_ctx = l < depth - 1
        mx = mod[l, :b].reshape(b, 1, 6, d)
        mc = jnp.broadcast_to(mod[l, b].reshape(1, 1, 6, d), (b, 1, 6, d))
        sh1x, sc1x, g1x, sh2x, sc2x, g2x = (mx[:, :, i] for i in range(6))
        sh1c, sc1c, g1c, sh2c, sc2c, g2c = (mc[:, :, i] for i in range(6))

        w_in_l = _layout_w_in(w_in[l])
        wuq, wk, wv = _layout_mla(mla_w_uq[l], mla_w_ukv[l])
        consts = (mla_q_norm[l].reshape(1, -1), wuq, mla_kv_norm[l].reshape(1, -1), wk, wv,
                  jnp.tile(gqa_q_norm[l], N_HEADS).reshape(1, -1),
                  jnp.tile(gqa_k_norm[l], GQA_KV_HEADS).reshape(1, -1), gseg, pm, pg, ex, ex_t)
        n1w = norm1_w[l].reshape(1, d)
        n2w = norm2_w[l].reshape(1, d)
        wb = w_branch[l].astype(BF16)
        wo = w_out[l].astype(BF16)
        keys_hl = _split_f32(_layout_peer_keys(peer_keys[l]))
        wq_hl = _split_f32(peer_w_q[l])
        ne = peer_u.shape[1]
        u_blk = jnp.swapaxes(peer_u[l].astype(BF16).reshape(ne // PEER_EXPERT_CHUNK, PEER_EXPERT_CHUNK, d), 1, 2)
        v_b = peer_v[l].astype(BF16)
        log_g = jax.nn.log_sigmoid(ret_decay_logit[l].astype(F32))

        projx = _inproj(x, n1w, sc1x, sh1x, w_in_l)
        projc = _inproj(ctx, n1w, sc1c, sh1c, w_in_l)
        qmx, kmx, vmx, qgx, kgx, vgx = _prep(projx, consts, tables)
        qmc, kmc, vmc, qgc, kgc, vgc = _prep(projc, consts, None)

        oa = _attention((qmx, 0), [((kmc, 0), vmc), ((kmx, 0), vmx)], MLA_HEAD_PAD, "attn_mla")
        ob = _attention((qgx, 0), [((kgc, 0), vgc), ((kgx, 0), vgx)], HEAD_W, "attn_gqa")
        oc = _na_attention(projx, projc, na_bias[l])
        od, od_c = _retention(projx, projc, log_g, ret_gn_w[l], gseg, need_ctx)

        x, h2x = _merge((oa, ob, oc, od), projx, x, g1x, sc2x, sh2x, n2w, wb, wo)
        ax, bx, gx = _peer_route(h2x.reshape(b * s, d), wq_hl, keys_hl)
        x = _peer_ffn(h2x.reshape(b * s, d), ax, bx, gx, x.reshape(b * s, d), g2x, u_blk, v_b, s,
                      final_norm_w, final_norm=(l == depth - 1)).reshape(b, s, d)

        if need_ctx:
            ca = _attention((qmc, 0), [((kmc, 0), vmc)], MLA_HEAD_PAD, "attn_mla_ctx")
            cb = _attention((qgc, 0), [((kgc, 0), vgc)], HEAD_W, "attn_gqa_ctx")
            nv_t = jnp.swapaxes(projc[:, :, COL_NV:COL_NV + BRANCH_W], 1, 2)
            ccx = _attention((projc, COL_NQ), [((projc, COL_NK), nv_t)], HEAD_W, "attn_na_ctx",
                             qscale=HEAD_W ** -0.5, log2_scores=False)
            ctx, h2c = _merge((ca, cb, ccx, od_c), projc, ctx, g1c, sc2c, sh2c, n2w, wb, wo)
            ac, bc, gc = _peer_route(h2c.reshape(b * lc, d), wq_hl, keys_hl)
            ctx = _peer_ffn(h2c.reshape(b * lc, d), ac, bc, gc, ctx.reshape(b * lc, d), g2c, u_blk, v_b,
                            lc, final_norm_w, final_norm=False).reshape(b, lc, d)

    return x
```

```python
import functools

import numpy as np
import jax
import jax.numpy as jnp
from jax import lax
from jax.experimental import pallas as pl
from jax.experimental.pallas import tpu as pltpu

F32 = jnp.float32
BF16 = jnp.bfloat16
HIGHEST = lax.Precision.HIGHEST

D_MODEL = 1024
GRID_W = 64
ROPE_THETA = 10000.0
EPS = 1e-6
N_HEADS = 4
HEAD_W = 64
BRANCH_W = N_HEADS * HEAD_W
MLA_NOPE, MLA_ROPE, MLA_V = 64, 32, 64
MLA_Q_LORA, MLA_KV_LORA = 256, 128
MLA_SCALE = (MLA_NOPE + MLA_ROPE) ** -0.5
MLA_HEAD_PAD = 128
GQA_KV_HEADS = 2
NA_WIN_R, NA_WIN_C = 8, 16
NA_Q_ROWS = 4
ATTN_KEY_CHUNK = 1024
RET_CHUNK = 128
N_BRANCH = 4
PEER_HEADS, PEER_N_KEYS, PEER_TOPK, PEER_DK = 8, 128, 16, 128
PEER_SLOTS = PEER_HEADS * PEER_TOPK
PEER_W_PITCH = PEER_N_KEYS + 8
PEER_HEADS_PER_TRIP = 4
PEER_EXPERT_CHUNK = 1024
SQRT_HALF = 0.7071067811865476
LOG2E = 1.4426950408889634
NEG_BIG = -1e30

IN_SIZES = (256, 128, 32, 256, 128, 128, 256, 256, 256, 256, 256, 256, 256, 256, 4096)
IN_OFFSETS = tuple(int(v) for v in np.cumsum(IN_SIZES)[:-1])
PROJ_COLS = 7168
COL_NQ, COL_NK, COL_NV = 1024, 1280, 1536
COL_RET = 1792
COL_GATES = 3072

VMEM_LIMIT_V7X = 56 * 1024 * 1024


def _cparams(sem, vmem=None):
    return pltpu.CompilerParams(dimension_semantics=sem, vmem_limit_bytes=vmem)


def _dot(a, b):
    return jnp.dot(a, b, preferred_element_type=F32)


def _dot_hi(a, b):
    return jnp.dot(a, b, preferred_element_type=F32, precision=HIGHEST)


def _dot_nt(a, b):
    return lax.dot_general(a, b, (((1,), (1,)), ((), ())), preferred_element_type=F32)


def _dot_sel(x, sel):
    hi = x.astype(BF16)
    r1 = x - hi.astype(F32)
    mid = r1.astype(BF16)
    lo = (r1 - mid.astype(F32)).astype(BF16)
    return _dot(hi, sel) + (_dot(mid, sel) + _dot(lo, sel))


def _rms(x):
    return x * lax.rsqrt(jnp.mean(x * x, axis=-1, keepdims=True) + EPS)


def _silu(x):
    return x * jax.nn.sigmoid(x)


def _head_mask(shape, h, width=HEAD_W):
    lane = lax.broadcasted_iota(jnp.int32, shape, len(shape) - 1)
    lo = h * width
    return (lane >= lo) & (lane < lo + width)


def _mod_kernel(c_ref, w_ref, b_ref, o_ref):
    o_ref[0] = _dot_hi(_silu(c_ref[...]), w_ref[0]) + b_ref[0]


def _modulation(cc, mod_w, mod_b):
    depth, d, n = mod_w.shape
    rows = cc.shape[0]
    tn = 1536
    return pl.pallas_call(
        _mod_kernel,
        grid=(depth, n // tn),
        in_specs=[pl.BlockSpec((rows, d), lambda l, j: (0, 0)),
                  pl.BlockSpec((1, d, tn), lambda l, j: (l, 0, j)),
                  pl.BlockSpec((1, 1, tn), lambda l, j: (l, 0, j))],
        out_specs=pl.BlockSpec((1, rows, tn), lambda l, j: (l, 0, j)),
        out_shape=jax.ShapeDtypeStruct((depth, rows, n), F32),
        compiler_params=_cparams(("parallel", "parallel"), 40 * 1024 * 1024),
        name="modulation",
    )(cc, mod_w, mod_b.reshape(depth, 1, n))


def _inproj_kernel(x_ref, nw_ref, sc_ref, sh_ref, w_ref, o_ref, h_scr):
    @pl.when(pl.program_id(2) == 0)
    def _():
        h = _rms(x_ref[0]) * nw_ref[...] * (1.0 + sc_ref[0]) + sh_ref[0]
        h_scr[...] = h.astype(BF16)

    o_ref[0] = _dot(h_scr[...], w_ref[...]).astype(o_ref.dtype)


def _inproj(x, nw, sc, sh, w):
    b, l, d = x.shape
    n = w.shape[1]
    tm = min(512, l)
    tn = 1792
    return pl.pallas_call(
        _inproj_kernel,
        grid=(b, l // tm, n // tn),
        in_specs=[pl.BlockSpec((1, tm, d), lambda bi, i, j: (bi, i, 0)),
                  pl.BlockSpec((1, d), lambda bi, i, j: (0, 0)),
                  pl.BlockSpec((1, 1, d), lambda bi, i, j: (bi, 0, 0)),
                  pl.BlockSpec((1, 1, d), lambda bi, i, j: (bi, 0, 0)),
                  pl.BlockSpec((d, tn), lambda bi, i, j: (0, j))],
        out_specs=pl.BlockSpec((1, tm, tn), lambda bi, i, j: (bi, i, j)),
        out_shape=jax.ShapeDtypeStruct((b, l, n), BF16),
        scratch_shapes=[pltpu.VMEM((tm, d), BF16)],
        compiler_params=_cparams(("parallel", "parallel", "arbitrary"), 40 * 1024 * 1024),
        name="inproj",
    )(x, nw, sc, sh, w)


def _prep_kernel(*refs, use_rope):
    (p_ref, qn_ref, wuq_ref, kvn_ref, wk_ref, wv_ref, gqn_ref, gkn_ref, gseg_ref, pm_ref, pg_ref,
     e_ref, et_ref) = refs[:13]
    if use_rope:
        cm_ref, sm_ref, cg_ref, sg_ref = refs[13:17]
        outs = refs[17:]
    else:
        outs = refs[13:]
    qm_o, km_o, vm_o, qg_o, kg_o, vg_o = outs

    pb = p_ref[0]
    cq = pb[:, 0:256].astype(F32)
    ckv = pb[:, 256:384].astype(F32)
    kpe = pb[:, 384:512].astype(F32)
    gq = pb[:, 512:768].astype(F32)
    gk = pb[:, 768:896].astype(F32)
    gv = pb[:, 896:1024]

    cqn = (_rms(cq) * qn_ref[...]).astype(BF16)
    qa = _dot(cqn, wuq_ref[...])
    ckn = (_rms(ckv) * kvn_ref[...]).astype(BF16)
    kn = _dot(ckn, wk_ref[...])
    vm_t = _dot_nt(wv_ref[...], ckn)
    if use_rope:
        cm, sm = cm_ref[...], sm_ref[...]
        pm = pm_ref[...]

        def rope_m(t):
            return t * cm + _dot_sel(t, pm) * sm

        qa = jnp.concatenate([rope_m(qa[:, h * 128:(h + 1) * 128]) for h in range(N_HEADS)], axis=1)
        kpe = rope_m(kpe)
    km = kn + jnp.concatenate([kpe] * N_HEADS, axis=1)
    qm_o[0] = (qa * (MLA_SCALE * LOG2E)).astype(BF16)
    km_o[0] = km.astype(BF16)
    vm_o[0] = vm_t.astype(BF16)

    gseg = gseg_ref[...]
    gqn = gq * lax.rsqrt(_dot_sel(gq * gq, gseg) * (1.0 / HEAD_W) + EPS) * gqn_ref[...]
    gkn = gk * lax.rsqrt(_dot_sel(gk * gk, gseg[:128, :128]) * (1.0 / HEAD_W) + EPS) * gkn_ref[...]
    if use_rope:
        cg, sg = cg_ref[...], sg_ref[...]
        pg = pg_ref[...]
        gqn = gqn * cg + _dot_sel(gqn, pg) * sg
        gkn = gkn * cg[:, :128] + _dot_sel(gkn, pg[:128, :128]) * sg[:, :128]
    qg_o[0] = (gqn * (HEAD_W ** -0.5 * LOG2E)).astype(BF16)
    e = e_ref[...]
    kg_o[0] = _dot(gkn.astype(BF16), e).astype(BF16)
    vg_o[0] = _dot_nt(et_ref[...], gv).astype(BF16)


def _prep(proj, consts, tables):
    b, l, _ = proj.shape
    tm = min(512, l)
    use_rope = tables is not None
    full = lambda a: pl.BlockSpec(a.shape, lambda bi, i: (0,) * a.ndim)
    in_specs = [pl.BlockSpec((1, tm, 1024), lambda bi, i: (bi, i, 0))] + [full(a) for a in consts]
    args = [proj] + list(consts)
    if use_rope:
        in_specs += [pl.BlockSpec((tm, t.shape[1]), lambda bi, i: (i, 0)) for t in tables]
        args += list(tables)
    tok = lambda w: (pl.BlockSpec((1, tm, w), lambda bi, i: (bi, i, 0)), jax.ShapeDtypeStruct((b, l, w), BF16))
    tr = (pl.BlockSpec((1, BRANCH_W, tm), lambda bi, i: (bi, 0, i)), jax.ShapeDtypeStruct((b, BRANCH_W, l), BF16))
    outs = (tok(512), tok(512), tr, tok(256), tok(256), tr)
    return pl.pallas_call(
        functools.partial(_prep_kernel, use_rope=use_rope),
        grid=(b, l // tm),
        in_specs=in_specs,
        out_specs=[o[0] for o in outs],
        out_shape=[o[1] for o in outs],
        compiler_params=_cparams(("parallel", "parallel"), 40 * 1024 * 1024),
        name="prep_rope" if use_rope else "prep",
    )(*args)


def _attn_kernel(*refs, nseg, dq, tk, qscale, log2_scores):
    exp = jnp.exp2 if log2_scores else jnp.exp
    q_ref = refs[0]
    segs = [(refs[1 + 2 * i], refs[2 + 2 * i]) for i in range(nseg)]
    o_ref = refs[1 + 2 * nseg]
    tq = q_ref.shape[1]
    gw = 256
    hpg = gw // dq
    qstacks = []
    for g in range(N_HEADS // hpg):
        qg = q_ref[0, :, g * gw:(g + 1) * gw]
        if qscale is not None:
            qg = qg * jnp.asarray(qscale, BF16)
        qstacks.append(jnp.concatenate([jnp.where(_head_mask(qg.shape, j, dq), qg, jnp.zeros_like(qg))
                                        for j in range(hpg)], axis=0))
    carry = tuple((jnp.full((1, tq), -jnp.inf, F32), jnp.zeros((1, tq), F32), jnp.zeros((HEAD_W, tq), F32))
                  for _ in range(N_HEADS))

    def scores(k_ref, off, tkk):
        return [_dot_nt(k_ref[0, pl.ds(off, tkk), g * gw:(g + 1) * gw], qstacks[g])
                for g in range(N_HEADS // hpg)]

    def softmax_step(carry, head_scores, vt_ref, off, tkk):
        new = []
        for h in range(N_HEADS):
            m, l, acc = carry[h]
            vt = vt_ref[0, h * HEAD_W:(h + 1) * HEAD_W, pl.ds(off, tkk)]
            st = head_scores(h)
            mn = jnp.maximum(m, jnp.max(st, axis=0, keepdims=True))
            alpha = exp(m - mn)
            p = exp(st - mn)
            l = alpha * l + jnp.sum(p, axis=0, keepdims=True)
            acc = alpha * acc + _dot(vt, p.astype(BF16))
            new.append((mn, l, acc))
        return tuple(new)

    for k_ref, vt_ref in segs:
        lk = k_ref.shape[1]
        tkk = min(tk, lk)
        n = lk // tkk

        def body(c, carry, k_ref=k_ref, vt_ref=vt_ref, tkk=tkk):
            off = c * tkk if isinstance(c, int) else pl.multiple_of(c * tkk, tkk)
            st_g = scores(k_ref, off, tkk)
            return softmax_step(carry, lambda h: st_g[h // hpg][:, (h % hpg) * tq:(h % hpg + 1) * tq],
                                vt_ref, off, tkk)

        carry = body(0, carry) if n == 1 else lax.fori_loop(0, n, body, carry)
    out_t = jnp.concatenate([acc * (1.0 / l) for _, l, acc in carry], axis=0)
    o_ref[0] = out_t.T.astype(o_ref.dtype)


def _attention(q, segs, dq, name, qscale=None, log2_scores=True):
    (qa, qcol) = q
    b, lq, _ = qa.shape
    wq = N_HEADS * dq
    tq = min(512, lq)
    assert qcol % wq == 0
    in_specs = [pl.BlockSpec((1, tq, wq), lambda bi, i: (bi, i, qcol // wq))]
    args = [qa]
    for (ka, kcol), vt in segs:
        assert kcol % wq == 0 and vt.shape[1] == BRANCH_W and vt.shape[2] == ka.shape[1]
        in_specs.append(pl.BlockSpec((1, ka.shape[1], wq), lambda bi, i, kcol=kcol: (bi, 0, kcol // wq)))
        in_specs.append(pl.BlockSpec((1, BRANCH_W, vt.shape[2]), lambda bi, i: (bi, 0, 0)))
        args += [ka, vt]
    return pl.pallas_call(
        functools.partial(_attn_kernel, nseg=len(segs), dq=dq, tk=ATTN_KEY_CHUNK, qscale=qscale,
                          log2_scores=log2_scores),
        grid=(b, lq // tq),
        in_specs=in_specs,
        out_specs=pl.BlockSpec((1, tq, BRANCH_W), lambda bi, i: (bi, i, 0)),
        out_shape=jax.ShapeDtypeStruct((b, lq, BRANCH_W), BF16),
        compiler_params=_cparams(("parallel", "arbitrary"), 48 * 1024 * 1024),
        name=name,
    )(*args)


def _proj_cols(arr, col, width, rows):
    assert col % width == 0
    return pl.BlockSpec((1, rows, width), lambda *idx: (idx[0], 0, col // width))


def _na_kernel(pat_ref, ks_ref, q_ref, k_ref, v_ref, kc_ref, vc_ref, m_ref, o_ref, *, kw):
    del pat_ref
    g = pl.program_id(1)
    off = pl.multiple_of(ks_ref[g] * GRID_W, GRID_W)
    q = q_ref[0] * jnp.asarray(HEAD_W ** -0.5, BF16)
    kwin = k_ref[0, pl.ds(off, kw), :]
    vwin = v_ref[0, pl.ds(off, kw), :]
    kc = kc_ref[0]
    vc = vc_ref[0]
    out = jnp.zeros((q.shape[0], BRANCH_W), F32)
    for h in range(N_HEADS):
        qh = jnp.where(_head_mask(q.shape, h), q, jnp.zeros_like(q))
        sw = _dot_nt(qh, kwin) + m_ref[0, h]
        sc = _dot_nt(qh, kc)
        mx = jnp.maximum(jnp.max(sw, axis=-1, keepdims=True), jnp.max(sc, axis=-1, keepdims=True))
        pw = jnp.exp(sw - mx)
        pc = jnp.exp(sc - mx)
        l = jnp.sum(pw, axis=-1, keepdims=True) + jnp.sum(pc, axis=-1, keepdims=True)
        o = _dot(pw.astype(BF16), vwin) + _dot(pc.astype(BF16), vc)
        out = out + jnp.where(_head_mask(out.shape, h), o * (1.0 / l), 0.0)
    o_ref[0] = out.astype(o_ref.dtype)


def _na_plan(s):
    rows = s // GRID_W
    wr = min(NA_WIN_R, rows)
    wc = NA_WIN_C
    qr = min(NA_Q_ROWS, rows)
    kwr = min(qr + wr - 1 + (1 if qr + wr - 1 < rows else 0), rows)
    ngrp = rows // qr
    qc = np.arange(GRID_W)[:, None]
    kc = np.arange(GRID_W)[None, :]
    cs = np.clip(qc - wc // 2, 0, GRID_W - wc)
    valid_c = (kc >= cs) & (kc < cs + wc)
    rel_c = np.where(valid_c, kc - qc + (NA_WIN_C - 1), 0)
    assert (valid_c.sum(1) == wc).all()
    pats, pat_ids, ks_rows = [], [], []
    for g in range(ngrp):
        r0 = g * qr
        ks = int(np.clip(r0 - wr // 2, 0, rows - kwr))
        r = (r0 + np.arange(qr))[:, None]
        kr = (ks + np.arange(kwr))[None, :]
        rs = np.clip(r - wr // 2, 0, rows - wr)
        valid_r = (kr >= rs) & (kr < rs + wr)
        assert (valid_r.sum(1) == wr).all()
        rel_r = np.where(valid_r, kr - r + (NA_WIN_R - 1), 0)
        key = (valid_r.tobytes(), rel_r.tobytes())
        for pi, (pk, *_rest) in enumerate(pats):
            if pk == key:
                pat_ids.append(pi)
                break
        else:
            pat_ids.append(len(pats))
            pats.append((key, valid_r, rel_r))
        ks_rows.append(ks)
    valid_r = np.stack([p[1] for p in pats])
    rel_r = np.stack([p[2] for p in pats])
    return (qr, kwr, np.asarray(pat_ids, np.int32), np.asarray(ks_rows, np.int32), valid_r, rel_r, valid_c, rel_c)


def _na_bias_masks(na_bias, valid_r, rel_r, valid_c, rel_c):
    h = na_bias.shape[0]
    npat, qr, kwr = valid_r.shape
    ncol = 2 * NA_WIN_C - 1
    brow = na_bias[:, rel_r, :].astype(F32)
    onehot_c = ((rel_c[None] == np.arange(ncol)[:, None, None]) & valid_c[None]).astype(np.float32)
    m = jnp.einsum('hpqkc,cxy->phqxky', brow, jnp.asarray(onehot_c), precision=HIGHEST)
    valid = valid_r[:, None, :, None, :, None] & valid_c[None, None, None, :, None, :]
    m = jnp.where(valid, m, NEG_BIG)
    return m.reshape(npat, h, qr * GRID_W, kwr * GRID_W)


def _na_attention(projx, projc, na_bias):
    b, s, _ = projx.shape
    lc = projc.shape[1]
    qr, kwr, pat_ids, ks_rows, valid_r, rel_r, valid_c, rel_c = _na_plan(s)
    qb, kw = qr * GRID_W, kwr * GRID_W
    mb = _na_bias_masks(na_bias, valid_r, rel_r, valid_c, rel_c)
    grid_spec = pltpu.PrefetchScalarGridSpec(
        num_scalar_prefetch=2,
        grid=(b, s // qb),
        in_specs=[pl.BlockSpec((1, qb, 256), lambda bi, g, pat, ks: (bi, g, COL_NQ // 256)),
                  pl.BlockSpec((1, s, 256), lambda bi, g, pat, ks: (bi, 0, COL_NK // 256)),
                  pl.BlockSpec((1, s, 256), lambda bi, g, pat, ks: (bi, 0, COL_NV // 256)),
                  pl.BlockSpec((1, lc, 256), lambda bi, g, pat, ks: (bi, 0, COL_NK // 256)),
                  pl.BlockSpec((1, lc, 256), lambda bi, g, pat, ks: (bi, 0, COL_NV // 256)),
                  pl.BlockSpec((1, N_HEADS, qb, kw), lambda bi, g, pat, ks: (pat[g], 0, 0, 0))],
        out_specs=pl.BlockSpec((1, qb, BRANCH_W), lambda bi, g, pat, ks: (bi, g, 0)),
    )
    return pl.pallas_call(
        functools.partial(_na_kernel, kw=kw),
        grid_spec=grid_spec,
        out_shape=jax.ShapeDtypeStruct((b, s, BRANCH_W), BF16),
        compiler_params=_cparams(("parallel", "arbitrary"), 48 * 1024 * 1024),
        name="na_attention",
    )(jnp.asarray(pat_ids), jnp.asarray(ks_rows), projx, projx, projx, projc, projc, mb)


def _ret_kernel(lgs_ref, lgl_ref, gnw_ref, gseg_ref,
                qx, kx, vx, gfx, gbx, qc, kc, vc, gfc, gbc,
                yx_o, yc_o, of_s, ob_s, st_s, dec_s, qk_s, *, need_ctx):
    c = RET_CHUNK
    lc = qc.shape[1]
    sx = qx.shape[1]
    n_col = lax.broadcasted_iota(jnp.int32, (c, c), 0).astype(F32)
    m_row = lax.broadcasted_iota(jnp.int32, (c, c), 1).astype(F32)
    diff = n_col - m_row
    for h in range(N_HEADS):
        dec_s[0, :, h * c:(h + 1) * c] = jnp.where(diff >= 0, jnp.exp(lgs_ref[h] * jnp.maximum(diff, 0.0)), 0.0)
        dec_s[1, :, h * c:(h + 1) * c] = jnp.where(diff <= 0,
                                                   jnp.exp(lgs_ref[N_HEADS + h] * jnp.maximum(-diff, 0.0)), 0.0)
    pos = lax.broadcasted_iota(jnp.int32, (c, BRANCH_W), 0).astype(F32)
    lgf, lgb = lgl_ref[0], lgl_ref[1]
    qk_s[0] = jnp.exp(lgf * (pos + 1.0))
    qk_s[1] = jnp.exp(lgf * (c - 1.0 - pos))
    qk_s[2] = jnp.exp(lgb * (c - pos))
    qk_s[3] = jnp.exp(lgb * pos)
    cd_f = jnp.exp(lgf * float(c))
    cd_b = jnp.exp(lgb * float(c))
    st_s[...] = jnp.zeros_like(st_s)
    rowb = lax.broadcasted_iota(jnp.int32, (BRANCH_W, BRANCH_W), 0) // HEAD_W
    colb = lax.broadcasted_iota(jnp.int32, (BRANCH_W, BRANCH_W), 1) // HEAD_W
    bd_mask = rowb == colb

    def chunk_step(q, k, v, d, cd):
        kk = k * jnp.asarray(HEAD_W ** -0.5, BF16)
        state = st_s[d]
        o = _dot(q, state.astype(BF16)) * qk_s[2 * d]
        kstack = jnp.concatenate([jnp.where(_head_mask(kk.shape, h), kk, jnp.zeros_like(kk))
                                  for h in range(N_HEADS)], axis=0)
        vstack = jnp.concatenate([jnp.where(_head_mask(v.shape, h), v, jnp.zeros_like(v))
                                  for h in range(N_HEADS)], axis=0)
        inner = _dot_nt(q, kstack) * dec_s[d]
        o = o + _dot(inner.astype(BF16), vstack)
        kd = (kk.astype(F32) * qk_s[2 * d + 1]).astype(BF16)
        upd = lax.dot_general(kd, v, (((0,), (0,)), ((), ())), preferred_element_type=F32)
        st_s[d] = state * cd + jnp.where(bd_mask, upd, 0.0)
        return o

    def scan(q_ref, k_ref, v_ref, base, n):
        def body(i, _):
            fo = pl.multiple_of(i * c, c)
            bo = pl.multiple_of((n - 1 - i) * c, c)
            of_s[pl.ds(base + fo, c), :] = chunk_step(
                q_ref[0, pl.ds(fo, c), :], k_ref[0, pl.ds(fo, c), :], v_ref[0, pl.ds(fo, c), :], 0, cd_f)
            ob_s[pl.ds(base + bo, c), :] = chunk_step(
                q_ref[0, pl.ds(bo, c), :], k_ref[0, pl.ds(bo, c), :], v_ref[0, pl.ds(bo, c), :], 1, cd_b)
            return 0
        lax.fori_loop(0, n, body, 0)

    scan(qc, kc, vc, 0, lc // c)
    scan(qx, kx, vx, lc, sx // c)

    gseg = gseg_ref[...]
    gnw = gnw_ref[...]

    def gnorm(o):
        mu = _dot_sel(o, gseg) * (1.0 / HEAD_W)
        dlt = o - mu
        var = _dot_sel(dlt * dlt, gseg) * (1.0 / HEAD_W)
        return dlt * lax.rsqrt(var + EPS) * gnw

    def combine(gf_ref, gb_ref, y_ref, base, n):
        def body(i, _):
            ro = pl.multiple_of(i * c, c)
            y = (gnorm(of_s[pl.ds(base + ro, c), :]) * _silu(gf_ref[0, pl.ds(ro, c), :].astype(F32))
                 + gnorm(ob_s[pl.ds(base + ro, c), :]) * _silu(gb_ref[0, pl.ds(ro, c), :].astype(F32)))
            y_ref[0, pl.ds(ro, c), :] = y.astype(y_ref.dtype)
            return 0
        lax.fori_loop(0, n, body, 0)

    combine(gfx, gbx, yx_o, lc, sx // c)
    if need_ctx:
        combine(gfc, gbc, yc_o, 0, lc // c)
    else:
        yc_o[...] = jnp.zeros_like(yc_o)


def _retention(projx, projc, log_g, gn_w, gseg, need_ctx):
    b, s, _ = projx.shape
    lc = projc.shape[1]
    lgs = log_g.reshape(2 * N_HEADS)
    lgl = jnp.repeat(log_g, HEAD_W, axis=1).reshape(2, 1, BRANCH_W)
    xs = [_proj_cols(projx, COL_RET + 256 * i, 256, s) for i in range(5)]
    cs = [_proj_cols(projc, COL_RET + 256 * i, 256, lc) for i in range(5)]
    c = RET_CHUNK
    yx, yc = pl.pallas_call(
        functools.partial(_ret_kernel, need_ctx=need_ctx),
        grid=(b,),
        in_specs=[pl.BlockSpec(memory_space=pltpu.SMEM),
                  pl.BlockSpec((2, 1, BRANCH_W), lambda bi: (0, 0, 0)),
                  pl.BlockSpec((1, BRANCH_W), lambda bi: (0, 0)),
                  pl.BlockSpec((BRANCH_W, BRANCH_W), lambda bi: (0, 0))] + xs + cs,
        out_specs=[pl.BlockSpec((1, s, BRANCH_W), lambda bi: (bi, 0, 0)),
                   pl.BlockSpec((1, lc, BRANCH_W), lambda bi: (bi, 0, 0))],
        out_shape=[jax.ShapeDtypeStruct((b, s, BRANCH_W), BF16),
                   jax.ShapeDtypeStruct((b, lc, BRANCH_W), BF16)],
        scratch_shapes=[pltpu.VMEM((lc + s, BRANCH_W), F32),
                        pltpu.VMEM((lc + s, BRANCH_W), F32),
                        pltpu.VMEM((2, BRANCH_W, BRANCH_W), F32),
                        pltpu.VMEM((2, c, N_HEADS * c), F32),
                        pltpu.VMEM((4, c, BRANCH_W), F32)],
        compiler_params=_cparams(("parallel",), 48 * 1024 * 1024),
        name="retention",
    )(lgs, lgl, gn_w.reshape(1, BRANCH_W), gseg, *([projx] * 5), *([projc] * 5))
    return yx, yc


def _merge_kernel(oa, ob, oc, od, g0, g1, g2, g3, x_ref, gate_ref, sc_ref, sh_ref, nw_ref, wb_ref, wo_ref,
                  xn_o, h2_o):
    acc = None
    for i, (o, g) in enumerate(((oa, g0), (ob, g1), (oc, g2), (od, g3))):
        t = jax.nn.sigmoid(g[0].astype(F32)) * _dot(o[0], wb_ref[i])
        acc = t if acc is None else acc + t
    y = _dot(acc.astype(BF16), wo_ref[...])
    xn = x_ref[0] + gate_ref[0] * y
    xn_o[0] = xn
    h2_o[0] = _rms(xn) * nw_ref[...] * (1.0 + sc_ref[0]) + sh_ref[0]


def _merge(outs, proj, x, gate, sc2, sh2, n2w, wb, wo):
    b, l, d = x.shape
    tm = min(512, l)
    tok = lambda w: pl.BlockSpec((1, tm, w), lambda bi, i: (bi, i, 0))
    vec = pl.BlockSpec((1, 1, d), lambda bi, i: (bi, 0, 0))
    gates = [pl.BlockSpec((1, tm, d), lambda bi, i, k=k: (bi, i, COL_GATES // d + k)) for k in range(N_BRANCH)]
    return pl.pallas_call(
        _merge_kernel,
        grid=(b, l // tm),
        in_specs=[tok(BRANCH_W)] * 4 + gates + [tok(d), vec, vec, vec,
                                                pl.BlockSpec((1, d), lambda bi, i: (0, 0)),
                                                pl.BlockSpec(wb.shape, lambda bi, i: (0, 0, 0)),
                                                pl.BlockSpec(wo.shape, lambda bi, i: (0, 0))],
        out_specs=[tok(d), tok(d)],
        out_shape=[jax.ShapeDtypeStruct((b, l, d), F32), jax.ShapeDtypeStruct((b, l, d), F32)],
        compiler_params=_cparams(("parallel", "parallel"), 48 * 1024 * 1024),
        name="merge",
    )(*outs, proj, proj, proj, proj, x, gate, sc2, sh2, n2w, wb, wo)


def _sorting_network(n):
    pairs = []
    p = 1
    while p < n:
        k = p
        while k >= 1:
            for j in range(k % p, n - k, 2 * k):
                for i in range(min(k, n - j - k)):
                    if (i + j) // (2 * p) == (i + j + k) // (2 * p):
                        pairs.append((i + j, i + j + k))
            k //= 2
        p *= 2
    return tuple(pairs)


def _topk_rows_slabs(s, k):
    r, t = s.shape
    assert r == 8 * k
    sub = lax.broadcasted_iota(jnp.int32, (8, t), 0).astype(F32)
    vs = [s[8 * j:8 * j + 8] for j in range(k)]
    ix = [sub + float(8 * j) for j in range(k)]
    for p, q in _sorting_network(k):
        va, ia, vb, ib = vs[p], ix[p], vs[q], ix[q]
        swap = (vb > va) | ((vb == va) & (ib < ia))
        vs[p], vs[q] = jnp.maximum(va, vb), jnp.minimum(va, vb)
        ix[p], ix[q] = jnp.where(swap, ib, ia), jnp.where(swap, ia, ib)
    out_row = lax.broadcasted_iota(jnp.int32, (k, t), 0)
    vals = jnp.zeros((k, t), F32)
    idxs = jnp.zeros((k, t), F32)
    for rnd in range(k):
        hv, hi = vs[0], ix[0]
        m = jnp.max(hv, axis=0, keepdims=True)
        idx = jnp.min(jnp.where(hv == m, hi, float(r)), axis=0, keepdims=True)
        vals = jnp.where(out_row == rnd, m, vals)
        idxs = jnp.where(out_row == rnd, idx, idxs)
        won = hi == idx
        for d in range(k - 1 - rnd):
            vs[d] = jnp.where(won, vs[d + 1], vs[d])
            ix[d] = jnp.where(won, ix[d + 1], ix[d])
    return vals, idxs.astype(jnp.int32)


def _select_rows(table, sel, k):
    out = jnp.zeros_like(table)
    for r in range(k):
        out = jnp.where(sel == r, table[r:r + 1, :], out)
    return out


def _split_bf16(x):
    hi = x.astype(BF16)
    return hi, (x - hi.astype(F32)).astype(BF16)


def _peer_joint_topk(s1, s2):
    k = PEER_TOPK
    t = s1.shape[1]
    sub = lax.broadcasted_iota(jnp.int32, (8, t), 0)
    depth = jnp.full((8, t), k // 8, jnp.int32)
    for i in range(6, -1, -1):
        depth = jnp.where(sub == i, k // (i + 1), depth)
    subf = sub.astype(F32)
    lo = [jnp.where(depth > d, s1[0:8] + s2[d:d + 1], -jnp.inf) for d in range(k)]
    hi0 = s1[8:16] + s2[0:1]
    pos_hi = (subf + 8.0) * float(k)
    cnt = jnp.zeros((8, t), F32)
    out_row = lax.broadcasted_iota(jnp.int32, (k, t), 0)
    vals = jnp.zeros((k, t), F32)
    poss = jnp.zeros((k, t), F32)
    big = float(k * k)
    for rnd in range(k):
        pos_lo = subf * float(k) + cnt
        m = jnp.max(jnp.maximum(lo[0], hi0), axis=0, keepdims=True)
        cand = jnp.minimum(jnp.where(lo[0] == m, pos_lo, big), jnp.where(hi0 == m, pos_hi, big))
        pos = jnp.min(cand, axis=0, keepdims=True)
        vals = jnp.where(out_row == rnd, m, vals)
        poss = jnp.where(out_row == rnd, pos, poss)
        won_lo = pos_lo == pos
        for d in range(k - 1 - rnd):
            lo[d] = jnp.where(won_lo, lo[d + 1], lo[d])
        cnt = cnt + jnp.where(won_lo, 1.0, 0.0)
        hi0 = jnp.where(pos_hi == pos, -jnp.inf, hi0)
    p = poss.astype(jnp.int32)
    return vals, p >> 4, p & (k - 1)


def _peer_route_kernel(h_ref, wqh_ref, wql_ref, kh_ref, kl_ref, a_o, b_o, g_o, q_s, a_s, b_s, g_s):
    k = PEER_TOPK
    assert k == 16
    hh, hl = _split_bf16(h_ref[...])
    q_s[...] = _dot(hh, wqh_ref[...]) + (_dot(hl, wqh_ref[...]) + _dot(hh, wql_ref[...]))

    def head(h):
        lo = pl.multiple_of(h * PEER_DK, PEER_DK)
        qh, ql = _split_bf16(q_s[:, pl.ds(lo, PEER_DK)])
        kh, kl = kh_ref[h], kl_ref[h]
        s = _dot_nt(kh, qh) + (_dot_nt(kh, ql) + _dot_nt(kl, qh))
        s1, i1 = _topk_rows_slabs(s[:PEER_N_KEYS], k)
        s2, i2 = _topk_rows_slabs(s[PEER_N_KEYS:], k)
        ts, ri, rj = _peer_joint_topk(s1, s2)
        e = jnp.exp(ts - ts[0:1, :])
        gate = e / jnp.sum(e, axis=0, keepdims=True)
        ro = pl.multiple_of(h * k, k)
        a_s[pl.ds(ro, k), :] = _select_rows(i1, ri, k)
        b_s[pl.ds(ro, k), :] = _select_rows(i2, rj, k)
        g_s[pl.ds(ro, k), :] = gate

    def head_group(i, _):
        for j in range(PEER_HEADS_PER_TRIP):
            head(PEER_HEADS_PER_TRIP * i + j)
        return 0

    lax.fori_loop(0, PEER_HEADS // PEER_HEADS_PER_TRIP, head_group, 0)
    a_o[...] = a_s[...].T
    b_o[...] = b_s[...].T
    g_o[...] = g_s[...].T


def _peer_route(h2, wq_hl, keys_hl):
    n, d = h2.shape
    t = 256
    wq_hi, wq_lo = wq_hl
    k_hi, k_lo = keys_hl
    return pl.pallas_call(
        _peer_route_kernel,
        grid=(n // t,),
        in_specs=[pl.BlockSpec((t, d), lambda i: (i, 0)),
                  pl.BlockSpec(wq_hi.shape, lambda i: (0, 0)),
                  pl.BlockSpec(wq_lo.shape, lambda i: (0, 0)),
                  pl.BlockSpec(k_hi.shape, lambda i: (0, 0, 0)),
                  pl.BlockSpec(k_lo.shape, lambda i: (0, 0, 0))],
        out_specs=[pl.BlockSpec((t, PEER_SLOTS), lambda i: (i, 0))] * 3,
        out_shape=[jax.ShapeDtypeStruct((n, PEER_SLOTS), jnp.int32),
                   jax.ShapeDtypeStruct((n, PEER_SLOTS), jnp.int32),
                   jax.ShapeDtypeStruct((n, PEER_SLOTS), F32)],
        scratch_shapes=[pltpu.VMEM((t, PEER_HEADS * PEER_DK), F32),
                        pltpu.VMEM((PEER_SLOTS, t), jnp.int32),
                        pltpu.VMEM((PEER_SLOTS, t), jnp.int32),
                        pltpu.VMEM((PEER_SLOTS, t), F32)],
        compiler_params=_cparams(("parallel",), 48 * 1024 * 1024),
        name="peer_route",
    )(h2, wq_hi, wq_lo, k_hi, k_lo)


_HI16 = -65536


def _bf16_bits(w):
    return lax.bitcast_convert_type(w, jnp.int32) & _HI16


def _peer_ffn_kernel(h_ref, a_ref, b_ref, g_ref, x_ref, gate_ref, u_ref, v_ref, fw_ref, o_ref, hb_s, w_s, *,
                     ec, unroll, final_norm):
    e = pl.program_id(1)
    t = h_ref.shape[0]
    half = t // 2
    nk = PEER_N_KEYS

    @pl.when(e == 0)
    def _():
        hb_s[...] = h_ref[...].astype(BF16)
        o_ref[...] = jnp.zeros_like(o_ref)
        jio = lax.broadcasted_iota(jnp.int32, (nk, PEER_SLOTS), 0)

        def tile(tt):
            arow = jnp.broadcast_to(a_ref[pl.ds(tt, 1), :], (nk, PEER_SLOTS))
            brow = jnp.broadcast_to(b_ref[pl.ds(tt, 1), :], (nk, PEER_SLOTS))
            grow = jnp.broadcast_to(g_ref[pl.ds(tt, 1), :], (nk, PEER_SLOTS))
            cm = jnp.where(jio == arow, grow, 0.0).astype(BF16)
            bm_t = jnp.where(jio == brow, 1.0, 0.0).T.astype(BF16)
            return _dot(cm, bm_t)

        def build(tb, _):
            for u in range(unroll):
                tt = tb * unroll + u
                word = _bf16_bits(tile(tt + half)) | lax.shift_right_logical(_bf16_bits(tile(tt)), 16)
                w_s[pl.ds(pl.multiple_of(tt * PEER_W_PITCH, 8), nk), :] = word
            return 0

        lax.fori_loop(0, half // unroll, build, 0)

    hid = _dot(hb_s[...], u_ref[0])
    j0 = e * (ec // nk)
    words = jnp.concatenate([w_s[pl.ds(j0 + j, half, stride=PEER_W_PITCH), :] for j in range(ec // nk)], axis=1)
    w_lo = lax.bitcast_convert_type(lax.shift_left(words, 16), F32)
    w_hi = lax.bitcast_convert_type(words & _HI16, F32)
    wc = jnp.concatenate([w_lo, w_hi], axis=0)
    act = 0.5 * hid * (1.0 + lax.erf(hid * SQRT_HALF))
    o_ref[...] += _dot((wc * act).astype(BF16), v_ref[...])

    @pl.when(e == pl.num_programs(1) - 1)
    def _():
        y = x_ref[...] + gate_ref[0] * o_ref[...]
        o_ref[...] = _rms(y) * fw_ref[...] if final_norm else y


def _peer_ffn(h2, a, b_idx, g, x, gate, u_blk, v, l, final_w, final_norm):
    n, d = h2.shape
    ne = v.shape[0]
    neb, _, ec = u_blk.shape
    t = min(512, l)
    unroll = 16
    assert l % t == 0 and ne == PEER_N_KEYS * PEER_N_KEYS and neb * ec == ne and (t // 2) % unroll == 0
    tok = lambda w: pl.BlockSpec((t, w), lambda i, e: (i, 0))
    return pl.pallas_call(
        functools.partial(_peer_ffn_kernel, ec=ec, unroll=unroll, final_norm=final_norm),
        grid=(n // t, neb),
        in_specs=[tok(d), tok(PEER_SLOTS), tok(PEER_SLOTS), tok(PEER_SLOTS), tok(d),
                  pl.BlockSpec((1, 1, d), lambda i, e: ((i * t) // l, 0, 0)),
                  pl.BlockSpec((1, d, ec), lambda i, e: (e, 0, 0)),
                  pl.BlockSpec((ec, d), lambda i, e: (e, 0)),
                  pl.BlockSpec((1, d), lambda i, e: (0, 0))],
        out_specs=tok(d),
        out_shape=jax.ShapeDtypeStruct((n, d), F32),
        scratch_shapes=[pltpu.VMEM((t, d), BF16),
                        pltpu.VMEM((t // 2 * PEER_W_PITCH, PEER_N_KEYS), jnp.int32)],
        compiler_params=_cparams(("parallel", "arbitrary"), VMEM_LIMIT_V7X),
        name="peer_ffn",
    )(h2, a, b_idx, g, x, gate, u_blk, v, final_w.reshape(1, d))


def _layout_w_in(w):
    parts = jnp.split(w, IN_OFFSETS, axis=1)
    z = lambda n: jnp.zeros((w.shape[0], n), w.dtype)
    kpe_blk = jnp.concatenate([z(MLA_NOPE), parts[2], z(MLA_HEAD_PAD - MLA_NOPE - MLA_ROPE)], axis=1)
    return jnp.concatenate([parts[0], parts[1], kpe_blk] + list(parts[3:]), axis=1).astype(BF16)


def _layout_mla(w_uq, w_ukv):
    qh = w_uq.reshape(MLA_Q_LORA, N_HEADS, MLA_NOPE + MLA_ROPE)
    qh = jnp.pad(qh, ((0, 0), (0, 0), (0, MLA_HEAD_PAD - MLA_NOPE - MLA_ROPE)))
    kv = w_ukv.reshape(MLA_KV_LORA, N_HEADS, MLA_NOPE + MLA_V)
    kh = jnp.pad(kv[:, :, :MLA_NOPE], ((0, 0), (0, 0), (0, MLA_HEAD_PAD - MLA_NOPE)))
    vh = kv[:, :, MLA_NOPE:]
    return (qh.reshape(MLA_Q_LORA, -1).astype(BF16), kh.reshape(MLA_KV_LORA, -1).astype(BF16),
            vh.reshape(MLA_KV_LORA, -1).T.astype(BF16))


def _static_mats():
    gseg = np.kron(np.eye(N_HEADS), np.ones((HEAD_W, HEAD_W))).astype(np.float32)
    pm = np.zeros((MLA_HEAD_PAD, MLA_HEAD_PAD), np.float32)
    for dd in range(MLA_ROPE):
        blk, j = dd // 16, dd % 16
        pm[MLA_NOPE + blk * 16 + (j + 8) % 16, MLA_NOPE + dd] = 1.0
    pg = np.zeros((BRANCH_W, BRANCH_W), np.float32)
    for i in range(BRANCH_W):
        off, dd = (i // HEAD_W) * HEAD_W, i % HEAD_W
        blk, j = dd // 32, dd % 32
        pg[off + blk * 32 + (j + 16) % 32, i] = 1.0
    ex = np.zeros((GQA_KV_HEADS * HEAD_W, BRANCH_W), np.float32)
    for i in range(BRANCH_W):
        ex[((i // HEAD_W) // (N_HEADS // GQA_KV_HEADS)) * HEAD_W + i % HEAD_W, i] = 1.0
    return tuple(jnp.asarray(m, dtype=BF16) for m in (gseg, pm, pg, ex, ex.T))


def _rope_half_tables(pos, hf):
    freqs = ROPE_THETA ** (-jnp.arange(hf, dtype=F32) / hf)
    ang = pos[:, None] * freqs[None, :]
    c, s = jnp.cos(ang), jnp.sin(ang)
    return jnp.concatenate([c, c], axis=1), jnp.concatenate([-s, s], axis=1)


def _axial_tables(row, col, dims):
    cr, sr = _rope_half_tables(row, dims // 4)
    cc, sc = _rope_half_tables(col, dims // 4)
    return jnp.concatenate([cr, cc], axis=1), jnp.concatenate([sr, sc], axis=1)


def _rope_tables(s):
    t = jnp.arange(s)
    row, col = (t // GRID_W).astype(F32), (t % GRID_W).astype(F32)
    c32, s32 = _axial_tables(row, col, MLA_ROPE)
    pad = MLA_HEAD_PAD - MLA_NOPE - MLA_ROPE
    cm = jnp.concatenate([jnp.ones((s, MLA_NOPE), F32), c32, jnp.ones((s, pad), F32)], axis=1)
    sm = jnp.concatenate([jnp.zeros((s, MLA_NOPE), F32), s32, jnp.zeros((s, pad), F32)], axis=1)
    c64, s64 = _axial_tables(row, col, HEAD_W)
    return cm, sm, jnp.tile(c64, (1, N_HEADS)), jnp.tile(s64, (1, N_HEADS))


def _split_f32(w):
    hi = w.astype(BF16)
    return hi, (w - hi.astype(F32)).astype(BF16)


def _layout_peer_keys(keys):
    h, _, nk, dh = keys.shape
    z = jnp.zeros((h, nk, dh), keys.dtype)
    top = jnp.concatenate([keys[:, 0], z], axis=2)
    bot = jnp.concatenate([z, keys[:, 1]], axis=2)
    return jnp.concatenate([top, bot], axis=1)


def kernel(x, c, ctx, c_ctx, mod_w, mod_b, norm1_w, norm2_w, w_in, mla_q_norm, mla_w_uq, mla_kv_norm, mla_w_ukv, gqa_q_norm, gqa_k_norm, na_bias, ret_decay_logit, ret_gn_w, w_branch, w_out, peer_w_q, peer_keys, peer_u, peer_v, final_norm_w):
    b, s, d = x.shape
    lc = ctx.shape[1]
    depth = mod_w.shape[0]
    assert d == D_MODEL and s % (GRID_W * NA_Q_ROWS) == 0 and s % 256 == 0 and lc % 256 == 0

    rows = -(-(b + 1) // 8) * 8
    cc = jnp.zeros((rows, d), F32).at[:b].set(c).at[b].set(c_ctx)
    mod = _modulation(cc, mod_w, mod_b)

    gseg, pm, pg, ex, ex_t = _static_mats()
    tables = _rope_tables(s)

    for l in range(depth):
        need_ctx = l < depth - 1
        mx = mod[l, :b].reshape(b, 1, 6, d)
        mc = jnp.broadcast_to(mod[l, b].reshape(1, 1, 6, d), (b, 1, 6, d))
        sh1x, sc1x, g1x, sh2x, sc2x, g2x = (mx[:, :, i] for i in range(6))
        sh1c, sc1c, g1c, sh2c, sc2c, g2c = (mc[:, :, i] for i in range(6))

        w_in_l = _layout_w_in(w_in[l])
        wuq, wk, wv = _layout_mla(mla_w_uq[l], mla_w_ukv[l])
        consts = (mla_q_norm[l].reshape(1, -1), wuq, mla_kv_norm[l].reshape(1, -1), wk, wv,
                  jnp.tile(gqa_q_norm[l], N_HEADS).reshape(1, -1),
                  jnp.tile(gqa_k_norm[l], GQA_KV_HEADS).reshape(1, -1), gseg, pm, pg, ex, ex_t)
        n1w = norm1_w[l].reshape(1, d)
        n2w = norm2_w[l].reshape(1, d)
        wb = w_branch[l].astype(BF16)
        wo = w_out[l].astype(BF16)
        keys_hl = _split_f32(_layout_peer_keys(peer_keys[l]))
        wq_hl = _split_f32(peer_w_q[l])
        ne = peer_u.shape[1]
        u_blk = jnp.swapaxes(peer_u[l].astype(BF16).reshape(ne // PEER_EXPERT_CHUNK, PEER_EXPERT_CHUNK, d), 1, 2)
        v_b = peer_v[l].astype(BF16)
        log_g = jax.nn.log_sigmoid(ret_decay_logit[l].astype(F32))

        projx = _inproj(x, n1w, sc1x, sh1x, w_in_l)
        projc = _inproj(ctx, n1w, sc1c, sh1c, w_in_l)
        qmx, kmx, vmx, qgx, kgx, vgx = _prep(projx, consts, tables)
        qmc, kmc, vmc, qgc, kgc, vgc = _prep(projc, consts, None)

        oa = _attention((qmx, 0), [((kmc, 0), vmc), ((kmx, 0), vmx)], MLA_HEAD_PAD, "attn_mla")
        ob = _attention((qgx, 0), [((kgc, 0), vgc), ((kgx, 0), vgx)], HEAD_W, "attn_gqa")
        oc = _na_attention(projx, projc, na_bias[l])
        od, od_c = _retention(projx, projc, log_g, ret_gn_w[l], gseg, need_ctx)

        x, h2x = _merge((oa, ob, oc, od), projx, x, g1x, sc2x, sh2x, n2w, wb, wo)
        ax, bx, gx = _peer_route(h2x.reshape(b * s, d), wq_hl, keys_hl)
        x = _peer_ffn(h2x.reshape(b * s, d), ax, bx, gx, x.reshape(b * s, d), g2x, u_blk, v_b, s,
                      final_norm_w, final_norm=(l == depth - 1)).reshape(b, s, d)

        if need_ctx:
            ca = _attention((qmc, 0), [((kmc, 0), vmc)], MLA_HEAD_PAD, "attn_mla_ctx")
            cb = _attention((qgc, 0), [((kgc, 0), vgc)], HEAD_W, "attn_gqa_ctx")
            nv_t = jnp.swapaxes(projc[:, :, COL_NV:COL_NV + BRANCH_W], 1, 2)
            ccx = _attention((projc, COL_NQ), [((projc, COL_NK), nv_t)], HEAD_W, "attn_na_ctx",
                             qscale=HEAD_W ** -0.5, log2_scores=False)
            ctx, h2c = _merge((ca, cb, ccx, od_c), projc, ctx, g1c, sc2c, sh2c, n2w, wb, wo)
            ac, bc, gc = _peer_route(h2c.reshape(b * lc, d), wq_hl, keys_hl)
            ctx = _peer_ffn(h2c.reshape(b * lc, d), ac, bc, gc, ctx.reshape(b * lc, d), g2c, u_blk, v_b,
                            lc, final_norm_w, final_norm=False).reshape(b, lc, d)

    return x
```

```python
import functools

import numpy as np
import jax
import jax.numpy as jnp
from jax import lax
from jax.experimental import pallas as pl
from jax.experimental.pallas import tpu as pltpu

F32 = jnp.float32
BF16 = jnp.bfloat16
HIGHEST = lax.Precision.HIGHEST

D_MODEL = 1024
GRID_W = 64
ROPE_THETA = 10000.0
EPS = 1e-6
N_HEADS = 4
HEAD_W = 64
BRANCH_W = N_HEADS * HEAD_W
MLA_NOPE, MLA_ROPE, MLA_V = 64, 32, 64
MLA_Q_LORA, MLA_KV_LORA = 256, 128
MLA_SCALE = (MLA_NOPE + MLA_ROPE) ** -0.5
MLA_HEAD_PAD = 128
GQA_KV_HEADS = 2
NA_WIN_R, NA_WIN_C = 8, 16
NA_Q_ROWS = 4
ATTN_KEY_CHUNK = 1024
RET_CHUNK = 128
N_BRANCH = 4
PEER_HEADS, PEER_N_KEYS, PEER_TOPK, PEER_DK = 8, 128, 16, 128
PEER_SLOTS = PEER_HEADS * PEER_TOPK
PEER_W_PITCH = PEER_N_KEYS + 8
PEER_HEADS_PER_TRIP = 4
PEER_EXPERT_CHUNK = 1024
SQRT_HALF = 0.7071067811865476
LOG2E = 1.4426950408889634
NEG_BIG = -1e30

IN_SIZES = (256, 128, 32, 256, 128, 128, 256, 256, 256, 256, 256, 256, 256, 256, 4096)
IN_OFFSETS = tuple(int(v) for v in np.cumsum(IN_SIZES)[:-1])
PROJ_COLS = 7168
COL_NQ, COL_NK, COL_NV = 1024, 1280, 1536
COL_RET = 1792
COL_GATES = 3072

VMEM_LIMIT_V7X = 56 * 1024 * 1024


def _cparams(sem, vmem=None):
    return pltpu.CompilerParams(dimension_semantics=sem, vmem_limit_bytes=vmem)


def _dot(a, b):
    return jnp.dot(a, b, preferred_element_type=F32)


def _dot_hi(a, b):
    return jnp.dot(a, b, preferred_element_type=F32, precision=HIGHEST)


def _dot_nt(a, b):
    return lax.dot_general(a, b, (((1,), (1,)), ((), ())), preferred_element_type=F32)


def _dot_sel(x, sel):
    hi = x.astype(BF16)
    r1 = x - hi.astype(F32)
    mid = r1.astype(BF16)
    lo = (r1 - mid.astype(F32)).astype(BF16)
    return _dot(hi, sel) + (_dot(mid, sel) + _dot(lo, sel))


def _rms(x):
    return x * lax.rsqrt(jnp.mean(x * x, axis=-1, keepdims=True) + EPS)


def _silu(x):
    return x * jax.nn.sigmoid(x)


def _head_mask(shape, h, width=HEAD_W):
    lane = lax.broadcasted_iota(jnp.int32, shape, len(shape) - 1)
    lo = h * width
    return (lane >= lo) & (lane < lo + width)


def _mod_kernel(c_ref, w_ref, b_ref, o_ref):
    o_ref[0] = _dot_hi(_silu(c_ref[...]), w_ref[0]) + b_ref[0]


def _modulation(cc, mod_w, mod_b):
    depth, d, n = mod_w.shape
    rows = cc.shape[0]
    tn = 1536
    return pl.pallas_call(
        _mod_kernel,
        grid=(depth, n // tn),
        in_specs=[pl.BlockSpec((rows, d), lambda l, j: (0, 0)),
                  pl.BlockSpec((1, d, tn), lambda l, j: (l, 0, j)),
                  pl.BlockSpec((1, 1, tn), lambda l, j: (l, 0, j))],
        out_specs=pl.BlockSpec((1, rows, tn), lambda l, j: (l, 0, j)),
        out_shape=jax.ShapeDtypeStruct((depth, rows, n), F32),
        compiler_params=_cparams(("parallel", "parallel"), 40 * 1024 * 1024),
        name="modulation",
    )(cc, mod_w, mod_b.reshape(depth, 1, n))


def _inproj_kernel(x_ref, nw_ref, sc_ref, sh_ref, w_ref, o_ref, h_scr):
    @pl.when(pl.program_id(2) == 0)
    def _():
        h = _rms(x_ref[0]) * nw_ref[...] * (1.0 + sc_ref[0]) + sh_ref[0]
        h_scr[...] = h.astype(BF16)

    o_ref[0] = _dot(h_scr[...], w_ref[...]).astype(o_ref.dtype)


def _inproj(x, nw, sc, sh, w):
    b, l, d = x.shape
    n = w.shape[1]
    tm = min(1024, l)
    tn = 1792
    return pl.pallas_call(
        _inproj_kernel,
        grid=(b, l // tm, n // tn),
        in_specs=[pl.BlockSpec((1, tm, d), lambda bi, i, j: (bi, i, 0)),
                  pl.BlockSpec((1, d), lambda bi, i, j: (0, 0)),
                  pl.BlockSpec((1, 1, d), lambda bi, i, j: (bi, 0, 0)),
                  pl.BlockSpec((1, 1, d), lambda bi, i, j: (bi, 0, 0)),
                  pl.BlockSpec((d, tn), lambda bi, i, j: (0, j))],
        out_specs=pl.BlockSpec((1, tm, tn), lambda bi, i, j: (bi, i, j)),
        out_shape=jax.ShapeDtypeStruct((b, l, n), BF16),
        scratch_shapes=[pltpu.VMEM((tm, d), BF16)],
        compiler_params=_cparams(("parallel", "parallel", "arbitrary"), 40 * 1024 * 1024),
        name="inproj",
    )(x, nw, sc, sh, w)


def _prep_kernel(*refs, use_rope):
    (p_ref, qn_ref, wuq_ref, kvn_ref, wk_ref, wv_ref, gqn_ref, gkn_ref, gseg_ref, pm_ref, pg_ref,
     e_ref, et_ref) = refs[:13]
    if use_rope:
        cm_ref, sm_ref, cg_ref, sg_ref = refs[13:17]
        outs = refs[17:]
    else:
        outs = refs[13:]
    qm_o, km_o, vm_o, qg_o, kg_o, vg_o = outs

    pb = p_ref[0]
    cq = pb[:, 0:256].astype(F32)
    ckv = pb[:, 256:384].astype(F32)
    kpe = pb[:, 384:512].astype(F32)
    gq = pb[:, 512:768].astype(F32)
    gk = pb[:, 768:896].astype(F32)
    gv = pb[:, 896:1024]

    cqn = (_rms(cq) * qn_ref[...]).astype(BF16)
    qa = _dot(cqn, wuq_ref[...])
    ckn = (_rms(ckv) * kvn_ref[...]).astype(BF16)
    kn = _dot(ckn, wk_ref[...])
    vm_t = _dot_nt(wv_ref[...], ckn)
    if use_rope:
        cm, sm = cm_ref[...], sm_ref[...]
        pm = pm_ref[...]

        def rope_m(t):
            return t * cm + _dot_sel(t, pm) * sm

        qa = jnp.concatenate([rope_m(qa[:, h * 128:(h + 1) * 128]) for h in range(N_HEADS)], axis=1)
        kpe = rope_m(kpe)
    km = kn + jnp.concatenate([kpe] * N_HEADS, axis=1)
    qm_o[0] = (qa * (MLA_SCALE * LOG2E)).astype(BF16)
    km_o[0] = km.astype(BF16)
    vm_o[0] = vm_t.astype(BF16)

    gseg = gseg_ref[...]
    gqn = gq * lax.rsqrt(_dot_sel(gq * gq, gseg) * (1.0 / HEAD_W) + EPS) * gqn_ref[...]
    gkn = gk * lax.rsqrt(_dot_sel(gk * gk, gseg[:128, :128]) * (1.0 / HEAD_W) + EPS) * gkn_ref[...]
    if use_rope:
        cg, sg = cg_ref[...], sg_ref[...]
        pg = pg_ref[...]
        gqn = gqn * cg + _dot_sel(gqn, pg) * sg
        gkn = gkn * cg[:, :128] + _dot_sel(gkn, pg[:128, :128]) * sg[:, :128]
    qg_o[0] = (gqn * (HEAD_W ** -0.5 * LOG2E)).astype(BF16)
    e = e_ref[...]
    kg_o[0] = _dot(gkn.astype(BF16), e).astype(BF16)
    vg_o[0] = _dot_nt(et_ref[...], gv).astype(BF16)


def _prep(proj, consts, tables):
    b, l, _ = proj.shape
    tm = min(512, l)
    use_rope = tables is not None
    full = lambda a: pl.BlockSpec(a.shape, lambda bi, i: (0,) * a.ndim)
    in_specs = [pl.BlockSpec((1, tm, 1024), lambda bi, i: (bi, i, 0))] + [full(a) for a in consts]
    args = [proj] + list(consts)
    if use_rope:
        in_specs += [pl.BlockSpec((tm, t.shape[1]), lambda bi, i: (i, 0)) for t in tables]
        args += list(tables)
    tok = lambda w: (pl.BlockSpec((1, tm, w), lambda bi, i: (bi, i, 0)), jax.ShapeDtypeStruct((b, l, w), BF16))
    tr = (pl.BlockSpec((1, BRANCH_W, tm), lambda bi, i: (bi, 0, i)), jax.ShapeDtypeStruct((b, BRANCH_W, l), BF16))
    outs = (tok(512), tok(512), tr, tok(256), tok(256), tr)
    return pl.pallas_call(
        functools.partial(_prep_kernel, use_rope=use_rope),
        grid=(b, l // tm),
        in_specs=in_specs,
        out_specs=[o[0] for o in outs],
        out_shape=[o[1] for o in outs],
        compiler_params=_cparams(("parallel", "parallel"), 40 * 1024 * 1024),
        name="prep_rope" if use_rope else "prep",
    )(*args)


def _attn_kernel(*refs, nseg, dq, tk, qscale, log2_scores):
    exp = jnp.exp2 if log2_scores else jnp.exp
    q_ref = refs[0]
    segs = [(refs[1 + 2 * i], refs[2 + 2 * i]) for i in range(nseg)]
    o_ref = refs[1 + 2 * nseg]
    tq = q_ref.shape[1]
    gw = 256
    hpg = gw // dq
    qstacks = []
    for g in range(N_HEADS // hpg):
        qg = q_ref[0, :, g * gw:(g + 1) * gw]
        if qscale is not None:
            qg = qg * jnp.asarray(qscale, BF16)
        qstacks.append(jnp.concatenate([jnp.where(_head_mask(qg.shape, j, dq), qg, jnp.zeros_like(qg))
                                        for j in range(hpg)], axis=0))
    ones_rows = 16
    carry = tuple((jnp.full((1, tq), -jnp.inf, F32), jnp.zeros((HEAD_W + ones_rows, tq), F32))
                  for _ in range(N_HEADS))

    def scores(k_ref, off, tkk):
        return [_dot_nt(k_ref[0, pl.ds(off, tkk), g * gw:(g + 1) * gw], qstacks[g])
                for g in range(N_HEADS // hpg)]

    def softmax_step(carry, head_scores, vt_ref, off, tkk):
        new = []
        ones = jnp.ones((ones_rows, tkk), BF16)
        for h in range(N_HEADS):
            m, acc = carry[h]
            vt = jnp.concatenate([vt_ref[0, h * HEAD_W:(h + 1) * HEAD_W, pl.ds(off, tkk)], ones], axis=0)
            st = head_scores(h)
            mn = jnp.maximum(m, jnp.max(st, axis=0, keepdims=True))
            acc = exp(m - mn) * acc + _dot(vt, exp(st - mn).astype(BF16))
            new.append((mn, acc))
        return tuple(new)

    for k_ref, vt_ref in segs:
        lk = k_ref.shape[1]
        tkk = min(tk, lk)
        n = lk // tkk

        def body(c, carry, k_ref=k_ref, vt_ref=vt_ref, tkk=tkk):
            off = c * tkk if isinstance(c, int) else pl.multiple_of(c * tkk, tkk)
            st_g = scores(k_ref, off, tkk)
            return softmax_step(carry, lambda h: st_g[h // hpg][:, (h % hpg) * tq:(h % hpg + 1) * tq],
                                vt_ref, off, tkk)

        carry = body(0, carry) if n == 1 else lax.fori_loop(0, n, body, carry)
    out_t = jnp.concatenate([acc[:HEAD_W] * (1.0 / acc[HEAD_W:HEAD_W + 1]) for _, acc in carry], axis=0)
    o_ref[0] = out_t.T.astype(o_ref.dtype)


def _attention(q, segs, dq, name, qscale=None, log2_scores=True):
    (qa, qcol) = q
    b, lq, _ = qa.shape
    wq = N_HEADS * dq
    tq = min(512, lq)
    assert qcol % wq == 0
    in_specs = [pl.BlockSpec((1, tq, wq), lambda bi, i: (bi, i, qcol // wq))]
    args = [qa]
    for (ka, kcol), vt in segs:
        assert kcol % wq == 0 and vt.shape[1] == BRANCH_W and vt.shape[2] == ka.shape[1]
        in_specs.append(pl.BlockSpec((1, ka.shape[1], wq), lambda bi, i, kcol=kcol: (bi, 0, kcol // wq)))
        in_specs.append(pl.BlockSpec((1, BRANCH_W, vt.shape[2]), lambda bi, i: (bi, 0, 0)))
        args += [ka, vt]
    return pl.pallas_call(
        functools.partial(_attn_kernel, nseg=len(segs), dq=dq, tk=ATTN_KEY_CHUNK, qscale=qscale,
                          log2_scores=log2_scores),
        grid=(b, lq // tq),
        in_specs=in_specs,
        out_specs=pl.BlockSpec((1, tq, BRANCH_W), lambda bi, i: (bi, i, 0)),
        out_shape=jax.ShapeDtypeStruct((b, lq, BRANCH_W), BF16),
        compiler_params=_cparams(("parallel", "arbitrary"), 48 * 1024 * 1024),
        name=name,
    )(*args)


def _proj_cols(arr, col, width, rows):
    assert col % width == 0
    return pl.BlockSpec((1, rows, width), lambda *idx: (idx[0], 0, col // width))


def _na_kernel(pat_ref, ks_ref, q_ref, k_ref, v_ref, kc_ref, vc_ref, m_ref, o_ref, *, kw):
    del pat_ref
    g = pl.program_id(1)
    off = pl.multiple_of(ks_ref[g] * GRID_W, GRID_W)
    q = q_ref[0] * jnp.asarray(HEAD_W ** -0.5, BF16)
    kwin = k_ref[0, pl.ds(off, kw), :]
    vwin = v_ref[0, pl.ds(off, kw), :]
    kc = kc_ref[0]
    vc = vc_ref[0]
    out = jnp.zeros((q.shape[0], BRANCH_W), F32)
    for h in range(N_HEADS):
        qh = jnp.where(_head_mask(q.shape, h), q, jnp.zeros_like(q))
        sw = _dot_nt(qh, kwin) + m_ref[0, h]
        sc = _dot_nt(qh, kc)
        mx = jnp.maximum(jnp.max(sw, axis=-1, keepdims=True), jnp.max(sc, axis=-1, keepdims=True))
        pw = jnp.exp(sw - mx)
        pc = jnp.exp(sc - mx)
        l = jnp.sum(pw, axis=-1, keepdims=True) + jnp.sum(pc, axis=-1, keepdims=True)
        o = _dot(pw.astype(BF16), vwin) + _dot(pc.astype(BF16), vc)
        out = out + jnp.where(_head_mask(out.shape, h), o * (1.0 / l), 0.0)
    o_ref[0] = out.astype(o_ref.dtype)


def _na_plan(s):
    rows = s // GRID_W
    wr = min(NA_WIN_R, rows)
    wc = NA_WIN_C
    qr = min(NA_Q_ROWS, rows)
    kwr = min(qr + wr - 1 + (1 if qr + wr - 1 < rows else 0), rows)
    ngrp = rows // qr
    qc = np.arange(GRID_W)[:, None]
    kc = np.arange(GRID_W)[None, :]
    cs = np.clip(qc - wc // 2, 0, GRID_W - wc)
    valid_c = (kc >= cs) & (kc < cs + wc)
    rel_c = np.where(valid_c, kc - qc + (NA_WIN_C - 1), 0)
    assert (valid_c.sum(1) == wc).all()
    pats, pat_ids, ks_rows = [], [], []
    for g in range(ngrp):
        r0 = g * qr
        ks = int(np.clip(r0 - wr // 2, 0, rows - kwr))
        r = (r0 + np.arange(qr))[:, None]
        kr = (ks + np.arange(kwr))[None, :]
        rs = np.clip(r - wr // 2, 0, rows - wr)
        valid_r = (kr >= rs) & (kr < rs + wr)
        assert (valid_r.sum(1) == wr).all()
        rel_r = np.where(valid_r, kr - r + (NA_WIN_R - 1), 0)
        key = (valid_r.tobytes(), rel_r.tobytes())
        for pi, (pk, *_rest) in enumerate(pats):
            if pk == key:
                pat_ids.append(pi)
                break
        else:
            pat_ids.append(len(pats))
            pats.append((key, valid_r, rel_r))
        ks_rows.append(ks)
    valid_r = np.stack([p[1] for p in pats])
    rel_r = np.stack([p[2] for p in pats])
    return (qr, kwr, np.asarray(pat_ids, np.int32), np.asarray(ks_rows, np.int32), valid_r, rel_r, valid_c, rel_c)


def _na_bias_masks(na_bias, valid_r, rel_r, valid_c, rel_c):
    h = na_bias.shape[0]
    npat, qr, kwr = valid_r.shape
    ncol = 2 * NA_WIN_C - 1
    brow = na_bias[:, rel_r, :].astype(F32)
    onehot_c = ((rel_c[None] == np.arange(ncol)[:, None, None]) & valid_c[None]).astype(np.float32)
    m = jnp.einsum('hpqkc,cxy->phqxky', brow, jnp.asarray(onehot_c), precision=HIGHEST)
    valid = valid_r[:, None, :, None, :, None] & valid_c[None, None, None, :, None, :]
    m = jnp.where(valid, m, NEG_BIG)
    return m.reshape(npat, h, qr * GRID_W, kwr * GRID_W)


def _na_attention(projx, projc, na_bias):
    b, s, _ = projx.shape
    lc = projc.shape[1]
    qr, kwr, pat_ids, ks_rows, valid_r, rel_r, valid_c, rel_c = _na_plan(s)
    qb, kw = qr * GRID_W, kwr * GRID_W
    mb = _na_bias_masks(na_bias, valid_r, rel_r, valid_c, rel_c)
    grid_spec = pltpu.PrefetchScalarGridSpec(
        num_scalar_prefetch=2,
        grid=(b, s // qb),
        in_specs=[pl.BlockSpec((1, qb, 256), lambda bi, g, pat, ks: (bi, g, COL_NQ // 256)),
                  pl.BlockSpec((1, s, 256), lambda bi, g, pat, ks: (bi, 0, COL_NK // 256)),
                  pl.BlockSpec((1, s, 256), lambda bi, g, pat, ks: (bi, 0, COL_NV // 256)),
                  pl.BlockSpec((1, lc, 256), lambda bi, g, pat, ks: (bi, 0, COL_NK // 256)),
                  pl.BlockSpec((1, lc, 256), lambda bi, g, pat, ks: (bi, 0, COL_NV // 256)),
                  pl.BlockSpec((1, N_HEADS, qb, kw), lambda bi, g, pat, ks: (pat[g], 0, 0, 0))],
        out_specs=pl.BlockSpec((1, qb, BRANCH_W), lambda bi, g, pat, ks: (bi, g, 0)),
    )
    return pl.pallas_call(
        functools.partial(_na_kernel, kw=kw),
        grid_spec=grid_spec,
        out_shape=jax.ShapeDtypeStruct((b, s, BRANCH_W), BF16),
        compiler_params=_cparams(("parallel", "arbitrary"), 48 * 1024 * 1024),
        name="na_attention",
    )(jnp.asarray(pat_ids), jnp.asarray(ks_rows), projx, projx, projx, projc, projc, mb)


def _ret_kernel(lgs_ref, lgl_ref, gnw_ref, gseg_ref,
                qx, kx, vx, gfx, gbx, qc, kc, vc, gfc, gbc,
                yx_o, yc_o, of_s, ob_s, st_s, dec_s, qk_s, *, need_ctx):
    c = RET_CHUNK
    lc = qc.shape[1]
    sx = qx.shape[1]
    n_col = lax.broadcasted_iota(jnp.int32, (c, c), 0).astype(F32)
    m_row = lax.broadcasted_iota(jnp.int32, (c, c), 1).astype(F32)
    diff = n_col - m_row
    for h in range(N_HEADS):
        dec_s[0, :, h * c:(h + 1) * c] = jnp.where(diff >= 0, jnp.exp(lgs_ref[h] * jnp.maximum(diff, 0.0)), 0.0)
        dec_s[1, :, h * c:(h + 1) * c] = jnp.where(diff <= 0,
                                                   jnp.exp(lgs_ref[N_HEADS + h] * jnp.maximum(-diff, 0.0)), 0.0)
    pos = lax.broadcasted_iota(jnp.int32, (c, BRANCH_W), 0).astype(F32)
    lgf, lgb = lgl_ref[0], lgl_ref[1]
    qk_s[0] = jnp.exp(lgf * (pos + 1.0))
    qk_s[1] = jnp.exp(lgf * (c - 1.0 - pos))
    qk_s[2] = jnp.exp(lgb * (c - pos))
    qk_s[3] = jnp.exp(lgb * pos)
    cd_f = jnp.exp(lgf * float(c))
    cd_b = jnp.exp(lgb * float(c))
    st_s[...] = jnp.zeros_like(st_s)
    rowb = lax.broadcasted_iota(jnp.int32, (BRANCH_W, BRANCH_W), 0) // HEAD_W
    colb = lax.broadcasted_iota(jnp.int32, (BRANCH_W, BRANCH_W), 1) // HEAD_W
    bd_mask = rowb == colb

    def chunk_step(q, k, v, d, cd):
        kk = k * jnp.asarray(HEAD_W ** -0.5, BF16)
        state = st_s[d]
        o = _dot(q, state.astype(BF16)) * qk_s[2 * d]
        kstack = jnp.concatenate([jnp.where(_head_mask(kk.shape, h), kk, jnp.zeros_like(kk))
                                  for h in range(N_HEADS)], axis=0)
        vstack = jnp.concatenate([jnp.where(_head_mask(v.shape, h), v, jnp.zeros_like(v))
                                  for h in range(N_HEADS)], axis=0)
        inner = _dot_nt(q, kstack) * dec_s[d]
        o = o + _dot(inner.astype(BF16), vstack)
        kd = (kk.astype(F32) * qk_s[2 * d + 1]).astype(BF16)
        upd = lax.dot_general(kd, v, (((0,), (0,)), ((), ())), preferred_element_type=F32)
        st_s[d] = state * cd + jnp.where(bd_mask, upd, 0.0)
        return o

    def scan(q_ref, k_ref, v_ref, base, n):
        def body(i, _):
            fo = pl.multiple_of(i * c, c)
            bo = pl.multiple_of((n - 1 - i) * c, c)
            of_s[pl.ds(base + fo, c), :] = chunk_step(
                q_ref[0, pl.ds(fo, c), :], k_ref[0, pl.ds(fo, c), :], v_ref[0, pl.ds(fo, c), :], 0, cd_f)
            ob_s[pl.ds(base + bo, c), :] = chunk_step(
                q_ref[0, pl.ds(bo, c), :], k_ref[0, pl.ds(bo, c), :], v_ref[0, pl.ds(bo, c), :], 1, cd_b)
            return 0
        lax.fori_loop(0, n, body, 0)

    scan(qc, kc, vc, 0, lc // c)
    scan(qx, kx, vx, lc, sx // c)

    gseg = gseg_ref[...]
    gnw = gnw_ref[...]

    def gnorm(o):
        mu = _dot_sel(o, gseg) * (1.0 / HEAD_W)
        dlt = o - mu
        var = _dot_sel(dlt * dlt, gseg) * (1.0 / HEAD_W)
        return dlt * lax.rsqrt(var + EPS) * gnw

    def combine(gf_ref, gb_ref, y_ref, base, n):
        def body(i, _):
            ro = pl.multiple_of(i * c, c)
            y = (gnorm(of_s[pl.ds(base + ro, c), :]) * _silu(gf_ref[0, pl.ds(ro, c), :].astype(F32))
                 + gnorm(ob_s[pl.ds(base + ro, c), :]) * _silu(gb_ref[0, pl.ds(ro, c), :].astype(F32)))
            y_ref[0, pl.ds(ro, c), :] = y.astype(y_ref.dtype)
            return 0
        lax.fori_loop(0, n, body, 0)

    combine(gfx, gbx, yx_o, lc, sx // c)
    if need_ctx:
        combine(gfc, gbc, yc_o, 0, lc // c)
    else:
        yc_o[...] = jnp.zeros_like(yc_o)


def _retention(projx, projc, log_g, gn_w, gseg, need_ctx):
    b, s, _ = projx.shape
    lc = projc.shape[1]
    lgs = log_g.reshape(2 * N_HEADS)
    lgl = jnp.repeat(log_g, HEAD_W, axis=1).reshape(2, 1, BRANCH_W)
    xs = [_proj_cols(projx, COL_RET + 256 * i, 256, s) for i in range(5)]
    cs = [_proj_cols(projc, COL_RET + 256 * i, 256, lc) for i in range(5)]
    c = RET_CHUNK
    yx, yc = pl.pallas_call(
        functools.partial(_ret_kernel, need_ctx=need_ctx),
        grid=(b,),
        in_specs=[pl.BlockSpec(memory_space=pltpu.SMEM),
                  pl.BlockSpec((2, 1, BRANCH_W), lambda bi: (0, 0, 0)),
                  pl.BlockSpec((1, BRANCH_W), lambda bi: (0, 0)),
                  pl.BlockSpec((BRANCH_W, BRANCH_W), lambda bi: (0, 0))] + xs + cs,
        out_specs=[pl.BlockSpec((1, s, BRANCH_W), lambda bi: (bi, 0, 0)),
                   pl.BlockSpec((1, lc, BRANCH_W), lambda bi: (bi, 0, 0))],
        out_shape=[jax.ShapeDtypeStruct((b, s, BRANCH_W), BF16),
                   jax.ShapeDtypeStruct((b, lc, BRANCH_W), BF16)],
        scratch_shapes=[pltpu.VMEM((lc + s, BRANCH_W), F32),
                        pltpu.VMEM((lc + s, BRANCH_W), F32),
                        pltpu.VMEM((2, BRANCH_W, BRANCH_W), F32),
                        pltpu.VMEM((2, c, N_HEADS * c), F32),
                        pltpu.VMEM((4, c, BRANCH_W), F32)],
        compiler_params=_cparams(("parallel",), 48 * 1024 * 1024),
        name="retention",
    )(lgs, lgl, gn_w.reshape(1, BRANCH_W), gseg, *([projx] * 5), *([projc] * 5))
    return yx, yc


def _merge_kernel(oa, ob, oc, od, g0, g1, g2, g3, x_ref, gate_ref, sc_ref, sh_ref, nw_ref, wb_ref, wo_ref,
                  xn_o, h2_o):
    acc = None
    for i, (o, g) in enumerate(((oa, g0), (ob, g1), (oc, g2), (od, g3))):
        t = jax.nn.sigmoid(g[0].astype(F32)) * _dot(o[0], wb_ref[i])
        acc = t if acc is None else acc + t
    y = _dot(acc.astype(BF16), wo_ref[...])
    xn = x_ref[0] + gate_ref[0] * y
    xn_o[0] = xn
    h2_o[0] = _rms(xn) * nw_ref[...] * (1.0 + sc_ref[0]) + sh_ref[0]


def _merge(outs, proj, x, gate, sc2, sh2, n2w, wb, wo):
    b, l, d = x.shape
    tm = min(512, l)
    tok = lambda w: pl.BlockSpec((1, tm, w), lambda bi, i: (bi, i, 0))
    vec = pl.BlockSpec((1, 1, d), lambda bi, i: (bi, 0, 0))
    gates = [pl.BlockSpec((1, tm, d), lambda bi, i, k=k: (bi, i, COL_GATES // d + k)) for k in range(N_BRANCH)]
    return pl.pallas_call(
        _merge_kernel,
        grid=(b, l // tm),
        in_specs=[tok(BRANCH_W)] * 4 + gates + [tok(d), vec, vec, vec,
                                                pl.BlockSpec((1, d), lambda bi, i: (0, 0)),
                                                pl.BlockSpec(wb.shape, lambda bi, i: (0, 0, 0)),
                                                pl.BlockSpec(wo.shape, lambda bi, i: (0, 0))],
        out_specs=[tok(d), tok(d)],
        out_shape=[jax.ShapeDtypeStruct((b, l, d), F32), jax.ShapeDtypeStruct((b, l, d), F32)],
        compiler_params=_cparams(("parallel", "parallel"), 48 * 1024 * 1024),
        name="merge",
    )(*outs, proj, proj, proj, proj, x, gate, sc2, sh2, n2w, wb, wo)


def _sorting_network(n):
    pairs = []
    p = 1
    while p < n:
        k = p
        while k >= 1:
            for j in range(k % p, n - k, 2 * k):
                for i in range(min(k, n - j - k)):
                    if (i + j) // (2 * p) == (i + j + k) // (2 * p):
                        pairs.append((i + j, i + j + k))
            k //= 2
        p *= 2
    return tuple(pairs)


def _topk_rows_slabs(s, k):
    r, t = s.shape
    assert r == 8 * k
    sub = lax.broadcasted_iota(jnp.int32, (8, t), 0).astype(F32)
    vs = [s[8 * j:8 * j + 8] for j in range(k)]
    ix = [sub + float(8 * j) for j in range(k)]
    for p, q in _sorting_network(k):
        va, ia, vb, ib = vs[p], ix[p], vs[q], ix[q]
        swap = (vb > va) | ((vb == va) & (ib < ia))
        vs[p], vs[q] = jnp.maximum(va, vb), jnp.minimum(va, vb)
        ix[p], ix[q] = jnp.where(swap, ib, ia), jnp.where(swap, ia, ib)
    out_row = lax.broadcasted_iota(jnp.int32, (k, t), 0)
    vals = jnp.zeros((k, t), F32)
    idxs = jnp.zeros((k, t), F32)
    for rnd in range(k):
        hv, hi = vs[0], ix[0]
        m = jnp.max(hv, axis=0, keepdims=True)
        idx = jnp.min(jnp.where(hv == m, hi, float(r)), axis=0, keepdims=True)
        vals = jnp.where(out_row == rnd, m, vals)
        idxs = jnp.where(out_row == rnd, idx, idxs)
        won = hi == idx
        for d in range(k - 1 - rnd):
            vs[d] = jnp.where(won, vs[d + 1], vs[d])
            ix[d] = jnp.where(won, ix[d + 1], ix[d])
    return vals, idxs.astype(jnp.int32)


def _select_rows(table, sel, k):
    out = jnp.zeros_like(table)
    for r in range(k):
        out = jnp.where(sel == r, table[r:r + 1, :], out)
    return out


def _split_bf16(x):
    hi = x.astype(BF16)
    return hi, (x - hi.astype(F32)).astype(BF16)


def _peer_joint_topk(s1, s2):
    k = PEER_TOPK
    t = s1.shape[1]
    sub = lax.broadcasted_iota(jnp.int32, (8, t), 0)
    depth = jnp.full((8, t), k // 8, jnp.int32)
    for i in range(6, -1, -1):
        depth = jnp.where(sub == i, k // (i + 1), depth)
    subf = sub.astype(F32)
    lo = [jnp.where(depth > d, s1[0:8] + s2[d:d + 1], -jnp.inf) for d in range(k)]
    hi0 = s1[8:16] + s2[0:1]
    pos_hi = (subf + 8.0) * float(k)
    cnt = jnp.zeros((8, t), F32)
    out_row = lax.broadcasted_iota(jnp.int32, (k, t), 0)
    vals = jnp.zeros((k, t), F32)
    poss = jnp.zeros((k, t), F32)
    big = float(k * k)
    for rnd in range(k):
        pos_lo = subf * float(k) + cnt
        m = jnp.max(jnp.maximum(lo[0], hi0), axis=0, keepdims=True)
        cand = jnp.minimum(jnp.where(lo[0] == m, pos_lo, big), jnp.where(hi0 == m, pos_hi, big))
        pos = jnp.min(cand, axis=0, keepdims=True)
        vals = jnp.where(out_row == rnd, m, vals)
        poss = jnp.where(out_row == rnd, pos, poss)
        won_lo = pos_lo == pos
        for d in range(k - 1 - rnd):
            lo[d] = jnp.where(won_lo, lo[d + 1], lo[d])
        cnt = cnt + jnp.where(won_lo, 1.0, 0.0)
        hi0 = jnp.where(pos_hi == pos, -jnp.inf, hi0)
    p = poss.astype(jnp.int32)
    return vals, p >> 4, p & (k - 1)


def _peer_route_kernel(h_ref, wqh_ref, wql_ref, kh_ref, kl_ref, a_o, b_o, g_o, q_s, a_s, b_s, g_s):
    k = PEER_TOPK
    assert k == 16
    hh, hl = _split_bf16(h_ref[...])
    q_s[...] = _dot(hh, wqh_ref[...]) + (_dot(hl, wqh_ref[...]) + _dot(hh, wql_ref[...]))

    def head(h):
        lo = pl.multiple_of(h * PEER_DK, PEER_DK)
        qh, ql = _split_bf16(q_s[:, pl.ds(lo, PEER_DK)])
        kh, kl = kh_ref[h], kl_ref[h]
        s = _dot_nt(kh, qh) + (_dot_nt(kh, ql) + _dot_nt(kl, qh))
        s1, i1 = _topk_rows_slabs(s[:PEER_N_KEYS], k)
        s2, i2 = _topk_rows_slabs(s[PEER_N_KEYS:], k)
        ts, ri, rj = _peer_joint_topk(s1, s2)
        e = jnp.exp(ts - ts[0:1, :])
        gate = e / jnp.sum(e, axis=0, keepdims=True)
        ro = pl.multiple_of(h * k, k)
        a_s[pl.ds(ro, k), :] = _select_rows(i1, ri, k)
        b_s[pl.ds(ro, k), :] = _select_rows(i2, rj, k)
        g_s[pl.ds(ro, k), :] = gate

    def head_group(i, _):
        for j in range(PEER_HEADS_PER_TRIP):
            head(PEER_HEADS_PER_TRIP * i + j)
        return 0

    lax.fori_loop(0, PEER_HEADS // PEER_HEADS_PER_TRIP, head_group, 0)
    a_o[...] = a_s[...].T
    b_o[...] = b_s[...].T
    g_o[...] = g_s[...].T


def _peer_route(h2, wq_hl, keys_hl):
    n, d = h2.shape
    t = 256
    wq_hi, wq_lo = wq_hl
    k_hi, k_lo = keys_hl
    return pl.pallas_call(
        _peer_route_kernel,
        grid=(n // t,),
        in_specs=[pl.BlockSpec((t, d), lambda i: (i, 0)),
                  pl.BlockSpec(wq_hi.shape, lambda i: (0, 0)),
                  pl.BlockSpec(wq_lo.shape, lambda i: (0, 0)),
                  pl.BlockSpec(k_hi.shape, lambda i: (0, 0, 0)),
                  pl.BlockSpec(k_lo.shape, lambda i: (0, 0, 0))],
        out_specs=[pl.BlockSpec((t, PEER_SLOTS), lambda i: (i, 0))] * 3,
        out_shape=[jax.ShapeDtypeStruct((n, PEER_SLOTS), jnp.int32),
                   jax.ShapeDtypeStruct((n, PEER_SLOTS), jnp.int32),
                   jax.ShapeDtypeStruct((n, PEER_SLOTS), F32)],
        scratch_shapes=[pltpu.VMEM((t, PEER_HEADS * PEER_DK), F32),
                        pltpu.VMEM((PEER_SLOTS, t), jnp.int32),
                        pltpu.VMEM((PEER_SLOTS, t), jnp.int32),
                        pltpu.VMEM((PEER_SLOTS, t), F32)],
        compiler_params=_cparams(("parallel",), 48 * 1024 * 1024),
        name="peer_route",
    )(h2, wq_hi, wq_lo, k_hi, k_lo)


_HI16 = -65536


def _bf16_bits(w):
    return lax.bitcast_convert_type(w, jnp.int32) & _HI16


def _peer_ffn_kernel(h_ref, a_ref, b_ref, g_ref, x_ref, gate_ref, u_ref, v_ref, fw_ref, o_ref, hb_s, w_s, *,
                     ec, unroll, final_norm):
    e = pl.program_id(1)
    t = h_ref.shape[0]
    half = t // 2
    nk = PEER_N_KEYS

    @pl.when(e == 0)
    def _():
        hb_s[...] = h_ref[...].astype(BF16)
        o_ref[...] = jnp.zeros_like(o_ref)
        jio = lax.broadcasted_iota(jnp.int32, (nk, PEER_SLOTS), 0)

        def tile(tt):
            arow = jnp.broadcast_to(a_ref[pl.ds(tt, 1), :], (nk, PEER_SLOTS))
            brow = jnp.broadcast_to(b_ref[pl.ds(tt, 1), :], (nk, PEER_SLOTS))
            grow = jnp.broadcast_to(g_ref[pl.ds(tt, 1), :], (nk, PEER_SLOTS))
            cm = jnp.where(jio == arow, grow, 0.0).astype(BF16)
            bm_t = jnp.where(jio == brow, 1.0, 0.0).T.astype(BF16)
            return _dot(cm, bm_t)

        def build(tb, _):
            for u in range(unroll):
                tt = tb * unroll + u
                word = _bf16_bits(tile(tt + half)) | lax.shift_right_logical(_bf16_bits(tile(tt)), 16)
                w_s[pl.ds(pl.multiple_of(tt * PEER_W_PITCH, 8), nk), :] = word
            return 0

        lax.fori_loop(0, half // unroll, build, 0)

    hid = _dot(hb_s[...], u_ref[0])
    j0 = e * (ec // nk)
    words = jnp.concatenate([w_s[pl.ds(j0 + j, half, stride=PEER_W_PITCH), :] for j in range(ec // nk)], axis=1)
    w_lo = lax.bitcast_convert_type(lax.shift_left(words, 16), F32)
    w_hi = lax.bitcast_convert_type(words & _HI16, F32)
    wc = jnp.concatenate([w_lo, w_hi], axis=0)
    act = 0.5 * hid * (1.0 + lax.erf(hid * SQRT_HALF))
    o_ref[...] += _dot((wc * act).astype(BF16), v_ref[...])

    @pl.when(e == pl.num_programs(1) - 1)
    def _():
        y = x_ref[...] + gate_ref[0] * o_ref[...]
        o_ref[...] = _rms(y) * fw_ref[...] if final_norm else y


def _peer_ffn(h2, a, b_idx, g, x, gate, u_blk, v, l, final_w, final_norm):
    n, d = h2.shape
    ne = v.shape[0]
    neb, _, ec = u_blk.shape
    t = min(512, l)
    unroll = 16
    assert l % t == 0 and ne == PEER_N_KEYS * PEER_N_KEYS and neb * ec == ne and (t // 2) % unroll == 0
    tok = lambda w: pl.BlockSpec((t, w), lambda i, e: (i, 0))
    return pl.pallas_call(
        functools.partial(_peer_ffn_kernel, ec=ec, unroll=unroll, final_norm=final_norm),
        grid=(n // t, neb),
        in_specs=[tok(d), tok(PEER_SLOTS), tok(PEER_SLOTS), tok(PEER_SLOTS), tok(d),
                  pl.BlockSpec((1, 1, d), lambda i, e: ((i * t) // l, 0, 0)),
                  pl.BlockSpec((1, d, ec), lambda i, e: (e, 0, 0)),
                  pl.BlockSpec((ec, d), lambda i, e: (e, 0)),
                  pl.BlockSpec((1, d), lambda i, e: (0, 0))],
        out_specs=tok(d),
        out_shape=jax.ShapeDtypeStruct((n, d), F32),
        scratch_shapes=[pltpu.VMEM((t, d), BF16),
                        pltpu.VMEM((t // 2 * PEER_W_PITCH, PEER_N_KEYS), jnp.int32)],
        compiler_params=_cparams(("parallel", "arbitrary"), VMEM_LIMIT_V7X),
        name="peer_ffn",
    )(h2, a, b_idx, g, x, gate, u_blk, v, final_w.reshape(1, d))


def _layout_w_in(w):
    parts = jnp.split(w, IN_OFFSETS, axis=1)
    z = lambda n: jnp.zeros((w.shape[0], n), w.dtype)
    kpe_blk = jnp.concatenate([z(MLA_NOPE), parts[2], z(MLA_HEAD_PAD - MLA_NOPE - MLA_ROPE)], axis=1)
    return jnp.concatenate([parts[0], parts[1], kpe_blk] + list(parts[3:]), axis=1).astype(BF16)


def _layout_mla(w_uq, w_ukv):
    qh = w_uq.reshape(MLA_Q_LORA, N_HEADS, MLA_NOPE + MLA_ROPE)
    qh = jnp.pad(qh, ((0, 0), (0, 0), (0, MLA_HEAD_PAD - MLA_NOPE - MLA_ROPE)))
    kv = w_ukv.reshape(MLA_KV_LORA, N_HEADS, MLA_NOPE + MLA_V)
    kh = jnp.pad(kv[:, :, :MLA_NOPE], ((0, 0), (0, 0), (0, MLA_HEAD_PAD - MLA_NOPE)))
    vh = kv[:, :, MLA_NOPE:]
    return (qh.reshape(MLA_Q_LORA, -1).astype(BF16), kh.reshape(MLA_KV_LORA, -1).astype(BF16),
            vh.reshape(MLA_KV_LORA, -1).T.astype(BF16))


def _static_mats():
    gseg = np.kron(np.eye(N_HEADS), np.ones((HEAD_W, HEAD_W))).astype(np.float32)
    pm = np.zeros((MLA_HEAD_PAD, MLA_HEAD_PAD), np.float32)
    for dd in range(MLA_ROPE):
        blk, j = dd // 16, dd % 16
        pm[MLA_NOPE + blk * 16 + (j + 8) % 16, MLA_NOPE + dd] = 1.0
    pg = np.zeros((BRANCH_W, BRANCH_W), np.float32)
    for i in range(BRANCH_W):
        off, dd = (i // HEAD_W) * HEAD_W, i % HEAD_W
        blk, j = dd // 32, dd % 32
        pg[off + blk * 32 + (j + 16) % 32, i] = 1.0
    ex = np.zeros((GQA_KV_HEADS * HEAD_W, BRANCH_W), np.float32)
    for i in range(BRANCH_W):
        ex[((i // HEAD_W) // (N_HEADS // GQA_KV_HEADS)) * HEAD_W + i % HEAD_W, i] = 1.0
    return tuple(jnp.asarray(m, dtype=BF16) for m in (gseg, pm, pg, ex, ex.T))


def _rope_half_tables(pos, hf):
    freqs = ROPE_THETA ** (-jnp.arange(hf, dtype=F32) / hf)
    ang = pos[:, None] * freqs[None, :]
    c, s = jnp.cos(ang), jnp.sin(ang)
    return jnp.concatenate([c, c], axis=1), jnp.concatenate([-s, s], axis=1)


def _axial_tables(row, col, dims):
    cr, sr = _rope_half_tables(row, dims // 4)
    cc, sc = _rope_half_tables(col, dims // 4)
    return jnp.concatenate([cr, cc], axis=1), jnp.concatenate([sr, sc], axis=1)


def _rope_tables(s):
    t = jnp.arange(s)
    row, col = (t // GRID_W).astype(F32), (t % GRID_W).astype(F32)
    c32, s32 = _axial_tables(row, col, MLA_ROPE)
    pad = MLA_HEAD_PAD - MLA_NOPE - MLA_ROPE
    cm = jnp.concatenate([jnp.ones((s, MLA_NOPE), F32), c32, jnp.ones((s, pad), F32)], axis=1)
    sm = jnp.concatenate([jnp.zeros((s, MLA_NOPE), F32), s32, jnp.zeros((s, pad), F32)], axis=1)
    c64, s64 = _axial_tables(row, col, HEAD_W)
    return cm, sm, jnp.tile(c64, (1, N_HEADS)), jnp.tile(s64, (1, N_HEADS))


def _split_f32(w):
    hi = w.astype(BF16)
    return hi, (w - hi.astype(F32)).astype(BF16)


def _layout_peer_keys(keys):
    h, _, nk, dh = keys.shape
    z = jnp.zeros((h, nk, dh), keys.dtype)
    top = jnp.concatenate([keys[:, 0], z], axis=2)
    bot = jnp.concatenate([z, keys[:, 1]], axis=2)
    return jnp.concatenate([top, bot], axis=1)


def kernel(x, c, ctx, c_ctx, mod_w, mod_b, norm1_w, norm2_w, w_in, mla_q_norm, mla_w_uq, mla_kv_norm, mla_w_ukv, gqa_q_norm, gqa_k_norm, na_bias, ret_decay_logit, ret_gn_w, w_branch, w_out, peer_w_q, peer_keys, peer_u, peer_v, final_norm_w):
    b, s, d = x.shape
    lc = ctx.shape[1]
    depth = mod_w.shape[0]
    assert d == D_MODEL and s % (GRID_W * NA_Q_ROWS) == 0 and s % 256 == 0 and lc % 256 == 0

    rows = -(-(b + 1) // 8) * 8
    cc = jnp.zeros((rows, d), F32).at[:b].set(c).at[b].set(c_ctx)
    mod = _modulation(cc, mod_w, mod_b)

    gseg, pm, pg, ex, ex_t = _static_mats()
    tables = _rope_tables(s)

    for l in range(depth):
        need_ctx = l < depth - 1
        mx = mod[l, :b].reshape(b, 1, 6, d)
        mc = jnp.broadcast_to(mod[l, b].reshape(1, 1, 6, d), (b, 1, 6, d))
        sh1x, sc1x, g1x, sh2x, sc2x, g2x = (mx[:, :, i] for i in range(6))
        sh1c, sc1c, g1c, sh2c, sc2c, g2c = (mc[:, :, i] for i in range(6))

        w_in_l = _layout_w_in(w_in[l])
        wuq, wk, wv = _layout_mla(mla_w_uq[l], mla_w_ukv[l])
        consts = (mla_q_norm[l].reshape(1, -1), wuq, mla_kv_norm[l].reshape(1, -1), wk, wv,
                  jnp.tile(gqa_q_norm[l], N_HEADS).reshape(1, -1),
                  jnp.tile(gqa_k_norm[l], GQA_KV_HEADS).reshape(1, -1), gseg, pm, pg, ex, ex_t)
        n1w = norm1_w[l].reshape(1, d)
        n2w = norm2_w[l].reshape(1, d)
        wb = w_branch[l].astype(BF16)
        wo = w_out[l].astype(BF16)
        keys_hl = _split_f32(_layout_peer_keys(peer_keys[l]))
        wq_hl = _split_f32(peer_w_q[l])
        ne = peer_u.shape[1]
        u_blk = jnp.swapaxes(peer_u[l].astype(BF16).reshape(ne // PEER_EXPERT_CHUNK, PEER_EXPERT_CHUNK, d), 1, 2)
        v_b = peer_v[l].astype(BF16)
        log_g = jax.nn.log_sigmoid(ret_decay_logit[l].astype(F32))

        projx = _inproj(x, n1w, sc1x, sh1x, w_in_l)
        projc = _inproj(ctx, n1w, sc1c, sh1c, w_in_l)
        qmx, kmx, vmx, qgx, kgx, vgx = _prep(projx, consts, tables)
        qmc, kmc, vmc, qgc, kgc, vgc = _prep(projc, consts, None)

        oa = _attention((qmx, 0), [((kmc, 0), vmc), ((kmx, 0), vmx)], MLA_HEAD_PAD, "attn_mla")
        ob = _attention((qgx, 0), [((kgc, 0), vgc), ((kgx, 0), vgx)], HEAD_W, "attn_gqa")
        oc = _na_attention(projx, projc, na_bias[l])
        od, od_c = _retention(projx, projc, log_g, ret_gn_w[l], gseg, need_ctx)

        x, h2x = _merge((oa, ob, oc, od), projx, x, g1x, sc2x, sh2x, n2w, wb, wo)
        ax, bx, gx = _peer_route(h2x.reshape(b * s, d), wq_hl, keys_hl)
        x = _peer_ffn(h2x.reshape(b * s, d), ax, bx, gx, x.reshape(b * s, d), g2x, u_blk, v_b, s,
                      final_norm_w, final_norm=(l == depth - 1)).reshape(b, s, d)

        if need_ctx:
            ca = _attention((qmc, 0), [((kmc, 0), vmc)], MLA_HEAD_PAD, "attn_mla_ctx")
            cb = _attention((qgc, 0), [((kgc, 0), vgc)], HEAD_W, "attn_gqa_ctx")
            nv_t = jnp.swapaxes(projc[:, :, COL_NV:COL_NV + BRANCH_W], 1, 2)
            ccx = _attention((projc, COL_NQ), [((projc, COL_NK), nv_t)], HEAD_W, "attn_na_ctx",
                             qscale=HEAD_W ** -0.5, log2_scores=False)
            ctx, h2c = _merge((ca, cb, ccx, od_c), projc, ctx, g1c, sc2c, sh2c, n2w, wb, wo)
            ac, bc, gc = _peer_route(h2c.reshape(b * lc, d), wq_hl, keys_hl)
            ctx = _peer_ffn(h2c.reshape(b * lc, d), ac, bc, gc, ctx.reshape(b * lc, d), g2c, u_blk, v_b,
                            lc, final_norm_w, final_norm=False).reshape(b, lc, d)

    return x
```

```python
import functools

import numpy as np
import jax
import jax.numpy as jnp
from jax import lax
from jax.experimental import pallas as pl
from jax.experimental.pallas import tpu as pltpu

F32 = jnp.float32
BF16 = jnp.bfloat16
HIGHEST = lax.Precision.HIGHEST

D_MODEL = 1024
GRID_W = 64
ROPE_THETA = 10000.0
EPS = 1e-6
N_HEADS = 4
HEAD_W = 64
BRANCH_W = N_HEADS * HEAD_W
MLA_NOPE, MLA_ROPE, MLA_V = 64, 32, 64
MLA_Q_LORA, MLA_KV_LORA = 256, 128
MLA_SCALE = (MLA_NOPE + MLA_ROPE) ** -0.5
MLA_HEAD_PAD = 128
GQA_KV_HEADS = 2
NA_WIN_R, NA_WIN_C = 8, 16
NA_Q_ROWS = 4
ATTN_KEY_CHUNK = 1024
RET_CHUNK = 128
N_BRANCH = 4
PEER_HEADS, PEER_N_KEYS, PEER_TOPK, PEER_DK = 8, 128, 16, 128
PEER_SLOTS = PEER_HEADS * PEER_TOPK
PEER_W_PITCH = PEER_N_KEYS + 8
PEER_HEADS_PER_TRIP = 4
PEER_EXPERT_CHUNK = 1024
SQRT_HALF = 0.7071067811865476
LOG2E = 1.4426950408889634
NEG_BIG = -1e30

IN_SIZES = (256, 128, 32, 256, 128, 128, 256, 256, 256, 256, 256, 256, 256, 256, 4096)
IN_OFFSETS = tuple(int(v) for v in np.cumsum(IN_SIZES)[:-1])
PROJ_COLS = 7168
COL_NQ, COL_NK, COL_NV = 1024, 1280, 1536
COL_RET = 1792
COL_GATES = 3072

VMEM_LIMIT_V7X = 56 * 1024 * 1024


def _cparams(sem, vmem=None):
    return pltpu.CompilerParams(dimension_semantics=sem, vmem_limit_bytes=vmem)


def _dot(a, b):
    return jnp.dot(a, b, preferred_element_type=F32)


def _dot_hi(a, b):
    return jnp.dot(a, b, preferred_element_type=F32, precision=HIGHEST)


def _dot_nt(a, b):
    return lax.dot_general(a, b, (((1,), (1,)), ((), ())), preferred_element_type=F32)


def _dot_sel(x, sel):
    hi = x.astype(BF16)
    r1 = x - hi.astype(F32)
    mid = r1.astype(BF16)
    lo = (r1 - mid.astype(F32)).astype(BF16)
    return _dot(hi, sel) + (_dot(mid, sel) + _dot(lo, sel))


def _rms(x):
    return x * lax.rsqrt(jnp.mean(x * x, axis=-1, keepdims=True) + EPS)


def _silu(x):
    return x * jax.nn.sigmoid(x)


def _head_mask(shape, h, width=HEAD_W):
    lane = lax.broadcasted_iota(jnp.int32, shape, len(shape) - 1)
    lo = h * width
    return (lane >= lo) & (lane < lo + width)


def _mod_kernel(c_ref, w_ref, b_ref, o_ref):
    o_ref[0] = _dot_hi(_silu(c_ref[...]), w_ref[0]) + b_ref[0]


def _modulation(cc, mod_w, mod_b):
    depth, d, n = mod_w.shape
    rows = cc.shape[0]
    tn = 1536
    return pl.pallas_call(
        _mod_kernel,
        grid=(depth, n // tn),
        in_specs=[pl.BlockSpec((rows, d), lambda l, j: (0, 0)),
                  pl.BlockSpec((1, d, tn), lambda l, j: (l, 0, j)),
                  pl.BlockSpec((1, 1, tn), lambda l, j: (l, 0, j))],
        out_specs=pl.BlockSpec((1, rows, tn), lambda l, j: (l, 0, j)),
        out_shape=jax.ShapeDtypeStruct((depth, rows, n), F32),
        compiler_params=_cparams(("parallel", "parallel"), 40 * 1024 * 1024),
        name="modulation",
    )(cc, mod_w, mod_b.reshape(depth, 1, n))


def _inproj_kernel(x_ref, nw_ref, sc_ref, sh_ref, w_ref, o_ref, h_scr):
    @pl.when(pl.program_id(2) == 0)
    def _():
        h = _rms(x_ref[0]) * nw_ref[...] * (1.0 + sc_ref[0]) + sh_ref[0]
        h_scr[...] = h.astype(BF16)

    o_ref[0] = _dot(h_scr[...], w_ref[...]).astype(o_ref.dtype)


def _inproj(x, nw, sc, sh, w):
    b, l, d = x.shape
    n = w.shape[1]
    tm = min(1024, l)
    tn = 1792
    return pl.pallas_call(
        _inproj_kernel,
        grid=(b, l // tm, n // tn),
        in_specs=[pl.BlockSpec((1, tm, d), lambda bi, i, j: (bi, i, 0)),
                  pl.BlockSpec((1, d), lambda bi, i, j: (0, 0)),
                  pl.BlockSpec((1, 1, d), lambda bi, i, j: (bi, 0, 0)),
                  pl.BlockSpec((1, 1, d), lambda bi, i, j: (bi, 0, 0)),
                  pl.BlockSpec((d, tn), lambda bi, i, j: (0, j))],
        out_specs=pl.BlockSpec((1, tm, tn), lambda bi, i, j: (bi, i, j)),
        out_shape=jax.ShapeDtypeStruct((b, l, n), BF16),
        scratch_shapes=[pltpu.VMEM((tm, d), BF16)],
        compiler_params=_cparams(("parallel", "parallel", "arbitrary"), 40 * 1024 * 1024),
        name="inproj",
    )(x, nw, sc, sh, w)


def _prep_kernel(*refs, use_rope):
    (p_ref, nv_ref, qn_ref, wuq_ref, kvn_ref, wk_ref, wv_ref, gqn_ref, gkn_ref, gseg_ref, pm_ref, pg_ref,
     e_ref, et_ref, eye_ref) = refs[:15]
    if use_rope:
        cm_ref, sm_ref, cg_ref, sg_ref = refs[15:19]
        outs = refs[19:]
    else:
        outs = refs[15:]
    qm_o, km_o, vm_o, qg_o, kg_o, vg_o, nvt_o = outs
    nvt_o[0] = _dot_nt(eye_ref[...], nv_ref[0]).astype(BF16)

    pb = p_ref[0]
    cq = pb[:, 0:256].astype(F32)
    ckv = pb[:, 256:384].astype(F32)
    kpe = pb[:, 384:512].astype(F32)
    gq = pb[:, 512:768].astype(F32)
    gk = pb[:, 768:896].astype(F32)
    gv = pb[:, 896:1024]

    cqn = (_rms(cq) * qn_ref[...]).astype(BF16)
    qa = _dot(cqn, wuq_ref[...])
    ckn = (_rms(ckv) * kvn_ref[...]).astype(BF16)
    kn = _dot(ckn, wk_ref[...])
    vm_t = _dot_nt(wv_ref[...], ckn)
    if use_rope:
        cm, sm = cm_ref[...], sm_ref[...]
        pm = pm_ref[...]

        def rope_m(t):
            return t * cm + _dot_sel(t, pm) * sm

        qa = jnp.concatenate([rope_m(qa[:, h * 128:(h + 1) * 128]) for h in range(N_HEADS)], axis=1)
        kpe = rope_m(kpe)
    km = kn + jnp.concatenate([kpe] * N_HEADS, axis=1)
    qm_o[0] = (qa * (MLA_SCALE * LOG2E)).astype(BF16)
    km_o[0] = km.astype(BF16)
    vm_o[0] = vm_t.astype(BF16)

    gseg = gseg_ref[...]
    gqn = gq * lax.rsqrt(_dot_sel(gq * gq, gseg) * (1.0 / HEAD_W) + EPS) * gqn_ref[...]
    gkn = gk * lax.rsqrt(_dot_sel(gk * gk, gseg[:128, :128]) * (1.0 / HEAD_W) + EPS) * gkn_ref[...]
    if use_rope:
        cg, sg = cg_ref[...], sg_ref[...]
        pg = pg_ref[...]
        gqn = gqn * cg + _dot_sel(gqn, pg) * sg
        gkn = gkn * cg[:, :128] + _dot_sel(gkn, pg[:128, :128]) * sg[:, :128]
    qg_o[0] = (gqn * (HEAD_W ** -0.5 * LOG2E)).astype(BF16)
    e = e_ref[...]
    kg_o[0] = _dot(gkn.astype(BF16), e).astype(BF16)
    vg_o[0] = _dot_nt(et_ref[...], gv).astype(BF16)


def _prep(proj, consts, tables):
    b, l, _ = proj.shape
    tm = min(512, l)
    use_rope = tables is not None
    full = lambda a: pl.BlockSpec(a.shape, lambda bi, i: (0,) * a.ndim)
    in_specs = [pl.BlockSpec((1, tm, 1024), lambda bi, i: (bi, i, 0)),
                pl.BlockSpec((1, tm, BRANCH_W), lambda bi, i: (bi, i, COL_NV // BRANCH_W))] + [full(a) for a in consts]
    args = [proj, proj] + list(consts)
    if use_rope:
        in_specs += [pl.BlockSpec((tm, t.shape[1]), lambda bi, i: (i, 0)) for t in tables]
        args += list(tables)
    tok = lambda w: (pl.BlockSpec((1, tm, w), lambda bi, i: (bi, i, 0)), jax.ShapeDtypeStruct((b, l, w), BF16))
    tr = (pl.BlockSpec((1, BRANCH_W, tm), lambda bi, i: (bi, 0, i)), jax.ShapeDtypeStruct((b, BRANCH_W, l), BF16))
    outs = (tok(512), tok(512), tr, tok(256), tok(256), tr, tr)
    return pl.pallas_call(
        functools.partial(_prep_kernel, use_rope=use_rope),
        grid=(b, l // tm),
        in_specs=in_specs,
        out_specs=[o[0] for o in outs],
        out_shape=[o[1] for o in outs],
        compiler_params=_cparams(("parallel", "parallel"), 40 * 1024 * 1024),
        name="prep_rope" if use_rope else "prep",
    )(*args)


def _attn_kernel(*refs, nseg, dq, tk, qscale, log2_scores):
    exp = jnp.exp2 if log2_scores else jnp.exp
    q_ref = refs[0]
    segs = [(refs[1 + 2 * i], refs[2 + 2 * i]) for i in range(nseg)]
    o_ref = refs[1 + 2 * nseg]
    tq = q_ref.shape[1]
    gw = 256
    hpg = gw // dq
    qstacks = []
    for g in range(N_HEADS // hpg):
        qg = q_ref[0, :, g * gw:(g + 1) * gw]
        if qscale is not None:
            qg = qg * jnp.asarray(qscale, BF16)
        qstacks.append(jnp.concatenate([jnp.where(_head_mask(qg.shape, j, dq), qg, jnp.zeros_like(qg))
                                        for j in range(hpg)], axis=0))
    ones_rows = 16
    carry = tuple((jnp.full((1, tq), -jnp.inf, F32), jnp.zeros((HEAD_W + ones_rows, tq), F32))
                  for _ in range(N_HEADS))

    def scores(k_ref, off, tkk):
        return [_dot_nt(k_ref[0, pl.ds(off, tkk), g * gw:(g + 1) * gw], qstacks[g])
                for g in range(N_HEADS // hpg)]

    def softmax_step(carry, head_scores, vt_ref, off, tkk):
        new = []
        ones = jnp.ones((ones_rows, tkk), BF16)
        for h in range(N_HEADS):
            m, acc = carry[h]
            vt = jnp.concatenate([vt_ref[0, h * HEAD_W:(h + 1) * HEAD_W, pl.ds(off, tkk)], ones], axis=0)
            st = head_scores(h)
            mn = jnp.maximum(m, jnp.max(st, axis=0, keepdims=True))
            acc = exp(m - mn) * acc + _dot(vt, exp(st - mn).astype(BF16))
            new.append((mn, acc))
        return tuple(new)

    for k_ref, vt_ref in segs:
        lk = k_ref.shape[1]
        tkk = min(tk, lk)
        n = lk // tkk

        def body(c, carry, k_ref=k_ref, vt_ref=vt_ref, tkk=tkk):
            off = c * tkk if isinstance(c, int) else pl.multiple_of(c * tkk, tkk)
            st_g = scores(k_ref, off, tkk)
            return softmax_step(carry, lambda h: st_g[h // hpg][:, (h % hpg) * tq:(h % hpg + 1) * tq],
                                vt_ref, off, tkk)

        carry = body(0, carry) if n == 1 else lax.fori_loop(0, n, body, carry)
    out_t = jnp.concatenate([acc[:HEAD_W] * (1.0 / acc[HEAD_W:HEAD_W + 1]) for _, acc in carry], axis=0)
    o_ref[0] = out_t.T.astype(o_ref.dtype)


def _attention(q, segs, dq, name, qscale=None, log2_scores=True):
    (qa, qcol) = q
    b, lq, _ = qa.shape
    wq = N_HEADS * dq
    tq = min(512, lq)
    assert qcol % wq == 0
    in_specs = [pl.BlockSpec((1, tq, wq), lambda bi, i: (bi, i, qcol // wq))]
    args = [qa]
    for (ka, kcol), vt in segs:
        assert kcol % wq == 0 and vt.shape[1] == BRANCH_W and vt.shape[2] == ka.shape[1]
        in_specs.append(pl.BlockSpec((1, ka.shape[1], wq), lambda bi, i, kcol=kcol: (bi, 0, kcol // wq)))
        in_specs.append(pl.BlockSpec((1, BRANCH_W, vt.shape[2]), lambda bi, i: (bi, 0, 0)))
        args += [ka, vt]
    return pl.pallas_call(
        functools.partial(_attn_kernel, nseg=len(segs), dq=dq, tk=ATTN_KEY_CHUNK, qscale=qscale,
                          log2_scores=log2_scores),
        grid=(b, lq // tq),
        in_specs=in_specs,
        out_specs=pl.BlockSpec((1, tq, BRANCH_W), lambda bi, i: (bi, i, 0)),
        out_shape=jax.ShapeDtypeStruct((b, lq, BRANCH_W), BF16),
        compiler_params=_cparams(("parallel", "arbitrary"), 48 * 1024 * 1024),
        name=name,
    )(*args)


def _proj_cols(arr, col, width, rows):
    assert col % width == 0
    return pl.BlockSpec((1, rows, width), lambda *idx: (idx[0], 0, col // width))


def _na_kernel(pat_ref, ks_ref, q_ref, k_ref, vt_ref, kc_ref, vct_ref, m_ref, o_ref, *, kw):
    del pat_ref
    g = pl.program_id(1)
    off = pl.multiple_of(ks_ref[g] * GRID_W, 128)
    q = q_ref[0] * jnp.asarray(HEAD_W ** -0.5, BF16)
    qb = q.shape[0]
    qstack = jnp.concatenate([jnp.where(_head_mask(q.shape, h), q, jnp.zeros_like(q))
                              for h in range(N_HEADS)], axis=0)
    st_w = _dot_nt(k_ref[0, pl.ds(off, kw), :], qstack)
    st_c = _dot_nt(kc_ref[0], qstack)
    ones_w = jnp.ones((16, kw), BF16)
    ones_c = jnp.ones((16, kc_ref.shape[1]), BF16)
    outs = []
    for h in range(N_HEADS):
        sw = st_w[:, h * qb:(h + 1) * qb] + m_ref[0, h]
        sc = st_c[:, h * qb:(h + 1) * qb]
        mx = jnp.maximum(jnp.max(sw, axis=0, keepdims=True), jnp.max(sc, axis=0, keepdims=True))
        vtw = jnp.concatenate([vt_ref[0, h * HEAD_W:(h + 1) * HEAD_W, pl.ds(off, kw)], ones_w], axis=0)
        vtc = jnp.concatenate([vct_ref[0, h * HEAD_W:(h + 1) * HEAD_W, :], ones_c], axis=0)
        acc = _dot(vtw, jnp.exp(sw - mx).astype(BF16)) + _dot(vtc, jnp.exp(sc - mx).astype(BF16))
        outs.append(acc[:HEAD_W] * (1.0 / acc[HEAD_W:HEAD_W + 1]))
    o_ref[0] = jnp.concatenate(outs, axis=0).T.astype(o_ref.dtype)


def _na_plan(s):
    rows = s // GRID_W
    wr = min(NA_WIN_R, rows)
    wc = NA_WIN_C
    qr = min(NA_Q_ROWS, rows)
    kwr = min(qr + wr - 1 + (1 if qr + wr - 1 < rows else 0), rows)
    ngrp = rows // qr
    qc = np.arange(GRID_W)[:, None]
    kc = np.arange(GRID_W)[None, :]
    cs = np.clip(qc - wc // 2, 0, GRID_W - wc)
    valid_c = (kc >= cs) & (kc < cs + wc)
    rel_c = np.where(valid_c, kc - qc + (NA_WIN_C - 1), 0)
    assert (valid_c.sum(1) == wc).all()
    pats, pat_ids, ks_rows = [], [], []
    for g in range(ngrp):
        r0 = g * qr
        ks = int(np.clip(r0 - wr // 2, 0, rows - kwr))
        assert (ks * GRID_W) % 128 == 0
        r = (r0 + np.arange(qr))[:, None]
        kr = (ks + np.arange(kwr))[None, :]
        rs = np.clip(r - wr // 2, 0, rows - wr)
        valid_r = (kr >= rs) & (kr < rs + wr)
        assert (valid_r.sum(1) == wr).all()
        rel_r = np.where(valid_r, kr - r + (NA_WIN_R - 1), 0)
        key = (valid_r.tobytes(), rel_r.tobytes())
        for pi, (pk, *_rest) in enumerate(pats):
            if pk == key:
                pat_ids.append(pi)
                break
        else:
            pat_ids.append(len(pats))
            pats.append((key, valid_r, rel_r))
        ks_rows.append(ks)
    valid_r = np.stack([p[1] for p in pats])
    rel_r = np.stack([p[2] for p in pats])
    return (qr, kwr, np.asarray(pat_ids, np.int32), np.asarray(ks_rows, np.int32), valid_r, rel_r, valid_c, rel_c)


def _na_bias_masks(na_bias, valid_r, rel_r, valid_c, rel_c):
    h = na_bias.shape[0]
    npat, qr, kwr = valid_r.shape
    ncol = 2 * NA_WIN_C - 1
    brow = na_bias[:, rel_r, :].astype(F32)
    onehot_c = ((rel_c[None] == np.arange(ncol)[:, None, None]) & valid_c[None]).astype(np.float32)
    m = jnp.einsum('hpqkc,cxy->phkyqx', brow, jnp.asarray(onehot_c), precision=HIGHEST)
    valid = (valid_r.transpose(0, 2, 1)[:, None, :, None, :, None]
             & valid_c.T[None, None, None, :, None, :])
    m = jnp.where(valid, m, NEG_BIG)
    return m.reshape(npat, h, kwr * GRID_W, qr * GRID_W)


def _na_attention(projx, projc, nvx_t, nvc_t, na_bias):
    b, s, _ = projx.shape
    lc = projc.shape[1]
    qr, kwr, pat_ids, ks_rows, valid_r, rel_r, valid_c, rel_c = _na_plan(s)
    qb, kw = qr * GRID_W, kwr * GRID_W
    assert kw % 128 == 0
    mb = _na_bias_masks(na_bias, valid_r, rel_r, valid_c, rel_c)
    grid_spec = pltpu.PrefetchScalarGridSpec(
        num_scalar_prefetch=2,
        grid=(b, s // qb),
        in_specs=[pl.BlockSpec((1, qb, 256), lambda bi, g, pat, ks: (bi, g, COL_NQ // 256)),
                  pl.BlockSpec((1, s, 256), lambda bi, g, pat, ks: (bi, 0, COL_NK // 256)),
                  pl.BlockSpec((1, BRANCH_W, s), lambda bi, g, pat, ks: (bi, 0, 0)),
                  pl.BlockSpec((1, lc, 256), lambda bi, g, pat, ks: (bi, 0, COL_NK // 256)),
                  pl.BlockSpec((1, BRANCH_W, lc), lambda bi, g, pat, ks: (bi, 0, 0)),
                  pl.BlockSpec((1, N_HEADS, kw, qb), lambda bi, g, pat, ks: (pat[g], 0, 0, 0))],
        out_specs=pl.BlockSpec((1, qb, BRANCH_W), lambda bi, g, pat, ks: (bi, g, 0)),
    )
    return pl.pallas_call(
        functools.partial(_na_kernel, kw=kw),
        grid_spec=grid_spec,
        out_shape=jax.ShapeDtypeStruct((b, s, BRANCH_W), BF16),
        compiler_params=_cparams(("parallel", "arbitrary"), 48 * 1024 * 1024),
        name="na_attention",
    )(jnp.asarray(pat_ids), jnp.asarray(ks_rows), projx, projx, nvx_t, projc, nvc_t, mb)


def _ret_kernel(lgs_ref, lgl_ref, gnw_ref, gseg_ref,
                qx, kx, vx, gfx, gbx, qc, kc, vc, gfc, gbc,
                yx_o, yc_o, of_s, ob_s, st_s, dec_s, qk_s, *, need_ctx):
    c = RET_CHUNK
    lc = qc.shape[1]
    sx = qx.shape[1]
    n_col = lax.broadcasted_iota(jnp.int32, (c, c), 0).astype(F32)
    m_row = lax.broadcasted_iota(jnp.int32, (c, c), 1).astype(F32)
    diff = n_col - m_row
    for h in range(N_HEADS):
        dec_s[0, :, h * c:(h + 1) * c] = jnp.where(diff >= 0, jnp.exp(lgs_ref[h] * jnp.maximum(diff, 0.0)), 0.0)
        dec_s[1, :, h * c:(h + 1) * c] = jnp.where(diff <= 0,
                                                   jnp.exp(lgs_ref[N_HEADS + h] * jnp.maximum(-diff, 0.0)), 0.0)
    pos = lax.broadcasted_iota(jnp.int32, (c, BRANCH_W), 0).astype(F32)
    lgf, lgb = lgl_ref[0], lgl_ref[1]
    qk_s[0] = jnp.exp(lgf * (pos + 1.0))
    qk_s[1] = jnp.exp(lgf * (c - 1.0 - pos))
    qk_s[2] = jnp.exp(lgb * (c - pos))
    qk_s[3] = jnp.exp(lgb * pos)
    cd_f = jnp.exp(lgf * float(c))
    cd_b = jnp.exp(lgb * float(c))
    st_s[...] = jnp.zeros_like(st_s)
    rowb = lax.broadcasted_iota(jnp.int32, (BRANCH_W, BRANCH_W), 0) // HEAD_W
    colb = lax.broadcasted_iota(jnp.int32, (BRANCH_W, BRANCH_W), 1) // HEAD_W
    bd_mask = rowb == colb

    def chunk_step(q, k, v, d, cd):
        kk = k * jnp.asarray(HEAD_W ** -0.5, BF16)
        state = st_s[d]
        o = _dot(q, state.astype(BF16)) * qk_s[2 * d]
        kstack = jnp.concatenate([jnp.where(_head_mask(kk.shape, h), kk, jnp.zeros_like(kk))
                                  for h in range(N_HEADS)], axis=0)
        vstack = jnp.concatenate([jnp.where(_head_mask(v.shape, h), v, jnp.zeros_like(v))
                                  for h in range(N_HEADS)], axis=0)
        inner = _dot_nt(q, kstack) * dec_s[d]
        o = o + _dot(inner.astype(BF16), vstack)
        kd = (kk.astype(F32) * qk_s[2 * d + 1]).astype(BF16)
        upd = lax.dot_general(kd, v, (((0,), (0,)), ((), ())), preferred_element_type=F32)
        st_s[d] = state * cd + jnp.where(bd_mask, upd, 0.0)
        return o

    gseg = gseg_ref[...]
    gnw = gnw_ref[...]

    def gnorm(o):
        mu = _dot_sel(o, gseg) * (1.0 / HEAD_W)
        dlt = o - mu
        var = _dot_sel(dlt * dlt, gseg) * (1.0 / HEAD_W)
        return dlt * lax.rsqrt(var + EPS) * gnw

    def combine(gf_ref, gb_ref, y_ref, base, ro):
        y = (gnorm(of_s[pl.ds(base + ro, c), :]) * _silu(gf_ref[0, pl.ds(ro, c), :].astype(F32))
             + gnorm(ob_s[pl.ds(base + ro, c), :]) * _silu(gb_ref[0, pl.ds(ro, c), :].astype(F32)))
        y_ref[0, pl.ds(ro, c), :] = y.astype(y_ref.dtype)

    def scan(q_ref, k_ref, v_ref, base, n, gated):
        def body(i, _, emit):
            fo = pl.multiple_of(i * c, c)
            bo = pl.multiple_of((n - 1 - i) * c, c)
            of_s[pl.ds(base + fo, c), :] = chunk_step(
                q_ref[0, pl.ds(fo, c), :], k_ref[0, pl.ds(fo, c), :], v_ref[0, pl.ds(fo, c), :], 0, cd_f)
            ob_s[pl.ds(base + bo, c), :] = chunk_step(
                q_ref[0, pl.ds(bo, c), :], k_ref[0, pl.ds(bo, c), :], v_ref[0, pl.ds(bo, c), :], 1, cd_b)
            if emit:
                combine(*gated, base, fo)
                combine(*gated, base, bo)
            return 0

        assert n % 2 == 0
        lax.fori_loop(0, n // 2, functools.partial(body, emit=False), 0)
        lax.fori_loop(n // 2, n, functools.partial(body, emit=gated is not None), 0)

    scan(qc, kc, vc, 0, lc // c, (gfc, gbc, yc_o) if need_ctx else None)
    scan(qx, kx, vx, lc, sx // c, (gfx, gbx, yx_o))
    if not need_ctx:
        yc_o[...] = jnp.zeros_like(yc_o)


def _retention(projx, projc, log_g, gn_w, gseg, need_ctx):
    b, s, _ = projx.shape
    lc = projc.shape[1]
    lgs = log_g.reshape(2 * N_HEADS)
    lgl = jnp.repeat(log_g, HEAD_W, axis=1).reshape(2, 1, BRANCH_W)
    xs = [_proj_cols(projx, COL_RET + 256 * i, 256, s) for i in range(5)]
    cs = [_proj_cols(projc, COL_RET + 256 * i, 256, lc) for i in range(5)]
    c = RET_CHUNK
    yx, yc = pl.pallas_call(
        functools.partial(_ret_kernel, need_ctx=need_ctx),
        grid=(b,),
        in_specs=[pl.BlockSpec(memory_space=pltpu.SMEM),
                  pl.BlockSpec((2, 1, BRANCH_W), lambda bi: (0, 0, 0)),
                  pl.BlockSpec((1, BRANCH_W), lambda bi: (0, 0)),
                  pl.BlockSpec((BRANCH_W, BRANCH_W), lambda bi: (0, 0))] + xs + cs,
        out_specs=[pl.BlockSpec((1, s, BRANCH_W), lambda bi: (bi, 0, 0)),
                   pl.BlockSpec((1, lc, BRANCH_W), lambda bi: (bi, 0, 0))],
        out_shape=[jax.ShapeDtypeStruct((b, s, BRANCH_W), BF16),
                   jax.ShapeDtypeStruct((b, lc, BRANCH_W), BF16)],
        scratch_shapes=[pltpu.VMEM((lc + s, BRANCH_W), F32),
                        pltpu.VMEM((lc + s, BRANCH_W), F32),
                        pltpu.VMEM((2, BRANCH_W, BRANCH_W), F32),
                        pltpu.VMEM((2, c, N_HEADS * c), F32),
                        pltpu.VMEM((4, c, BRANCH_W), F32)],
        compiler_params=_cparams(("parallel",), 48 * 1024 * 1024),
        name="retention",
    )(lgs, lgl, gn_w.reshape(1, BRANCH_W), gseg, *([projx] * 5), *([projc] * 5))
    return yx, yc


def _merge_kernel(oa, ob, oc, od, g0, g1, g2, g3, x_ref, gate_ref, sc_ref, sh_ref, nw_ref, wb_ref, wo_ref,
                  xn_o, h2_o):
    acc = None
    for i, (o, g) in enumerate(((oa, g0), (ob, g1), (oc, g2), (od, g3))):
        t = jax.nn.sigmoid(g[0].astype(F32)) * _dot(o[0], wb_ref[i])
        acc = t if acc is None else acc + t
    y = _dot(acc.astype(BF16), wo_ref[...])
    xn = x_ref[0] + gate_ref[0] * y
    xn_o[0] = xn
    h2_o[0] = _rms(xn) * nw_ref[...] * (1.0 + sc_ref[0]) + sh_ref[0]


def _merge(outs, proj, x, gate, sc2, sh2, n2w, wb, wo):
    b, l, d = x.shape
    tm = min(512, l)
    tok = lambda w: pl.BlockSpec((1, tm, w), lambda bi, i: (bi, i, 0))
    vec = pl.BlockSpec((1, 1, d), lambda bi, i: (bi, 0, 0))
    gates = [pl.BlockSpec((1, tm, d), lambda bi, i, k=k: (bi, i, COL_GATES // d + k)) for k in range(N_BRANCH)]
    return pl.pallas_call(
        _merge_kernel,
        grid=(b, l // tm),
        in_specs=[tok(BRANCH_W)] * 4 + gates + [tok(d), vec, vec, vec,
                                                pl.BlockSpec((1, d), lambda bi, i: (0, 0)),
                                                pl.BlockSpec(wb.shape, lambda bi, i: (0, 0, 0)),
                                                pl.BlockSpec(wo.shape, lambda bi, i: (0, 0))],
        out_specs=[tok(d), tok(d)],
        out_shape=[jax.ShapeDtypeStruct((b, l, d), F32), jax.ShapeDtypeStruct((b, l, d), F32)],
        compiler_params=_cparams(("parallel", "parallel"), 48 * 1024 * 1024),
        name="merge",
    )(*outs, proj, proj, proj, proj, x, gate, sc2, sh2, n2w, wb, wo)


def _sorting_network(n):
    pairs = []
    p = 1
    while p < n:
        k = p
        while k >= 1:
            for j in range(k % p, n - k, 2 * k):
                for i in range(min(k, n - j - k)):
                    if (i + j) // (2 * p) == (i + j + k) // (2 * p):
                        pairs.append((i + j, i + j + k))
            k //= 2
        p *= 2
    return tuple(pairs)


def _topk_rows_slabs(s, k):
    r, t = s.shape
    assert r == 8 * k
    sub = lax.broadcasted_iota(jnp.int32, (8, t), 0).astype(F32)
    vs = [s[8 * j:8 * j + 8] for j in range(k)]
    ix = [sub + float(8 * j) for j in range(k)]
    for p, q in _sorting_network(k):
        va, ia, vb, ib = vs[p], ix[p], vs[q], ix[q]
        swap = (vb > va) | ((vb == va) & (ib < ia))
        vs[p], vs[q] = jnp.maximum(va, vb), jnp.minimum(va, vb)
        ix[p], ix[q] = jnp.where(swap, ib, ia), jnp.where(swap, ia, ib)
    out_row = lax.broadcasted_iota(jnp.int32, (k, t), 0)
    vals = jnp.zeros((k, t), F32)
    idxs = jnp.zeros((k, t), F32)
    for rnd in range(k):
        hv, hi = vs[0], ix[0]
        m = jnp.max(hv, axis=0, keepdims=True)
        idx = jnp.min(jnp.where(hv == m, hi, float(r)), axis=0, keepdims=True)
        vals = jnp.where(out_row == rnd, m, vals)
        idxs = jnp.where(out_row == rnd, idx, idxs)
        won = hi == idx
        for d in range(k - 1 - rnd):
            vs[d] = jnp.where(won, vs[d + 1], vs[d])
            ix[d] = jnp.where(won, ix[d + 1], ix[d])
    return vals, idxs.astype(jnp.int32)


def _select_rows(table, sel, k):
    out = jnp.zeros_like(table)
    for r in range(k):
        out = jnp.where(sel == r, table[r:r + 1, :], out)
    return out


def _split_bf16(x):
    hi = x.astype(BF16)
    return hi, (x - hi.astype(F32)).astype(BF16)


def _peer_joint_topk(s1, s2):
    k = PEER_TOPK
    t = s1.shape[1]
    sub = lax.broadcasted_iota(jnp.int32, (8, t), 0)
    depth = jnp.full((8, t), k // 8, jnp.int32)
    for i in range(6, -1, -1):
        depth = jnp.where(sub == i, k // (i + 1), depth)
    subf = sub.astype(F32)
    lo = [jnp.where(depth > d, s1[0:8] + s2[d:d + 1], -jnp.inf) for d in range(k)]
    hi0 = s1[8:16] + s2[0:1]
    pos_hi = (subf + 8.0) * float(k)
    cnt = jnp.zeros((8, t), F32)
    out_row = lax.broadcasted_iota(jnp.int32, (k, t), 0)
    vals = jnp.zeros((k, t), F32)
    poss = jnp.zeros((k, t), F32)
    big = float(k * k)
    for rnd in range(k):
        pos_lo = subf * float(k) + cnt
        m = jnp.max(jnp.maximum(lo[0], hi0), axis=0, keepdims=True)
        cand = jnp.minimum(jnp.where(lo[0] == m, pos_lo, big), jnp.where(hi0 == m, pos_hi, big))
        pos = jnp.min(cand, axis=0, keepdims=True)
        vals = jnp.where(out_row == rnd, m, vals)
        poss = jnp.where(out_row == rnd, pos, poss)
        won_lo = pos_lo == pos
        for d in range(k - 1 - rnd):
            lo[d] = jnp.where(won_lo, lo[d + 1], lo[d])
        cnt = cnt + jnp.where(won_lo, 1.0, 0.0)
        hi0 = jnp.where(pos_hi == pos, -jnp.inf, hi0)
    p = poss.astype(jnp.int32)
    return vals, p >> 4, p & (k - 1)


def _peer_route_kernel(h_ref, wqh_ref, wql_ref, kh_ref, kl_ref, a_o, b_o, g_o, q_s, a_s, b_s, g_s):
    k = PEER_TOPK
    assert k == 16
    hh, hl = _split_bf16(h_ref[...])
    q_s[...] = _dot(hh, wqh_ref[...]) + (_dot(hl, wqh_ref[...]) + _dot(hh, wql_ref[...]))

    def head(h):
        lo = pl.multiple_of(h * PEER_DK, PEER_DK)
        qh, ql = _split_bf16(q_s[:, pl.ds(lo, PEER_DK)])
        kh, kl = kh_ref[h], kl_ref[h]
        s = _dot_nt(kh, qh) + (_dot_nt(kh, ql) + _dot_nt(kl, qh))
        s1, i1 = _topk_rows_slabs(s[:PEER_N_KEYS], k)
        s2, i2 = _topk_rows_slabs(s[PEER_N_KEYS:], k)
        ts, ri, rj = _peer_joint_topk(s1, s2)
        e = jnp.exp(ts - ts[0:1, :])
        gate = e / jnp.sum(e, axis=0, keepdims=True)
        ro = pl.multiple_of(h * k, k)
        a_s[pl.ds(ro, k), :] = _select_rows(i1, ri, k)
        b_s[pl.ds(ro, k), :] = _select_rows(i2, rj, k)
        g_s[pl.ds(ro, k), :] = gate

    def head_group(i, _):
        for j in range(PEER_HEADS_PER_TRIP):
            head(PEER_HEADS_PER_TRIP * i + j)
        return 0

    lax.fori_loop(0, PEER_HEADS // PEER_HEADS_PER_TRIP, head_group, 0)
    a_o[...] = a_s[...].T
    b_o[...] = b_s[...].T
    g_o[...] = g_s[...].T


def _peer_route(h2, wq_hl, keys_hl):
    n, d = h2.shape
    t = 256
    wq_hi, wq_lo = wq_hl
    k_hi, k_lo = keys_hl
    return pl.pallas_call(
        _peer_route_kernel,
        grid=(n // t,),
        in_specs=[pl.BlockSpec((t, d), lambda i: (i, 0)),
                  pl.BlockSpec(wq_hi.shape, lambda i: (0, 0)),
                  pl.BlockSpec(wq_lo.shape, lambda i: (0, 0)),
                  pl.BlockSpec(k_hi.shape, lambda i: (0, 0, 0)),
                  pl.BlockSpec(k_lo.shape, lambda i: (0, 0, 0))],
        out_specs=[pl.BlockSpec((t, PEER_SLOTS), lambda i: (i, 0))] * 3,
        out_shape=[jax.ShapeDtypeStruct((n, PEER_SLOTS), jnp.int32),
                   jax.ShapeDtypeStruct((n, PEER_SLOTS), jnp.int32),
                   jax.ShapeDtypeStruct((n, PEER_SLOTS), F32)],
        scratch_shapes=[pltpu.VMEM((t, PEER_HEADS * PEER_DK), F32),
                        pltpu.VMEM((PEER_SLOTS, t), jnp.int32),
                        pltpu.VMEM((PEER_SLOTS, t), jnp.int32),
                        pltpu.VMEM((PEER_SLOTS, t), F32)],
        compiler_params=_cparams(("parallel",), 48 * 1024 * 1024),
        name="peer_route",
    )(h2, wq_hi, wq_lo, k_hi, k_lo)


_HI16 = -65536


def _bf16_bits(w):
    return lax.bitcast_convert_type(w, jnp.int32) & _HI16


def _peer_ffn_kernel(h_ref, a_ref, b_ref, g_ref, x_ref, gate_ref, u_ref, v_ref, fw_ref, o_ref, hb_s, w_s, *,
                     ec, unroll, final_norm):
    e = pl.program_id(1)
    t = h_ref.shape[0]
    half = t // 2
    nk = PEER_N_KEYS

    @pl.when(e == 0)
    def _():
        hb_s[...] = h_ref[...].astype(BF16)
        o_ref[...] = jnp.zeros_like(o_ref)
        jio = lax.broadcasted_iota(jnp.int32, (nk, PEER_SLOTS), 0)

        def tile(tt):
            arow = jnp.broadcast_to(a_ref[pl.ds(tt, 1), :], (nk, PEER_SLOTS))
            brow = jnp.broadcast_to(b_ref[pl.ds(tt, 1), :], (nk, PEER_SLOTS))
            grow = jnp.broadcast_to(g_ref[pl.ds(tt, 1), :], (nk, PEER_SLOTS))
            cm = jnp.where(jio == arow, grow, 0.0).astype(BF16)
            bm_t = jnp.where(jio == brow, 1.0, 0.0).T.astype(BF16)
            return _dot(cm, bm_t)

        def build(tb, _):
            for u in range(unroll):
                tt = tb * unroll + u
                word = _bf16_bits(tile(tt + half)) | lax.shift_right_logical(_bf16_bits(tile(tt)), 16)
                w_s[pl.ds(pl.multiple_of(tt * PEER_W_PITCH, 8), nk), :] = word
            return 0

        lax.fori_loop(0, half // unroll, build, 0)

    hid = _dot(hb_s[...], u_ref[0])
    j0 = e * (ec // nk)
    words = jnp.concatenate([w_s[pl.ds(j0 + j, half, stride=PEER_W_PITCH), :] for j in range(ec // nk)], axis=1)
    w_lo = lax.bitcast_convert_type(lax.shift_left(words, 16), F32)
    w_hi = lax.bitcast_convert_type(words & _HI16, F32)
    wc = jnp.concatenate([w_lo, w_hi], axis=0)
    act = 0.5 * hid * (1.0 + lax.erf(hid * SQRT_HALF))
    o_ref[...] += _dot((wc * act).astype(BF16), v_ref[...])

    @pl.when(e == pl.num_programs(1) - 1)
    def _():
        y = x_ref[...] + gate_ref[0] * o_ref[...]
        o_ref[...] = _rms(y) * fw_ref[...] if final_norm else y


def _peer_ffn(h2, a, b_idx, g, x, gate, u_blk, v, l, final_w, final_norm):
    n, d = h2.shape
    ne = v.shape[0]
    neb, _, ec = u_blk.shape
    t = min(512, l)
    unroll = 16
    assert l % t == 0 and ne == PEER_N_KEYS * PEER_N_KEYS and neb * ec == ne and (t // 2) % unroll == 0
    tok = lambda w: pl.BlockSpec((t, w), lambda i, e: (i, 0))
    return pl.pallas_call(
        functools.partial(_peer_ffn_kernel, ec=ec, unroll=unroll, final_norm=final_norm),
        grid=(n // t, neb),
        in_specs=[tok(d), tok(PEER_SLOTS), tok(PEER_SLOTS), tok(PEER_SLOTS), tok(d),
                  pl.BlockSpec((1, 1, d), lambda i, e: ((i * t) // l, 0, 0)),
                  pl.BlockSpec((1, d, ec), lambda i, e: (e, 0, 0)),
                  pl.BlockSpec((ec, d), lambda i, e: (e, 0)),
                  pl.BlockSpec((1, d), lambda i, e: (0, 0))],
        out_specs=tok(d),
        out_shape=jax.ShapeDtypeStruct((n, d), F32),
        scratch_shapes=[pltpu.VMEM((t, d), BF16),
                        pltpu.VMEM((t // 2 * PEER_W_PITCH, PEER_N_KEYS), jnp.int32)],
        compiler_params=_cparams(("parallel", "arbitrary"), VMEM_LIMIT_V7X),
        name="peer_ffn",
    )(h2, a, b_idx, g, x, gate, u_blk, v, final_w.reshape(1, d))


def _layout_w_in(w):
    parts = jnp.split(w, IN_OFFSETS, axis=1)
    z = lambda n: jnp.zeros((w.shape[0], n), w.dtype)
    kpe_blk = jnp.concatenate([z(MLA_NOPE), parts[2], z(MLA_HEAD_PAD - MLA_NOPE - MLA_ROPE)], axis=1)
    return jnp.concatenate([parts[0], parts[1], kpe_blk] + list(parts[3:]), axis=1).astype(BF16)


def _layout_mla(w_uq, w_ukv):
    qh = w_uq.reshape(MLA_Q_LORA, N_HEADS, MLA_NOPE + MLA_ROPE)
    qh = jnp.pad(qh, ((0, 0), (0, 0), (0, MLA_HEAD_PAD - MLA_NOPE - MLA_ROPE)))
    kv = w_ukv.reshape(MLA_KV_LORA, N_HEADS, MLA_NOPE + MLA_V)
    kh = jnp.pad(kv[:, :, :MLA_NOPE], ((0, 0), (0, 0), (0, MLA_HEAD_PAD - MLA_NOPE)))
    vh = kv[:, :, MLA_NOPE:]
    return (qh.reshape(MLA_Q_LORA, -1).astype(BF16), kh.reshape(MLA_KV_LORA, -1).astype(BF16),
            vh.reshape(MLA_KV_LORA, -1).T.astype(BF16))


def _static_mats():
    gseg = np.kron(np.eye(N_HEADS), np.ones((HEAD_W, HEAD_W))).astype(np.float32)
    pm = np.zeros((MLA_HEAD_PAD, MLA_HEAD_PAD), np.float32)
    for dd in range(MLA_ROPE):
        blk, j = dd // 16, dd % 16
        pm[MLA_NOPE + blk * 16 + (j + 8) % 16, MLA_NOPE + dd] = 1.0
    pg = np.zeros((BRANCH_W, BRANCH_W), np.float32)
    for i in range(BRANCH_W):
        off, dd = (i // HEAD_W) * HEAD_W, i % HEAD_W
        blk, j = dd // 32, dd % 32
        pg[off + blk * 32 + (j + 16) % 32, i] = 1.0
    ex = np.zeros((GQA_KV_HEADS * HEAD_W, BRANCH_W), np.float32)
    for i in range(BRANCH_W):
        ex[((i // HEAD_W) // (N_HEADS // GQA_KV_HEADS)) * HEAD_W + i % HEAD_W, i] = 1.0
    return tuple(jnp.asarray(m, dtype=BF16) for m in (gseg, pm, pg, ex, ex.T, np.eye(BRANCH_W)))


def _rope_half_tables(pos, hf):
    freqs = ROPE_THETA ** (-jnp.arange(hf, dtype=F32) / hf)
    ang = pos[:, None] * freqs[None, :]
    c, s = jnp.cos(ang), jnp.sin(ang)
    return jnp.concatenate([c, c], axis=1), jnp.concatenate([-s, s], axis=1)


def _axial_tables(row, col, dims):
    cr, sr = _rope_half_tables(row, dims // 4)
    cc, sc = _rope_half_tables(col, dims // 4)
    return jnp.concatenate([cr, cc], axis=1), jnp.concatenate([sr, sc], axis=1)


def _rope_tables(s):
    t = jnp.arange(s)
    row, col = (t // GRID_W).astype(F32), (t % GRID_W).astype(F32)
    c32, s32 = _axial_tables(row, col, MLA_ROPE)
    pad = MLA_HEAD_PAD - MLA_NOPE - MLA_ROPE
    cm = jnp.concatenate([jnp.ones((s, MLA_NOPE), F32), c32, jnp.ones((s, pad), F32)], axis=1)
    sm = jnp.concatenate([jnp.zeros((s, MLA_NOPE), F32), s32, jnp.zeros((s, pad), F32)], axis=1)
    c64, s64 = _axial_tables(row, col, HEAD_W)
    return cm, sm, jnp.tile(c64, (1, N_HEADS)), jnp.tile(s64, (1, N_HEADS))


def _split_f32(w):
    hi = w.astype(BF16)
    return hi, (w - hi.astype(F32)).astype(BF16)


def _layout_peer_keys(keys):
    h, _, nk, dh = keys.shape
    z = jnp.zeros((h, nk, dh), keys.dtype)
    top = jnp.concatenate([keys[:, 0], z], axis=2)
    bot = jnp.concatenate([z, keys[:, 1]], axis=2)
    return jnp.concatenate([top, bot], axis=1)


def kernel(x, c, ctx, c_ctx, mod_w, mod_b, norm1_w, norm2_w, w_in, mla_q_norm, mla_w_uq, mla_kv_norm, mla_w_ukv, gqa_q_norm, gqa_k_norm, na_bias, ret_decay_logit, ret_gn_w, w_branch, w_out, peer_w_q, peer_keys, peer_u, peer_v, final_norm_w):
    b, s, d = x.shape
    lc = ctx.shape[1]
    depth = mod_w.shape[0]
    assert d == D_MODEL and s % (GRID_W * NA_Q_ROWS) == 0 and s % 256 == 0 and lc % 256 == 0

    rows = -(-(b + 1) // 8) * 8
    cc = jnp.zeros((rows, d), F32).at[:b].set(c).at[b].set(c_ctx)
    mod = _modulation(cc, mod_w, mod_b)

    gseg, pm, pg, ex, ex_t, eye = _static_mats()
    tables = _rope_tables(s)

    for l in range(depth):
        need_ctx = l < depth - 1
        mx = mod[l, :b].reshape(b, 1, 6, d)
        mc = jnp.broadcast_to(mod[l, b].reshape(1, 1, 6, d), (b, 1, 6, d))
        sh1x, sc1x, g1x, sh2x, sc2x, g2x = (mx[:, :, i] for i in range(6))
        sh1c, sc1c, g1c, sh2c, sc2c, g2c = (mc[:, :, i] for i in range(6))

        w_in_l = _layout_w_in(w_in[l])
        wuq, wk, wv = _layout_mla(mla_w_uq[l], mla_w_ukv[l])
        consts = (mla_q_norm[l].reshape(1, -1), wuq, mla_kv_norm[l].reshape(1, -1), wk, wv,
                  jnp.tile(gqa_q_norm[l], N_HEADS).reshape(1, -1),
                  jnp.tile(gqa_k_norm[l], GQA_KV_HEADS).reshape(1, -1), gseg, pm, pg, ex, ex_t, eye)
        n1w = norm1_w[l].reshape(1, d)
        n2w = norm2_w[l].reshape(1, d)
        wb = w_branch[l].astype(BF16)
        wo = w_out[l].astype(BF16)
        keys_hl = _split_f32(_layout_peer_keys(peer_keys[l]))
        wq_hl = _split_f32(peer_w_q[l])
        ne = peer_u.shape[1]
        u_blk = jnp.swapaxes(peer_u[l].astype(BF16).reshape(ne // PEER_EXPERT_CHUNK, PEER_EXPERT_CHUNK, d), 1, 2)
        v_b = peer_v[l].astype(BF16)
        log_g = jax.nn.log_sigmoid(ret_decay_logit[l].astype(F32))

        projx = _inproj(x, n1w, sc1x, sh1x, w_in_l)
        projc = _inproj(ctx, n1w, sc1c, sh1c, w_in_l)
        qmx, kmx, vmx, qgx, kgx, vgx, nvx = _prep(projx, consts, tables)
        qmc, kmc, vmc, qgc, kgc, vgc, nvc = _prep(projc, consts, None)

        oa = _attention((qmx, 0), [((kmc, 0), vmc), ((kmx, 0), vmx)], MLA_HEAD_PAD, "attn_mla")
        ob = _attention((qgx, 0), [((kgc, 0), vgc), ((kgx, 0), vgx)], HEAD_W, "attn_gqa")
        oc = _na_attention(projx, projc, nvx, nvc, na_bias[l])
        od, od_c = _retention(projx, projc, log_g, ret_gn_w[l], gseg, need_ctx)

        x, h2x = _merge((oa, ob, oc, od), projx, x, g1x, sc2x, sh2x, n2w, wb, wo)
        ax, bx, gx = _peer_route(h2x.reshape(b * s, d), wq_hl, keys_hl)
        x = _peer_ffn(h2x.reshape(b * s, d), ax, bx, gx, x.reshape(b * s, d), g2x, u_blk, v_b, s,
                      final_norm_w, final_norm=(l == depth - 1)).reshape(b, s, d)

        if need_ctx:
            ca = _attention((qmc, 0), [((kmc, 0), vmc)], MLA_HEAD_PAD, "attn_mla_ctx")
            cb = _attention((qgc, 0), [((kgc, 0), vgc)], HEAD_W, "attn_gqa_ctx")
            ccx = _attention((projc, COL_NQ), [((projc, COL_NK), nvc)], HEAD_W, "attn_na_ctx",
                             qscale=HEAD_W ** -0.5, log2_scores=False)
            ctx, h2c = _merge((ca, cb, ccx, od_c), projc, ctx, g1c, sc2c, sh2c, n2w, wb, wo)
            ac, bc, gc = _peer_route(h2c.reshape(b * lc, d), wq_hl, keys_hl)
            ctx = _peer_ffn(h2c.reshape(b * lc, d), ac, bc, gc, ctx.reshape(b * lc, d), g2c, u_blk, v_b,
                            lc, final_norm_w, final_norm=False).reshape(b, lc, d)

    return x
```

```python
import functools

import numpy as np
import jax
import jax.numpy as jnp
from jax import lax
from jax.experimental import pallas as pl
from jax.experimental.pallas import tpu as pltpu

F32 = jnp.float32
BF16 = jnp.bfloat16
HIGHEST = lax.Precision.HIGHEST

D_MODEL = 1024
GRID_W = 64
ROPE_THETA = 10000.0
EPS = 1e-6
N_HEADS = 4
HEAD_W = 64
BRANCH_W = N_HEADS * HEAD_W
MLA_NOPE, MLA_ROPE, MLA_V = 64, 32, 64
MLA_Q_LORA, MLA_KV_LORA = 256, 128
MLA_SCALE = (MLA_NOPE + MLA_ROPE) ** -0.5
MLA_HEAD_PAD = 128
GQA_KV_HEADS = 2
NA_WIN_R, NA_WIN_C = 8, 16
NA_Q_ROWS = 4
ATTN_KEY_CHUNK = 1024
RET_CHUNK = 128
N_BRANCH = 4
PEER_HEADS, PEER_N_KEYS, PEER_TOPK, PEER_DK = 8, 128, 16, 128
PEER_SLOTS = PEER_HEADS * PEER_TOPK
PEER_W_PITCH = PEER_N_KEYS + 8
PEER_HEADS_PER_TRIP = 4
PEER_ROUTE_TOKENS = 256
PEER_EXPERT_CHUNK = 1024
SQRT_HALF = 0.7071067811865476
LOG2E = 1.4426950408889634
NEG_BIG = -1e30

IN_SIZES = (256, 128, 32, 256, 128, 128, 256, 256, 256, 256, 256, 256, 256, 256, 4096)
IN_OFFSETS = tuple(int(v) for v in np.cumsum(IN_SIZES)[:-1])
PROJ_COLS = 7168
COL_NQ, COL_NK, COL_NV = 1024, 1280, 1536
COL_RET = 1792
COL_GATES = 3072

VMEM_LIMIT_V7X = 56 * 1024 * 1024


def _cparams(sem, vmem=None):
    return pltpu.CompilerParams(dimension_semantics=sem, vmem_limit_bytes=vmem)


def _dot(a, b):
    return jnp.dot(a, b, preferred_element_type=F32)


def _dot_hi(a, b):
    return jnp.dot(a, b, preferred_element_type=F32, precision=HIGHEST)


def _dot_nt(a, b):
    return lax.dot_general(a, b, (((1,), (1,)), ((), ())), preferred_element_type=F32)


def _dot_sel(x, sel):
    hi = x.astype(BF16)
    r1 = x - hi.astype(F32)
    mid = r1.astype(BF16)
    lo = (r1 - mid.astype(F32)).astype(BF16)
    return _dot(hi, sel) + (_dot(mid, sel) + _dot(lo, sel))


def _rms(x):
    return x * lax.rsqrt(jnp.mean(x * x, axis=-1, keepdims=True) + EPS)


def _silu(x):
    return x * jax.nn.sigmoid(x)


def _head_mask(shape, h, width=HEAD_W):
    lane = lax.broadcasted_iota(jnp.int32, shape, len(shape) - 1)
    lo = h * width
    return (lane >= lo) & (lane < lo + width)


def _mod_kernel(c_ref, w_ref, b_ref, o_ref):
    o_ref[0] = _dot_hi(_silu(c_ref[...]), w_ref[0]) + b_ref[0]


def _modulation(cc, mod_w, mod_b):
    depth, d, n = mod_w.shape
    rows = cc.shape[0]
    tn = 1536
    return pl.pallas_call(
        _mod_kernel,
        grid=(depth, n // tn),
        in_specs=[pl.BlockSpec((rows, d), lambda l, j: (0, 0)),
                  pl.BlockSpec((1, d, tn), lambda l, j: (l, 0, j)),
                  pl.BlockSpec((1, 1, tn), lambda l, j: (l, 0, j))],
        out_specs=pl.BlockSpec((1, rows, tn), lambda l, j: (l, 0, j)),
        out_shape=jax.ShapeDtypeStruct((depth, rows, n), F32),
        compiler_params=_cparams(("parallel", "parallel"), 40 * 1024 * 1024),
        name="modulation",
    )(cc, mod_w, mod_b.reshape(depth, 1, n))


def _inproj_kernel(x_ref, nw_ref, sc_ref, sh_ref, w_ref, o_ref, h_scr):
    @pl.when(pl.program_id(2) == 0)
    def _():
        h = _rms(x_ref[0]) * nw_ref[...] * (1.0 + sc_ref[0]) + sh_ref[0]
        h_scr[...] = h.astype(BF16)

    o_ref[0] = _dot(h_scr[...], w_ref[...]).astype(o_ref.dtype)


def _inproj(x, nw, sc, sh, w):
    b, l, d = x.shape
    n = w.shape[1]
    tm = min(1024, l)
    tn = 1792
    return pl.pallas_call(
        _inproj_kernel,
        grid=(b, l // tm, n // tn),
        in_specs=[pl.BlockSpec((1, tm, d), lambda bi, i, j: (bi, i, 0)),
                  pl.BlockSpec((1, d), lambda bi, i, j: (0, 0)),
                  pl.BlockSpec((1, 1, d), lambda bi, i, j: (bi, 0, 0)),
                  pl.BlockSpec((1, 1, d), lambda bi, i, j: (bi, 0, 0)),
                  pl.BlockSpec((d, tn), lambda bi, i, j: (0, j))],
        out_specs=pl.BlockSpec((1, tm, tn), lambda bi, i, j: (bi, i, j)),
        out_shape=jax.ShapeDtypeStruct((b, l, n), BF16),
        scratch_shapes=[pltpu.VMEM((tm, d), BF16)],
        compiler_params=_cparams(("parallel", "parallel", "arbitrary"), 40 * 1024 * 1024),
        name="inproj",
    )(x, nw, sc, sh, w)


def _prep_kernel(*refs, use_rope):
    (p_ref, nv_ref, qn_ref, wuq_ref, kvn_ref, wk_ref, wv_ref, gqn_ref, gkn_ref, gseg_ref, pm_ref, pg_ref,
     e_ref, et_ref, eye_ref) = refs[:15]
    if use_rope:
        cm_ref, sm_ref, cg_ref, sg_ref = refs[15:19]
        outs = refs[19:]
    else:
        outs = refs[15:]
    qm_o, km_o, vm_o, qg_o, kg_o, vg_o, nvt_o = outs
    nvt_o[0] = _dot_nt(eye_ref[...], nv_ref[0]).astype(BF16)

    pb = p_ref[0]
    cq = pb[:, 0:256].astype(F32)
    ckv = pb[:, 256:384].astype(F32)
    kpe = pb[:, 384:512].astype(F32)
    gq = pb[:, 512:768].astype(F32)
    gk = pb[:, 768:896].astype(F32)
    gv = pb[:, 896:1024]

    cqn = (_rms(cq) * qn_ref[...]).astype(BF16)
    qa = _dot(cqn, wuq_ref[...])
    ckn = (_rms(ckv) * kvn_ref[...]).astype(BF16)
    kn = _dot(ckn, wk_ref[...])
    vm_t = _dot_nt(wv_ref[...], ckn)
    if use_rope:
        cm, sm = cm_ref[...], sm_ref[...]
        pm = pm_ref[...]

        def rope_m(t):
            return t * cm + _dot_sel(t, pm) * sm

        qa = jnp.concatenate([rope_m(qa[:, h * 128:(h + 1) * 128]) for h in range(N_HEADS)], axis=1)
        kpe = rope_m(kpe)
    km = kn + jnp.concatenate([kpe] * N_HEADS, axis=1)
    qm_o[0] = (qa * (MLA_SCALE * LOG2E)).astype(BF16)
    km_o[0] = km.astype(BF16)
    vm_o[0] = vm_t.astype(BF16)

    gseg = gseg_ref[...]
    gqn = gq * lax.rsqrt(_dot_sel(gq * gq, gseg) * (1.0 / HEAD_W) + EPS) * gqn_ref[...]
    gkn = gk * lax.rsqrt(_dot_sel(gk * gk, gseg[:128, :128]) * (1.0 / HEAD_W) + EPS) * gkn_ref[...]
    if use_rope:
        cg, sg = cg_ref[...], sg_ref[...]
        pg = pg_ref[...]
        gqn = gqn * cg + _dot_sel(gqn, pg) * sg
        gkn = gkn * cg[:, :128] + _dot_sel(gkn, pg[:128, :128]) * sg[:, :128]
    qg_o[0] = (gqn * (HEAD_W ** -0.5 * LOG2E)).astype(BF16)
    e = e_ref[...]
    kg_o[0] = _dot(gkn.astype(BF16), e).astype(BF16)
    vg_o[0] = _dot_nt(et_ref[...], gv).astype(BF16)


def _prep(proj, consts, tables):
    b, l, _ = proj.shape
    tm = min(512, l)
    use_rope = tables is not None
    full = lambda a: pl.BlockSpec(a.shape, lambda bi, i: (0,) * a.ndim)
    in_specs = [pl.BlockSpec((1, tm, 1024), lambda bi, i: (bi, i, 0)),
                pl.BlockSpec((1, tm, BRANCH_W), lambda bi, i: (bi, i, COL_NV // BRANCH_W))] + [full(a) for a in consts]
    args = [proj, proj] + list(consts)
    if use_rope:
        in_specs += [pl.BlockSpec((tm, t.shape[1]), lambda bi, i: (i, 0)) for t in tables]
        args += list(tables)
    tok = lambda w: (pl.BlockSpec((1, tm, w), lambda bi, i: (bi, i, 0)), jax.ShapeDtypeStruct((b, l, w), BF16))
    tr = (pl.BlockSpec((1, BRANCH_W, tm), lambda bi, i: (bi, 0, i)), jax.ShapeDtypeStruct((b, BRANCH_W, l), BF16))
    outs = (tok(512), tok(512), tr, tok(256), tok(256), tr, tr)
    return pl.pallas_call(
        functools.partial(_prep_kernel, use_rope=use_rope),
        grid=(b, l // tm),
        in_specs=in_specs,
        out_specs=[o[0] for o in outs],
        out_shape=[o[1] for o in outs],
        compiler_params=_cparams(("parallel", "parallel"), 40 * 1024 * 1024),
        name="prep_rope" if use_rope else "prep",
    )(*args)


def _attn_kernel(*refs, nseg, dq, tk, qscale, log2_scores):
    exp = jnp.exp2 if log2_scores else jnp.exp
    q_ref = refs[0]
    segs = [(refs[1 + 2 * i], refs[2 + 2 * i]) for i in range(nseg)]
    o_ref = refs[1 + 2 * nseg]
    tq = q_ref.shape[1]
    gw = 256
    hpg = gw // dq
    qstacks = []
    for g in range(N_HEADS // hpg):
        qg = q_ref[0, :, g * gw:(g + 1) * gw]
        if qscale is not None:
            qg = qg * jnp.asarray(qscale, BF16)
        qstacks.append(jnp.concatenate([jnp.where(_head_mask(qg.shape, j, dq), qg, jnp.zeros_like(qg))
                                        for j in range(hpg)], axis=0))
    ones_rows = 16
    carry = tuple((jnp.full((1, tq), -jnp.inf, F32), jnp.zeros((HEAD_W + ones_rows, tq), F32))
                  for _ in range(N_HEADS))

    def scores(k_ref, off, tkk):
        return [_dot_nt(k_ref[0, pl.ds(off, tkk), g * gw:(g + 1) * gw], qstacks[g])
                for g in range(N_HEADS // hpg)]

    def softmax_step(carry, head_scores, vt_ref, off, tkk):
        new = []
        ones = jnp.ones((ones_rows, tkk), BF16)
        for h in range(N_HEADS):
            m, acc = carry[h]
            vt = jnp.concatenate([vt_ref[0, h * HEAD_W:(h + 1) * HEAD_W, pl.ds(off, tkk)], ones], axis=0)
            st = head_scores(h)
            mn = jnp.maximum(m, jnp.max(st, axis=0, keepdims=True))
            acc = exp(m - mn) * acc + _dot(vt, exp(st - mn).astype(BF16))
            new.append((mn, acc))
        return tuple(new)

    for k_ref, vt_ref in segs:
        lk = k_ref.shape[1]
        tkk = min(tk, lk)
        n = lk // tkk

        def body(c, carry, k_ref=k_ref, vt_ref=vt_ref, tkk=tkk):
            off = c * tkk if isinstance(c, int) else pl.multiple_of(c * tkk, tkk)
            st_g = scores(k_ref, off, tkk)
            return softmax_step(carry, lambda h: st_g[h // hpg][:, (h % hpg) * tq:(h % hpg + 1) * tq],
                                vt_ref, off, tkk)

        carry = body(0, carry) if n == 1 else lax.fori_loop(0, n, body, carry)
    out_t = jnp.concatenate([acc[:HEAD_W] * (1.0 / acc[HEAD_W:HEAD_W + 1]) for _, acc in carry], axis=0)
    o_ref[0] = out_t.T.astype(o_ref.dtype)


def _attention(q, segs, dq, name, qscale=None, log2_scores=True):
    (qa, qcol) = q
    b, lq, _ = qa.shape
    wq = N_HEADS * dq
    tq = min(512, lq)
    assert qcol % wq == 0
    in_specs = [pl.BlockSpec((1, tq, wq), lambda bi, i: (bi, i, qcol // wq))]
    args = [qa]
    for (ka, kcol), vt in segs:
        assert kcol % wq == 0 and vt.shape[1] == BRANCH_W and vt.shape[2] == ka.shape[1]
        in_specs.append(pl.BlockSpec((1, ka.shape[1], wq), lambda bi, i, kcol=kcol: (bi, 0, kcol // wq)))
        in_specs.append(pl.BlockSpec((1, BRANCH_W, vt.shape[2]), lambda bi, i: (bi, 0, 0)))
        args += [ka, vt]
    return pl.pallas_call(
        functools.partial(_attn_kernel, nseg=len(segs), dq=dq, tk=ATTN_KEY_CHUNK, qscale=qscale,
                          log2_scores=log2_scores),
        grid=(b, lq // tq),
        in_specs=in_specs,
        out_specs=pl.BlockSpec((1, tq, BRANCH_W), lambda bi, i: (bi, i, 0)),
        out_shape=jax.ShapeDtypeStruct((b, lq, BRANCH_W), BF16),
        compiler_params=_cparams(("parallel", "arbitrary"), 48 * 1024 * 1024),
        name=name,
    )(*args)


def _proj_cols(arr, col, width, rows):
    assert col % width == 0
    return pl.BlockSpec((1, rows, width), lambda *idx: (idx[0], 0, col // width))


def _na_kernel(pat_ref, ks_ref, q_ref, k_ref, vt_ref, kc_ref, vct_ref, m_ref, o_ref, *, kw):
    del pat_ref
    g = pl.program_id(1)
    off = pl.multiple_of(ks_ref[g] * GRID_W, 128)
    q = q_ref[0] * jnp.asarray(HEAD_W ** -0.5, BF16)
    qb = q.shape[0]
    qstack = jnp.concatenate([jnp.where(_head_mask(q.shape, h), q, jnp.zeros_like(q))
                              for h in range(N_HEADS)], axis=0)
    st_w = _dot_nt(k_ref[0, pl.ds(off, kw), :], qstack)
    st_c = _dot_nt(kc_ref[0], qstack)
    ones_w = jnp.ones((16, kw), BF16)
    ones_c = jnp.ones((16, kc_ref.shape[1]), BF16)
    outs = []
    for h in range(N_HEADS):
        sw = st_w[:, h * qb:(h + 1) * qb] + m_ref[0, h]
        sc = st_c[:, h * qb:(h + 1) * qb]
        mx = jnp.maximum(jnp.max(sw, axis=0, keepdims=True), jnp.max(sc, axis=0, keepdims=True))
        vtw = jnp.concatenate([vt_ref[0, h * HEAD_W:(h + 1) * HEAD_W, pl.ds(off, kw)], ones_w], axis=0)
        vtc = jnp.concatenate([vct_ref[0, h * HEAD_W:(h + 1) * HEAD_W, :], ones_c], axis=0)
        acc = _dot(vtw, jnp.exp(sw - mx).astype(BF16)) + _dot(vtc, jnp.exp(sc - mx).astype(BF16))
        outs.append(acc[:HEAD_W] * (1.0 / acc[HEAD_W:HEAD_W + 1]))
    o_ref[0] = jnp.concatenate(outs, axis=0).T.astype(o_ref.dtype)


def _na_plan(s):
    rows = s // GRID_W
    wr = min(NA_WIN_R, rows)
    wc = NA_WIN_C
    qr = min(NA_Q_ROWS, rows)
    kwr = min(qr + wr - 1 + (1 if qr + wr - 1 < rows else 0), rows)
    ngrp = rows // qr
    qc = np.arange(GRID_W)[:, None]
    kc = np.arange(GRID_W)[None, :]
    cs = np.clip(qc - wc // 2, 0, GRID_W - wc)
    valid_c = (kc >= cs) & (kc < cs + wc)
    rel_c = np.where(valid_c, kc - qc + (NA_WIN_C - 1), 0)
    assert (valid_c.sum(1) == wc).all()
    pats, pat_ids, ks_rows = [], [], []
    for g in range(ngrp):
        r0 = g * qr
        ks = int(np.clip(r0 - wr // 2, 0, rows - kwr))
        assert (ks * GRID_W) % 128 == 0
        r = (r0 + np.arange(qr))[:, None]
        kr = (ks + np.arange(kwr))[None, :]
        rs = np.clip(r - wr // 2, 0, rows - wr)
        valid_r = (kr >= rs) & (kr < rs + wr)
        assert (valid_r.sum(1) == wr).all()
        rel_r = np.where(valid_r, kr - r + (NA_WIN_R - 1), 0)
        key = (valid_r.tobytes(), rel_r.tobytes())
        for pi, (pk, *_rest) in enumerate(pats):
            if pk == key:
                pat_ids.append(pi)
                break
        else:
            pat_ids.append(len(pats))
            pats.append((key, valid_r, rel_r))
        ks_rows.append(ks)
    valid_r = np.stack([p[1] for p in pats])
    rel_r = np.stack([p[2] for p in pats])
    return (qr, kwr, np.asarray(pat_ids, np.int32), np.asarray(ks_rows, np.int32), valid_r, rel_r, valid_c, rel_c)


def _na_bias_masks(na_bias, valid_r, rel_r, valid_c, rel_c):
    h = na_bias.shape[0]
    npat, qr, kwr = valid_r.shape
    ncol = 2 * NA_WIN_C - 1
    brow = na_bias[:, rel_r, :].astype(F32)
    onehot_c = ((rel_c[None] == np.arange(ncol)[:, None, None]) & valid_c[None]).astype(np.float32)
    m = jnp.einsum('hpqkc,cxy->phkyqx', brow, jnp.asarray(onehot_c), precision=HIGHEST)
    valid = (valid_r.transpose(0, 2, 1)[:, None, :, None, :, None]
             & valid_c.T[None, None, None, :, None, :])
    m = jnp.where(valid, m, NEG_BIG)
    return m.reshape(npat, h, kwr * GRID_W, qr * GRID_W)


def _na_attention(projx, projc, nvx_t, nvc_t, na_bias):
    b, s, _ = projx.shape
    lc = projc.shape[1]
    qr, kwr, pat_ids, ks_rows, valid_r, rel_r, valid_c, rel_c = _na_plan(s)
    qb, kw = qr * GRID_W, kwr * GRID_W
    assert kw % 128 == 0
    mb = _na_bias_masks(na_bias, valid_r, rel_r, valid_c, rel_c)
    grid_spec = pltpu.PrefetchScalarGridSpec(
        num_scalar_prefetch=2,
        grid=(b, s // qb),
        in_specs=[pl.BlockSpec((1, qb, 256), lambda bi, g, pat, ks: (bi, g, COL_NQ // 256)),
                  pl.BlockSpec((1, s, 256), lambda bi, g, pat, ks: (bi, 0, COL_NK // 256)),
                  pl.BlockSpec((1, BRANCH_W, s), lambda bi, g, pat, ks: (bi, 0, 0)),
                  pl.BlockSpec((1, lc, 256), lambda bi, g, pat, ks: (bi, 0, COL_NK // 256)),
                  pl.BlockSpec((1, BRANCH_W, lc), lambda bi, g, pat, ks: (bi, 0, 0)),
                  pl.BlockSpec((1, N_HEADS, kw, qb), lambda bi, g, pat, ks: (pat[g], 0, 0, 0))],
        out_specs=pl.BlockSpec((1, qb, BRANCH_W), lambda bi, g, pat, ks: (bi, g, 0)),
    )
    return pl.pallas_call(
        functools.partial(_na_kernel, kw=kw),
        grid_spec=grid_spec,
        out_shape=jax.ShapeDtypeStruct((b, s, BRANCH_W), BF16),
        compiler_params=_cparams(("parallel", "arbitrary"), 48 * 1024 * 1024),
        name="na_attention",
    )(jnp.asarray(pat_ids), jnp.asarray(ks_rows), projx, projx, nvx_t, projc, nvc_t, mb)


def _ret_kernel(lgs_ref, lgl_ref, gnw_ref, gseg_ref,
                qx, kx, vx, gfx, gbx, qc, kc, vc, gfc, gbc,
                yx_o, yc_o, of_s, ob_s, st_s, dec_s, qk_s, *, need_ctx):
    c = RET_CHUNK
    lc = qc.shape[1]
    sx = qx.shape[1]
    n_col = lax.broadcasted_iota(jnp.int32, (c, c), 0).astype(F32)
    m_row = lax.broadcasted_iota(jnp.int32, (c, c), 1).astype(F32)
    diff = n_col - m_row
    for h in range(N_HEADS):
        dec_s[0, :, h * c:(h + 1) * c] = jnp.where(diff >= 0, jnp.exp(lgs_ref[h] * jnp.maximum(diff, 0.0)), 0.0)
        dec_s[1, :, h * c:(h + 1) * c] = jnp.where(diff <= 0,
                                                   jnp.exp(lgs_ref[N_HEADS + h] * jnp.maximum(-diff, 0.0)), 0.0)
    pos = lax.broadcasted_iota(jnp.int32, (c, BRANCH_W), 0).astype(F32)
    lgf, lgb = lgl_ref[0], lgl_ref[1]
    qk_s[0] = jnp.exp(lgf * (pos + 1.0))
    qk_s[1] = jnp.exp(lgf * (c - 1.0 - pos))
    qk_s[2] = jnp.exp(lgb * (c - pos))
    qk_s[3] = jnp.exp(lgb * pos)
    cd_f = jnp.exp(lgf * float(c))
    cd_b = jnp.exp(lgb * float(c))
    st_s[...] = jnp.zeros_like(st_s)
    rowb = lax.broadcasted_iota(jnp.int32, (BRANCH_W, BRANCH_W), 0) // HEAD_W
    colb = lax.broadcasted_iota(jnp.int32, (BRANCH_W, BRANCH_W), 1) // HEAD_W
    bd_mask = rowb == colb

    def chunk_step(q, k, v, d, cd):
        kk = k * jnp.asarray(HEAD_W ** -0.5, BF16)
        state = st_s[d]
        o = _dot(q, state.astype(BF16)) * qk_s[2 * d]
        kstack = jnp.concatenate([jnp.where(_head_mask(kk.shape, h), kk, jnp.zeros_like(kk))
                                  for h in range(N_HEADS)], axis=0)
        vstack = jnp.concatenate([jnp.where(_head_mask(v.shape, h), v, jnp.zeros_like(v))
                                  for h in range(N_HEADS)], axis=0)
        inner = _dot_nt(q, kstack) * dec_s[d]
        o = o + _dot(inner.astype(BF16), vstack)
        kd = (kk.astype(F32) * qk_s[2 * d + 1]).astype(BF16)
        upd = lax.dot_general(kd, v, (((0,), (0,)), ((), ())), preferred_element_type=F32)
        st_s[d] = state * cd + jnp.where(bd_mask, upd, 0.0)
        return o

    gseg = gseg_ref[...]
    gnw = gnw_ref[...]

    def gnorm(o):
        mu = _dot_sel(o, gseg) * (1.0 / HEAD_W)
        dlt = o - mu
        var = _dot_sel(dlt * dlt, gseg) * (1.0 / HEAD_W)
        return dlt * lax.rsqrt(var + EPS) * gnw

    def combine(gf_ref, gb_ref, y_ref, base, ro):
        y = (gnorm(of_s[pl.ds(base + ro, c), :]) * _silu(gf_ref[0, pl.ds(ro, c), :].astype(F32))
             + gnorm(ob_s[pl.ds(base + ro, c), :]) * _silu(gb_ref[0, pl.ds(ro, c), :].astype(F32)))
        y_ref[0, pl.ds(ro, c), :] = y.astype(y_ref.dtype)

    def scan(q_ref, k_ref, v_ref, base, n, gated):
        def body(i, _, emit):
            fo = pl.multiple_of(i * c, c)
            bo = pl.multiple_of((n - 1 - i) * c, c)
            of_s[pl.ds(base + fo, c), :] = chunk_step(
                q_ref[0, pl.ds(fo, c), :], k_ref[0, pl.ds(fo, c), :], v_ref[0, pl.ds(fo, c), :], 0, cd_f)
            ob_s[pl.ds(base + bo, c), :] = chunk_step(
                q_ref[0, pl.ds(bo, c), :], k_ref[0, pl.ds(bo, c), :], v_ref[0, pl.ds(bo, c), :], 1, cd_b)
            if emit:
                combine(*gated, base, fo)
                combine(*gated, base, bo)
            return 0

        assert n % 2 == 0
        lax.fori_loop(0, n // 2, functools.partial(body, emit=False), 0)
        lax.fori_loop(n // 2, n, functools.partial(body, emit=gated is not None), 0)

    scan(qc, kc, vc, 0, lc // c, (gfc, gbc, yc_o) if need_ctx else None)
    scan(qx, kx, vx, lc, sx // c, (gfx, gbx, yx_o))
    if not need_ctx:
        yc_o[...] = jnp.zeros_like(yc_o)


def _retention(projx, projc, log_g, gn_w, gseg, need_ctx):
    b, s, _ = projx.shape
    lc = projc.shape[1]
    lgs = log_g.reshape(2 * N_HEADS)
    lgl = jnp.repeat(log_g, HEAD_W, axis=1).reshape(2, 1, BRANCH_W)
    xs = [_proj_cols(projx, COL_RET + 256 * i, 256, s) for i in range(5)]
    cs = [_proj_cols(projc, COL_RET + 256 * i, 256, lc) for i in range(5)]
    c = RET_CHUNK
    yx, yc = pl.pallas_call(
        functools.partial(_ret_kernel, need_ctx=need_ctx),
        grid=(b,),
        in_specs=[pl.BlockSpec(memory_space=pltpu.SMEM),
                  pl.BlockSpec((2, 1, BRANCH_W), lambda bi: (0, 0, 0)),
                  pl.BlockSpec((1, BRANCH_W), lambda bi: (0, 0)),
                  pl.BlockSpec((BRANCH_W, BRANCH_W), lambda bi: (0, 0))] + xs + cs,
        out_specs=[pl.BlockSpec((1, s, BRANCH_W), lambda bi: (bi, 0, 0)),
                   pl.BlockSpec((1, lc, BRANCH_W), lambda bi: (bi, 0, 0))],
        out_shape=[jax.ShapeDtypeStruct((b, s, BRANCH_W), BF16),
                   jax.ShapeDtypeStruct((b, lc, BRANCH_W), BF16)],
        scratch_shapes=[pltpu.VMEM((lc + s, BRANCH_W), F32),
                        pltpu.VMEM((lc + s, BRANCH_W), F32),
                        pltpu.VMEM((2, BRANCH_W, BRANCH_W), F32),
                        pltpu.VMEM((2, c, N_HEADS * c), F32),
                        pltpu.VMEM((4, c, BRANCH_W), F32)],
        compiler_params=_cparams(("parallel",), 48 * 1024 * 1024),
        name="retention",
    )(lgs, lgl, gn_w.reshape(1, BRANCH_W), gseg, *([projx] * 5), *([projc] * 5))
    return yx, yc


def _merge_kernel(oa, ob, oc, od, g0, g1, g2, g3, x_ref, gate_ref, sc_ref, sh_ref, nw_ref, wb_ref, wo_ref,
                  xn_o, h2_o):
    acc = None
    for i, (o, g) in enumerate(((oa, g0), (ob, g1), (oc, g2), (od, g3))):
        t = (0.5 * jnp.tanh(0.5 * g[0].astype(F32)) + 0.5) * _dot(o[0], wb_ref[i])
        acc = t if acc is None else acc + t
    y = _dot(acc.astype(BF16), wo_ref[...])
    xn = x_ref[0] + gate_ref[0] * y
    xn_o[0] = xn
    h2_o[0] = _rms(xn) * nw_ref[...] * (1.0 + sc_ref[0]) + sh_ref[0]


def _merge(outs, proj, x, gate, sc2, sh2, n2w, wb, wo):
    b, l, d = x.shape
    tm = min(512, l)
    tok = lambda w: pl.BlockSpec((1, tm, w), lambda bi, i: (bi, i, 0))
    vec = pl.BlockSpec((1, 1, d), lambda bi, i: (bi, 0, 0))
    gates = [pl.BlockSpec((1, tm, d), lambda bi, i, k=k: (bi, i, COL_GATES // d + k)) for k in range(N_BRANCH)]
    return pl.pallas_call(
        _merge_kernel,
        grid=(b, l // tm),
        in_specs=[tok(BRANCH_W)] * 4 + gates + [tok(d), vec, vec, vec,
                                                pl.BlockSpec((1, d), lambda bi, i: (0, 0)),
                                                pl.BlockSpec(wb.shape, lambda bi, i: (0, 0, 0)),
                                                pl.BlockSpec(wo.shape, lambda bi, i: (0, 0))],
        out_specs=[tok(d), tok(d)],
        out_shape=[jax.ShapeDtypeStruct((b, l, d), F32), jax.ShapeDtypeStruct((b, l, d), F32)],
        compiler_params=_cparams(("parallel", "parallel"), 48 * 1024 * 1024),
        name="merge",
    )(*outs, proj, proj, proj, proj, x, gate, sc2, sh2, n2w, wb, wo)


def _sorting_network(n):
    pairs = []
    p = 1
    while p < n:
        k = p
        while k >= 1:
            for j in range(k % p, n - k, 2 * k):
                for i in range(min(k, n - j - k)):
                    if (i + j) // (2 * p) == (i + j + k) // (2 * p):
                        pairs.append((i + j, i + j + k))
            k //= 2
        p *= 2
    return tuple(pairs)


def _topk_rows_slabs(s, k):
    r, t = s.shape
    assert r == 8 * k
    sub = lax.broadcasted_iota(jnp.int32, (8, t), 0).astype(F32)
    vs = [s[8 * j:8 * j + 8] for j in range(k)]
    ix = [sub + float(8 * j) for j in range(k)]
    for p, q in _sorting_network(k):
        va, ia, vb, ib = vs[p], ix[p], vs[q], ix[q]
        swap = (vb > va) | ((vb == va) & (ib < ia))
        vs[p], vs[q] = jnp.maximum(va, vb), jnp.minimum(va, vb)
        ix[p], ix[q] = jnp.where(swap, ib, ia), jnp.where(swap, ia, ib)
    out_row = lax.broadcasted_iota(jnp.int32, (k, t), 0)
    vals = jnp.zeros((k, t), F32)
    idxs = jnp.zeros((k, t), F32)
    for rnd in range(k):
        hv, hi = vs[0], ix[0]
        m = jnp.max(hv, axis=0, keepdims=True)
        idx = jnp.min(jnp.where(hv == m, hi, float(r)), axis=0, keepdims=True)
        vals = jnp.where(out_row == rnd, m, vals)
        idxs = jnp.where(out_row == rnd, idx, idxs)
        won = hi == idx
        for d in range(k - 1 - rnd):
            vs[d] = jnp.where(won, vs[d + 1], vs[d])
            ix[d] = jnp.where(won, ix[d + 1], ix[d])
    return vals, idxs.astype(jnp.int32)


def _select_rows(table, sel, k):
    out = jnp.zeros_like(table)
    for r in range(k):
        out = jnp.where(sel == r, table[r:r + 1, :], out)
    return out


def _split_bf16(x):
    hi = x.astype(BF16)
    return hi, (x - hi.astype(F32)).astype(BF16)


def _peer_joint_topk(s1, s2):
    k = PEER_TOPK
    t = s1.shape[1]
    sub = lax.broadcasted_iota(jnp.int32, (8, t), 0)
    depth = jnp.full((8, t), k // 8, jnp.int32)
    for i in range(6, -1, -1):
        depth = jnp.where(sub == i, k // (i + 1), depth)
    subf = sub.astype(F32)
    lo = [jnp.where(depth > d, s1[0:8] + s2[d:d + 1], -jnp.inf) for d in range(k)]
    hi0 = s1[8:16] + s2[0:1]
    pos_hi = (subf + 8.0) * float(k)
    cnt = jnp.zeros((8, t), F32)
    out_row = lax.broadcasted_iota(jnp.int32, (k, t), 0)
    vals = jnp.zeros((k, t), F32)
    poss = jnp.zeros((k, t), F32)
    big = float(k * k)
    for rnd in range(k):
        pos_lo = subf * float(k) + cnt
        m = jnp.max(jnp.maximum(lo[0], hi0), axis=0, keepdims=True)
        cand = jnp.minimum(jnp.where(lo[0] == m, pos_lo, big), jnp.where(hi0 == m, pos_hi, big))
        pos = jnp.min(cand, axis=0, keepdims=True)
        vals = jnp.where(out_row == rnd, m, vals)
        poss = jnp.where(out_row == rnd, pos, poss)
        won_lo = pos_lo == pos
        for d in range(k - 1 - rnd):
            lo[d] = jnp.where(won_lo, lo[d + 1], lo[d])
        cnt = cnt + jnp.where(won_lo, 1.0, 0.0)
        hi0 = jnp.where(pos_hi == pos, -jnp.inf, hi0)
    p = poss.astype(jnp.int32)
    return vals, p >> 4, p & (k - 1)


def _peer_route_head(q, kh, kl):
    k = PEER_TOPK
    assert k == 16
    qh, ql = _split_bf16(q)
    s = _dot_nt(kh, qh) + (_dot_nt(kh, ql) + _dot_nt(kl, qh))
    s1, i1 = _topk_rows_slabs(s[:PEER_N_KEYS], k)
    s2, i2 = _topk_rows_slabs(s[PEER_N_KEYS:], k)
    ts, ri, rj = _peer_joint_topk(s1, s2)
    e = jnp.exp(ts - ts[0:1, :])
    return _select_rows(i1, ri, k), _select_rows(i2, rj, k), e / jnp.sum(e, axis=0, keepdims=True)


def _peer_query(h, wqh, wql):
    hh, hl = _split_bf16(h)
    return _dot(hh, wqh) + (_dot(hl, wqh) + _dot(hh, wql))


def _peer_route_kernel(h_ref, wqh_ref, wql_ref, kh_ref, kl_ref, a_o, b_o, g_o, q_s, a_s, b_s, g_s):
    k = PEER_TOPK
    q_s[...] = _peer_query(h_ref[...], wqh_ref[...], wql_ref[...])

    def head(h):
        lo = pl.multiple_of(h * PEER_DK, PEER_DK)
        ro = pl.multiple_of(h * k, k)
        a_s[pl.ds(ro, k), :], b_s[pl.ds(ro, k), :], g_s[pl.ds(ro, k), :] = _peer_route_head(
            q_s[:, pl.ds(lo, PEER_DK)], kh_ref[h], kl_ref[h])

    def head_group(i, _):
        for j in range(PEER_HEADS_PER_TRIP):
            head(PEER_HEADS_PER_TRIP * i + j)
        return 0

    lax.fori_loop(0, PEER_HEADS // PEER_HEADS_PER_TRIP, head_group, 0)
    a_o[...] = a_s[...].T
    b_o[...] = b_s[...].T
    g_o[...] = g_s[...].T


def _peer_route(h2, wq_hl, keys_hl):
    n, d = h2.shape
    t = 256
    wq_hi, wq_lo = wq_hl
    k_hi, k_lo = keys_hl
    return pl.pallas_call(
        _peer_route_kernel,
        grid=(n // t,),
        in_specs=[pl.BlockSpec((t, d), lambda i: (i, 0)),
                  pl.BlockSpec(wq_hi.shape, lambda i: (0, 0)),
                  pl.BlockSpec(wq_lo.shape, lambda i: (0, 0)),
                  pl.BlockSpec(k_hi.shape, lambda i: (0, 0, 0)),
                  pl.BlockSpec(k_lo.shape, lambda i: (0, 0, 0))],
        out_specs=[pl.BlockSpec((t, PEER_SLOTS), lambda i: (i, 0))] * 3,
        out_shape=[jax.ShapeDtypeStruct((n, PEER_SLOTS), jnp.int32),
                   jax.ShapeDtypeStruct((n, PEER_SLOTS), jnp.int32),
                   jax.ShapeDtypeStruct((n, PEER_SLOTS), F32)],
        scratch_shapes=[pltpu.VMEM((t, PEER_HEADS * PEER_DK), F32),
                        pltpu.VMEM((PEER_SLOTS, t), jnp.int32),
                        pltpu.VMEM((PEER_SLOTS, t), jnp.int32),
                        pltpu.VMEM((PEER_SLOTS, t), F32)],
        compiler_params=_cparams(("parallel",), 48 * 1024 * 1024),
        name="peer_route",
    )(h2, wq_hi, wq_lo, k_hi, k_lo)


_HI16 = -65536


def _bf16_bits(w):
    return lax.bitcast_convert_type(w, jnp.int32) & _HI16


def _peer_ffn_kernel(*refs, ec, unroll, final_norm, route_next):
    if route_next:
        (h_ref, a_ref, b_ref, g_ref, x_ref, gate_ref, u_ref, v_ref, fw_ref, qn_ref, kh_ref, kl_ref,
         o_ref, hb_s, w_s, a_c, b_c, g_c, a_n, b_n, g_n) = refs
    else:
        h_ref, a_ref, b_ref, g_ref, x_ref, gate_ref, u_ref, v_ref, fw_ref, o_ref, hb_s, w_s = refs
        a_c, b_c, g_c = a_ref, b_ref, g_ref
    e = pl.program_id(1)
    t = h_ref.shape[0]
    half = t // 2
    nk = PEER_N_KEYS

    @pl.when(e == 0)
    def _():
        if route_next:
            @pl.when(pl.program_id(0) == 0)
            def _():
                a_c[...] = a_ref[...]
                b_c[...] = b_ref[...]
                g_c[...] = g_ref[...]

        hb_s[...] = h_ref[...].astype(BF16)
        o_ref[...] = jnp.zeros_like(o_ref)
        jio = lax.broadcasted_iota(jnp.int32, (nk, PEER_SLOTS), 0)

        def tile(tt):
            arow = jnp.broadcast_to(a_c[pl.ds(tt, 1), :], (nk, PEER_SLOTS))
            brow = jnp.broadcast_to(b_c[pl.ds(tt, 1), :], (nk, PEER_SLOTS))
            grow = jnp.broadcast_to(g_c[pl.ds(tt, 1), :], (nk, PEER_SLOTS))
            cm = jnp.where(jio == arow, grow, 0.0).astype(BF16)
            bm_t = jnp.where(jio == brow, 1.0, 0.0).T.astype(BF16)
            return _dot(cm, bm_t)

        def build(tb, _):
            for u in range(unroll):
                tt = tb * unroll + u
                word = _bf16_bits(tile(tt + half)) | lax.shift_right_logical(_bf16_bits(tile(tt)), 16)
                w_s[pl.ds(pl.multiple_of(tt * PEER_W_PITCH, 8), nk), :] = word
            return 0

        lax.fori_loop(0, half // unroll, build, 0)

    if route_next:
        nsub = t // PEER_ROUTE_TOKENS
        hd = e // nsub
        co = pl.multiple_of((e % nsub) * PEER_ROUTE_TOKENS, PEER_ROUTE_TOKENS)
        ro = pl.multiple_of(hd * PEER_TOPK, PEER_TOPK)
        routed = _peer_route_head(
            qn_ref[pl.ds(co, PEER_ROUTE_TOKENS), pl.ds(pl.multiple_of(hd * PEER_DK, PEER_DK), PEER_DK)],
            kh_ref[hd], kl_ref[hd])

    hid = _dot(hb_s[...], u_ref[0])
    j0 = e * (ec // nk)
    words = jnp.concatenate([w_s[pl.ds(j0 + j, half, stride=PEER_W_PITCH), :] for j in range(ec // nk)], axis=1)
    w_lo = lax.bitcast_convert_type(lax.shift_left(words, 16), F32)
    w_hi = lax.bitcast_convert_type(words & _HI16, F32)
    wc = jnp.concatenate([w_lo, w_hi], axis=0)
    act = 0.5 * hid * (1.0 + lax.erf(hid * SQRT_HALF))
    o_ref[...] += _dot((wc * act).astype(BF16), v_ref[...])

    if route_next:
        a_n[pl.ds(ro, PEER_TOPK), pl.ds(co, PEER_ROUTE_TOKENS)] = routed[0]
        b_n[pl.ds(ro, PEER_TOPK), pl.ds(co, PEER_ROUTE_TOKENS)] = routed[1]
        g_n[pl.ds(ro, PEER_TOPK), pl.ds(co, PEER_ROUTE_TOKENS)] = routed[2]

    @pl.when(e == pl.num_programs(1) - 1)
    def _():
        y = x_ref[...] + gate_ref[0] * o_ref[...]
        o_ref[...] = _rms(y) * fw_ref[...] if final_norm else y
        if route_next:
            a_c[...] = a_n[...].T
            b_c[...] = b_n[...].T
            g_c[...] = g_n[...].T


def _peer_ffn(h2, a, b_idx, g, x, gate, u_blk, v, l, final_w, final_norm, route_next=None):
    n, d = h2.shape
    ne = v.shape[0]
    neb, _, ec = u_blk.shape
    t = min(512, l)
    nb = n // t
    unroll = 16
    assert l % t == 0 and ne == PEER_N_KEYS * PEER_N_KEYS and neb * ec == ne and (t // 2) % unroll == 0
    tok = lambda w: pl.BlockSpec((t, w), lambda i, e: (i, 0))
    in_specs = [tok(d), tok(PEER_SLOTS), tok(PEER_SLOTS), tok(PEER_SLOTS), tok(d),
                pl.BlockSpec((1, 1, d), lambda i, e: ((i * t) // l, 0, 0)),
                pl.BlockSpec((1, d, ec), lambda i, e: (e, 0, 0)),
                pl.BlockSpec((ec, d), lambda i, e: (e, 0)),
                pl.BlockSpec((1, d), lambda i, e: (0, 0))]
    args = [h2, a, b_idx, g, x, gate, u_blk, v, final_w.reshape(1, d)]
    scratch = [pltpu.VMEM((t, d), BF16), pltpu.VMEM((t // 2 * PEER_W_PITCH, PEER_N_KEYS), jnp.int32)]
    if route_next is not None:
        q_all, k_hi, k_lo = route_next
        assert neb == PEER_HEADS * (t // PEER_ROUTE_TOKENS) and a.shape[0] == t
        first = pl.BlockSpec((t, PEER_SLOTS), lambda i, e: (0, 0))
        in_specs[1:4] = [first, first, first]
        in_specs += [pl.BlockSpec((t, q_all.shape[1]), lambda i, e: (jnp.minimum(i + 1, nb - 1), 0)),
                     pl.BlockSpec(k_hi.shape, lambda i, e: (0, 0, 0)),
                     pl.BlockSpec(k_lo.shape, lambda i, e: (0, 0, 0))]
        args += [q_all, k_hi, k_lo]
        scratch += [pltpu.VMEM((t, PEER_SLOTS), jnp.int32), pltpu.VMEM((t, PEER_SLOTS), jnp.int32),
                    pltpu.VMEM((t, PEER_SLOTS), F32),
                    pltpu.VMEM((PEER_SLOTS, t), jnp.int32), pltpu.VMEM((PEER_SLOTS, t), jnp.int32),
                    pltpu.VMEM((PEER_SLOTS, t), F32)]
    return pl.pallas_call(
        functools.partial(_peer_ffn_kernel, ec=ec, unroll=unroll, final_norm=final_norm,
                          route_next=route_next is not None),
        grid=(nb, neb),
        in_specs=in_specs,
        out_specs=tok(d),
        out_shape=jax.ShapeDtypeStruct((n, d), F32),
        scratch_shapes=scratch,
        compiler_params=_cparams(("arbitrary", "arbitrary"), VMEM_LIMIT_V7X),
        name="peer_ffn_routed" if route_next is not None else "peer_ffn",
    )(*args)


def _peer_query_kernel(h_ref, wqh_ref, wql_ref, q_o):
    q_o[...] = _peer_query(h_ref[...], wqh_ref[...], wql_ref[...])


def _peer_queries(h2, wq_hl):
    n, d = h2.shape
    wq_hi, wq_lo = wq_hl
    t = min(512, n)
    return pl.pallas_call(
        _peer_query_kernel,
        grid=(n // t,),
        in_specs=[pl.BlockSpec((t, d), lambda i: (i, 0)),
                  pl.BlockSpec(wq_hi.shape, lambda i: (0, 0)),
                  pl.BlockSpec(wq_lo.shape, lambda i: (0, 0))],
        out_specs=pl.BlockSpec((t, wq_hi.shape[1]), lambda i: (i, 0)),
        out_shape=jax.ShapeDtypeStruct((n, wq_hi.shape[1]), F32),
        compiler_params=_cparams(("parallel",), 40 * 1024 * 1024),
        name="peer_queries",
    )(h2, wq_hi, wq_lo)


def _layout_w_in(w):
    parts = jnp.split(w, IN_OFFSETS, axis=1)
    z = lambda n: jnp.zeros((w.shape[0], n), w.dtype)
    kpe_blk = jnp.concatenate([z(MLA_NOPE), parts[2], z(MLA_HEAD_PAD - MLA_NOPE - MLA_ROPE)], axis=1)
    return jnp.concatenate([parts[0], parts[1], kpe_blk] + list(parts[3:]), axis=1).astype(BF16)


def _layout_mla(w_uq, w_ukv):
    qh = w_uq.reshape(MLA_Q_LORA, N_HEADS, MLA_NOPE + MLA_ROPE)
    qh = jnp.pad(qh, ((0, 0), (0, 0), (0, MLA_HEAD_PAD - MLA_NOPE - MLA_ROPE)))
    kv = w_ukv.reshape(MLA_KV_LORA, N_HEADS, MLA_NOPE + MLA_V)
    kh = jnp.pad(kv[:, :, :MLA_NOPE], ((0, 0), (0, 0), (0, MLA_HEAD_PAD - MLA_NOPE)))
    vh = kv[:, :, MLA_NOPE:]
    return (qh.reshape(MLA_Q_LORA, -1).astype(BF16), kh.reshape(MLA_KV_LORA, -1).astype(BF16),
            vh.reshape(MLA_KV_LORA, -1).T.astype(BF16))


def _static_mats():
    gseg = np.kron(np.eye(N_HEADS), np.ones((HEAD_W, HEAD_W))).astype(np.float32)
    pm = np.zeros((MLA_HEAD_PAD, MLA_HEAD_PAD), np.float32)
    for dd in range(MLA_ROPE):
        blk, j = dd // 16, dd % 16
        pm[MLA_NOPE + blk * 16 + (j + 8) % 16, MLA_NOPE + dd] = 1.0
    pg = np.zeros((BRANCH_W, BRANCH_W), np.float32)
    for i in range(BRANCH_W):
        off, dd = (i // HEAD_W) * HEAD_W, i % HEAD_W
        blk, j = dd // 32, dd % 32
        pg[off + blk * 32 + (j + 16) % 32, i] = 1.0
    ex = np.zeros((GQA_KV_HEADS * HEAD_W, BRANCH_W), np.float32)
    for i in range(BRANCH_W):
        ex[((i // HEAD_W) // (N_HEADS // GQA_KV_HEADS)) * HEAD_W + i % HEAD_W, i] = 1.0
    return tuple(jnp.asarray(m, dtype=BF16) for m in (gseg, pm, pg, ex, ex.T, np.eye(BRANCH_W)))


def _rope_half_tables(pos, hf):
    freqs = ROPE_THETA ** (-jnp.arange(hf, dtype=F32) / hf)
    ang = pos[:, None] * freqs[None, :]
    c, s = jnp.cos(ang), jnp.sin(ang)
    return jnp.concatenate([c, c], axis=1), jnp.concatenate([-s, s], axis=1)


def _axial_tables(row, col, dims):
    cr, sr = _rope_half_tables(row, dims // 4)
    cc, sc = _rope_half_tables(col, dims // 4)
    return jnp.concatenate([cr, cc], axis=1), jnp.concatenate([sr, sc], axis=1)


def _rope_tables(s):
    t = jnp.arange(s)
    row, col = (t // GRID_W).astype(F32), (t % GRID_W).astype(F32)
    c32, s32 = _axial_tables(row, col, MLA_ROPE)
    pad = MLA_HEAD_PAD - MLA_NOPE - MLA_ROPE
    cm = jnp.concatenate([jnp.ones((s, MLA_NOPE), F32), c32, jnp.ones((s, pad), F32)], axis=1)
    sm = jnp.concatenate([jnp.zeros((s, MLA_NOPE), F32), s32, jnp.zeros((s, pad), F32)], axis=1)
    c64, s64 = _axial_tables(row, col, HEAD_W)
    return cm, sm, jnp.tile(c64, (1, N_HEADS)), jnp.tile(s64, (1, N_HEADS))


def _split_f32(w):
    hi = w.astype(BF16)
    return hi, (w - hi.astype(F32)).astype(BF16)


def _layout_peer_keys(keys):
    h, _, nk, dh = keys.shape
    z = jnp.zeros((h, nk, dh), keys.dtype)
    top = jnp.concatenate([keys[:, 0], z], axis=2)
    bot = jnp.concatenate([z, keys[:, 1]], axis=2)
    return jnp.concatenate([top, bot], axis=1)


def kernel(x, c, ctx, c_ctx, mod_w, mod_b, norm1_w, norm2_w, w_in, mla_q_norm, mla_w_uq, mla_kv_norm, mla_w_ukv, gqa_q_norm, gqa_k_norm, na_bias, ret_decay_logit, ret_gn_w, w_branch, w_out, peer_w_q, peer_keys, peer_u, peer_v, final_norm_w):
    b, s, d = x.shape
    lc = ctx.shape[1]
    depth = mod_w.shape[0]
    assert d == D_MODEL and s % (GRID_W * NA_Q_ROWS) == 0 and s % 256 == 0 and lc % 256 == 0

    rows = -(-(b + 1) // 8) * 8
    cc = jnp.zeros((rows, d), F32).at[:b].set(c).at[b].set(c_ctx)
    mod = _modulation(cc, mod_w, mod_b)

    gseg, pm, pg, ex, ex_t, eye = _static_mats()
    tables = _rope_tables(s)

    for l in range(depth):
        need_ctx = l < depth - 1
        mx = mod[l, :b].reshape(b, 1, 6, d)
        mc = jnp.broadcast_to(mod[l, b].reshape(1, 1, 6, d), (b, 1, 6, d))
        sh1x, sc1x, g1x, sh2x, sc2x, g2x = (mx[:, :, i] for i in range(6))
        sh1c, sc1c, g1c, sh2c, sc2c, g2c = (mc[:, :, i] for i in range(6))

        w_in_l = _layout_w_in(w_in[l])
        wuq, wk, wv = _layout_mla(mla_w_uq[l], mla_w_ukv[l])
        consts = (mla_q_norm[l].reshape(1, -1), wuq, mla_kv_norm[l].reshape(1, -1), wk, wv,
                  jnp.tile(gqa_q_norm[l], N_HEADS).reshape(1, -1),
                  jnp.tile(gqa_k_norm[l], GQA_KV_HEADS).reshape(1, -1), gseg, pm, pg, ex, ex_t, eye)
        n1w = norm1_w[l].reshape(1, d)
        n2w = norm2_w[l].reshape(1, d)
        wb = w_branch[l].astype(BF16)
        wo = w_out[l].astype(BF16)
        keys_hl = _split_f32(_layout_peer_keys(peer_keys[l]))
        wq_hl = _split_f32(peer_w_q[l])
        ne = peer_u.shape[1]
        u_blk = jnp.swapaxes(peer_u[l].astype(BF16).reshape(ne // PEER_EXPERT_CHUNK, PEER_EXPERT_CHUNK, d), 1, 2)
        v_b = peer_v[l].astype(BF16)
        log_g = jax.nn.log_sigmoid(ret_decay_logit[l].astype(F32))

        projx = _inproj(x, n1w, sc1x, sh1x, w_in_l)
        projc = _inproj(ctx, n1w, sc1c, sh1c, w_in_l)
        qmx, kmx, vmx, qgx, kgx, vgx, nvx = _prep(projx, consts, tables)
        qmc, kmc, vmc, qgc, kgc, vgc, nvc = _prep(projc, consts, None)

        oa = _attention((qmx, 0), [((kmc, 0), vmc), ((kmx, 0), vmx)], MLA_HEAD_PAD, "attn_mla")
        ob = _attention((qgx, 0), [((kgc, 0), vgc), ((kgx, 0), vgx)], HEAD_W, "attn_gqa")
        oc = _na_attention(projx, projc, nvx, nvc, na_bias[l])
        od, od_c = _retention(projx, projc, log_g, ret_gn_w[l], gseg, need_ctx)

        x, h2x = _merge((oa, ob, oc, od), projx, x, g1x, sc2x, sh2x, n2w, wb, wo)
        h2f = h2x.reshape(b * s, d)
        first = min(512, s)
        ax, bx, gx = _peer_route(h2f[:first], wq_hl, keys_hl)
        x = _peer_ffn(h2f, ax, bx, gx, x.reshape(b * s, d), g2x, u_blk, v_b, s, final_norm_w,
                      final_norm=(l == depth - 1),
                      route_next=(_peer_queries(h2f, wq_hl),) + keys_hl).reshape(b, s, d)

        if need_ctx:
            ca = _attention((qmc, 0), [((kmc, 0), vmc)], MLA_HEAD_PAD, "attn_mla_ctx")
            cb = _attention((qgc, 0), [((kgc, 0), vgc)], HEAD_W, "attn_gqa_ctx")
            ccx = _attention((projc, COL_NQ), [((projc, COL_NK), nvc)], HEAD_W, "attn_na_ctx",
                             qscale=HEAD_W ** -0.5, log2_scores=False)
            ctx, h2c = _merge((ca, cb, ccx, od_c), projc, ctx, g1c, sc2c, sh2c, n2w, wb, wo)
            ac, bc, gc = _peer_route(h2c.reshape(b * lc, d), wq_hl, keys_hl)
            ctx = _peer_ffn(h2c.reshape(b * lc, d), ac, bc, gc, ctx.reshape(b * lc, d), g2c, u_blk, v_b,
                            lc, final_norm_w, final_norm=False).reshape(b, lc, d)

    return x
```

```python
import functools

import numpy as np
import jax
import jax.numpy as jnp
from jax import lax
from jax.experimental import pallas as pl
from jax.experimental.pallas import tpu as pltpu

F32 = jnp.float32
BF16 = jnp.bfloat16
HIGHEST = lax.Precision.HIGHEST

D_MODEL = 1024
GRID_W = 64
ROPE_THETA = 10000.0
EPS = 1e-6
N_HEADS = 4
HEAD_W = 64
BRANCH_W = N_HEADS * HEAD_W
MLA_NOPE, MLA_ROPE, MLA_V = 64, 32, 64
MLA_Q_LORA, MLA_KV_LORA = 256, 128
MLA_SCALE = (MLA_NOPE + MLA_ROPE) ** -0.5
MLA_HEAD_PAD = 128
GQA_KV_HEADS = 2
NA_WIN_R, NA_WIN_C = 8, 16
NA_Q_ROWS = 4
ATTN_KEY_CHUNK = 1024
RET_CHUNK = 128
N_BRANCH = 4
PEER_HEADS, PEER_N_KEYS, PEER_TOPK, PEER_DK = 8, 128, 16, 128
PEER_SLOTS = PEER_HEADS * PEER_TOPK
PEER_W_PITCH = PEER_N_KEYS + 8
PEER_HEADS_PER_TRIP = 4
PEER_ROUTE_TOKENS = 256
PEER_EXPERT_CHUNK = 1024
SQRT_HALF = 0.7071067811865476
LOG2E = 1.4426950408889634
NEG_BIG = -1e30

IN_SIZES = (256, 128, 32, 256, 128, 128, 256, 256, 256, 256, 256, 256, 256, 256, 4096)
IN_OFFSETS = tuple(int(v) for v in np.cumsum(IN_SIZES)[:-1])
PROJ_COLS = 7168
COL_NQ, COL_NK, COL_NV = 1024, 1280, 1536
COL_RET = 1792
COL_GATES = 3072

VMEM_LIMIT_V7X = 56 * 1024 * 1024


def _cparams(sem, vmem=None):
    return pltpu.CompilerParams(dimension_semantics=sem, vmem_limit_bytes=vmem)


def _dot(a, b):
    return jnp.dot(a, b, preferred_element_type=F32)


def _dot_hi(a, b):
    return jnp.dot(a, b, preferred_element_type=F32, precision=HIGHEST)


def _dot_nt(a, b):
    return lax.dot_general(a, b, (((1,), (1,)), ((), ())), preferred_element_type=F32)


def _dot_sel(x, sel):
    hi = x.astype(BF16)
    r1 = x - hi.astype(F32)
    mid = r1.astype(BF16)
    lo = (r1 - mid.astype(F32)).astype(BF16)
    return _dot(hi, sel) + (_dot(mid, sel) + _dot(lo, sel))


def _rms(x):
    return x * lax.rsqrt(jnp.mean(x * x, axis=-1, keepdims=True) + EPS)


def _silu(x):
    return x * jax.nn.sigmoid(x)


def _head_mask(shape, h, width=HEAD_W):
    lane = lax.broadcasted_iota(jnp.int32, shape, len(shape) - 1)
    lo = h * width
    return (lane >= lo) & (lane < lo + width)


def _mod_kernel(c_ref, w_ref, b_ref, o_ref):
    o_ref[0] = _dot_hi(_silu(c_ref[...]), w_ref[0]) + b_ref[0]


def _modulation(cc, mod_w, mod_b):
    depth, d, n = mod_w.shape
    rows = cc.shape[0]
    tn = 1536
    return pl.pallas_call(
        _mod_kernel,
        grid=(depth, n // tn),
        in_specs=[pl.BlockSpec((rows, d), lambda l, j: (0, 0)),
                  pl.BlockSpec((1, d, tn), lambda l, j: (l, 0, j)),
                  pl.BlockSpec((1, 1, tn), lambda l, j: (l, 0, j))],
        out_specs=pl.BlockSpec((1, rows, tn), lambda l, j: (l, 0, j)),
        out_shape=jax.ShapeDtypeStruct((depth, rows, n), F32),
        compiler_params=_cparams(("parallel", "parallel"), 40 * 1024 * 1024),
        name="modulation",
    )(cc, mod_w, mod_b.reshape(depth, 1, n))


def _inproj_kernel(x_ref, nw_ref, sc_ref, sh_ref, w_ref, o_ref, h_scr):
    @pl.when(pl.program_id(2) == 0)
    def _():
        h = _rms(x_ref[0]) * nw_ref[...] * (1.0 + sc_ref[0]) + sh_ref[0]
        h_scr[...] = h.astype(BF16)

    o_ref[0] = _dot(h_scr[...], w_ref[...]).astype(o_ref.dtype)


def _inproj(x, nw, sc, sh, w):
    b, l, d = x.shape
    n = w.shape[1]
    tm = min(1024, l)
    tn = 1792
    return pl.pallas_call(
        _inproj_kernel,
        grid=(b, l // tm, n // tn),
        in_specs=[pl.BlockSpec((1, tm, d), lambda bi, i, j: (bi, i, 0)),
                  pl.BlockSpec((1, d), lambda bi, i, j: (0, 0)),
                  pl.BlockSpec((1, 1, d), lambda bi, i, j: (bi, 0, 0)),
                  pl.BlockSpec((1, 1, d), lambda bi, i, j: (bi, 0, 0)),
                  pl.BlockSpec((d, tn), lambda bi, i, j: (0, j))],
        out_specs=pl.BlockSpec((1, tm, tn), lambda bi, i, j: (bi, i, j)),
        out_shape=jax.ShapeDtypeStruct((b, l, n), BF16),
        scratch_shapes=[pltpu.VMEM((tm, d), BF16)],
        compiler_params=_cparams(("parallel", "parallel", "arbitrary"), 40 * 1024 * 1024),
        name="inproj",
    )(x, nw, sc, sh, w)


def _prep_kernel(*refs, use_rope):
    (p_ref, nv_ref, qn_ref, wuq_ref, kvn_ref, wk_ref, wv_ref, gqn_ref, gkn_ref, gseg_ref, pm_ref, pg_ref,
     e_ref, et_ref, eye_ref) = refs[:15]
    if use_rope:
        cm_ref, sm_ref, cg_ref, sg_ref = refs[15:19]
        outs = refs[19:]
    else:
        outs = refs[15:]
    qm_o, km_o, vm_o, qg_o, kg_o, vg_o, nvt_o = outs
    nvt_o[0] = _dot_nt(eye_ref[...], nv_ref[0]).astype(BF16)

    pb = p_ref[0]
    cq = pb[:, 0:256].astype(F32)
    ckv = pb[:, 256:384].astype(F32)
    kpe = pb[:, 384:512].astype(F32)
    gq = pb[:, 512:768].astype(F32)
    gk = pb[:, 768:896].astype(F32)
    gv = pb[:, 896:1024]

    cqn = (_rms(cq) * qn_ref[...]).astype(BF16)
    qa = _dot(cqn, wuq_ref[...])
    ckn = (_rms(ckv) * kvn_ref[...]).astype(BF16)
    kn = _dot(ckn, wk_ref[...])
    vm_t = _dot_nt(wv_ref[...], ckn)
    if use_rope:
        cm, sm = cm_ref[...], sm_ref[...]
        pm = pm_ref[...]

        def rope_m(t):
            return t * cm + _dot_sel(t, pm) * sm

        qa = jnp.concatenate([rope_m(qa[:, h * 128:(h + 1) * 128]) for h in range(N_HEADS)], axis=1)
        kpe = rope_m(kpe)
    km = kn + jnp.concatenate([kpe] * N_HEADS, axis=1)
    qm_o[0] = (qa * (MLA_SCALE * LOG2E)).astype(BF16)
    km_o[0] = km.astype(BF16)
    vm_o[0] = vm_t.astype(BF16)

    gseg = gseg_ref[...]
    gqn = gq * lax.rsqrt(_dot_sel(gq * gq, gseg) * (1.0 / HEAD_W) + EPS) * gqn_ref[...]
    gkn = gk * lax.rsqrt(_dot_sel(gk * gk, gseg[:128, :128]) * (1.0 / HEAD_W) + EPS) * gkn_ref[...]
    if use_rope:
        cg, sg = cg_ref[...], sg_ref[...]
        pg = pg_ref[...]
        gqn = gqn * cg + _dot_sel(gqn, pg) * sg
        gkn = gkn * cg[:, :128] + _dot_sel(gkn, pg[:128, :128]) * sg[:, :128]
    qg_o[0] = (gqn * (HEAD_W ** -0.5 * LOG2E)).astype(BF16)
    e = e_ref[...]
    kg_o[0] = _dot(gkn.astype(BF16), e).astype(BF16)
    vg_o[0] = _dot_nt(et_ref[...], gv).astype(BF16)


def _prep(proj, consts, tables):
    b, l, _ = proj.shape
    tm = min(512, l)
    use_rope = tables is not None
    full = lambda a: pl.BlockSpec(a.shape, lambda bi, i: (0,) * a.ndim)
    in_specs = [pl.BlockSpec((1, tm, 1024), lambda bi, i: (bi, i, 0)),
                pl.BlockSpec((1, tm, BRANCH_W), lambda bi, i: (bi, i, COL_NV // BRANCH_W))] + [full(a) for a in consts]
    args = [proj, proj] + list(consts)
    if use_rope:
        in_specs += [pl.BlockSpec((tm, t.shape[1]), lambda bi, i: (i, 0)) for t in tables]
        args += list(tables)
    tok = lambda w: (pl.BlockSpec((1, tm, w), lambda bi, i: (bi, i, 0)), jax.ShapeDtypeStruct((b, l, w), BF16))
    tr = (pl.BlockSpec((1, BRANCH_W, tm), lambda bi, i: (bi, 0, i)), jax.ShapeDtypeStruct((b, BRANCH_W, l), BF16))
    outs = (tok(512), tok(512), tr, tok(256), tok(256), tr, tr)
    return pl.pallas_call(
        functools.partial(_prep_kernel, use_rope=use_rope),
        grid=(b, l // tm),
        in_specs=in_specs,
        out_specs=[o[0] for o in outs],
        out_shape=[o[1] for o in outs],
        compiler_params=_cparams(("parallel", "parallel"), 40 * 1024 * 1024),
        name="prep_rope" if use_rope else "prep",
    )(*args)


def _attn_kernel(*refs, nseg, probs, tk):
    per = 1 + 2 * nseg
    nprob = len(probs)
    gw = 256
    ones_rows = 16
    state = []
    for pi, (dq, qscale, log2_scores) in enumerate(probs):
        q_ref = refs[pi * per]
        segs = [(refs[pi * per + 1 + 2 * i], refs[pi * per + 2 + 2 * i]) for i in range(nseg)]
        tq = q_ref.shape[1]
        hpg = gw // dq
        qstacks = []
        for g in range(N_HEADS // hpg):
            qg = q_ref[0, :, g * gw:(g + 1) * gw]
            if qscale is not None:
                qg = qg * jnp.asarray(qscale, BF16)
            qstacks.append(jnp.concatenate([jnp.where(_head_mask(qg.shape, j, dq), qg, jnp.zeros_like(qg))
                                            for j in range(hpg)], axis=0))
        state.append((segs, tq, hpg, qstacks, jnp.exp2 if log2_scores else jnp.exp))
    carry = tuple(tuple((jnp.full((1, st[1]), -jnp.inf, F32), jnp.zeros((HEAD_W + ones_rows, st[1]), F32))
                        for _ in range(N_HEADS)) for st in state)

    def chunk_step(pi, carry, si, off, tkk):
        segs, tq, hpg, qstacks, exp = state[pi]
        k_ref, vt_ref = segs[si]
        st_g = [_dot_nt(k_ref[0, pl.ds(off, tkk), g * gw:(g + 1) * gw], qstacks[g])
                for g in range(N_HEADS // hpg)]
        ones = jnp.ones((ones_rows, tkk), BF16)
        new = []
        for h in range(N_HEADS):
            m, acc = carry[h]
            vt = jnp.concatenate([vt_ref[0, h * HEAD_W:(h + 1) * HEAD_W, pl.ds(off, tkk)], ones], axis=0)
            st = st_g[h // hpg][:, (h % hpg) * tq:(h % hpg + 1) * tq]
            mn = jnp.maximum(m, jnp.max(st, axis=0, keepdims=True))
            acc = exp(m - mn) * acc + _dot(vt, exp(st - mn).astype(BF16))
            new.append((mn, acc))
        return tuple(new)

    for si in range(nseg):
        lk = state[0][0][si][0].shape[1]
        tkk = min(tk, lk)
        n = lk // tkk

        def body(c, carry, si=si, tkk=tkk):
            off = c * tkk if isinstance(c, int) else pl.multiple_of(c * tkk, tkk)
            return tuple(chunk_step(pi, carry[pi], si, off, tkk) for pi in range(nprob))

        carry = body(0, carry) if n == 1 else lax.fori_loop(0, n, body, carry)
    for pi in range(nprob):
        o_ref = refs[nprob * per + pi]
        out_t = jnp.concatenate([acc[:HEAD_W] * (1.0 / acc[HEAD_W:HEAD_W + 1]) for _, acc in carry[pi]], axis=0)
        o_ref[0] = out_t.T.astype(o_ref.dtype)


def _attention(problems, name):
    b, lq, _ = problems[0]["q"][0].shape
    tq = min(512, lq)
    in_specs, args, probs = [], [], []
    for p in problems:
        (qa, qcol), dq = p["q"], p["dq"]
        wq = N_HEADS * dq
        assert qcol % wq == 0 and qa.shape[:2] == (b, lq)
        in_specs.append(pl.BlockSpec((1, tq, wq), lambda bi, i, qcol=qcol, wq=wq: (bi, i, qcol // wq)))
        args.append(qa)
        for (ka, kcol), vt in p["segs"]:
            assert kcol % wq == 0 and vt.shape[1] == BRANCH_W and vt.shape[2] == ka.shape[1]
            in_specs.append(pl.BlockSpec((1, ka.shape[1], wq), lambda bi, i, kcol=kcol, wq=wq: (bi, 0, kcol // wq)))
            in_specs.append(pl.BlockSpec((1, BRANCH_W, vt.shape[2]), lambda bi, i: (bi, 0, 0)))
            args += [ka, vt]
        probs.append((dq, p.get("qscale"), p.get("log2_scores", True)))
    nseg = len(problems[0]["segs"])
    assert all(len(p["segs"]) == nseg for p in problems)
    outs = pl.pallas_call(
        functools.partial(_attn_kernel, nseg=nseg, probs=tuple(probs), tk=ATTN_KEY_CHUNK),
        grid=(b, lq // tq),
        in_specs=in_specs,
        out_specs=[pl.BlockSpec((1, tq, BRANCH_W), lambda bi, i: (bi, i, 0))] * len(problems),
        out_shape=[jax.ShapeDtypeStruct((b, lq, BRANCH_W), BF16)] * len(problems),
        compiler_params=_cparams(("parallel", "arbitrary"), VMEM_LIMIT_V7X),
        name=name,
    )(*args)
    return outs


def _proj_cols(arr, col, width, rows):
    assert col % width == 0
    return pl.BlockSpec((1, rows, width), lambda *idx: (idx[0], 0, col // width))


def _na_kernel(pat_ref, ks_ref, q_ref, k_ref, vt_ref, kc_ref, vct_ref, m_ref, o_ref, *, kw):
    del pat_ref
    g = pl.program_id(1)
    off = pl.multiple_of(ks_ref[g] * GRID_W, 128)
    q = q_ref[0] * jnp.asarray(HEAD_W ** -0.5, BF16)
    qb = q.shape[0]
    qstack = jnp.concatenate([jnp.where(_head_mask(q.shape, h), q, jnp.zeros_like(q))
                              for h in range(N_HEADS)], axis=0)
    st_w = _dot_nt(k_ref[0, pl.ds(off, kw), :], qstack)
    st_c = _dot_nt(kc_ref[0], qstack)
    ones_w = jnp.ones((16, kw), BF16)
    ones_c = jnp.ones((16, kc_ref.shape[1]), BF16)
    outs = []
    for h in range(N_HEADS):
        sw = st_w[:, h * qb:(h + 1) * qb] + m_ref[0, h]
        sc = st_c[:, h * qb:(h + 1) * qb]
        mx = jnp.maximum(jnp.max(sw, axis=0, keepdims=True), jnp.max(sc, axis=0, keepdims=True))
        vtw = jnp.concatenate([vt_ref[0, h * HEAD_W:(h + 1) * HEAD_W, pl.ds(off, kw)], ones_w], axis=0)
        vtc = jnp.concatenate([vct_ref[0, h * HEAD_W:(h + 1) * HEAD_W, :], ones_c], axis=0)
        acc = _dot(vtw, jnp.exp(sw - mx).astype(BF16)) + _dot(vtc, jnp.exp(sc - mx).astype(BF16))
        outs.append(acc[:HEAD_W] * (1.0 / acc[HEAD_W:HEAD_W + 1]))
    o_ref[0] = jnp.concatenate(outs, axis=0).T.astype(o_ref.dtype)


def _na_plan(s):
    rows = s // GRID_W
    wr = min(NA_WIN_R, rows)
    wc = NA_WIN_C
    qr = min(NA_Q_ROWS, rows)
    kwr = min(qr + wr - 1 + (1 if qr + wr - 1 < rows else 0), rows)
    ngrp = rows // qr
    qc = np.arange(GRID_W)[:, None]
    kc = np.arange(GRID_W)[None, :]
    cs = np.clip(qc - wc // 2, 0, GRID_W - wc)
    valid_c = (kc >= cs) & (kc < cs + wc)
    rel_c = np.where(valid_c, kc - qc + (NA_WIN_C - 1), 0)
    assert (valid_c.sum(1) == wc).all()
    pats, pat_ids, ks_rows = [], [], []
    for g in range(ngrp):
        r0 = g * qr
        ks = int(np.clip(r0 - wr // 2, 0, rows - kwr))
        assert (ks * GRID_W) % 128 == 0
        r = (r0 + np.arange(qr))[:, None]
        kr = (ks + np.arange(kwr))[None, :]
        rs = np.clip(r - wr // 2, 0, rows - wr)
        valid_r = (kr >= rs) & (kr < rs + wr)
        assert (valid_r.sum(1) == wr).all()
        rel_r = np.where(valid_r, kr - r + (NA_WIN_R - 1), 0)
        key = (valid_r.tobytes(), rel_r.tobytes())
        for pi, (pk, *_rest) in enumerate(pats):
            if pk == key:
                pat_ids.append(pi)
                break
        else:
            pat_ids.append(len(pats))
            pats.append((key, valid_r, rel_r))
        ks_rows.append(ks)
    valid_r = np.stack([p[1] for p in pats])
    rel_r = np.stack([p[2] for p in pats])
    return (qr, kwr, np.asarray(pat_ids, np.int32), np.asarray(ks_rows, np.int32), valid_r, rel_r, valid_c, rel_c)


def _na_bias_masks(na_bias, valid_r, rel_r, valid_c, rel_c):
    h = na_bias.shape[0]
    npat, qr, kwr = valid_r.shape
    ncol = 2 * NA_WIN_C - 1
    brow = na_bias[:, rel_r, :].astype(F32)
    onehot_c = ((rel_c[None] == np.arange(ncol)[:, None, None]) & valid_c[None]).astype(np.float32)
    m = jnp.einsum('hpqkc,cxy->phkyqx', brow, jnp.asarray(onehot_c), precision=HIGHEST)
    valid = (valid_r.transpose(0, 2, 1)[:, None, :, None, :, None]
             & valid_c.T[None, None, None, :, None, :])
    m = jnp.where(valid, m, NEG_BIG)
    return m.reshape(npat, h, kwr * GRID_W, qr * GRID_W)


def _na_attention(projx, projc, nvx_t, nvc_t, na_bias):
    b, s, _ = projx.shape
    lc = projc.shape[1]
    qr, kwr, pat_ids, ks_rows, valid_r, rel_r, valid_c, rel_c = _na_plan(s)
    qb, kw = qr * GRID_W, kwr * GRID_W
    assert kw % 128 == 0
    mb = _na_bias_masks(na_bias, valid_r, rel_r, valid_c, rel_c)
    grid_spec = pltpu.PrefetchScalarGridSpec(
        num_scalar_prefetch=2,
        grid=(b, s // qb),
        in_specs=[pl.BlockSpec((1, qb, 256), lambda bi, g, pat, ks: (bi, g, COL_NQ // 256)),
                  pl.BlockSpec((1, s, 256), lambda bi, g, pat, ks: (bi, 0, COL_NK // 256)),
                  pl.BlockSpec((1, BRANCH_W, s), lambda bi, g, pat, ks: (bi, 0, 0)),
                  pl.BlockSpec((1, lc, 256), lambda bi, g, pat, ks: (bi, 0, COL_NK // 256)),
                  pl.BlockSpec((1, BRANCH_W, lc), lambda bi, g, pat, ks: (bi, 0, 0)),
                  pl.BlockSpec((1, N_HEADS, kw, qb), lambda bi, g, pat, ks: (pat[g], 0, 0, 0))],
        out_specs=pl.BlockSpec((1, qb, BRANCH_W), lambda bi, g, pat, ks: (bi, g, 0)),
    )
    return pl.pallas_call(
        functools.partial(_na_kernel, kw=kw),
        grid_spec=grid_spec,
        out_shape=jax.ShapeDtypeStruct((b, s, BRANCH_W), BF16),
        compiler_params=_cparams(("parallel", "arbitrary"), 48 * 1024 * 1024),
        name="na_attention",
    )(jnp.asarray(pat_ids), jnp.asarray(ks_rows), projx, projx, nvx_t, projc, nvc_t, mb)


def _ret_kernel(lgs_ref, lgl_ref, gnw_ref, gseg_ref,
                qx, kx, vx, gfx, gbx, qc, kc, vc, gfc, gbc,
                yx_o, yc_o, of_s, ob_s, st_s, dec_s, qk_s, *, need_ctx):
    c = RET_CHUNK
    lc = qc.shape[1]
    sx = qx.shape[1]
    n_col = lax.broadcasted_iota(jnp.int32, (c, c), 0).astype(F32)
    m_row = lax.broadcasted_iota(jnp.int32, (c, c), 1).astype(F32)
    diff = n_col - m_row
    for h in range(N_HEADS):
        dec_s[0, :, h * c:(h + 1) * c] = jnp.where(diff >= 0, jnp.exp(lgs_ref[h] * jnp.maximum(diff, 0.0)), 0.0)
        dec_s[1, :, h * c:(h + 1) * c] = jnp.where(diff <= 0,
                                                   jnp.exp(lgs_ref[N_HEADS + h] * jnp.maximum(-diff, 0.0)), 0.0)
    pos = lax.broadcasted_iota(jnp.int32, (c, BRANCH_W), 0).astype(F32)
    lgf, lgb = lgl_ref[0], lgl_ref[1]
    qk_s[0] = jnp.exp(lgf * (pos + 1.0))
    qk_s[1] = jnp.exp(lgf * (c - 1.0 - pos))
    qk_s[2] = jnp.exp(lgb * (c - pos))
    qk_s[3] = jnp.exp(lgb * pos)
    cd_f = jnp.exp(lgf * float(c))
    cd_b = jnp.exp(lgb * float(c))
    st_s[...] = jnp.zeros_like(st_s)
    rowb = lax.broadcasted_iota(jnp.int32, (BRANCH_W, BRANCH_W), 0) // HEAD_W
    colb = lax.broadcasted_iota(jnp.int32, (BRANCH_W, BRANCH_W), 1) // HEAD_W
    bd_mask = rowb == colb

    def chunk_step(q, k, v, d, cd):
        kk = k * jnp.asarray(HEAD_W ** -0.5, BF16)
        state = st_s[d]
        o = _dot(q, state.astype(BF16)) * qk_s[2 * d]
        kstack = jnp.concatenate([jnp.where(_head_mask(kk.shape, h), kk, jnp.zeros_like(kk))
                                  for h in range(N_HEADS)], axis=0)
        vstack = jnp.concatenate([jnp.where(_head_mask(v.shape, h), v, jnp.zeros_like(v))
                                  for h in range(N_HEADS)], axis=0)
        inner = _dot_nt(q, kstack) * dec_s[d]
        o = o + _dot(inner.astype(BF16), vstack)
        kd = (kk.astype(F32) * qk_s[2 * d + 1]).astype(BF16)
        upd = lax.dot_general(kd, v, (((0,), (0,)), ((), ())), preferred_element_type=F32)
        st_s[d] = state * cd + jnp.where(bd_mask, upd, 0.0)
        return o

    gseg = gseg_ref[...]
    gnw = gnw_ref[...]

    def gnorm(o):
        mu = _dot_sel(o, gseg) * (1.0 / HEAD_W)
        dlt = o - mu
        var = _dot_sel(dlt * dlt, gseg) * (1.0 / HEAD_W)
        return dlt * lax.rsqrt(var + EPS) * gnw

    def combine(gf_ref, gb_ref, y_ref, base, ro):
        y = (gnorm(of_s[pl.ds(base + ro, c), :]) * _silu(gf_ref[0, pl.ds(ro, c), :].astype(F32))
             + gnorm(ob_s[pl.ds(base + ro, c), :]) * _silu(gb_ref[0, pl.ds(ro, c), :].astype(F32)))
        y_ref[0, pl.ds(ro, c), :] = y.astype(y_ref.dtype)

    def scan(q_ref, k_ref, v_ref, base, n, gated):
        def body(i, _, emit):
            fo = pl.multiple_of(i * c, c)
            bo = pl.multiple_of((n - 1 - i) * c, c)
            of_s[pl.ds(base + fo, c), :] = chunk_step(
                q_ref[0, pl.ds(fo, c), :], k_ref[0, pl.ds(fo, c), :], v_ref[0, pl.ds(fo, c), :], 0, cd_f)
            ob_s[pl.ds(base + bo, c), :] = chunk_step(
                q_ref[0, pl.ds(bo, c), :], k_ref[0, pl.ds(bo, c), :], v_ref[0, pl.ds(bo, c), :], 1, cd_b)
            if emit:
                combine(*gated, base, fo)
                combine(*gated, base, bo)
            return 0

        assert n % 2 == 0
        lax.fori_loop(0, n // 2, functools.partial(body, emit=False), 0)
        lax.fori_loop(n // 2, n, functools.partial(body, emit=gated is not None), 0)

    scan(qc, kc, vc, 0, lc // c, (gfc, gbc, yc_o) if need_ctx else None)
    scan(qx, kx, vx, lc, sx // c, (gfx, gbx, yx_o))
    if not need_ctx:
        yc_o[...] = jnp.zeros_like(yc_o)


def _retention(projx, projc, log_g, gn_w, gseg, need_ctx):
    b, s, _ = projx.shape
    lc = projc.shape[1]
    lgs = log_g.reshape(2 * N_HEADS)
    lgl = jnp.repeat(log_g, HEAD_W, axis=1).reshape(2, 1, BRANCH_W)
    xs = [_proj_cols(projx, COL_RET + 256 * i, 256, s) for i in range(5)]
    cs = [_proj_cols(projc, COL_RET + 256 * i, 256, lc) for i in range(5)]
    c = RET_CHUNK
    yx, yc = pl.pallas_call(
        functools.partial(_ret_kernel, need_ctx=need_ctx),
        grid=(b,),
        in_specs=[pl.BlockSpec(memory_space=pltpu.SMEM),
                  pl.BlockSpec((2, 1, BRANCH_W), lambda bi: (0, 0, 0)),
                  pl.BlockSpec((1, BRANCH_W), lambda bi: (0, 0)),
                  pl.BlockSpec((BRANCH_W, BRANCH_W), lambda bi: (0, 0))] + xs + cs,
        out_specs=[pl.BlockSpec((1, s, BRANCH_W), lambda bi: (bi, 0, 0)),
                   pl.BlockSpec((1, lc, BRANCH_W), lambda bi: (bi, 0, 0))],
        out_shape=[jax.ShapeDtypeStruct((b, s, BRANCH_W), BF16),
                   jax.ShapeDtypeStruct((b, lc, BRANCH_W), BF16)],
        scratch_shapes=[pltpu.VMEM((lc + s, BRANCH_W), F32),
                        pltpu.VMEM((lc + s, BRANCH_W), F32),
                        pltpu.VMEM((2, BRANCH_W, BRANCH_W), F32),
                        pltpu.VMEM((2, c, N_HEADS * c), F32),
                        pltpu.VMEM((4, c, BRANCH_W), F32)],
        compiler_params=_cparams(("parallel",), 48 * 1024 * 1024),
        name="retention",
    )(lgs, lgl, gn_w.reshape(1, BRANCH_W), gseg, *([projx] * 5), *([projc] * 5))
    return yx, yc


def _merge_kernel(oa, ob, oc, od, g0, g1, g2, g3, x_ref, gate_ref, sc_ref, sh_ref, nw_ref, wb_ref, wo_ref,
                  xn_o, h2_o):
    acc = None
    for i, (o, g) in enumerate(((oa, g0), (ob, g1), (oc, g2), (od, g3))):
        t = (0.5 * jnp.tanh(0.5 * g[0].astype(F32)) + 0.5) * _dot(o[0], wb_ref[i])
        acc = t if acc is None else acc + t
    y = _dot(acc.astype(BF16), wo_ref[...])
    xn = x_ref[0] + gate_ref[0] * y
    xn_o[0] = xn
    h2_o[0] = _rms(xn) * nw_ref[...] * (1.0 + sc_ref[0]) + sh_ref[0]


def _merge(outs, proj, x, gate, sc2, sh2, n2w, wb, wo):
    b, l, d = x.shape
    tm = min(512, l)
    tok = lambda w: pl.BlockSpec((1, tm, w), lambda bi, i: (bi, i, 0))
    vec = pl.BlockSpec((1, 1, d), lambda bi, i: (bi, 0, 0))
    gates = [pl.BlockSpec((1, tm, d), lambda bi, i, k=k: (bi, i, COL_GATES // d + k)) for k in range(N_BRANCH)]
    return pl.pallas_call(
        _merge_kernel,
        grid=(b, l // tm),
        in_specs=[tok(BRANCH_W)] * 4 + gates + [tok(d), vec, vec, vec,
                                                pl.BlockSpec((1, d), lambda bi, i: (0, 0)),
                                                pl.BlockSpec(wb.shape, lambda bi, i: (0, 0, 0)),
                                                pl.BlockSpec(wo.shape, lambda bi, i: (0, 0))],
        out_specs=[tok(d), tok(d)],
        out_shape=[jax.ShapeDtypeStruct((b, l, d), F32), jax.ShapeDtypeStruct((b, l, d), F32)],
        compiler_params=_cparams(("parallel", "parallel"), 48 * 1024 * 1024),
        name="merge",
    )(*outs, proj, proj, proj, proj, x, gate, sc2, sh2, n2w, wb, wo)


def _sorting_network(n):
    pairs = []
    p = 1
    while p < n:
        k = p
        while k >= 1:
            for j in range(k % p, n - k, 2 * k):
                for i in range(min(k, n - j - k)):
                    if (i + j) // (2 * p) == (i + j + k) // (2 * p):
                        pairs.append((i + j, i + j + k))
            k //= 2
        p *= 2
    return tuple(pairs)


def _topk_rows_slabs(s, k):
    r, t = s.shape
    assert r == 8 * k
    sub = lax.broadcasted_iota(jnp.int32, (8, t), 0).astype(F32)
    vs = [s[8 * j:8 * j + 8] for j in range(k)]
    ix = [sub + float(8 * j) for j in range(k)]
    for p, q in _sorting_network(k):
        va, ia, vb, ib = vs[p], ix[p], vs[q], ix[q]
        swap = (vb > va) | ((vb == va) & (ib < ia))
        vs[p], vs[q] = jnp.maximum(va, vb), jnp.minimum(va, vb)
        ix[p], ix[q] = jnp.where(swap, ib, ia), jnp.where(swap, ia, ib)
    out_row = lax.broadcasted_iota(jnp.int32, (k, t), 0)
    vals = jnp.zeros((k, t), F32)
    idxs = jnp.zeros((k, t), F32)
    for rnd in range(k):
        hv, hi = vs[0], ix[0]
        m = jnp.max(hv, axis=0, keepdims=True)
        idx = jnp.min(jnp.where(hv == m, hi, float(r)), axis=0, keepdims=True)
        vals = jnp.where(out_row == rnd, m, vals)
        idxs = jnp.where(out_row == rnd, idx, idxs)
        won = hi == idx
        for d in range(k - 1 - rnd):
            vs[d] = jnp.where(won, vs[d + 1], vs[d])
            ix[d] = jnp.where(won, ix[d + 1], ix[d])
    return vals, idxs.astype(jnp.int32)


def _select_rows(table, sel, k):
    out = jnp.zeros_like(table)
    for r in range(k):
        out = jnp.where(sel == r, table[r:r + 1, :], out)
    return out


def _split_bf16(x):
    hi = x.astype(BF16)
    return hi, (x - hi.astype(F32)).astype(BF16)


def _peer_joint_topk(s1, s2):
    k = PEER_TOPK
    t = s1.shape[1]
    sub = lax.broadcasted_iota(jnp.int32, (8, t), 0)
    depth = jnp.full((8, t), k // 8, jnp.int32)
    for i in range(6, -1, -1):
        depth = jnp.where(sub == i, k // (i + 1), depth)
    subf = sub.astype(F32)
    lo = [jnp.where(depth > d, s1[0:8] + s2[d:d + 1], -jnp.inf) for d in range(k)]
    hi0 = s1[8:16] + s2[0:1]
    pos_hi = (subf + 8.0) * float(k)
    cnt = jnp.zeros((8, t), F32)
    out_row = lax.broadcasted_iota(jnp.int32, (k, t), 0)
    vals = jnp.zeros((k, t), F32)
    poss = jnp.zeros((k, t), F32)
    big = float(k * k)
    for rnd in range(k):
        pos_lo = subf * float(k) + cnt
        m = jnp.max(jnp.maximum(lo[0], hi0), axis=0, keepdims=True)
        cand = jnp.minimum(jnp.where(lo[0] == m, pos_lo, big), jnp.where(hi0 == m, pos_hi, big))
        pos = jnp.min(cand, axis=0, keepdims=True)
        vals = jnp.where(out_row == rnd, m, vals)
        poss = jnp.where(out_row == rnd, pos, poss)
        won_lo = pos_lo == pos
        for d in range(k - 1 - rnd):
            lo[d] = jnp.where(won_lo, lo[d + 1], lo[d])
        cnt = cnt + jnp.where(won_lo, 1.0, 0.0)
        hi0 = jnp.where(pos_hi == pos, -jnp.inf, hi0)
    p = poss.astype(jnp.int32)
    return vals, p >> 4, p & (k - 1)


def _peer_route_head(q, kh, kl):
    k = PEER_TOPK
    assert k == 16
    qh, ql = _split_bf16(q)
    s = _dot_nt(kh, qh) + (_dot_nt(kh, ql) + _dot_nt(kl, qh))
    s1, i1 = _topk_rows_slabs(s[:PEER_N_KEYS], k)
    s2, i2 = _topk_rows_slabs(s[PEER_N_KEYS:], k)
    ts, ri, rj = _peer_joint_topk(s1, s2)
    e = jnp.exp(ts - ts[0:1, :])
    return _select_rows(i1, ri, k), _select_rows(i2, rj, k), e / jnp.sum(e, axis=0, keepdims=True)


def _peer_query(h, wqh, wql):
    hh, hl = _split_bf16(h)
    return _dot(hh, wqh) + (_dot(hl, wqh) + _dot(hh, wql))


def _peer_route_kernel(h_ref, wqh_ref, wql_ref, kh_ref, kl_ref, a_o, b_o, g_o, q_s, a_s, b_s, g_s):
    k = PEER_TOPK
    q_s[...] = _peer_query(h_ref[...], wqh_ref[...], wql_ref[...])

    def head(h):
        lo = pl.multiple_of(h * PEER_DK, PEER_DK)
        ro = pl.multiple_of(h * k, k)
        a_s[pl.ds(ro, k), :], b_s[pl.ds(ro, k), :], g_s[pl.ds(ro, k), :] = _peer_route_head(
            q_s[:, pl.ds(lo, PEER_DK)], kh_ref[h], kl_ref[h])

    def head_group(i, _):
        for j in range(PEER_HEADS_PER_TRIP):
            head(PEER_HEADS_PER_TRIP * i + j)
        return 0

    lax.fori_loop(0, PEER_HEADS // PEER_HEADS_PER_TRIP, head_group, 0)
    a_o[...] = a_s[...].T
    b_o[...] = b_s[...].T
    g_o[...] = g_s[...].T


def _peer_route(h2, wq_hl, keys_hl):
    n, d = h2.shape
    t = 256
    wq_hi, wq_lo = wq_hl
    k_hi, k_lo = keys_hl
    return pl.pallas_call(
        _peer_route_kernel,
        grid=(n // t,),
        in_specs=[pl.BlockSpec((t, d), lambda i: (i, 0)),
                  pl.BlockSpec(wq_hi.shape, lambda i: (0, 0)),
                  pl.BlockSpec(wq_lo.shape, lambda i: (0, 0)),
                  pl.BlockSpec(k_hi.shape, lambda i: (0, 0, 0)),
                  pl.BlockSpec(k_lo.shape, lambda i: (0, 0, 0))],
        out_specs=[pl.BlockSpec((t, PEER_SLOTS), lambda i: (i, 0))] * 3,
        out_shape=[jax.ShapeDtypeStruct((n, PEER_SLOTS), jnp.int32),
                   jax.ShapeDtypeStruct((n, PEER_SLOTS), jnp.int32),
                   jax.ShapeDtypeStruct((n, PEER_SLOTS), F32)],
        scratch_shapes=[pltpu.VMEM((t, PEER_HEADS * PEER_DK), F32),
                        pltpu.VMEM((PEER_SLOTS, t), jnp.int32),
                        pltpu.VMEM((PEER_SLOTS, t), jnp.int32),
                        pltpu.VMEM((PEER_SLOTS, t), F32)],
        compiler_params=_cparams(("parallel",), 48 * 1024 * 1024),
        name="peer_route",
    )(h2, wq_hi, wq_lo, k_hi, k_lo)


_HI16 = -65536


def _bf16_bits(w):
    return lax.bitcast_convert_type(w, jnp.int32) & _HI16


def _peer_ffn_kernel(*refs, ec, unroll, final_norm, route_next):
    if route_next:
        (h_ref, a_ref, b_ref, g_ref, x_ref, gate_ref, u_ref, v_ref, fw_ref, qn_ref, kh_ref, kl_ref,
         o_ref, hb_s, w_s, a_c, b_c, g_c, a_n, b_n, g_n) = refs
    else:
        h_ref, a_ref, b_ref, g_ref, x_ref, gate_ref, u_ref, v_ref, fw_ref, o_ref, hb_s, w_s = refs
        a_c, b_c, g_c = a_ref, b_ref, g_ref
    e = pl.program_id(1)
    t = h_ref.shape[0]
    half = t // 2
    nk = PEER_N_KEYS

    @pl.when(e == 0)
    def _():
        if route_next:
            @pl.when(pl.program_id(0) == 0)
            def _():
                a_c[...] = a_ref[...]
                b_c[...] = b_ref[...]
                g_c[...] = g_ref[...]

        hb_s[...] = h_ref[...].astype(BF16)
        o_ref[...] = jnp.zeros_like(o_ref)
        jio = lax.broadcasted_iota(jnp.int32, (nk, PEER_SLOTS), 0)

        def tile(tt):
            arow = jnp.broadcast_to(a_c[pl.ds(tt, 1), :], (nk, PEER_SLOTS))
            brow = jnp.broadcast_to(b_c[pl.ds(tt, 1), :], (nk, PEER_SLOTS))
            grow = jnp.broadcast_to(g_c[pl.ds(tt, 1), :], (nk, PEER_SLOTS))
            cm = jnp.where(jio == arow, grow, 0.0).astype(BF16)
            bm_t = jnp.where(jio == brow, 1.0, 0.0).T.astype(BF16)
            return _dot(cm, bm_t)

        def build(tb, _):
            for u in range(unroll):
                tt = tb * unroll + u
                word = _bf16_bits(tile(tt + half)) | lax.shift_right_logical(_bf16_bits(tile(tt)), 16)
                w_s[pl.ds(pl.multiple_of(tt * PEER_W_PITCH, 8), nk), :] = word
            return 0

        lax.fori_loop(0, half // unroll, build, 0)

    if route_next:
        nsub = t // PEER_ROUTE_TOKENS
        hd = e // nsub
        co = pl.multiple_of((e % nsub) * PEER_ROUTE_TOKENS, PEER_ROUTE_TOKENS)
        ro = pl.multiple_of(hd * PEER_TOPK, PEER_TOPK)
        routed = _peer_route_head(
            qn_ref[pl.ds(co, PEER_ROUTE_TOKENS), pl.ds(pl.multiple_of(hd * PEER_DK, PEER_DK), PEER_DK)],
            kh_ref[hd], kl_ref[hd])

    hid = _dot(hb_s[...], u_ref[0])
    j0 = e * (ec // nk)
    words = jnp.concatenate([w_s[pl.ds(j0 + j, half, stride=PEER_W_PITCH), :] for j in range(ec // nk)], axis=1)
    w_lo = lax.bitcast_convert_type(lax.shift_left(words, 16), F32)
    w_hi = lax.bitcast_convert_type(words & _HI16, F32)
    wc = jnp.concatenate([w_lo, w_hi], axis=0)
    act = 0.5 * hid * (1.0 + lax.erf(hid * SQRT_HALF))
    o_ref[...] += _dot((wc * act).astype(BF16), v_ref[...])

    if route_next:
        a_n[pl.ds(ro, PEER_TOPK), pl.ds(co, PEER_ROUTE_TOKENS)] = routed[0]
        b_n[pl.ds(ro, PEER_TOPK), pl.ds(co, PEER_ROUTE_TOKENS)] = routed[1]
        g_n[pl.ds(ro, PEER_TOPK), pl.ds(co, PEER_ROUTE_TOKENS)] = routed[2]

    @pl.when(e == pl.num_programs(1) - 1)
    def _():
        y = x_ref[...] + gate_ref[0] * o_ref[...]
        o_ref[...] = _rms(y) * fw_ref[...] if final_norm else y
        if route_next:
            a_c[...] = a_n[...].T
            b_c[...] = b_n[...].T
            g_c[...] = g_n[...].T


def _peer_ffn(h2, a, b_idx, g, x, gate, u_blk, v, l, final_w, final_norm, route_next=None):
    n, d = h2.shape
    ne = v.shape[0]
    neb, _, ec = u_blk.shape
    t = min(512, l)
    nb = n // t
    unroll = 16
    assert l % t == 0 and ne == PEER_N_KEYS * PEER_N_KEYS and neb * ec == ne and (t // 2) % unroll == 0
    tok = lambda w: pl.BlockSpec((t, w), lambda i, e: (i, 0))
    in_specs = [tok(d), tok(PEER_SLOTS), tok(PEER_SLOTS), tok(PEER_SLOTS), tok(d),
                pl.BlockSpec((1, 1, d), lambda i, e: ((i * t) // l, 0, 0)),
                pl.BlockSpec((1, d, ec), lambda i, e: (e, 0, 0)),
                pl.BlockSpec((ec, d), lambda i, e: (e, 0)),
                pl.BlockSpec((1, d), lambda i, e: (0, 0))]
    args = [h2, a, b_idx, g, x, gate, u_blk, v, final_w.reshape(1, d)]
    scratch = [pltpu.VMEM((t, d), BF16), pltpu.VMEM((t // 2 * PEER_W_PITCH, PEER_N_KEYS), jnp.int32)]
    if route_next is not None:
        q_all, k_hi, k_lo = route_next
        assert neb == PEER_HEADS * (t // PEER_ROUTE_TOKENS) and a.shape[0] == t
        first = pl.BlockSpec((t, PEER_SLOTS), lambda i, e: (0, 0))
        in_specs[1:4] = [first, first, first]
        in_specs += [pl.BlockSpec((t, q_all.shape[1]), lambda i, e: (jnp.minimum(i + 1, nb - 1), 0)),
                     pl.BlockSpec(k_hi.shape, lambda i, e: (0, 0, 0)),
                     pl.BlockSpec(k_lo.shape, lambda i, e: (0, 0, 0))]
        args += [q_all, k_hi, k_lo]
        scratch += [pltpu.VMEM((t, PEER_SLOTS), jnp.int32), pltpu.VMEM((t, PEER_SLOTS), jnp.int32),
                    pltpu.VMEM((t, PEER_SLOTS), F32),
                    pltpu.VMEM((PEER_SLOTS, t), jnp.int32), pltpu.VMEM((PEER_SLOTS, t), jnp.int32),
                    pltpu.VMEM((PEER_SLOTS, t), F32)]
    return pl.pallas_call(
        functools.partial(_peer_ffn_kernel, ec=ec, unroll=unroll, final_norm=final_norm,
                          route_next=route_next is not None),
        grid=(nb, neb),
        in_specs=in_specs,
        out_specs=tok(d),
        out_shape=jax.ShapeDtypeStruct((n, d), F32),
        scratch_shapes=scratch,
        compiler_params=_cparams(("arbitrary", "arbitrary"), VMEM_LIMIT_V7X),
        name="peer_ffn_routed" if route_next is not None else "peer_ffn",
    )(*args)


def _peer_query_kernel(h_ref, wqh_ref, wql_ref, q_o):
    q_o[...] = _peer_query(h_ref[...], wqh_ref[...], wql_ref[...])


def _peer_queries(h2, wq_hl):
    n, d = h2.shape
    wq_hi, wq_lo = wq_hl
    t = min(512, n)
    return pl.pallas_call(
        _peer_query_kernel,
        grid=(n // t,),
        in_specs=[pl.BlockSpec((t, d), lambda i: (i, 0)),
                  pl.BlockSpec(wq_hi.shape, lambda i: (0, 0)),
                  pl.BlockSpec(wq_lo.shape, lambda i: (0, 0))],
        out_specs=pl.BlockSpec((t, wq_hi.shape[1]), lambda i: (i, 0)),
        out_shape=jax.ShapeDtypeStruct((n, wq_hi.shape[1]), F32),
        compiler_params=_cparams(("parallel",), 40 * 1024 * 1024),
        name="peer_queries",
    )(h2, wq_hi, wq_lo)


def _layout_w_in(w):
    parts = jnp.split(w, IN_OFFSETS, axis=1)
    z = lambda n: jnp.zeros((w.shape[0], n), w.dtype)
    kpe_blk = jnp.concatenate([z(MLA_NOPE), parts[2], z(MLA_HEAD_PAD - MLA_NOPE - MLA_ROPE)], axis=1)
    return jnp.concatenate([parts[0], parts[1], kpe_blk] + list(parts[3:]), axis=1).astype(BF16)


def _layout_mla(w_uq, w_ukv):
    qh = w_uq.reshape(MLA_Q_LORA, N_HEADS, MLA_NOPE + MLA_ROPE)
    qh = jnp.pad(qh, ((0, 0), (0, 0), (0, MLA_HEAD_PAD - MLA_NOPE - MLA_ROPE)))
    kv = w_ukv.reshape(MLA_KV_LORA, N_HEADS, MLA_NOPE + MLA_V)
    kh = jnp.pad(kv[:, :, :MLA_NOPE], ((0, 0), (0, 0), (0, MLA_HEAD_PAD - MLA_NOPE)))
    vh = kv[:, :, MLA_NOPE:]
    return (qh.reshape(MLA_Q_LORA, -1).astype(BF16), kh.reshape(MLA_KV_LORA, -1).astype(BF16),
            vh.reshape(MLA_KV_LORA, -1).T.astype(BF16))


def _static_mats():
    gseg = np.kron(np.eye(N_HEADS), np.ones((HEAD_W, HEAD_W))).astype(np.float32)
    pm = np.zeros((MLA_HEAD_PAD, MLA_HEAD_PAD), np.float32)
    for dd in range(MLA_ROPE):
        blk, j = dd // 16, dd % 16
        pm[MLA_NOPE + blk * 16 + (j + 8) % 16, MLA_NOPE + dd] = 1.0
    pg = np.zeros((BRANCH_W, BRANCH_W), np.float32)
    for i in range(BRANCH_W):
        off, dd = (i // HEAD_W) * HEAD_W, i % HEAD_W
        blk, j = dd // 32, dd % 32
        pg[off + blk * 32 + (j + 16) % 32, i] = 1.0
    ex = np.zeros((GQA_KV_HEADS * HEAD_W, BRANCH_W), np.float32)
    for i in range(BRANCH_W):
        ex[((i // HEAD_W) // (N_HEADS // GQA_KV_HEADS)) * HEAD_W + i % HEAD_W, i] = 1.0
    return tuple(jnp.asarray(m, dtype=BF16) for m in (gseg, pm, pg, ex, ex.T, np.eye(BRANCH_W)))


def _rope_half_tables(pos, hf):
    freqs = ROPE_THETA ** (-jnp.arange(hf, dtype=F32) / hf)
    ang = pos[:, None] * freqs[None, :]
    c, s = jnp.cos(ang), jnp.sin(ang)
    return jnp.concatenate([c, c], axis=1), jnp.concatenate([-s, s], axis=1)


def _axial_tables(row, col, dims):
    cr, sr = _rope_half_tables(row, dims // 4)
    cc, sc = _rope_half_tables(col, dims // 4)
    return jnp.concatenate([cr, cc], axis=1), jnp.concatenate([sr, sc], axis=1)


def _rope_tables(s):
    t = jnp.arange(s)
    row, col = (t // GRID_W).astype(F32), (t % GRID_W).astype(F32)
    c32, s32 = _axial_tables(row, col, MLA_ROPE)
    pad = MLA_HEAD_PAD - MLA_NOPE - MLA_ROPE
    cm = jnp.concatenate([jnp.ones((s, MLA_NOPE), F32), c32, jnp.ones((s, pad), F32)], axis=1)
    sm = jnp.concatenate([jnp.zeros((s, MLA_NOPE), F32), s32, jnp.zeros((s, pad), F32)], axis=1)
    c64, s64 = _axial_tables(row, col, HEAD_W)
    return cm, sm, jnp.tile(c64, (1, N_HEADS)), jnp.tile(s64, (1, N_HEADS))


def _split_f32(w):
    hi = w.astype(BF16)
    return hi, (w - hi.astype(F32)).astype(BF16)


def _layout_peer_keys(keys):
    h, _, nk, dh = keys.shape
    z = jnp.zeros((h, nk, dh), keys.dtype)
    top = jnp.concatenate([keys[:, 0], z], axis=2)
    bot = jnp.concatenate([z, keys[:, 1]], axis=2)
    return jnp.concatenate([top, bot], axis=1)


def kernel(x, c, ctx, c_ctx, mod_w, mod_b, norm1_w, norm2_w, w_in, mla_q_norm, mla_w_uq, mla_kv_norm, mla_w_ukv, gqa_q_norm, gqa_k_norm, na_bias, ret_decay_logit, ret_gn_w, w_branch, w_out, peer_w_q, peer_keys, peer_u, peer_v, final_norm_w):
    b, s, d = x.shape
    lc = ctx.shape[1]
    depth = mod_w.shape[0]
    assert d == D_MODEL and s % (GRID_W * NA_Q_ROWS) == 0 and s % 256 == 0 and lc % 256 == 0

    rows = -(-(b + 1) // 8) * 8
    cc = jnp.zeros((rows, d), F32).at[:b].set(c).at[b].set(c_ctx)
    mod = _modulation(cc, mod_w, mod_b)

    gseg, pm, pg, ex, ex_t, eye = _static_mats()
    tables = _rope_tables(s)

    for l in range(depth):
        need_ctx = l < depth - 1
        mx = mod[l, :b].reshape(b, 1, 6, d)
        mc = jnp.broadcast_to(mod[l, b].reshape(1, 1, 6, d), (b, 1, 6, d))
        sh1x, sc1x, g1x, sh2x, sc2x, g2x = (mx[:, :, i] for i in range(6))
        sh1c, sc1c, g1c, sh2c, sc2c, g2c = (mc[:, :, i] for i in range(6))

        w_in_l = _layout_w_in(w_in[l])
        wuq, wk, wv = _layout_mla(mla_w_uq[l], mla_w_ukv[l])
        consts = (mla_q_norm[l].reshape(1, -1), wuq, mla_kv_norm[l].reshape(1, -1), wk, wv,
                  jnp.tile(gqa_q_norm[l], N_HEADS).reshape(1, -1),
                  jnp.tile(gqa_k_norm[l], GQA_KV_HEADS).reshape(1, -1), gseg, pm, pg, ex, ex_t, eye)
        n1w = norm1_w[l].reshape(1, d)
        n2w = norm2_w[l].reshape(1, d)
        wb = w_branch[l].astype(BF16)
        wo = w_out[l].astype(BF16)
        keys_hl = _split_f32(_layout_peer_keys(peer_keys[l]))
        wq_hl = _split_f32(peer_w_q[l])
        ne = peer_u.shape[1]
        u_blk = jnp.swapaxes(peer_u[l].astype(BF16).reshape(ne // PEER_EXPERT_CHUNK, PEER_EXPERT_CHUNK, d), 1, 2)
        v_b = peer_v[l].astype(BF16)
        log_g = jax.nn.log_sigmoid(ret_decay_logit[l].astype(F32))

        projx = _inproj(x, n1w, sc1x, sh1x, w_in_l)
        projc = _inproj(ctx, n1w, sc1c, sh1c, w_in_l)
        qmx, kmx, vmx, qgx, kgx, vgx, nvx = _prep(projx, consts, tables)
        qmc, kmc, vmc, qgc, kgc, vgc, nvc = _prep(projc, consts, None)

        oa, ob = _attention([dict(q=(qmx, 0), segs=[((kmc, 0), vmc), ((kmx, 0), vmx)], dq=MLA_HEAD_PAD),
                             dict(q=(qgx, 0), segs=[((kgc, 0), vgc), ((kgx, 0), vgx)], dq=HEAD_W)], "attn_mla_gqa")
        oc = _na_attention(projx, projc, nvx, nvc, na_bias[l])
        od, od_c = _retention(projx, projc, log_g, ret_gn_w[l], gseg, need_ctx)

        x, h2x = _merge((oa, ob, oc, od), projx, x, g1x, sc2x, sh2x, n2w, wb, wo)
        h2f = h2x.reshape(b * s, d)
        first = min(512, s)
        ax, bx, gx = _peer_route(h2f[:first], wq_hl, keys_hl)
        x = _peer_ffn(h2f, ax, bx, gx, x.reshape(b * s, d), g2x, u_blk, v_b, s, final_norm_w,
                      final_norm=(l == depth - 1),
                      route_next=(_peer_queries(h2f, wq_hl),) + keys_hl).reshape(b, s, d)

        if need_ctx:
            ca, cb, ccx = _attention(
                [dict(q=(qmc, 0), segs=[((kmc, 0), vmc)], dq=MLA_HEAD_PAD),
                 dict(q=(qgc, 0), segs=[((kgc, 0), vgc)], dq=HEAD_W),
                 dict(q=(projc, COL_NQ), segs=[((projc, COL_NK), nvc)], dq=HEAD_W, qscale=HEAD_W ** -0.5,
                      log2_scores=False)], "attn_ctx")
            ctx, h2c = _merge((ca, cb, ccx, od_c), projc, ctx, g1c, sc2c, sh2c, n2w, wb, wo)
            ac, bc, gc = _peer_route(h2c.reshape(b * lc, d), wq_hl, keys_hl)
            ctx = _peer_ffn(h2c.reshape(b * lc, d), ac, bc, gc, ctx.reshape(b * lc, d), g2c, u_blk, v_b,
                            lc, final_norm_w, final_norm=False).reshape(b, lc, d)

    return x
```

```python
import functools

import numpy as np
import jax
import jax.numpy as jnp
from jax import lax
from jax.experimental import pallas as pl
from jax.experimental.pallas import tpu as pltpu

F32 = jnp.float32
BF16 = jnp.bfloat16
HIGHEST = lax.Precision.HIGHEST

D_MODEL = 1024
GRID_W = 64
ROPE_THETA = 10000.0
EPS = 1e-6
N_HEADS = 4
HEAD_W = 64
BRANCH_W = N_HEADS * HEAD_W
MLA_NOPE, MLA_ROPE, MLA_V = 64, 32, 64
MLA_Q_LORA, MLA_KV_LORA = 256, 128
MLA_SCALE = (MLA_NOPE + MLA_ROPE) ** -0.5
MLA_HEAD_PAD = 128
GQA_KV_HEADS = 2
NA_WIN_R, NA_WIN_C = 8, 16
NA_Q_ROWS = 4
ATTN_KEY_CHUNK = 1024
RET_CHUNK = 128
N_BRANCH = 4
PEER_HEADS, PEER_N_KEYS, PEER_TOPK, PEER_DK = 8, 128, 16, 128
PEER_SLOTS = PEER_HEADS * PEER_TOPK
PEER_W_PITCH = PEER_N_KEYS + 8
PEER_HEADS_PER_TRIP = 4
PEER_ROUTE_TOKENS = 256
PEER_EXPERT_CHUNK = 1024
SQRT_HALF = 0.7071067811865476
LOG2E = 1.4426950408889634
NEG_BIG = -1e30

IN_SIZES = (256, 128, 32, 256, 128, 128, 256, 256, 256, 256, 256, 256, 256, 256, 4096)
IN_OFFSETS = tuple(int(v) for v in np.cumsum(IN_SIZES)[:-1])
PROJ_COLS = 7168
COL_NQ, COL_NK, COL_NV = 1024, 1280, 1536
COL_RET = 1792
COL_GATES = 3072

VMEM_LIMIT_V7X = 56 * 1024 * 1024


def _cparams(sem, vmem=None):
    return pltpu.CompilerParams(dimension_semantics=sem, vmem_limit_bytes=vmem)


def _dot(a, b):
    return jnp.dot(a, b, preferred_element_type=F32)


def _dot_hi(a, b):
    return jnp.dot(a, b, preferred_element_type=F32, precision=HIGHEST)


def _dot_nt(a, b):
    return lax.dot_general(a, b, (((1,), (1,)), ((), ())), preferred_element_type=F32)


def _dot_sel(x, sel):
    hi = x.astype(BF16)
    r1 = x - hi.astype(F32)
    mid = r1.astype(BF16)
    lo = (r1 - mid.astype(F32)).astype(BF16)
    return _dot(hi, sel) + (_dot(mid, sel) + _dot(lo, sel))


def _rms(x):
    return x * lax.rsqrt(jnp.mean(x * x, axis=-1, keepdims=True) + EPS)


def _silu(x):
    return x * jax.nn.sigmoid(x)


def _head_mask(shape, h, width=HEAD_W):
    lane = lax.broadcasted_iota(jnp.int32, shape, len(shape) - 1)
    lo = h * width
    return (lane >= lo) & (lane < lo + width)


def _mod_kernel(c_ref, w_ref, b_ref, o_ref):
    o_ref[0] = _dot_hi(_silu(c_ref[...]), w_ref[0]) + b_ref[0]


def _modulation(cc, mod_w, mod_b):
    depth, d, n = mod_w.shape
    rows = cc.shape[0]
    tn = 1536
    return pl.pallas_call(
        _mod_kernel,
        grid=(depth, n // tn),
        in_specs=[pl.BlockSpec((rows, d), lambda l, j: (0, 0)),
                  pl.BlockSpec((1, d, tn), lambda l, j: (l, 0, j)),
                  pl.BlockSpec((1, 1, tn), lambda l, j: (l, 0, j))],
        out_specs=pl.BlockSpec((1, rows, tn), lambda l, j: (l, 0, j)),
        out_shape=jax.ShapeDtypeStruct((depth, rows, n), F32),
        compiler_params=_cparams(("parallel", "parallel"), 40 * 1024 * 1024),
        name="modulation",
    )(cc, mod_w, mod_b.reshape(depth, 1, n))


def _inproj_kernel(x_ref, nw_ref, sc_ref, sh_ref, w_ref, o_ref, h_scr):
    @pl.when(pl.program_id(2) == 0)
    def _():
        h = _rms(x_ref[0]) * nw_ref[...] * (1.0 + sc_ref[0]) + sh_ref[0]
        h_scr[...] = h.astype(BF16)

    o_ref[0] = _dot(h_scr[...], w_ref[...]).astype(o_ref.dtype)


def _inproj(x, nw, sc, sh, w):
    b, l, d = x.shape
    n = w.shape[1]
    tm = min(1024, l)
    tn = 1792
    return pl.pallas_call(
        _inproj_kernel,
        grid=(b, l // tm, n // tn),
        in_specs=[pl.BlockSpec((1, tm, d), lambda bi, i, j: (bi, i, 0)),
                  pl.BlockSpec((1, d), lambda bi, i, j: (0, 0)),
                  pl.BlockSpec((1, 1, d), lambda bi, i, j: (bi, 0, 0)),
                  pl.BlockSpec((1, 1, d), lambda bi, i, j: (bi, 0, 0)),
                  pl.BlockSpec((d, tn), lambda bi, i, j: (0, j))],
        out_specs=pl.BlockSpec((1, tm, tn), lambda bi, i, j: (bi, i, j)),
        out_shape=jax.ShapeDtypeStruct((b, l, n), BF16),
        scratch_shapes=[pltpu.VMEM((tm, d), BF16)],
        compiler_params=_cparams(("parallel", "parallel", "arbitrary"), 40 * 1024 * 1024),
        name="inproj",
    )(x, nw, sc, sh, w)


def _prep_kernel(*refs, use_rope):
    (p_ref, nv_ref, qn_ref, wuq_ref, kvn_ref, wk_ref, wv_ref, gqn_ref, gkn_ref, gseg_ref, pm_ref, pg_ref,
     e_ref, et_ref, eye_ref) = refs[:15]
    if use_rope:
        cm_ref, sm_ref, cg_ref, sg_ref = refs[15:19]
        outs = refs[19:]
    else:
        outs = refs[15:]
    qm_o, km_o, vm_o, qg_o, kg_o, vg_o, nvt_o = outs
    nvt_o[0] = _dot_nt(eye_ref[...], nv_ref[0]).astype(BF16)

    pb = p_ref[0]
    cq = pb[:, 0:256].astype(F32)
    ckv = pb[:, 256:384].astype(F32)
    kpe = pb[:, 384:512].astype(F32)
    gq = pb[:, 512:768].astype(F32)
    gk = pb[:, 768:896].astype(F32)
    gv = pb[:, 896:1024]

    cqn = (_rms(cq) * qn_ref[...]).astype(BF16)
    qa = _dot(cqn, wuq_ref[...])
    ckn = (_rms(ckv) * kvn_ref[...]).astype(BF16)
    kn = _dot(ckn, wk_ref[...])
    vm_t = _dot_nt(wv_ref[...], ckn)
    if use_rope:
        cm, sm = cm_ref[...], sm_ref[...]
        pm = pm_ref[...]

        def rope_m(t):
            return t * cm + _dot_sel(t, pm) * sm

        qa = jnp.concatenate([rope_m(qa[:, h * 128:(h + 1) * 128]) for h in range(N_HEADS)], axis=1)
        kpe = rope_m(kpe)
    km = kn + jnp.concatenate([kpe] * N_HEADS, axis=1)
    qm_o[0] = (qa * (MLA_SCALE * LOG2E)).astype(BF16)
    km_o[0] = km.astype(BF16)
    vm_o[0] = vm_t.astype(BF16)

    gseg = gseg_ref[...]
    gqn = gq * lax.rsqrt(_dot_sel(gq * gq, gseg) * (1.0 / HEAD_W) + EPS) * gqn_ref[...]
    gkn = gk * lax.rsqrt(_dot_sel(gk * gk, gseg[:128, :128]) * (1.0 / HEAD_W) + EPS) * gkn_ref[...]
    if use_rope:
        cg, sg = cg_ref[...], sg_ref[...]
        pg = pg_ref[...]
        gqn = gqn * cg + _dot_sel(gqn, pg) * sg
        gkn = gkn * cg[:, :128] + _dot_sel(gkn, pg[:128, :128]) * sg[:, :128]
    qg_o[0] = (gqn * (HEAD_W ** -0.5 * LOG2E)).astype(BF16)
    e = e_ref[...]
    kg_o[0] = _dot(gkn.astype(BF16), e).astype(BF16)
    vg_o[0] = _dot_nt(et_ref[...], gv).astype(BF16)


def _prep(proj, consts, tables):
    b, l, _ = proj.shape
    tm = min(512, l)
    use_rope = tables is not None
    full = lambda a: pl.BlockSpec(a.shape, lambda bi, i: (0,) * a.ndim)
    in_specs = [pl.BlockSpec((1, tm, 1024), lambda bi, i: (bi, i, 0)),
                pl.BlockSpec((1, tm, BRANCH_W), lambda bi, i: (bi, i, COL_NV // BRANCH_W))] + [full(a) for a in consts]
    args = [proj, proj] + list(consts)
    if use_rope:
        in_specs += [pl.BlockSpec((tm, t.shape[1]), lambda bi, i: (i, 0)) for t in tables]
        args += list(tables)
    tok = lambda w: (pl.BlockSpec((1, tm, w), lambda bi, i: (bi, i, 0)), jax.ShapeDtypeStruct((b, l, w), BF16))
    tr = (pl.BlockSpec((1, BRANCH_W, tm), lambda bi, i: (bi, 0, i)), jax.ShapeDtypeStruct((b, BRANCH_W, l), BF16))
    outs = (tok(512), tok(512), tr, tok(256), tok(256), tr, tr)
    return pl.pallas_call(
        functools.partial(_prep_kernel, use_rope=use_rope),
        grid=(b, l // tm),
        in_specs=in_specs,
        out_specs=[o[0] for o in outs],
        out_shape=[o[1] for o in outs],
        compiler_params=_cparams(("parallel", "parallel"), 40 * 1024 * 1024),
        name="prep_rope" if use_rope else "prep",
    )(*args)


def _attn_kernel(*refs, nseg, probs, tk):
    per = 1 + 2 * nseg
    nprob = len(probs)
    gw = 256
    ones_rows = 16
    state = []
    for pi, (dq, qscale, log2_scores) in enumerate(probs):
        q_ref = refs[pi * per]
        segs = [(refs[pi * per + 1 + 2 * i], refs[pi * per + 2 + 2 * i]) for i in range(nseg)]
        tq = q_ref.shape[1]
        hpg = gw // dq
        qstacks = []
        for g in range(N_HEADS // hpg):
            qg = q_ref[0, :, g * gw:(g + 1) * gw]
            if qscale is not None:
                qg = qg * jnp.asarray(qscale, BF16)
            qstacks.append(jnp.concatenate([jnp.where(_head_mask(qg.shape, j, dq), qg, jnp.zeros_like(qg))
                                            for j in range(hpg)], axis=0))
        state.append((segs, tq, hpg, qstacks, jnp.exp2 if log2_scores else jnp.exp))
    carry = tuple(tuple((jnp.full((1, st[1]), -jnp.inf, F32), jnp.zeros((HEAD_W + ones_rows, st[1]), F32))
                        for _ in range(N_HEADS)) for st in state)

    def chunk_step(pi, carry, si, off, tkk):
        segs, tq, hpg, qstacks, exp = state[pi]
        k_ref, vt_ref = segs[si]
        st_g = [_dot_nt(k_ref[0, pl.ds(off, tkk), g * gw:(g + 1) * gw], qstacks[g])
                for g in range(N_HEADS // hpg)]
        ones = jnp.ones((ones_rows, tkk), BF16)
        new = []
        for h in range(N_HEADS):
            m, acc = carry[h]
            vt = jnp.concatenate([vt_ref[0, h * HEAD_W:(h + 1) * HEAD_W, pl.ds(off, tkk)], ones], axis=0)
            st = st_g[h // hpg][:, (h % hpg) * tq:(h % hpg + 1) * tq]
            mn = jnp.maximum(m, jnp.max(st, axis=0, keepdims=True))
            acc = exp(m - mn) * acc + _dot(vt, exp(st - mn).astype(BF16))
            new.append((mn, acc))
        return tuple(new)

    for si in range(nseg):
        lk = state[0][0][si][0].shape[1]
        tkk = min(tk, lk)
        n = lk // tkk

        def body(c, carry, si=si, tkk=tkk):
            off = c * tkk if isinstance(c, int) else pl.multiple_of(c * tkk, tkk)
            return tuple(chunk_step(pi, carry[pi], si, off, tkk) for pi in range(nprob))

        carry = body(0, carry) if n == 1 else lax.fori_loop(0, n, body, carry)
    for pi in range(nprob):
        o_ref = refs[nprob * per + pi]
        out_t = jnp.concatenate([acc[:HEAD_W] * (1.0 / acc[HEAD_W:HEAD_W + 1]) for _, acc in carry[pi]], axis=0)
        o_ref[0] = out_t.T.astype(o_ref.dtype)


def _attention(problems, name):
    b, lq, _ = problems[0]["q"][0].shape
    tq = min(512, lq)
    in_specs, args, probs = [], [], []
    for p in problems:
        (qa, qcol), dq = p["q"], p["dq"]
        wq = N_HEADS * dq
        assert qcol % wq == 0 and qa.shape[:2] == (b, lq)
        in_specs.append(pl.BlockSpec((1, tq, wq), lambda bi, i, qcol=qcol, wq=wq: (bi, i, qcol // wq)))
        args.append(qa)
        for (ka, kcol), vt in p["segs"]:
            assert kcol % wq == 0 and vt.shape[1] == BRANCH_W and vt.shape[2] == ka.shape[1]
            in_specs.append(pl.BlockSpec((1, ka.shape[1], wq), lambda bi, i, kcol=kcol, wq=wq: (bi, 0, kcol // wq)))
            in_specs.append(pl.BlockSpec((1, BRANCH_W, vt.shape[2]), lambda bi, i: (bi, 0, 0)))
            args += [ka, vt]
        probs.append((dq, p.get("qscale"), p.get("log2_scores", True)))
    nseg = len(problems[0]["segs"])
    assert all(len(p["segs"]) == nseg for p in problems)
    outs = pl.pallas_call(
        functools.partial(_attn_kernel, nseg=nseg, probs=tuple(probs), tk=ATTN_KEY_CHUNK),
        grid=(b, lq // tq),
        in_specs=in_specs,
        out_specs=[pl.BlockSpec((1, tq, BRANCH_W), lambda bi, i: (bi, i, 0))] * len(problems),
        out_shape=[jax.ShapeDtypeStruct((b, lq, BRANCH_W), BF16)] * len(problems),
        compiler_params=_cparams(("parallel", "arbitrary"), VMEM_LIMIT_V7X),
        name=name,
    )(*args)
    return outs


def _proj_cols(arr, col, width, rows):
    assert col % width == 0
    return pl.BlockSpec((1, rows, width), lambda *idx: (idx[0], 0, col // width))


def _na_kernel(pat_ref, ks_ref, q_ref, k_ref, vt_ref, kc_ref, vct_ref, m_ref, o_ref, *, kw):
    del pat_ref
    g = pl.program_id(1)
    off = pl.multiple_of(ks_ref[g] * GRID_W, 128)
    q = q_ref[0] * jnp.asarray(HEAD_W ** -0.5, BF16)
    qb = q.shape[0]
    qstack = jnp.concatenate([jnp.where(_head_mask(q.shape, h), q, jnp.zeros_like(q))
                              for h in range(N_HEADS)], axis=0)
    st_w = _dot_nt(k_ref[0, pl.ds(off, kw), :], qstack)
    st_c = _dot_nt(kc_ref[0], qstack)
    ones_w = jnp.ones((16, kw), BF16)
    ones_c = jnp.ones((16, kc_ref.shape[1]), BF16)
    outs = []
    for h in range(N_HEADS):
        sw = st_w[:, h * qb:(h + 1) * qb] + m_ref[0, h]
        sc = st_c[:, h * qb:(h + 1) * qb]
        mx = jnp.maximum(jnp.max(sw, axis=0, keepdims=True), jnp.max(sc, axis=0, keepdims=True))
        vtw = jnp.concatenate([vt_ref[0, h * HEAD_W:(h + 1) * HEAD_W, pl.ds(off, kw)], ones_w], axis=0)
        vtc = jnp.concatenate([vct_ref[0, h * HEAD_W:(h + 1) * HEAD_W, :], ones_c], axis=0)
        acc = _dot(vtw, jnp.exp(sw - mx).astype(BF16)) + _dot(vtc, jnp.exp(sc - mx).astype(BF16))
        outs.append(acc[:HEAD_W] * (1.0 / acc[HEAD_W:HEAD_W + 1]))
    o_ref[0] = jnp.concatenate(outs, axis=0).T.astype(o_ref.dtype)


def _na_plan(s):
    rows = s // GRID_W
    wr = min(NA_WIN_R, rows)
    wc = NA_WIN_C
    qr = min(NA_Q_ROWS, rows)
    kwr = min(qr + wr - 1 + (1 if qr + wr - 1 < rows else 0), rows)
    ngrp = rows // qr
    qc = np.arange(GRID_W)[:, None]
    kc = np.arange(GRID_W)[None, :]
    cs = np.clip(qc - wc // 2, 0, GRID_W - wc)
    valid_c = (kc >= cs) & (kc < cs + wc)
    rel_c = np.where(valid_c, kc - qc + (NA_WIN_C - 1), 0)
    assert (valid_c.sum(1) == wc).all()
    pats, pat_ids, ks_rows = [], [], []
    for g in range(ngrp):
        r0 = g * qr
        ks = int(np.clip(r0 - wr // 2, 0, rows - kwr))
        assert (ks * GRID_W) % 128 == 0
        r = (r0 + np.arange(qr))[:, None]
        kr = (ks + np.arange(kwr))[None, :]
        rs = np.clip(r - wr // 2, 0, rows - wr)
        valid_r = (kr >= rs) & (kr < rs + wr)
        assert (valid_r.sum(1) == wr).all()
        rel_r = np.where(valid_r, kr - r + (NA_WIN_R - 1), 0)
        key = (valid_r.tobytes(), rel_r.tobytes())
        for pi, (pk, *_rest) in enumerate(pats):
            if pk == key:
                pat_ids.append(pi)
                break
        else:
            pat_ids.append(len(pats))
            pats.append((key, valid_r, rel_r))
        ks_rows.append(ks)
    valid_r = np.stack([p[1] for p in pats])
    rel_r = np.stack([p[2] for p in pats])
    return (qr, kwr, np.asarray(pat_ids, np.int32), np.asarray(ks_rows, np.int32), valid_r, rel_r, valid_c, rel_c)


def _na_bias_masks(na_bias, valid_r, rel_r, valid_c, rel_c):
    h = na_bias.shape[0]
    npat, qr, kwr = valid_r.shape
    ncol = 2 * NA_WIN_C - 1
    brow = na_bias[:, rel_r, :].astype(F32)
    onehot_c = ((rel_c[None] == np.arange(ncol)[:, None, None]) & valid_c[None]).astype(np.float32)
    m = jnp.einsum('hpqkc,cxy->phkyqx', brow, jnp.asarray(onehot_c), precision=HIGHEST)
    valid = (valid_r.transpose(0, 2, 1)[:, None, :, None, :, None]
             & valid_c.T[None, None, None, :, None, :])
    m = jnp.where(valid, m, NEG_BIG)
    return m.reshape(npat, h, kwr * GRID_W, qr * GRID_W)


def _na_attention(projx, projc, nvx_t, nvc_t, na_bias):
    b, s, _ = projx.shape
    lc = projc.shape[1]
    qr, kwr, pat_ids, ks_rows, valid_r, rel_r, valid_c, rel_c = _na_plan(s)
    qb, kw = qr * GRID_W, kwr * GRID_W
    assert kw % 128 == 0
    mb = _na_bias_masks(na_bias, valid_r, rel_r, valid_c, rel_c)
    grid_spec = pltpu.PrefetchScalarGridSpec(
        num_scalar_prefetch=2,
        grid=(b, s // qb),
        in_specs=[pl.BlockSpec((1, qb, 256), lambda bi, g, pat, ks: (bi, g, COL_NQ // 256)),
                  pl.BlockSpec((1, s, 256), lambda bi, g, pat, ks: (bi, 0, COL_NK // 256)),
                  pl.BlockSpec((1, BRANCH_W, s), lambda bi, g, pat, ks: (bi, 0, 0)),
                  pl.BlockSpec((1, lc, 256), lambda bi, g, pat, ks: (bi, 0, COL_NK // 256)),
                  pl.BlockSpec((1, BRANCH_W, lc), lambda bi, g, pat, ks: (bi, 0, 0)),
                  pl.BlockSpec((1, N_HEADS, kw, qb), lambda bi, g, pat, ks: (pat[g], 0, 0, 0))],
        out_specs=pl.BlockSpec((1, qb, BRANCH_W), lambda bi, g, pat, ks: (bi, g, 0)),
    )
    return pl.pallas_call(
        functools.partial(_na_kernel, kw=kw),
        grid_spec=grid_spec,
        out_shape=jax.ShapeDtypeStruct((b, s, BRANCH_W), BF16),
        compiler_params=_cparams(("parallel", "arbitrary"), 48 * 1024 * 1024),
        name="na_attention",
    )(jnp.asarray(pat_ids), jnp.asarray(ks_rows), projx, projx, nvx_t, projc, nvc_t, mb)


def _ret_kernel(lgs_ref, lgl_ref, gnw_ref, gseg_ref,
                qx, kx, vx, gfx, gbx, qc, kc, vc, gfc, gbc,
                yx_o, yc_o, of_s, ob_s, st_s, dec_s, qk_s, *, need_ctx):
    c = RET_CHUNK
    lc = qc.shape[1]
    sx = qx.shape[1]
    n_col = lax.broadcasted_iota(jnp.int32, (c, c), 0).astype(F32)
    m_row = lax.broadcasted_iota(jnp.int32, (c, c), 1).astype(F32)
    diff = n_col - m_row
    for h in range(N_HEADS):
        dec_s[0, :, h * c:(h + 1) * c] = jnp.where(diff >= 0, jnp.exp(lgs_ref[h] * jnp.maximum(diff, 0.0)), 0.0)
        dec_s[1, :, h * c:(h + 1) * c] = jnp.where(diff <= 0,
                                                   jnp.exp(lgs_ref[N_HEADS + h] * jnp.maximum(-diff, 0.0)), 0.0)
    pos = lax.broadcasted_iota(jnp.int32, (c, BRANCH_W), 0).astype(F32)
    lgf, lgb = lgl_ref[0], lgl_ref[1]
    qk_s[0] = jnp.exp(lgf * (pos + 1.0))
    qk_s[1] = jnp.exp(lgf * (c - 1.0 - pos))
    qk_s[2] = jnp.exp(lgb * (c - pos))
    qk_s[3] = jnp.exp(lgb * pos)
    cd_f = jnp.exp(lgf * float(c))
    cd_b = jnp.exp(lgb * float(c))
    st_s[...] = jnp.zeros_like(st_s)
    rowb = lax.broadcasted_iota(jnp.int32, (BRANCH_W, BRANCH_W), 0) // HEAD_W
    colb = lax.broadcasted_iota(jnp.int32, (BRANCH_W, BRANCH_W), 1) // HEAD_W
    bd_mask = rowb == colb

    def chunk_step(q, k, v, state, d, cd):
        kk = k * jnp.asarray(HEAD_W ** -0.5, BF16)
        o = _dot(q, state.astype(BF16)) * qk_s[2 * d]
        kstack = jnp.concatenate([jnp.where(_head_mask(kk.shape, h), kk, jnp.zeros_like(kk))
                                  for h in range(N_HEADS)], axis=0)
        vstack = jnp.concatenate([jnp.where(_head_mask(v.shape, h), v, jnp.zeros_like(v))
                                  for h in range(N_HEADS)], axis=0)
        inner = _dot_nt(q, kstack) * dec_s[d]
        o = o + _dot(inner.astype(BF16), vstack)
        kd = (kk.astype(F32) * qk_s[2 * d + 1]).astype(BF16)
        upd = lax.dot_general(kd, v, (((0,), (0,)), ((), ())), preferred_element_type=F32)
        return o, state * cd + jnp.where(bd_mask, upd, 0.0)

    gseg = gseg_ref[...]
    gnw = gnw_ref[...]

    def gnorm(o):
        mu = _dot_sel(o, gseg) * (1.0 / HEAD_W)
        dlt = o - mu
        var = _dot_sel(dlt * dlt, gseg) * (1.0 / HEAD_W)
        return dlt * lax.rsqrt(var + EPS) * gnw

    def combine(gf_ref, gb_ref, y_ref, base, ro):
        y = (gnorm(of_s[pl.ds(base + ro, c), :]) * _silu(gf_ref[0, pl.ds(ro, c), :].astype(F32))
             + gnorm(ob_s[pl.ds(base + ro, c), :]) * _silu(gb_ref[0, pl.ds(ro, c), :].astype(F32)))
        y_ref[0, pl.ds(ro, c), :] = y.astype(y_ref.dtype)

    def scan(q_ref, k_ref, v_ref, base, n, gated):
        def load(off):
            return q_ref[0, pl.ds(off, c), :], k_ref[0, pl.ds(off, c), :], v_ref[0, pl.ds(off, c), :]

        def body(j, _, emit):
            offs_f = [pl.multiple_of((2 * j + u) * c, c) for u in range(2)]
            offs_b = [pl.multiple_of((n - 1 - 2 * j - u) * c, c) for u in range(2)]
            ins_f = [load(o) for o in offs_f]
            ins_b = [load(o) for o in offs_b]
            sf, sb = st_s[0], st_s[1]
            outs_f, outs_b = [], []
            for u in range(2):
                o, sf = chunk_step(*ins_f[u], sf, 0, cd_f)
                outs_f.append(o)
                o, sb = chunk_step(*ins_b[u], sb, 1, cd_b)
                outs_b.append(o)
            st_s[0], st_s[1] = sf, sb
            for u in range(2):
                of_s[pl.ds(base + offs_f[u], c), :] = outs_f[u]
                ob_s[pl.ds(base + offs_b[u], c), :] = outs_b[u]
            for off in emit(offs_f, offs_b):
                combine(*gated, base, off)
            return 0

        none = lambda offs_f, offs_b: []
        both = none if gated is None else (lambda offs_f, offs_b: offs_f + offs_b)
        if n == 2:
            lax.fori_loop(0, 1, functools.partial(body, emit=none if gated is None else (lambda f, b: f)), 0)
        else:
            assert n % 4 == 0
            lax.fori_loop(0, n // 4, functools.partial(body, emit=none), 0)
            lax.fori_loop(n // 4, n // 2, functools.partial(body, emit=both), 0)

    scan(qc, kc, vc, 0, lc // c, (gfc, gbc, yc_o) if need_ctx else None)
    scan(qx, kx, vx, lc, sx // c, (gfx, gbx, yx_o))
    if not need_ctx:
        yc_o[...] = jnp.zeros_like(yc_o)


def _retention(projx, projc, log_g, gn_w, gseg, need_ctx):
    b, s, _ = projx.shape
    lc = projc.shape[1]
    lgs = log_g.reshape(2 * N_HEADS)
    lgl = jnp.repeat(log_g, HEAD_W, axis=1).reshape(2, 1, BRANCH_W)
    xs = [_proj_cols(projx, COL_RET + 256 * i, 256, s) for i in range(5)]
    cs = [_proj_cols(projc, COL_RET + 256 * i, 256, lc) for i in range(5)]
    c = RET_CHUNK
    yx, yc = pl.pallas_call(
        functools.partial(_ret_kernel, need_ctx=need_ctx),
        grid=(b,),
        in_specs=[pl.BlockSpec(memory_space=pltpu.SMEM),
                  pl.BlockSpec((2, 1, BRANCH_W), lambda bi: (0, 0, 0)),
                  pl.BlockSpec((1, BRANCH_W), lambda bi: (0, 0)),
                  pl.BlockSpec((BRANCH_W, BRANCH_W), lambda bi: (0, 0))] + xs + cs,
        out_specs=[pl.BlockSpec((1, s, BRANCH_W), lambda bi: (bi, 0, 0)),
                   pl.BlockSpec((1, lc, BRANCH_W), lambda bi: (bi, 0, 0))],
        out_shape=[jax.ShapeDtypeStruct((b, s, BRANCH_W), BF16),
                   jax.ShapeDtypeStruct((b, lc, BRANCH_W), BF16)],
        scratch_shapes=[pltpu.VMEM((lc + s, BRANCH_W), F32),
                        pltpu.VMEM((lc + s, BRANCH_W), F32),
                        pltpu.VMEM((2, BRANCH_W, BRANCH_W), F32),
                        pltpu.VMEM((2, c, N_HEADS * c), F32),
                        pltpu.VMEM((4, c, BRANCH_W), F32)],
        compiler_params=_cparams(("parallel",), 48 * 1024 * 1024),
        name="retention",
    )(lgs, lgl, gn_w.reshape(1, BRANCH_W), gseg, *([projx] * 5), *([projc] * 5))
    return yx, yc


def _merge_kernel(oa, ob, oc, od, g0, g1, g2, g3, x_ref, gate_ref, sc_ref, sh_ref, nw_ref, wb_ref, wo_ref,
                  xn_o, h2_o):
    acc = None
    for i, (o, g) in enumerate(((oa, g0), (ob, g1), (oc, g2), (od, g3))):
        t = (0.5 * jnp.tanh(0.5 * g[0].astype(F32)) + 0.5) * _dot(o[0], wb_ref[i])
        acc = t if acc is None else acc + t
    y = _dot(acc.astype(BF16), wo_ref[...])
    xn = x_ref[0] + gate_ref[0] * y
    xn_o[0] = xn
    h2_o[0] = _rms(xn) * nw_ref[...] * (1.0 + sc_ref[0]) + sh_ref[0]


def _merge(outs, proj, x, gate, sc2, sh2, n2w, wb, wo):
    b, l, d = x.shape
    tm = min(512, l)
    tok = lambda w: pl.BlockSpec((1, tm, w), lambda bi, i: (bi, i, 0))
    vec = pl.BlockSpec((1, 1, d), lambda bi, i: (bi, 0, 0))
    gates = [pl.BlockSpec((1, tm, d), lambda bi, i, k=k: (bi, i, COL_GATES // d + k)) for k in range(N_BRANCH)]
    return pl.pallas_call(
        _merge_kernel,
        grid=(b, l // tm),
        in_specs=[tok(BRANCH_W)] * 4 + gates + [tok(d), vec, vec, vec,
                                                pl.BlockSpec((1, d), lambda bi, i: (0, 0)),
                                                pl.BlockSpec(wb.shape, lambda bi, i: (0, 0, 0)),
                                                pl.BlockSpec(wo.shape, lambda bi, i: (0, 0))],
        out_specs=[tok(d), tok(d)],
        out_shape=[jax.ShapeDtypeStruct((b, l, d), F32), jax.ShapeDtypeStruct((b, l, d), F32)],
        compiler_params=_cparams(("parallel", "parallel"), 48 * 1024 * 1024),
        name="merge",
    )(*outs, proj, proj, proj, proj, x, gate, sc2, sh2, n2w, wb, wo)


def _sorting_network(n):
    pairs = []
    p = 1
    while p < n:
        k = p
        while k >= 1:
            for j in range(k % p, n - k, 2 * k):
                for i in range(min(k, n - j - k)):
                    if (i + j) // (2 * p) == (i + j + k) // (2 * p):
                        pairs.append((i + j, i + j + k))
            k //= 2
        p *= 2
    return tuple(pairs)


def _topk_rows_slabs(s, k):
    r, t = s.shape
    assert r == 8 * k
    sub = lax.broadcasted_iota(jnp.int32, (8, t), 0).astype(F32)
    vs = [s[8 * j:8 * j + 8] for j in range(k)]
    ix = [sub + float(8 * j) for j in range(k)]
    for p, q in _sorting_network(k):
        va, ia, vb, ib = vs[p], ix[p], vs[q], ix[q]
        swap = (vb > va) | ((vb == va) & (ib < ia))
        vs[p], vs[q] = jnp.maximum(va, vb), jnp.minimum(va, vb)
        ix[p], ix[q] = jnp.where(swap, ib, ia), jnp.where(swap, ia, ib)
    out_row = lax.broadcasted_iota(jnp.int32, (k, t), 0)
    vals = jnp.zeros((k, t), F32)
    idxs = jnp.zeros((k, t), F32)
    for rnd in range(k):
        hv, hi = vs[0], ix[0]
        m = jnp.max(hv, axis=0, keepdims=True)
        idx = jnp.min(jnp.where(hv == m, hi, float(r)), axis=0, keepdims=True)
        vals = jnp.where(out_row == rnd, m, vals)
        idxs = jnp.where(out_row == rnd, idx, idxs)
        won = hi == idx
        for d in range(k - 1 - rnd):
            vs[d] = jnp.where(won, vs[d + 1], vs[d])
            ix[d] = jnp.where(won, ix[d + 1], ix[d])
    return vals, idxs.astype(jnp.int32)


def _select_rows(table, sel, k):
    out = jnp.zeros_like(table)
    for r in range(k):
        out = jnp.where(sel == r, table[r:r + 1, :], out)
    return out


def _split_bf16(x):
    hi = x.astype(BF16)
    return hi, (x - hi.astype(F32)).astype(BF16)


def _peer_joint_topk(s1, s2):
    k = PEER_TOPK
    t = s1.shape[1]
    sub = lax.broadcasted_iota(jnp.int32, (8, t), 0)
    depth = jnp.full((8, t), k // 8, jnp.int32)
    for i in range(6, -1, -1):
        depth = jnp.where(sub == i, k // (i + 1), depth)
    subf = sub.astype(F32)
    lo = [jnp.where(depth > d, s1[0:8] + s2[d:d + 1], -jnp.inf) for d in range(k)]
    hi0 = s1[8:16] + s2[0:1]
    pos_hi = (subf + 8.0) * float(k)
    cnt = jnp.zeros((8, t), F32)
    out_row = lax.broadcasted_iota(jnp.int32, (k, t), 0)
    vals = jnp.zeros((k, t), F32)
    poss = jnp.zeros((k, t), F32)
    big = float(k * k)
    for rnd in range(k):
        pos_lo = subf * float(k) + cnt
        m = jnp.max(jnp.maximum(lo[0], hi0), axis=0, keepdims=True)
        cand = jnp.minimum(jnp.where(lo[0] == m, pos_lo, big), jnp.where(hi0 == m, pos_hi, big))
        pos = jnp.min(cand, axis=0, keepdims=True)
        vals = jnp.where(out_row == rnd, m, vals)
        poss = jnp.where(out_row == rnd, pos, poss)
        won_lo = pos_lo == pos
        for d in range(k - 1 - rnd):
            lo[d] = jnp.where(won_lo, lo[d + 1], lo[d])
        cnt = cnt + jnp.where(won_lo, 1.0, 0.0)
        hi0 = jnp.where(pos_hi == pos, -jnp.inf, hi0)
    p = poss.astype(jnp.int32)
    return vals, p >> 4, p & (k - 1)


def _peer_route_head(q, kh, kl):
    k = PEER_TOPK
    assert k == 16
    qh, ql = _split_bf16(q)
    s = _dot_nt(kh, qh) + (_dot_nt(kh, ql) + _dot_nt(kl, qh))
    s1, i1 = _topk_rows_slabs(s[:PEER_N_KEYS], k)
    s2, i2 = _topk_rows_slabs(s[PEER_N_KEYS:], k)
    ts, ri, rj = _peer_joint_topk(s1, s2)
    e = jnp.exp(ts - ts[0:1, :])
    return _select_rows(i1, ri, k), _select_rows(i2, rj, k), e / jnp.sum(e, axis=0, keepdims=True)


def _peer_query(h, wqh, wql):
    hh, hl = _split_bf16(h)
    return _dot(hh, wqh) + (_dot(hl, wqh) + _dot(hh, wql))


def _peer_route_kernel(h_ref, wqh_ref, wql_ref, kh_ref, kl_ref, a_o, b_o, g_o, q_s, a_s, b_s, g_s):
    k = PEER_TOPK
    q_s[...] = _peer_query(h_ref[...], wqh_ref[...], wql_ref[...])

    def head(h):
        lo = pl.multiple_of(h * PEER_DK, PEER_DK)
        ro = pl.multiple_of(h * k, k)
        a_s[pl.ds(ro, k), :], b_s[pl.ds(ro, k), :], g_s[pl.ds(ro, k), :] = _peer_route_head(
            q_s[:, pl.ds(lo, PEER_DK)], kh_ref[h], kl_ref[h])

    def head_group(i, _):
        for j in range(PEER_HEADS_PER_TRIP):
            head(PEER_HEADS_PER_TRIP * i + j)
        return 0

    lax.fori_loop(0, PEER_HEADS // PEER_HEADS_PER_TRIP, head_group, 0)
    a_o[...] = a_s[...].T
    b_o[...] = b_s[...].T
    g_o[...] = g_s[...].T


def _peer_route(h2, wq_hl, keys_hl):
    n, d = h2.shape
    t = 256
    wq_hi, wq_lo = wq_hl
    k_hi, k_lo = keys_hl
    return pl.pallas_call(
        _peer_route_kernel,
        grid=(n // t,),
        in_specs=[pl.BlockSpec((t, d), lambda i: (i, 0)),
                  pl.BlockSpec(wq_hi.shape, lambda i: (0, 0)),
                  pl.BlockSpec(wq_lo.shape, lambda i: (0, 0)),
                  pl.BlockSpec(k_hi.shape, lambda i: (0, 0, 0)),
                  pl.BlockSpec(k_lo.shape, lambda i: (0, 0, 0))],
        out_specs=[pl.BlockSpec((t, PEER_SLOTS), lambda i: (i, 0))] * 3,
        out_shape=[jax.ShapeDtypeStruct((n, PEER_SLOTS), jnp.int32),
                   jax.ShapeDtypeStruct((n, PEER_SLOTS), jnp.int32),
                   jax.ShapeDtypeStruct((n, PEER_SLOTS), F32)],
        scratch_shapes=[pltpu.VMEM((t, PEER_HEADS * PEER_DK), F32),
                        pltpu.VMEM((PEER_SLOTS, t), jnp.int32),
                        pltpu.VMEM((PEER_SLOTS, t), jnp.int32),
                        pltpu.VMEM((PEER_SLOTS, t), F32)],
        compiler_params=_cparams(("parallel",), 48 * 1024 * 1024),
        name="peer_route",
    )(h2, wq_hi, wq_lo, k_hi, k_lo)


_HI16 = -65536


def _bf16_bits(w):
    return lax.bitcast_convert_type(w, jnp.int32) & _HI16


def _peer_ffn_kernel(*refs, ec, unroll, final_norm, route_next):
    if route_next:
        (h_ref, a_ref, b_ref, g_ref, x_ref, gate_ref, u_ref, v_ref, fw_ref, qn_ref, kh_ref, kl_ref,
         o_ref, hb_s, w_s, a_c, b_c, g_c, a_n, b_n, g_n) = refs
    else:
        h_ref, a_ref, b_ref, g_ref, x_ref, gate_ref, u_ref, v_ref, fw_ref, o_ref, hb_s, w_s = refs
        a_c, b_c, g_c = a_ref, b_ref, g_ref
    e = pl.program_id(1)
    t = h_ref.shape[0]
    half = t // 2
    nk = PEER_N_KEYS

    @pl.when(e == 0)
    def _():
        if route_next:
            @pl.when(pl.program_id(0) == 0)
            def _():
                a_c[...] = a_ref[...]
                b_c[...] = b_ref[...]
                g_c[...] = g_ref[...]

        hb_s[...] = h_ref[...].astype(BF16)
        o_ref[...] = jnp.zeros_like(o_ref)
        jio = lax.broadcasted_iota(jnp.int32, (nk, PEER_SLOTS), 0)

        def tile(tt):
            arow = jnp.broadcast_to(a_c[pl.ds(tt, 1), :], (nk, PEER_SLOTS))
            brow = jnp.broadcast_to(b_c[pl.ds(tt, 1), :], (nk, PEER_SLOTS))
            grow = jnp.broadcast_to(g_c[pl.ds(tt, 1), :], (nk, PEER_SLOTS))
            cm = jnp.where(jio == arow, grow, 0.0).astype(BF16)
            bm_t = jnp.where(jio == brow, 1.0, 0.0).T.astype(BF16)
            return _dot(cm, bm_t)

        def build(tb, _):
            for u in range(unroll):
                tt = tb * unroll + u
                word = _bf16_bits(tile(tt + half)) | lax.shift_right_logical(_bf16_bits(tile(tt)), 16)
                w_s[pl.ds(pl.multiple_of(tt * PEER_W_PITCH, 8), nk), :] = word
            return 0

        lax.fori_loop(0, half // unroll, build, 0)

    if route_next:
        nsub = t // PEER_ROUTE_TOKENS
        hd = e // nsub
        co = pl.multiple_of((e % nsub) * PEER_ROUTE_TOKENS, PEER_ROUTE_TOKENS)
        ro = pl.multiple_of(hd * PEER_TOPK, PEER_TOPK)
        routed = _peer_route_head(
            qn_ref[pl.ds(co, PEER_ROUTE_TOKENS), pl.ds(pl.multiple_of(hd * PEER_DK, PEER_DK), PEER_DK)],
            kh_ref[hd], kl_ref[hd])

    hid = _dot(hb_s[...], u_ref[0])
    j0 = e * (ec // nk)
    words = jnp.concatenate([w_s[pl.ds(j0 + j, half, stride=PEER_W_PITCH), :] for j in range(ec // nk)], axis=1)
    w_lo = lax.bitcast_convert_type(lax.shift_left(words, 16), F32)
    w_hi = lax.bitcast_convert_type(words & _HI16, F32)
    wc = jnp.concatenate([w_lo, w_hi], axis=0)
    act = 0.5 * hid * (1.0 + lax.erf(hid * SQRT_HALF))
    o_ref[...] += _dot((wc * act).astype(BF16), v_ref[...])

    if route_next:
        a_n[pl.ds(ro, PEER_TOPK), pl.ds(co, PEER_ROUTE_TOKENS)] = routed[0]
        b_n[pl.ds(ro, PEER_TOPK), pl.ds(co, PEER_ROUTE_TOKENS)] = routed[1]
        g_n[pl.ds(ro, PEER_TOPK), pl.ds(co, PEER_ROUTE_TOKENS)] = routed[2]

    @pl.when(e == pl.num_programs(1) - 1)
    def _():
        y = x_ref[...] + gate_ref[0] * o_ref[...]
        o_ref[...] = _rms(y) * fw_ref[...] if final_norm else y
        if route_next:
            a_c[...] = a_n[...].T
            b_c[...] = b_n[...].T
            g_c[...] = g_n[...].T


def _peer_ffn(h2, a, b_idx, g, x, gate, u_blk, v, l, final_w, final_norm, route_next=None):
    n, d = h2.shape
    ne = v.shape[0]
    neb, _, ec = u_blk.shape
    t = min(512, l)
    nb = n // t
    unroll = 16
    assert l % t == 0 and ne == PEER_N_KEYS * PEER_N_KEYS and neb * ec == ne and (t // 2) % unroll == 0
    tok = lambda w: pl.BlockSpec((t, w), lambda i, e: (i, 0))
    in_specs = [tok(d), tok(PEER_SLOTS), tok(PEER_SLOTS), tok(PEER_SLOTS), tok(d),
                pl.BlockSpec((1, 1, d), lambda i, e: ((i * t) // l, 0, 0)),
                pl.BlockSpec((1, d, ec), lambda i, e: (e, 0, 0)),
                pl.BlockSpec((ec, d), lambda i, e: (e, 0)),
                pl.BlockSpec((1, d), lambda i, e: (0, 0))]
    args = [h2, a, b_idx, g, x, gate, u_blk, v, final_w.reshape(1, d)]
    scratch = [pltpu.VMEM((t, d), BF16), pltpu.VMEM((t // 2 * PEER_W_PITCH, PEER_N_KEYS), jnp.int32)]
    if route_next is not None:
        q_all, k_hi, k_lo = route_next
        assert neb == PEER_HEADS * (t // PEER_ROUTE_TOKENS) and a.shape[0] == t
        first = pl.BlockSpec((t, PEER_SLOTS), lambda i, e: (0, 0))
        in_specs[1:4] = [first, first, first]
        in_specs += [pl.BlockSpec((t, q_all.shape[1]), lambda i, e: (jnp.minimum(i + 1, nb - 1), 0)),
                     pl.BlockSpec(k_hi.shape, lambda i, e: (0, 0, 0)),
                     pl.BlockSpec(k_lo.shape, lambda i, e: (0, 0, 0))]
        args += [q_all, k_hi, k_lo]
        scratch += [pltpu.VMEM((t, PEER_SLOTS), jnp.int32), pltpu.VMEM((t, PEER_SLOTS), jnp.int32),
                    pltpu.VMEM((t, PEER_SLOTS), F32),
                    pltpu.VMEM((PEER_SLOTS, t), jnp.int32), pltpu.VMEM((PEER_SLOTS, t), jnp.int32),
                    pltpu.VMEM((PEER_SLOTS, t), F32)]
    return pl.pallas_call(
        functools.partial(_peer_ffn_kernel, ec=ec, unroll=unroll, final_norm=final_norm,
                          route_next=route_next is not None),
        grid=(nb, neb),
        in_specs=in_specs,
        out_specs=tok(d),
        out_shape=jax.ShapeDtypeStruct((n, d), F32),
        scratch_shapes=scratch,
        compiler_params=_cparams(("arbitrary", "arbitrary"), VMEM_LIMIT_V7X),
        name="peer_ffn_routed" if route_next is not None else "peer_ffn",
    )(*args)


def _peer_query_kernel(h_ref, wqh_ref, wql_ref, q_o):
    q_o[...] = _peer_query(h_ref[...], wqh_ref[...], wql_ref[...])


def _peer_queries(h2, wq_hl):
    n, d = h2.shape
    wq_hi, wq_lo = wq_hl
    t = min(512, n)
    return pl.pallas_call(
        _peer_query_kernel,
        grid=(n // t,),
        in_specs=[pl.BlockSpec((t, d), lambda i: (i, 0)),
                  pl.BlockSpec(wq_hi.shape, lambda i: (0, 0)),
                  pl.BlockSpec(wq_lo.shape, lambda i: (0, 0))],
        out_specs=pl.BlockSpec((t, wq_hi.shape[1]), lambda i: (i, 0)),
        out_shape=jax.ShapeDtypeStruct((n, wq_hi.shape[1]), F32),
        compiler_params=_cparams(("parallel",), 40 * 1024 * 1024),
        name="peer_queries",
    )(h2, wq_hi, wq_lo)


def _layout_w_in(w):
    parts = jnp.split(w, IN_OFFSETS, axis=1)
    z = lambda n: jnp.zeros((w.shape[0], n), w.dtype)
    kpe_blk = jnp.concatenate([z(MLA_NOPE), parts[2], z(MLA_HEAD_PAD - MLA_NOPE - MLA_ROPE)], axis=1)
    return jnp.concatenate([parts[0], parts[1], kpe_blk] + list(parts[3:]), axis=1).astype(BF16)


def _layout_mla(w_uq, w_ukv):
    qh = w_uq.reshape(MLA_Q_LORA, N_HEADS, MLA_NOPE + MLA_ROPE)
    qh = jnp.pad(qh, ((0, 0), (0, 0), (0, MLA_HEAD_PAD - MLA_NOPE - MLA_ROPE)))
    kv = w_ukv.reshape(MLA_KV_LORA, N_HEADS, MLA_NOPE + MLA_V)
    kh = jnp.pad(kv[:, :, :MLA_NOPE], ((0, 0), (0, 0), (0, MLA_HEAD_PAD - MLA_NOPE)))
    vh = kv[:, :, MLA_NOPE:]
    return (qh.reshape(MLA_Q_LORA, -1).astype(BF16), kh.reshape(MLA_KV_LORA, -1).astype(BF16),
            vh.reshape(MLA_KV_LORA, -1).T.astype(BF16))


def _static_mats():
    gseg = np.kron(np.eye(N_HEADS), np.ones((HEAD_W, HEAD_W))).astype(np.float32)
    pm = np.zeros((MLA_HEAD_PAD, MLA_HEAD_PAD), np.float32)
    for dd in range(MLA_ROPE):
        blk, j = dd // 16, dd % 16
        pm[MLA_NOPE + blk * 16 + (j + 8) % 16, MLA_NOPE + dd] = 1.0
    pg = np.zeros((BRANCH_W, BRANCH_W), np.float32)
    for i in range(BRANCH_W):
        off, dd = (i // HEAD_W) * HEAD_W, i % HEAD_W
        blk, j = dd // 32, dd % 32
        pg[off + blk * 32 + (j + 16) % 32, i] = 1.0
    ex = np.zeros((GQA_KV_HEADS * HEAD_W, BRANCH_W), np.float32)
    for i in range(BRANCH_W):
        ex[((i // HEAD_W) // (N_HEADS // GQA_KV_HEADS)) * HEAD_W + i % HEAD_W, i] = 1.0
    return tuple(jnp.asarray(m, dtype=BF16) for m in (gseg, pm, pg, ex, ex.T, np.eye(BRANCH_W)))


def _rope_half_tables(pos, hf):
    freqs = ROPE_THETA ** (-jnp.arange(hf, dtype=F32) / hf)
    ang = pos[:, None] * freqs[None, :]
    c, s = jnp.cos(ang), jnp.sin(ang)
    return jnp.concatenate([c, c], axis=1), jnp.concatenate([-s, s], axis=1)


def _axial_tables(row, col, dims):
    cr, sr = _rope_half_tables(row, dims // 4)
    cc, sc = _rope_half_tables(col, dims // 4)
    return jnp.concatenate([cr, cc], axis=1), jnp.concatenate([sr, sc], axis=1)


def _rope_tables(s):
    t = jnp.arange(s)
    row, col = (t // GRID_W).astype(F32), (t % GRID_W).astype(F32)
    c32, s32 = _axial_tables(row, col, MLA_ROPE)
    pad = MLA_HEAD_PAD - MLA_NOPE - MLA_ROPE
    cm = jnp.concatenate([jnp.ones((s, MLA_NOPE), F32), c32, jnp.ones((s, pad), F32)], axis=1)
    sm = jnp.concatenate([jnp.zeros((s, MLA_NOPE), F32), s32, jnp.zeros((s, pad), F32)], axis=1)
    c64, s64 = _axial_tables(row, col, HEAD_W)
    return cm, sm, jnp.tile(c64, (1, N_HEADS)), jnp.tile(s64, (1, N_HEADS))


def _split_f32(w):
    hi = w.astype(BF16)
    return hi, (w - hi.astype(F32)).astype(BF16)


def _layout_peer_keys(keys):
    h, _, nk, dh = keys.shape
    z = jnp.zeros((h, nk, dh), keys.dtype)
    top = jnp.concatenate([keys[:, 0], z], axis=2)
    bot = jnp.concatenate([z, keys[:, 1]], axis=2)
    return jnp.concatenate([top, bot], axis=1)


def kernel(x, c, ctx, c_ctx, mod_w, mod_b, norm1_w, norm2_w, w_in, mla_q_norm, mla_w_uq, mla_kv_norm, mla_w_ukv, gqa_q_norm, gqa_k_norm, na_bias, ret_decay_logit, ret_gn_w, w_branch, w_out, peer_w_q, peer_keys, peer_u, peer_v, final_norm_w):
    b, s, d = x.shape
    lc = ctx.shape[1]
    depth = mod_w.shape[0]
    assert d == D_MODEL and s % (GRID_W * NA_Q_ROWS) == 0 and s % 256 == 0 and lc % 256 == 0

    rows = -(-(b + 1) // 8) * 8
    cc = jnp.zeros((rows, d), F32).at[:b].set(c).at[b].set(c_ctx)
    mod = _modulation(cc, mod_w, mod_b)

    gseg, pm, pg, ex, ex_t, eye = _static_mats()
    tables = _rope_tables(s)

    for l in range(depth):
        need_ctx = l < depth - 1
        mx = mod[l, :b].reshape(b, 1, 6, d)
        mc = jnp.broadcast_to(mod[l, b].reshape(1, 1, 6, d), (b, 1, 6, d))
        sh1x, sc1x, g1x, sh2x, sc2x, g2x = (mx[:, :, i] for i in range(6))
        sh1c, sc1c, g1c, sh2c, sc2c, g2c = (mc[:, :, i] for i in range(6))

        w_in_l = _layout_w_in(w_in[l])
        wuq, wk, wv = _layout_mla(mla_w_uq[l], mla_w_ukv[l])
        consts = (mla_q_norm[l].reshape(1, -1), wuq, mla_kv_norm[l].reshape(1, -1), wk, wv,
                  jnp.tile(gqa_q_norm[l], N_HEADS).reshape(1, -1),
                  jnp.tile(gqa_k_norm[l], GQA_KV_HEADS).reshape(1, -1), gseg, pm, pg, ex, ex_t, eye)
        n1w = norm1_w[l].reshape(1, d)
        n2w = norm2_w[l].reshape(1, d)
        wb = w_branch[l].astype(BF16)
        wo = w_out[l].astype(BF16)
        keys_hl = _split_f32(_layout_peer_keys(peer_keys[l]))
        wq_hl = _split_f32(peer_w_q[l])
        ne = peer_u.shape[1]
        u_blk = jnp.swapaxes(peer_u[l].astype(BF16).reshape(ne // PEER_EXPERT_CHUNK, PEER_EXPERT_CHUNK, d), 1, 2)
        v_b = peer_v[l].astype(BF16)
        log_g = jax.nn.log_sigmoid(ret_decay_logit[l].astype(F32))

        projx = _inproj(x, n1w, sc1x, sh1x, w_in_l)
        projc = _inproj(ctx, n1w, sc1c, sh1c, w_in_l)
        qmx, kmx, vmx, qgx, kgx, vgx, nvx = _prep(projx, consts, tables)
        qmc, kmc, vmc, qgc, kgc, vgc, nvc = _prep(projc, consts, None)

        oa, ob = _attention([dict(q=(qmx, 0), segs=[((kmc, 0), vmc), ((kmx, 0), vmx)], dq=MLA_HEAD_PAD),
                             dict(q=(qgx, 0), segs=[((kgc, 0), vgc), ((kgx, 0), vgx)], dq=HEAD_W)], "attn_mla_gqa")
        oc = _na_attention(projx, projc, nvx, nvc, na_bias[l])
        od, od_c = _retention(projx, projc, log_g, ret_gn_w[l], gseg, need_ctx)

        x, h2x = _merge((oa, ob, oc, od), projx, x, g1x, sc2x, sh2x, n2w, wb, wo)
        h2f = h2x.reshape(b * s, d)
        first = min(512, s)
        ax, bx, gx = _peer_route(h2f[:first], wq_hl, keys_hl)
        x = _peer_ffn(h2f, ax, bx, gx, x.reshape(b * s, d), g2x, u_blk, v_b, s, final_norm_w,
                      final_norm=(l == depth - 1),
                      route_next=(_peer_queries(h2f, wq_hl),) + keys_hl).reshape(b, s, d)

        if need_ctx:
            ca, cb, ccx = _attention(
                [dict(q=(qmc, 0), segs=[((kmc, 0), vmc)], dq=MLA_HEAD_PAD),
                 dict(q=(qgc, 0), segs=[((kgc, 0), vgc)], dq=HEAD_W),
                 dict(q=(projc, COL_NQ), segs=[((projc, COL_NK), nvc)], dq=HEAD_W, qscale=HEAD_W ** -0.5,
                      log2_scores=False)], "attn_ctx")
            ctx, h2c = _merge((ca, cb, ccx, od_c), projc, ctx, g1c, sc2c, sh2c, n2w, wb, wo)
            ac, bc, gc = _peer_route(h2c.reshape(b * lc, d), wq_hl, keys_hl)
            ctx = _peer_ffn(h2c.reshape(b * lc, d), ac, bc, gc, ctx.reshape(b * lc, d), g2c, u_blk, v_b,
                            lc, final_norm_w, final_norm=False).reshape(b, lc, d)

    return x
```

```python
import functools

import numpy as np
import jax
import jax.numpy as jnp
from jax import lax
from jax.experimental import pallas as pl
from jax.experimental.pallas import tpu as pltpu

F32 = jnp.float32
BF16 = jnp.bfloat16
HIGHEST = lax.Precision.HIGHEST

D_MODEL = 1024
GRID_W = 64
ROPE_THETA = 10000.0
EPS = 1e-6
N_HEADS = 4
HEAD_W = 64
BRANCH_W = N_HEADS * HEAD_W
MLA_NOPE, MLA_ROPE, MLA_V = 64, 32, 64
MLA_Q_LORA, MLA_KV_LORA = 256, 128
MLA_SCALE = (MLA_NOPE + MLA_ROPE) ** -0.5
MLA_HEAD_PAD = 128
GQA_KV_HEADS = 2
NA_WIN_R, NA_WIN_C = 8, 16
NA_Q_ROWS = 4
ATTN_KEY_CHUNK = 1024
RET_CHUNK = 128
N_BRANCH = 4
PEER_HEADS, PEER_N_KEYS, PEER_TOPK, PEER_DK = 8, 128, 16, 128
PEER_SLOTS = PEER_HEADS * PEER_TOPK
PEER_W_PITCH = PEER_N_KEYS + 8
PEER_HEADS_PER_TRIP = 4
PEER_ROUTE_TOKENS = 256
PEER_EXPERT_CHUNK = 1024
SQRT_HALF = 0.7071067811865476
LOG2E = 1.4426950408889634
NEG_BIG = -1e30

IN_SIZES = (256, 128, 32, 256, 128, 128, 256, 256, 256, 256, 256, 256, 256, 256, 4096)
IN_OFFSETS = tuple(int(v) for v in np.cumsum(IN_SIZES)[:-1])
PROJ_COLS = 7168
COL_NQ, COL_NK, COL_NV = 1024, 1280, 1536
COL_RET = 1792
COL_GATES = 3072

VMEM_LIMIT_V7X = 56 * 1024 * 1024


def _cparams(sem, vmem=None):
    return pltpu.CompilerParams(dimension_semantics=sem, vmem_limit_bytes=vmem)


def _dot(a, b):
    return jnp.dot(a, b, preferred_element_type=F32)


def _dot_hi(a, b):
    return jnp.dot(a, b, preferred_element_type=F32, precision=HIGHEST)


def _dot_nt(a, b):
    return lax.dot_general(a, b, (((1,), (1,)), ((), ())), preferred_element_type=F32)


def _dot_sel(x, sel):
    hi = x.astype(BF16)
    r1 = x - hi.astype(F32)
    mid = r1.astype(BF16)
    lo = (r1 - mid.astype(F32)).astype(BF16)
    return _dot(hi, sel) + (_dot(mid, sel) + _dot(lo, sel))


def _rms(x):
    return x * lax.rsqrt(jnp.mean(x * x, axis=-1, keepdims=True) + EPS)


def _silu(x):
    return x * jax.nn.sigmoid(x)


def _head_mask(shape, h, width=HEAD_W):
    lane = lax.broadcasted_iota(jnp.int32, shape, len(shape) - 1)
    lo = h * width
    return (lane >= lo) & (lane < lo + width)


def _mod_kernel(c_ref, w_ref, b_ref, o_ref):
    o_ref[0] = _dot_hi(_silu(c_ref[...]), w_ref[0]) + b_ref[0]


def _modulation(cc, mod_w, mod_b):
    depth, d, n = mod_w.shape
    rows = cc.shape[0]
    tn = 1536
    return pl.pallas_call(
        _mod_kernel,
        grid=(depth, n // tn),
        in_specs=[pl.BlockSpec((rows, d), lambda l, j: (0, 0)),
                  pl.BlockSpec((1, d, tn), lambda l, j: (l, 0, j)),
                  pl.BlockSpec((1, 1, tn), lambda l, j: (l, 0, j))],
        out_specs=pl.BlockSpec((1, rows, tn), lambda l, j: (l, 0, j)),
        out_shape=jax.ShapeDtypeStruct((depth, rows, n), F32),
        compiler_params=_cparams(("parallel", "parallel"), 40 * 1024 * 1024),
        name="modulation",
    )(cc, mod_w, mod_b.reshape(depth, 1, n))


def _inproj_kernel(x_ref, nw_ref, sc_ref, sh_ref, w_ref, o_ref, h_scr):
    @pl.when(pl.program_id(2) == 0)
    def _():
        h = _rms(x_ref[0]) * nw_ref[...] * (1.0 + sc_ref[0]) + sh_ref[0]
        h_scr[...] = h.astype(BF16)

    o_ref[0] = _dot(h_scr[...], w_ref[...]).astype(o_ref.dtype)


def _inproj(x, nw, sc, sh, w):
    b, l, d = x.shape
    n = w.shape[1]
    tm = min(1024, l)
    tn = 1792
    return pl.pallas_call(
        _inproj_kernel,
        grid=(b, l // tm, n // tn),
        in_specs=[pl.BlockSpec((1, tm, d), lambda bi, i, j: (bi, i, 0)),
                  pl.BlockSpec((1, d), lambda bi, i, j: (0, 0)),
                  pl.BlockSpec((1, 1, d), lambda bi, i, j: (bi, 0, 0)),
                  pl.BlockSpec((1, 1, d), lambda bi, i, j: (bi, 0, 0)),
                  pl.BlockSpec((d, tn), lambda bi, i, j: (0, j))],
        out_specs=pl.BlockSpec((1, tm, tn), lambda bi, i, j: (bi, i, j)),
        out_shape=jax.ShapeDtypeStruct((b, l, n), BF16),
        scratch_shapes=[pltpu.VMEM((tm, d), BF16)],
        compiler_params=_cparams(("parallel", "parallel", "arbitrary"), 40 * 1024 * 1024),
        name="inproj",
    )(x, nw, sc, sh, w)


def _prep_kernel(*refs, use_rope):
    (p_ref, nv_ref, qn_ref, wuq_ref, kvn_ref, wk_ref, wv_ref, gqn_ref, gkn_ref, gseg_ref, pm_ref, pg_ref,
     e_ref, et_ref, eye_ref) = refs[:15]
    if use_rope:
        cm_ref, sm_ref, cg_ref, sg_ref = refs[15:19]
        outs = refs[19:]
    else:
        outs = refs[15:]
    qm_o, km_o, vm_o, qg_o, kg_o, vg_o, nvt_o = outs
    nvt_o[0] = _dot_nt(eye_ref[...], nv_ref[0]).astype(BF16)

    pb = p_ref[0]
    cq = pb[:, 0:256].astype(F32)
    ckv = pb[:, 256:384].astype(F32)
    kpe = pb[:, 384:512].astype(F32)
    gq = pb[:, 512:768].astype(F32)
    gk = pb[:, 768:896].astype(F32)
    gv = pb[:, 896:1024]

    cqn = (_rms(cq) * qn_ref[...]).astype(BF16)
    qa = _dot(cqn, wuq_ref[...])
    ckn = (_rms(ckv) * kvn_ref[...]).astype(BF16)
    kn = _dot(ckn, wk_ref[...])
    vm_t = _dot_nt(wv_ref[...], ckn)
    if use_rope:
        cm, sm = cm_ref[...], sm_ref[...]
        pm = pm_ref[...]

        def rope_m(t):
            return t * cm + _dot_sel(t, pm) * sm

        qa = jnp.concatenate([rope_m(qa[:, h * 128:(h + 1) * 128]) for h in range(N_HEADS)], axis=1)
        kpe = rope_m(kpe)
    km = kn + jnp.concatenate([kpe] * N_HEADS, axis=1)
    qm_o[0] = (qa * (MLA_SCALE * LOG2E)).astype(BF16)
    km_o[0] = km.astype(BF16)
    vm_o[0] = vm_t.astype(BF16)

    gseg = gseg_ref[...]
    gqn = gq * lax.rsqrt(_dot_sel(gq * gq, gseg) * (1.0 / HEAD_W) + EPS) * gqn_ref[...]
    gkn = gk * lax.rsqrt(_dot_sel(gk * gk, gseg[:128, :128]) * (1.0 / HEAD_W) + EPS) * gkn_ref[...]
    if use_rope:
        cg, sg = cg_ref[...], sg_ref[...]
        pg = pg_ref[...]
        gqn = gqn * cg + _dot_sel(gqn, pg) * sg
        gkn = gkn * cg[:, :128] + _dot_sel(gkn, pg[:128, :128]) * sg[:, :128]
    qg_o[0] = (gqn * (HEAD_W ** -0.5 * LOG2E)).astype(BF16)
    e = e_ref[...]
    kg_o[0] = _dot(gkn.astype(BF16), e).astype(BF16)
    vg_o[0] = _dot_nt(et_ref[...], gv).astype(BF16)


def _prep(proj, consts, tables):
    b, l, _ = proj.shape
    tm = min(512, l)
    use_rope = tables is not None
    full = lambda a: pl.BlockSpec(a.shape, lambda bi, i: (0,) * a.ndim)
    in_specs = [pl.BlockSpec((1, tm, 1024), lambda bi, i: (bi, i, 0)),
                pl.BlockSpec((1, tm, BRANCH_W), lambda bi, i: (bi, i, COL_NV // BRANCH_W))] + [full(a) for a in consts]
    args = [proj, proj] + list(consts)
    if use_rope:
        in_specs += [pl.BlockSpec((tm, t.shape[1]), lambda bi, i: (i, 0)) for t in tables]
        args += list(tables)
    tok = lambda w: (pl.BlockSpec((1, tm, w), lambda bi, i: (bi, i, 0)), jax.ShapeDtypeStruct((b, l, w), BF16))
    tr = (pl.BlockSpec((1, BRANCH_W, tm), lambda bi, i: (bi, 0, i)), jax.ShapeDtypeStruct((b, BRANCH_W, l), BF16))
    outs = (tok(512), tok(512), tr, tok(256), tok(256), tr, tr)
    return pl.pallas_call(
        functools.partial(_prep_kernel, use_rope=use_rope),
        grid=(b, l // tm),
        in_specs=in_specs,
        out_specs=[o[0] for o in outs],
        out_shape=[o[1] for o in outs],
        compiler_params=_cparams(("parallel", "parallel"), 40 * 1024 * 1024),
        name="prep_rope" if use_rope else "prep",
    )(*args)


def _attn_kernel(*refs, nseg, probs, tk):
    per = 1 + 2 * nseg
    nprob = len(probs)
    gw = 256
    ones_rows = 16
    state = []
    for pi, (dq, qscale, log2_scores) in enumerate(probs):
        q_ref = refs[pi * per]
        segs = [(refs[pi * per + 1 + 2 * i], refs[pi * per + 2 + 2 * i]) for i in range(nseg)]
        tq = q_ref.shape[1]
        hpg = gw // dq
        qstacks = []
        for g in range(N_HEADS // hpg):
            qg = q_ref[0, :, g * gw:(g + 1) * gw]
            if qscale is not None:
                qg = qg * jnp.asarray(qscale, BF16)
            qstacks.append(jnp.concatenate([jnp.where(_head_mask(qg.shape, j, dq), qg, jnp.zeros_like(qg))
                                            for j in range(hpg)], axis=0))
        state.append((segs, tq, hpg, qstacks, jnp.exp2 if log2_scores else jnp.exp))
    carry = tuple(tuple((jnp.full((1, st[1]), -jnp.inf, F32), jnp.zeros((HEAD_W + ones_rows, st[1]), F32))
                        for _ in range(N_HEADS)) for st in state)

    def chunk_step(pi, carry, si, off, tkk):
        segs, tq, hpg, qstacks, exp = state[pi]
        k_ref, vt_ref = segs[si]
        st_g = [_dot_nt(k_ref[0, pl.ds(off, tkk), g * gw:(g + 1) * gw], qstacks[g])
                for g in range(N_HEADS // hpg)]
        ones = jnp.ones((ones_rows, tkk), BF16)
        new = []
        for h in range(N_HEADS):
            m, acc = carry[h]
            vt = jnp.concatenate([vt_ref[0, h * HEAD_W:(h + 1) * HEAD_W, pl.ds(off, tkk)], ones], axis=0)
            st = st_g[h // hpg][:, (h % hpg) * tq:(h % hpg + 1) * tq]
            mn = jnp.maximum(m, jnp.max(st, axis=0, keepdims=True))
            acc = exp(m - mn) * acc + _dot(vt, exp(st - mn).astype(BF16))
            new.append((mn, acc))
        return tuple(new)

    for si in range(nseg):
        lk = state[0][0][si][0].shape[1]
        tkk = min(tk, lk)
        n = lk // tkk

        def body(c, carry, si=si, tkk=tkk):
            off = c * tkk if isinstance(c, int) else pl.multiple_of(c * tkk, tkk)
            return tuple(chunk_step(pi, carry[pi], si, off, tkk) for pi in range(nprob))

        carry = body(0, carry) if n == 1 else lax.fori_loop(0, n, body, carry)
    for pi in range(nprob):
        o_ref = refs[nprob * per + pi]
        out_t = jnp.concatenate([acc[:HEAD_W] * (1.0 / acc[HEAD_W:HEAD_W + 1]) for _, acc in carry[pi]], axis=0)
        o_ref[0] = out_t.T.astype(o_ref.dtype)


def _attention(problems, name):
    b, lq, _ = problems[0]["q"][0].shape
    tq = min(512, lq)
    in_specs, args, probs = [], [], []
    for p in problems:
        (qa, qcol), dq = p["q"], p["dq"]
        wq = N_HEADS * dq
        assert qcol % wq == 0 and qa.shape[:2] == (b, lq)
        in_specs.append(pl.BlockSpec((1, tq, wq), lambda bi, i, qcol=qcol, wq=wq: (bi, i, qcol // wq)))
        args.append(qa)
        for (ka, kcol), vt in p["segs"]:
            assert kcol % wq == 0 and vt.shape[1] == BRANCH_W and vt.shape[2] == ka.shape[1]
            in_specs.append(pl.BlockSpec((1, ka.shape[1], wq), lambda bi, i, kcol=kcol, wq=wq: (bi, 0, kcol // wq)))
            in_specs.append(pl.BlockSpec((1, BRANCH_W, vt.shape[2]), lambda bi, i: (bi, 0, 0)))
            args += [ka, vt]
        probs.append((dq, p.get("qscale"), p.get("log2_scores", True)))
    nseg = len(problems[0]["segs"])
    assert all(len(p["segs"]) == nseg for p in problems)
    outs = pl.pallas_call(
        functools.partial(_attn_kernel, nseg=nseg, probs=tuple(probs), tk=ATTN_KEY_CHUNK),
        grid=(b, lq // tq),
        in_specs=in_specs,
        out_specs=[pl.BlockSpec((1, tq, BRANCH_W), lambda bi, i: (bi, i, 0))] * len(problems),
        out_shape=[jax.ShapeDtypeStruct((b, lq, BRANCH_W), BF16)] * len(problems),
        compiler_params=_cparams(("parallel", "arbitrary"), VMEM_LIMIT_V7X),
        name=name,
    )(*args)
    return outs


def _proj_cols(arr, col, width, rows):
    assert col % width == 0
    return pl.BlockSpec((1, rows, width), lambda *idx: (idx[0], 0, col // width))


def _na_kernel(pat_ref, ks_ref, q_ref, k_ref, vt_ref, kc_ref, vct_ref, m_ref, o_ref, *, kw):
    del pat_ref
    g = pl.program_id(1)
    off = pl.multiple_of(ks_ref[g] * GRID_W, 128)
    q = q_ref[0] * jnp.asarray(HEAD_W ** -0.5, BF16)
    qb = q.shape[0]
    qstack = jnp.concatenate([jnp.where(_head_mask(q.shape, h), q, jnp.zeros_like(q))
                              for h in range(N_HEADS)], axis=0)
    st_w = _dot_nt(k_ref[0, pl.ds(off, kw), :], qstack)
    st_c = _dot_nt(kc_ref[0], qstack)
    ones_w = jnp.ones((16, kw), BF16)
    ones_c = jnp.ones((16, kc_ref.shape[1]), BF16)
    outs = []
    for h in range(N_HEADS):
        sw = st_w[:, h * qb:(h + 1) * qb] + m_ref[0, h]
        sc = st_c[:, h * qb:(h + 1) * qb]
        mx = jnp.maximum(jnp.max(sw, axis=0, keepdims=True), jnp.max(sc, axis=0, keepdims=True))
        vtw = jnp.concatenate([vt_ref[0, h * HEAD_W:(h + 1) * HEAD_W, pl.ds(off, kw)], ones_w], axis=0)
        vtc = jnp.concatenate([vct_ref[0, h * HEAD_W:(h + 1) * HEAD_W, :], ones_c], axis=0)
        acc = _dot(vtw, jnp.exp(sw - mx).astype(BF16)) + _dot(vtc, jnp.exp(sc - mx).astype(BF16))
        outs.append(acc[:HEAD_W] * (1.0 / acc[HEAD_W:HEAD_W + 1]))
    o_ref[0] = jnp.concatenate(outs, axis=0).T.astype(o_ref.dtype)


def _na_plan(s):
    rows = s // GRID_W
    wr = min(NA_WIN_R, rows)
    wc = NA_WIN_C
    qr = min(NA_Q_ROWS, rows)
    kwr = min(qr + wr - 1 + (1 if qr + wr - 1 < rows else 0), rows)
    ngrp = rows // qr
    qc = np.arange(GRID_W)[:, None]
    kc = np.arange(GRID_W)[None, :]
    cs = np.clip(qc - wc // 2, 0, GRID_W - wc)
    valid_c = (kc >= cs) & (kc < cs + wc)
    rel_c = np.where(valid_c, kc - qc + (NA_WIN_C - 1), 0)
    assert (valid_c.sum(1) == wc).all()
    pats, pat_ids, ks_rows = [], [], []
    for g in range(ngrp):
        r0 = g * qr
        ks = int(np.clip(r0 - wr // 2, 0, rows - kwr))
        assert (ks * GRID_W) % 128 == 0
        r = (r0 + np.arange(qr))[:, None]
        kr = (ks + np.arange(kwr))[None, :]
        rs = np.clip(r - wr // 2, 0, rows - wr)
        valid_r = (kr >= rs) & (kr < rs + wr)
        assert (valid_r.sum(1) == wr).all()
        rel_r = np.where(valid_r, kr - r + (NA_WIN_R - 1), 0)
        key = (valid_r.tobytes(), rel_r.tobytes())
        for pi, (pk, *_rest) in enumerate(pats):
            if pk == key:
                pat_ids.append(pi)
                break
        else:
            pat_ids.append(len(pats))
            pats.append((key, valid_r, rel_r))
        ks_rows.append(ks)
    valid_r = np.stack([p[1] for p in pats])
    rel_r = np.stack([p[2] for p in pats])
    return (qr, kwr, np.asarray(pat_ids, np.int32), np.asarray(ks_rows, np.int32), valid_r, rel_r, valid_c, rel_c)


def _na_bias_masks(na_bias, valid_r, rel_r, valid_c, rel_c):
    h = na_bias.shape[0]
    npat, qr, kwr = valid_r.shape
    ncol = 2 * NA_WIN_C - 1
    brow = na_bias[:, rel_r, :].astype(F32)
    onehot_c = ((rel_c[None] == np.arange(ncol)[:, None, None]) & valid_c[None]).astype(np.float32)
    m = jnp.einsum('hpqkc,cxy->phkyqx', brow, jnp.asarray(onehot_c), precision=HIGHEST)
    valid = (valid_r.transpose(0, 2, 1)[:, None, :, None, :, None]
             & valid_c.T[None, None, None, :, None, :])
    m = jnp.where(valid, m, NEG_BIG)
    return m.reshape(npat, h, kwr * GRID_W, qr * GRID_W)


def _na_attention(projx, projc, nvx_t, nvc_t, na_bias):
    b, s, _ = projx.shape
    lc = projc.shape[1]
    qr, kwr, pat_ids, ks_rows, valid_r, rel_r, valid_c, rel_c = _na_plan(s)
    qb, kw = qr * GRID_W, kwr * GRID_W
    assert kw % 128 == 0
    mb = _na_bias_masks(na_bias, valid_r, rel_r, valid_c, rel_c)
    grid_spec = pltpu.PrefetchScalarGridSpec(
        num_scalar_prefetch=2,
        grid=(b, s // qb),
        in_specs=[pl.BlockSpec((1, qb, 256), lambda bi, g, pat, ks: (bi, g, COL_NQ // 256)),
                  pl.BlockSpec((1, s, 256), lambda bi, g, pat, ks: (bi, 0, COL_NK // 256)),
                  pl.BlockSpec((1, BRANCH_W, s), lambda bi, g, pat, ks: (bi, 0, 0)),
                  pl.BlockSpec((1, lc, 256), lambda bi, g, pat, ks: (bi, 0, COL_NK // 256)),
                  pl.BlockSpec((1, BRANCH_W, lc), lambda bi, g, pat, ks: (bi, 0, 0)),
                  pl.BlockSpec((1, N_HEADS, kw, qb), lambda bi, g, pat, ks: (pat[g], 0, 0, 0))],
        out_specs=pl.BlockSpec((1, qb, BRANCH_W), lambda bi, g, pat, ks: (bi, g, 0)),
    )
    return pl.pallas_call(
        functools.partial(_na_kernel, kw=kw),
        grid_spec=grid_spec,
        out_shape=jax.ShapeDtypeStruct((b, s, BRANCH_W), BF16),
        compiler_params=_cparams(("parallel", "arbitrary"), 48 * 1024 * 1024),
        name="na_attention",
    )(jnp.asarray(pat_ids), jnp.asarray(ks_rows), projx, projx, nvx_t, projc, nvc_t, mb)


def _ret_kernel(lgs_ref, lgl_ref, gnw_ref, gseg_ref,
                qx, kx, vx, gfx, gbx, qc, kc, vc, gfc, gbc,
                yx_o, yc_o, of_s, ob_s, st_s, dec_s, qk_s, *, need_ctx):
    c = RET_CHUNK
    lc = qc.shape[1]
    sx = qx.shape[1]
    n_col = lax.broadcasted_iota(jnp.int32, (c, c), 0).astype(F32)
    m_row = lax.broadcasted_iota(jnp.int32, (c, c), 1).astype(F32)
    diff = n_col - m_row
    for h in range(N_HEADS):
        dec_s[0, :, h * c:(h + 1) * c] = jnp.where(diff >= 0, jnp.exp(lgs_ref[h] * jnp.maximum(diff, 0.0)), 0.0)
        dec_s[1, :, h * c:(h + 1) * c] = jnp.where(diff <= 0,
                                                   jnp.exp(lgs_ref[N_HEADS + h] * jnp.maximum(-diff, 0.0)), 0.0)
    pos = lax.broadcasted_iota(jnp.int32, (c, BRANCH_W), 0).astype(F32)
    lgf, lgb = lgl_ref[0], lgl_ref[1]
    qk_s[0] = jnp.exp(lgf * (pos + 1.0))
    qk_s[1] = jnp.exp(lgf * (c - 1.0 - pos))
    qk_s[2] = jnp.exp(lgb * (c - pos))
    qk_s[3] = jnp.exp(lgb * pos)
    cd_f = jnp.exp(lgf * float(c))
    cd_b = jnp.exp(lgb * float(c))
    st_s[...] = jnp.zeros_like(st_s)
    rowb = lax.broadcasted_iota(jnp.int32, (BRANCH_W, BRANCH_W), 0) // HEAD_W
    colb = lax.broadcasted_iota(jnp.int32, (BRANCH_W, BRANCH_W), 1) // HEAD_W
    bd_mask = rowb == colb

    def chunk_step(q, k, v, state, d, cd):
        kk = k * jnp.asarray(HEAD_W ** -0.5, BF16)
        o = _dot(q, state.astype(BF16)) * qk_s[2 * d]
        kstack = jnp.concatenate([jnp.where(_head_mask(kk.shape, h), kk, jnp.zeros_like(kk))
                                  for h in range(N_HEADS)], axis=0)
        vstack = jnp.concatenate([jnp.where(_head_mask(v.shape, h), v, jnp.zeros_like(v))
                                  for h in range(N_HEADS)], axis=0)
        inner = _dot_nt(q, kstack) * dec_s[d]
        o = o + _dot(inner.astype(BF16), vstack)
        kd = (kk.astype(F32) * qk_s[2 * d + 1]).astype(BF16)
        upd = lax.dot_general(kd, v, (((0,), (0,)), ((), ())), preferred_element_type=F32)
        return o, state * cd + jnp.where(bd_mask, upd, 0.0)

    gseg = gseg_ref[...]
    gnw = gnw_ref[...]

    def gnorm(o):
        mu = _dot_sel(o, gseg) * (1.0 / HEAD_W)
        dlt = o - mu
        var = _dot_sel(dlt * dlt, gseg) * (1.0 / HEAD_W)
        return dlt * lax.rsqrt(var + EPS) * gnw

    def combine(gf_ref, gb_ref, y_ref, base, ro):
        y = (gnorm(of_s[pl.ds(base + ro, c), :]) * _silu(gf_ref[0, pl.ds(ro, c), :].astype(F32))
             + gnorm(ob_s[pl.ds(base + ro, c), :]) * _silu(gb_ref[0, pl.ds(ro, c), :].astype(F32)))
        y_ref[0, pl.ds(ro, c), :] = y.astype(y_ref.dtype)

    def scan(q_ref, k_ref, v_ref, base, n, gated):
        def load(off):
            return q_ref[0, pl.ds(off, c), :], k_ref[0, pl.ds(off, c), :], v_ref[0, pl.ds(off, c), :]

        def body(j, _, emit):
            offs_f = [pl.multiple_of((2 * j + u) * c, c) for u in range(2)]
            offs_b = [pl.multiple_of((n - 1 - 2 * j - u) * c, c) for u in range(2)]
            ins_f = [load(o) for o in offs_f]
            ins_b = [load(o) for o in offs_b]
            sf, sb = st_s[0], st_s[1]
            outs_f, outs_b = [], []
            for u in range(2):
                o, sf = chunk_step(*ins_f[u], sf, 0, cd_f)
                outs_f.append(o)
                o, sb = chunk_step(*ins_b[u], sb, 1, cd_b)
                outs_b.append(o)
            st_s[0], st_s[1] = sf, sb
            for u in range(2):
                of_s[pl.ds(base + offs_f[u], c), :] = outs_f[u]
                ob_s[pl.ds(base + offs_b[u], c), :] = outs_b[u]
            for off in emit(offs_f, offs_b):
                combine(*gated, base, off)
            return 0

        none = lambda offs_f, offs_b: []
        both = none if gated is None else (lambda offs_f, offs_b: offs_f + offs_b)
        if n == 2:
            lax.fori_loop(0, 1, functools.partial(body, emit=none if gated is None else (lambda f, b: f)), 0)
        else:
            assert n % 4 == 0
            lax.fori_loop(0, n // 4, functools.partial(body, emit=none), 0)
            lax.fori_loop(n // 4, n // 2, functools.partial(body, emit=both), 0)

    scan(qc, kc, vc, 0, lc // c, (gfc, gbc, yc_o) if need_ctx else None)
    scan(qx, kx, vx, lc, sx // c, (gfx, gbx, yx_o))
    if not need_ctx:
        yc_o[...] = jnp.zeros_like(yc_o)


def _retention(projx, projc, log_g, gn_w, gseg, need_ctx):
    b, s, _ = projx.shape
    lc = projc.shape[1]
    lgs = log_g.reshape(2 * N_HEADS)
    lgl = jnp.repeat(log_g, HEAD_W, axis=1).reshape(2, 1, BRANCH_W)
    xs = [_proj_cols(projx, COL_RET + 256 * i, 256, s) for i in range(5)]
    cs = [_proj_cols(projc, COL_RET + 256 * i, 256, lc) for i in range(5)]
    c = RET_CHUNK
    yx, yc = pl.pallas_call(
        functools.partial(_ret_kernel, need_ctx=need_ctx),
        grid=(b,),
        in_specs=[pl.BlockSpec(memory_space=pltpu.SMEM),
                  pl.BlockSpec((2, 1, BRANCH_W), lambda bi: (0, 0, 0)),
                  pl.BlockSpec((1, BRANCH_W), lambda bi: (0, 0)),
                  pl.BlockSpec((BRANCH_W, BRANCH_W), lambda bi: (0, 0))] + xs + cs,
        out_specs=[pl.BlockSpec((1, s, BRANCH_W), lambda bi: (bi, 0, 0)),
                   pl.BlockSpec((1, lc, BRANCH_W), lambda bi: (bi, 0, 0))],
        out_shape=[jax.ShapeDtypeStruct((b, s, BRANCH_W), BF16),
                   jax.ShapeDtypeStruct((b, lc, BRANCH_W), BF16)],
        scratch_shapes=[pltpu.VMEM((lc + s, BRANCH_W), F32),
                        pltpu.VMEM((lc + s, BRANCH_W), F32),
                        pltpu.VMEM((2, BRANCH_W, BRANCH_W), F32),
                        pltpu.VMEM((2, c, N_HEADS * c), F32),
                        pltpu.VMEM((4, c, BRANCH_W), F32)],
        compiler_params=_cparams(("parallel",), 48 * 1024 * 1024),
        name="retention",
    )(lgs, lgl, gn_w.reshape(1, BRANCH_W), gseg, *([projx] * 5), *([projc] * 5))
    return yx, yc


def _merge_kernel(oa, ob, oc, od, g0, g1, g2, g3, x_ref, gate_ref, sc_ref, sh_ref, nw_ref, wb_ref, wo_ref,
                  xn_o, h2_o):
    acc = None
    for i, (o, g) in enumerate(((oa, g0), (ob, g1), (oc, g2), (od, g3))):
        t = (0.5 * jnp.tanh(0.5 * g[0].astype(F32)) + 0.5) * _dot(o[0], wb_ref[i])
        acc = t if acc is None else acc + t
    y = _dot(acc.astype(BF16), wo_ref[...])
    xn = x_ref[0] + gate_ref[0] * y
    xn_o[0] = xn
    h2_o[0] = _rms(xn) * nw_ref[...] * (1.0 + sc_ref[0]) + sh_ref[0]


def _merge(outs, proj, x, gate, sc2, sh2, n2w, wb, wo):
    b, l, d = x.shape
    tm = min(512, l)
    tok = lambda w: pl.BlockSpec((1, tm, w), lambda bi, i: (bi, i, 0))
    vec = pl.BlockSpec((1, 1, d), lambda bi, i: (bi, 0, 0))
    gates = [pl.BlockSpec((1, tm, d), lambda bi, i, k=k: (bi, i, COL_GATES // d + k)) for k in range(N_BRANCH)]
    return pl.pallas_call(
        _merge_kernel,
        grid=(b, l // tm),
        in_specs=[tok(BRANCH_W)] * 4 + gates + [tok(d), vec, vec, vec,
                                                pl.BlockSpec((1, d), lambda bi, i: (0, 0)),
                                                pl.BlockSpec(wb.shape, lambda bi, i: (0, 0, 0)),
                                                pl.BlockSpec(wo.shape, lambda bi, i: (0, 0))],
        out_specs=[tok(d), tok(d)],
        out_shape=[jax.ShapeDtypeStruct((b, l, d), F32), jax.ShapeDtypeStruct((b, l, d), F32)],
        compiler_params=_cparams(("parallel", "parallel"), 48 * 1024 * 1024),
        name="merge",
    )(*outs, proj, proj, proj, proj, x, gate, sc2, sh2, n2w, wb, wo)


def _sorting_network(n):
    pairs = []
    p = 1
    while p < n:
        k = p
        while k >= 1:
            for j in range(k % p, n - k, 2 * k):
                for i in range(min(k, n - j - k)):
                    if (i + j) // (2 * p) == (i + j + k) // (2 * p):
                        pairs.append((i + j, i + j + k))
            k //= 2
        p *= 2
    return tuple(pairs)


def _topk_rows_slabs(s, k):
    r, t = s.shape
    assert r == 8 * k
    sub = lax.broadcasted_iota(jnp.int32, (8, t), 0).astype(F32)
    vs = [s[8 * j:8 * j + 8] for j in range(k)]
    ix = [sub + float(8 * j) for j in range(k)]
    for p, q in _sorting_network(k):
        va, ia, vb, ib = vs[p], ix[p], vs[q], ix[q]
        swap = (vb > va) | ((vb == va) & (ib < ia))
        vs[p], vs[q] = jnp.maximum(va, vb), jnp.minimum(va, vb)
        ix[p], ix[q] = jnp.where(swap, ib, ia), jnp.where(swap, ia, ib)
    out_row = lax.broadcasted_iota(jnp.int32, (k, t), 0)
    vals = jnp.zeros((k, t), F32)
    idxs = jnp.zeros((k, t), F32)
    for rnd in range(k):
        hv, hi = vs[0], ix[0]
        m = jnp.max(hv, axis=0, keepdims=True)
        idx = jnp.min(jnp.where(hv == m, hi, float(r)), axis=0, keepdims=True)
        vals = jnp.where(out_row == rnd, m, vals)
        idxs = jnp.where(out_row == rnd, idx, idxs)
        won = hi == idx
        for d in range(k - 1 - rnd):
            vs[d] = jnp.where(won, vs[d + 1], vs[d])
            ix[d] = jnp.where(won, ix[d + 1], ix[d])
    return vals, idxs.astype(jnp.int32)


def _select_rows(table, sel, k):
    out = jnp.zeros_like(table)
    for r in range(k):
        out = jnp.where(sel == r, table[r:r + 1, :], out)
    return out


def _split_bf16(x):
    hi = x.astype(BF16)
    return hi, (x - hi.astype(F32)).astype(BF16)


def _peer_joint_topk(s1, s2):
    k = PEER_TOPK
    t = s1.shape[1]
    sub = lax.broadcasted_iota(jnp.int32, (8, t), 0)
    depth = jnp.full((8, t), k // 8, jnp.int32)
    for i in range(6, -1, -1):
        depth = jnp.where(sub == i, k // (i + 1), depth)
    subf = sub.astype(F32)
    lo = [jnp.where(depth > d, s1[0:8] + s2[d:d + 1], -jnp.inf) for d in range(k)]
    hi0 = s1[8:16] + s2[0:1]
    pos_hi = (subf + 8.0) * float(k)
    cnt = jnp.zeros((8, t), F32)
    out_row = lax.broadcasted_iota(jnp.int32, (k, t), 0)
    vals = jnp.zeros((k, t), F32)
    poss = jnp.zeros((k, t), F32)
    big = float(k * k)
    for rnd in range(k):
        pos_lo = subf * float(k) + cnt
        m = jnp.max(jnp.maximum(lo[0], hi0), axis=0, keepdims=True)
        cand = jnp.minimum(jnp.where(lo[0] == m, pos_lo, big), jnp.where(hi0 == m, pos_hi, big))
        pos = jnp.min(cand, axis=0, keepdims=True)
        vals = jnp.where(out_row == rnd, m, vals)
        poss = jnp.where(out_row == rnd, pos, poss)
        won_lo = pos_lo == pos
        for d in range(k - 1 - rnd):
            lo[d] = jnp.where(won_lo, lo[d + 1], lo[d])
        cnt = cnt + jnp.where(won_lo, 1.0, 0.0)
        hi0 = jnp.where(pos_hi == pos, -jnp.inf, hi0)
    p = poss.astype(jnp.int32)
    return vals, p >> 4, p & (k - 1)


def _peer_route_head(q, kh, kl):
    k = PEER_TOPK
    assert k == 16
    qh, ql = _split_bf16(q)
    s = _dot_nt(kh, qh) + (_dot_nt(kh, ql) + _dot_nt(kl, qh))
    s1, i1 = _topk_rows_slabs(s[:PEER_N_KEYS], k)
    s2, i2 = _topk_rows_slabs(s[PEER_N_KEYS:], k)
    ts, ri, rj = _peer_joint_topk(s1, s2)
    e = jnp.exp(ts - ts[0:1, :])
    return _select_rows(i1, ri, k), _select_rows(i2, rj, k), e / jnp.sum(e, axis=0, keepdims=True)


def _peer_query(h, wqh, wql):
    hh, hl = _split_bf16(h)
    return _dot(hh, wqh) + (_dot(hl, wqh) + _dot(hh, wql))


def _peer_route_kernel(h_ref, wqh_ref, wql_ref, kh_ref, kl_ref, a_o, b_o, g_o, q_s, a_s, b_s, g_s):
    k = PEER_TOPK
    q_s[...] = _peer_query(h_ref[...], wqh_ref[...], wql_ref[...])

    def head(h):
        lo = pl.multiple_of(h * PEER_DK, PEER_DK)
        ro = pl.multiple_of(h * k, k)
        a_s[pl.ds(ro, k), :], b_s[pl.ds(ro, k), :], g_s[pl.ds(ro, k), :] = _peer_route_head(
            q_s[:, pl.ds(lo, PEER_DK)], kh_ref[h], kl_ref[h])

    def head_group(i, _):
        for j in range(PEER_HEADS_PER_TRIP):
            head(PEER_HEADS_PER_TRIP * i + j)
        return 0

    lax.fori_loop(0, PEER_HEADS // PEER_HEADS_PER_TRIP, head_group, 0)
    a_o[...] = a_s[...].T
    b_o[...] = b_s[...].T
    g_o[...] = g_s[...].T


def _peer_route(h2, wq_hl, keys_hl):
    n, d = h2.shape
    t = 256
    wq_hi, wq_lo = wq_hl
    k_hi, k_lo = keys_hl
    return pl.pallas_call(
        _peer_route_kernel,
        grid=(n // t,),
        in_specs=[pl.BlockSpec((t, d), lambda i: (i, 0)),
                  pl.BlockSpec(wq_hi.shape, lambda i: (0, 0)),
                  pl.BlockSpec(wq_lo.shape, lambda i: (0, 0)),
                  pl.BlockSpec(k_hi.shape, lambda i: (0, 0, 0)),
                  pl.BlockSpec(k_lo.shape, lambda i: (0, 0, 0))],
        out_specs=[pl.BlockSpec((t, PEER_SLOTS), lambda i: (i, 0))] * 3,
        out_shape=[jax.ShapeDtypeStruct((n, PEER_SLOTS), jnp.int32),
                   jax.ShapeDtypeStruct((n, PEER_SLOTS), jnp.int32),
                   jax.ShapeDtypeStruct((n, PEER_SLOTS), F32)],
        scratch_shapes=[pltpu.VMEM((t, PEER_HEADS * PEER_DK), F32),
                        pltpu.VMEM((PEER_SLOTS, t), jnp.int32),
                        pltpu.VMEM((PEER_SLOTS, t), jnp.int32),
                        pltpu.VMEM((PEER_SLOTS, t), F32)],
        compiler_params=_cparams(("parallel",), 48 * 1024 * 1024),
        name="peer_route",
    )(h2, wq_hi, wq_lo, k_hi, k_lo)


_HI16 = -65536


def _bf16_bits(w):
    return lax.bitcast_convert_type(w, jnp.int32) & _HI16


def _peer_ffn_kernel(*refs, ec, unroll, final_norm, route_next):
    if route_next:
        (h_ref, a_ref, b_ref, g_ref, x_ref, gate_ref, u_ref, v_ref, fw_ref, qn_ref, kh_ref, kl_ref,
         o_ref, hb_s, w_s, a_c, b_c, g_c, a_n, b_n, g_n) = refs
    else:
        h_ref, a_ref, b_ref, g_ref, x_ref, gate_ref, u_ref, v_ref, fw_ref, o_ref, hb_s, w_s = refs
        a_c, b_c, g_c = a_ref, b_ref, g_ref
    e = pl.program_id(1)
    t = h_ref.shape[0]
    half = t // 2
    nk = PEER_N_KEYS

    @pl.when(e == 0)
    def _():
        if route_next:
            @pl.when(pl.program_id(0) == 0)
            def _():
                a_c[...] = a_ref[...]
                b_c[...] = b_ref[...]
                g_c[...] = g_ref[...]

        hb_s[...] = h_ref[...].astype(BF16)
        o_ref[...] = jnp.zeros_like(o_ref)
        jio = lax.broadcasted_iota(jnp.int32, (nk, PEER_SLOTS), 0)

        def tile(tt):
            arow = jnp.broadcast_to(a_c[pl.ds(tt, 1), :], (nk, PEER_SLOTS))
            brow = jnp.broadcast_to(b_c[pl.ds(tt, 1), :], (nk, PEER_SLOTS))
            grow = jnp.broadcast_to(g_c[pl.ds(tt, 1), :], (nk, PEER_SLOTS))
            cm = jnp.where(jio == arow, grow, 0.0).astype(BF16)
            bm_t = jnp.where(jio == brow, 1.0, 0.0).T.astype(BF16)
            return _dot(cm, bm_t)

        def build(tb, _):
            for u in range(unroll):
                tt = tb * unroll + u
                word = _bf16_bits(tile(tt + half)) | lax.shift_right_logical(_bf16_bits(tile(tt)), 16)
                w_s[pl.ds(pl.multiple_of(tt * PEER_W_PITCH, 8), nk), :] = word
            return 0

        lax.fori_loop(0, half // unroll, build, 0)

    if route_next:
        nsub = t // PEER_ROUTE_TOKENS
        hd = e // nsub
        co = pl.multiple_of((e % nsub) * PEER_ROUTE_TOKENS, PEER_ROUTE_TOKENS)
        ro = pl.multiple_of(hd * PEER_TOPK, PEER_TOPK)
        routed = _peer_route_head(
            qn_ref[pl.ds(co, PEER_ROUTE_TOKENS), pl.ds(pl.multiple_of(hd * PEER_DK, PEER_DK), PEER_DK)],
            kh_ref[hd], kl_ref[hd])

    hid = _dot(hb_s[...], u_ref[0])
    j0 = e * (ec // nk)
    words = jnp.concatenate([w_s[pl.ds(j0 + j, half, stride=PEER_W_PITCH), :] for j in range(ec // nk)], axis=1)
    w_lo = lax.bitcast_convert_type(lax.shift_left(words, 16), F32)
    w_hi = lax.bitcast_convert_type(words & _HI16, F32)
    wc = jnp.concatenate([w_lo, w_hi], axis=0)
    act = 0.5 * hid * (1.0 + lax.erf(hid * SQRT_HALF))
    o_ref[...] += _dot((wc * act).astype(BF16), v_ref[...])

    if route_next:
        a_n[pl.ds(ro, PEER_TOPK), pl.ds(co, PEER_ROUTE_TOKENS)] = routed[0]
        b_n[pl.ds(ro, PEER_TOPK), pl.ds(co, PEER_ROUTE_TOKENS)] = routed[1]
        g_n[pl.ds(ro, PEER_TOPK), pl.ds(co, PEER_ROUTE_TOKENS)] = routed[2]

    @pl.when(e == pl.num_programs(1) - 1)
    def _():
        y = x_ref[...] + gate_ref[0] * o_ref[...]
        o_ref[...] = _rms(y) * fw_ref[...] if final_norm else y
        if route_next:
            a_c[...] = a_n[...].T
            b_c[...] = b_n[...].T
            g_c[...] = g_n[...].T


def _peer_ffn(h2, a, b_idx, g, x, gate, u_blk, v, l, final_w, final_norm, route_next=None):
    n, d = h2.shape
    ne = v.shape[0]
    neb, _, ec = u_blk.shape
    t = min(512, l)
    nb = n // t
    unroll = 16
    assert l % t == 0 and ne == PEER_N_KEYS * PEER_N_KEYS and neb * ec == ne and (t // 2) % unroll == 0
    tok = lambda w: pl.BlockSpec((t, w), lambda i, e: (i, 0))
    in_specs = [tok(d), tok(PEER_SLOTS), tok(PEER_SLOTS), tok(PEER_SLOTS), tok(d),
                pl.BlockSpec((1, 1, d), lambda i, e: ((i * t) // l, 0, 0)),
                pl.BlockSpec((1, d, ec), lambda i, e: (e, 0, 0)),
                pl.BlockSpec((ec, d), lambda i, e: (e, 0)),
                pl.BlockSpec((1, d), lambda i, e: (0, 0))]
    args = [h2, a, b_idx, g, x, gate, u_blk, v, final_w.reshape(1, d)]
    scratch = [pltpu.VMEM((t, d), BF16), pltpu.VMEM((t // 2 * PEER_W_PITCH, PEER_N_KEYS), jnp.int32)]
    if route_next is not None:
        q_all, k_hi, k_lo = route_next
        assert neb == PEER_HEADS * (t // PEER_ROUTE_TOKENS) and a.shape[0] == t
        first = pl.BlockSpec((t, PEER_SLOTS), lambda i, e: (0, 0))
        in_specs[1:4] = [first, first, first]
        in_specs += [pl.BlockSpec((t, q_all.shape[1]), lambda i, e: (jnp.minimum(i + 1, nb - 1), 0)),
                     pl.BlockSpec(k_hi.shape, lambda i, e: (0, 0, 0)),
                     pl.BlockSpec(k_lo.shape, lambda i, e: (0, 0, 0))]
        args += [q_all, k_hi, k_lo]
        scratch += [pltpu.VMEM((t, PEER_SLOTS), jnp.int32), pltpu.VMEM((t, PEER_SLOTS), jnp.int32),
                    pltpu.VMEM((t, PEER_SLOTS), F32),
                    pltpu.VMEM((PEER_SLOTS, t), jnp.int32), pltpu.VMEM((PEER_SLOTS, t), jnp.int32),
                    pltpu.VMEM((PEER_SLOTS, t), F32)]
    return pl.pallas_call(
        functools.partial(_peer_ffn_kernel, ec=ec, unroll=unroll, final_norm=final_norm,
                          route_next=route_next is not None),
        grid=(nb, neb),
        in_specs=in_specs,
        out_specs=tok(d),
        out_shape=jax.ShapeDtypeStruct((n, d), F32),
        scratch_shapes=scratch,
        compiler_params=_cparams(("arbitrary", "arbitrary"), VMEM_LIMIT_V7X),
        name="peer_ffn_routed" if route_next is not None else "peer_ffn",
    )(*args)


def _peer_query_kernel(h_ref, wqh_ref, wql_ref, q_o):
    q_o[...] = _peer_query(h_ref[...], wqh_ref[...], wql_ref[...])


def _peer_queries(h2, wq_hl):
    n, d = h2.shape
    wq_hi, wq_lo = wq_hl
    t = min(512, n)
    return pl.pallas_call(
        _peer_query_kernel,
        grid=(n // t,),
        in_specs=[pl.BlockSpec((t, d), lambda i: (i, 0)),
                  pl.BlockSpec(wq_hi.shape, lambda i: (0, 0)),
                  pl.BlockSpec(wq_lo.shape, lambda i: (0, 0))],
        out_specs=pl.BlockSpec((t, wq_hi.shape[1]), lambda i: (i, 0)),
        out_shape=jax.ShapeDtypeStruct((n, wq_hi.shape[1]), F32),
        compiler_params=_cparams(("parallel",), 40 * 1024 * 1024),
        name="peer_queries",
    )(h2, wq_hi, wq_lo)


def _layout_w_in(w):
    parts = jnp.split(w, IN_OFFSETS, axis=1)
    z = lambda n: jnp.zeros((w.shape[0], n), w.dtype)
    kpe_blk = jnp.concatenate([z(MLA_NOPE), parts[2], z(MLA_HEAD_PAD - MLA_NOPE - MLA_ROPE)], axis=1)
    return jnp.concatenate([parts[0], parts[1], kpe_blk] + list(parts[3:]), axis=1).astype(BF16)


def _layout_mla(w_uq, w_ukv):
    qh = w_uq.reshape(MLA_Q_LORA, N_HEADS, MLA_NOPE + MLA_ROPE)
    qh = jnp.pad(qh, ((0, 0), (0, 0), (0, MLA_HEAD_PAD - MLA_NOPE - MLA_ROPE)))
    kv = w_ukv.reshape(MLA_KV_LORA, N_HEADS, MLA_NOPE + MLA_V)
    kh = jnp.pad(kv[:, :, :MLA_NOPE], ((0, 0), (0, 0), (0, MLA_HEAD_PAD - MLA_NOPE)))
    vh = kv[:, :, MLA_NOPE:]
    return (qh.reshape(MLA_Q_LORA, -1).astype(BF16), kh.reshape(MLA_KV_LORA, -1).astype(BF16),
            vh.reshape(MLA_KV_LORA, -1).T.astype(BF16))


def _static_mats():
    gseg = np.kron(np.eye(N_HEADS), np.ones((HEAD_W, HEAD_W))).astype(np.float32)
    pm = np.zeros((MLA_HEAD_PAD, MLA_HEAD_PAD), np.float32)
    for dd in range(MLA_ROPE):
        blk, j = dd // 16, dd % 16
        pm[MLA_NOPE + blk * 16 + (j + 8) % 16, MLA_NOPE + dd] = 1.0
    pg = np.zeros((BRANCH_W, BRANCH_W), np.float32)
    for i in range(BRANCH_W):
        off, dd = (i // HEAD_W) * HEAD_W, i % HEAD_W
        blk, j = dd // 32, dd % 32
        pg[off + blk * 32 + (j + 16) % 32, i] = 1.0
    ex = np.zeros((GQA_KV_HEADS * HEAD_W, BRANCH_W), np.float32)
    for i in range(BRANCH_W):
        ex[((i // HEAD_W) // (N_HEADS // GQA_KV_HEADS)) * HEAD_W + i % HEAD_W, i] = 1.0
    return tuple(jnp.asarray(m, dtype=BF16) for m in (gseg, pm, pg, ex, ex.T, np.eye(BRANCH_W)))


def _rope_half_tables(pos, hf):
    freqs = ROPE_THETA ** (-jnp.arange(hf, dtype=F32) / hf)
    ang = pos[:, None] * freqs[None, :]
    c, s = jnp.cos(ang), jnp.sin(ang)
    return jnp.concatenate([c, c], axis=1), jnp.concatenate([-s, s], axis=1)


def _axial_tables(row, col, dims):
    cr, sr = _rope_half_tables(row, dims // 4)
    cc, sc = _rope_half_tables(col, dims // 4)
    return jnp.concatenate([cr, cc], axis=1), jnp.concatenate([sr, sc], axis=1)


def _rope_tables(s):
    t = jnp.arange(s)
    row, col = (t // GRID_W).astype(F32), (t % GRID_W).astype(F32)
    c32, s32 = _axial_tables(row, col, MLA_ROPE)
    pad = MLA_HEAD_PAD - MLA_NOPE - MLA_ROPE
    cm = jnp.concatenate([jnp.ones((s, MLA_NOPE), F32), c32, jnp.ones((s, pad), F32)], axis=1)
    sm = jnp.concatenate([jnp.zeros((s, MLA_NOPE), F32), s32, jnp.zeros((s, pad), F32)], axis=1)
    c64, s64 = _axial_tables(row, col, HEAD_W)
    return cm, sm, jnp.tile(c64, (1, N_HEADS)), jnp.tile(s64, (1, N_HEADS))


def _split_f32(w):
    hi = w.astype(BF16)
    return hi, (w - hi.astype(F32)).astype(BF16)


def _layout_peer_keys(keys):
    h, _, nk, dh = keys.shape
    z = jnp.zeros((h, nk, dh), keys.dtype)
    top = jnp.concatenate([keys[:, 0], z], axis=2)
    bot = jnp.concatenate([z, keys[:, 1]], axis=2)
    return jnp.concatenate([top, bot], axis=1)


def kernel(x, c, ctx, c_ctx, mod_w, mod_b, norm1_w, norm2_w, w_in, mla_q_norm, mla_w_uq, mla_kv_norm, mla_w_ukv, gqa_q_norm, gqa_k_norm, na_bias, ret_decay_logit, ret_gn_w, w_branch, w_out, peer_w_q, peer_keys, peer_u, peer_v, final_norm_w):
    b, s, d = x.shape
    lc = ctx.shape[1]
    depth = mod_w.shape[0]
    assert d == D_MODEL and s % (GRID_W * NA_Q_ROWS) == 0 and s % 256 == 0 and lc % 256 == 0

    rows = -(-(b + 1) // 8) * 8
    cc = jnp.zeros((rows, d), F32).at[:b].set(c).at[b].set(c_ctx)
    mod = _modulation(cc, mod_w, mod_b)

    gseg, pm, pg, ex, ex_t, eye = _static_mats()
    tables = _rope_tables(s)

    for l in range(depth):
        need_ctx = l < depth - 1
        mx = mod[l, :b].reshape(b, 1, 6, d)
        mc = jnp.broadcast_to(mod[l, b].reshape(1, 1, 6, d), (b, 1, 6, d))
        sh1x, sc1x, g1x, sh2x, sc2x, g2x = (mx[:, :, i] for i in range(6))
        sh1c, sc1c, g1c, sh2c, sc2c, g2c = (mc[:, :, i] for i in range(6))

        w_in_l = _layout_w_in(w_in[l])
        wuq, wk, wv = _layout_mla(mla_w_uq[l], mla_w_ukv[l])
        consts = (mla_q_norm[l].reshape(1, -1), wuq, mla_kv_norm[l].reshape(1, -1), wk, wv,
                  jnp.tile(gqa_q_norm[l], N_HEADS).reshape(1, -1),
                  jnp.tile(gqa_k_norm[l], GQA_KV_HEADS).reshape(1, -1), gseg, pm, pg, ex, ex_t, eye)
        n1w = norm1_w[l].reshape(1, d)
        n2w = norm2_w[l].reshape(1, d)
        wb = w_branch[l].astype(BF16)
        wo = w_out[l].astype(BF16)
        keys_hl = _split_f32(_layout_peer_keys(peer_keys[l]))
        wq_hl = _split_f32(peer_w_q[l])
        ne = peer_u.shape[1]
        u_blk = jnp.swapaxes(peer_u[l].astype(BF16).reshape(ne // PEER_EXPERT_CHUNK, PEER_EXPERT_CHUNK, d), 1, 2)
        v_b = peer_v[l].astype(BF16)
        log_g = jax.nn.log_sigmoid(ret_decay_logit[l].astype(F32))

        projx = _inproj(x, n1w, sc1x, sh1x, w_in_l)
        projc = _inproj(ctx, n1w, sc1c, sh1c, w_in_l)
        qmx, kmx, vmx, qgx, kgx, vgx, nvx = _prep(projx, consts, tables)
        qmc, kmc, vmc, qgc, kgc, vgc, nvc = _prep(projc, consts, None)

        oa, ob = _attention([dict(q=(qmx, 0), segs=[((kmc, 0), vmc), ((kmx, 0), vmx)], dq=MLA_HEAD_PAD),
                             dict(q=(qgx, 0), segs=[((kgc, 0), vgc), ((kgx, 0), vgx)], dq=HEAD_W)], "attn_mla_gqa")
        oc = _na_attention(projx, projc, nvx, nvc, na_bias[l])
        od, od_c = _retention(projx, projc, log_g, ret_gn_w[l], gseg, need_ctx)

        def peer(h2f, xf, gate, l_blk, final):
            t = min(512, l_blk)
            if (t // PEER_ROUTE_TOKENS) * PEER_HEADS == u_blk.shape[0]:
                a0, b0, g0 = _peer_route(h2f[:t], wq_hl, keys_hl)
                return _peer_ffn(h2f, a0, b0, g0, xf, gate, u_blk, v_b, l_blk, final_norm_w, final_norm=final,
                                 route_next=(_peer_queries(h2f, wq_hl),) + keys_hl)
            a0, b0, g0 = _peer_route(h2f, wq_hl, keys_hl)
            return _peer_ffn(h2f, a0, b0, g0, xf, gate, u_blk, v_b, l_blk, final_norm_w, final_norm=final)

        x, h2x = _merge((oa, ob, oc, od), projx, x, g1x, sc2x, sh2x, n2w, wb, wo)
        x = peer(h2x.reshape(b * s, d), x.reshape(b * s, d), g2x, s, l == depth - 1).reshape(b, s, d)

        if need_ctx:
            ca, cb, ccx = _attention(
                [dict(q=(qmc, 0), segs=[((kmc, 0), vmc)], dq=MLA_HEAD_PAD),
                 dict(q=(qgc, 0), segs=[((kgc, 0), vgc)], dq=HEAD_W),
                 dict(q=(projc, COL_NQ), segs=[((projc, COL_NK), nvc)], dq=HEAD_W, qscale=HEAD_W ** -0.5,
                      log2_scores=False)], "attn_ctx")
            ctx, h2c = _merge((ca, cb, ccx, od_c), projc, ctx, g1c, sc2c, sh2c, n2w, wb, wo)
            ctx = peer(h2c.reshape(b * lc, d), ctx.reshape(b * lc, d), g2c[:1], b * lc, False).reshape(b, lc, d)

    return x
```

```python
import functools

import numpy as np
import jax
import jax.numpy as jnp
from jax import lax
from jax.experimental import pallas as pl
from jax.experimental.pallas import tpu as pltpu

F32 = jnp.float32
BF16 = jnp.bfloat16
HIGHEST = lax.Precision.HIGHEST

D_MODEL = 1024
GRID_W = 64
ROPE_THETA = 10000.0
EPS = 1e-6
N_HEADS = 4
HEAD_W = 64
BRANCH_W = N_HEADS * HEAD_W
MLA_NOPE, MLA_ROPE, MLA_V = 64, 32, 64
MLA_Q_LORA, MLA_KV_LORA = 256, 128
MLA_SCALE = (MLA_NOPE + MLA_ROPE) ** -0.5
MLA_HEAD_PAD = 128
GQA_KV_HEADS = 2
NA_WIN_R, NA_WIN_C = 8, 16
NA_Q_ROWS = 4
ATTN_KEY_CHUNK = 1024
RET_CHUNK = 128
N_BRANCH = 4
PEER_HEADS, PEER_N_KEYS, PEER_TOPK, PEER_DK = 8, 128, 16, 128
PEER_SLOTS = PEER_HEADS * PEER_TOPK
PEER_W_PITCH = PEER_N_KEYS + 8
PEER_HEADS_PER_TRIP = 4
PEER_ROUTE_TOKENS = 256
PEER_EXPERT_CHUNK = 1024
SQRT_HALF = 0.7071067811865476
LOG2E = 1.4426950408889634
NEG_BIG = -1e30

IN_SIZES = (256, 128, 32, 256, 128, 128, 256, 256, 256, 256, 256, 256, 256, 256, 4096)
IN_OFFSETS = tuple(int(v) for v in np.cumsum(IN_SIZES)[:-1])
PROJ_COLS = 7168
COL_NQ, COL_NK, COL_NV = 1024, 1280, 1536
COL_RET = 1792
COL_GATES = 3072

VMEM_LIMIT_V7X = 56 * 1024 * 1024


def _cparams(sem, vmem=None):
    return pltpu.CompilerParams(dimension_semantics=sem, vmem_limit_bytes=vmem)


def _dot(a, b):
    return jnp.dot(a, b, preferred_element_type=F32)


def _dot_hi(a, b):
    return jnp.dot(a, b, preferred_element_type=F32, precision=HIGHEST)


def _dot_nt(a, b):
    return lax.dot_general(a, b, (((1,), (1,)), ((), ())), preferred_element_type=F32)


def _dot_sel(x, sel):
    hi = x.astype(BF16)
    r1 = x - hi.astype(F32)
    mid = r1.astype(BF16)
    lo = (r1 - mid.astype(F32)).astype(BF16)
    return _dot(hi, sel) + (_dot(mid, sel) + _dot(lo, sel))


def _rms(x):
    return x * lax.rsqrt(jnp.mean(x * x, axis=-1, keepdims=True) + EPS)


def _silu(x):
    return x * jax.nn.sigmoid(x)


def _head_mask(shape, h, width=HEAD_W):
    lane = lax.broadcasted_iota(jnp.int32, shape, len(shape) - 1)
    lo = h * width
    return (lane >= lo) & (lane < lo + width)


def _mod_kernel(c_ref, w_ref, b_ref, o_ref):
    o_ref[0] = _dot_hi(_silu(c_ref[...]), w_ref[0]) + b_ref[0]


def _modulation(cc, mod_w, mod_b):
    depth, d, n = mod_w.shape
    rows = cc.shape[0]
    tn = 1536
    return pl.pallas_call(
        _mod_kernel,
        grid=(depth, n // tn),
        in_specs=[pl.BlockSpec((rows, d), lambda l, j: (0, 0)),
                  pl.BlockSpec((1, d, tn), lambda l, j: (l, 0, j)),
                  pl.BlockSpec((1, 1, tn), lambda l, j: (l, 0, j))],
        out_specs=pl.BlockSpec((1, rows, tn), lambda l, j: (l, 0, j)),
        out_shape=jax.ShapeDtypeStruct((depth, rows, n), F32),
        compiler_params=_cparams(("parallel", "parallel"), 40 * 1024 * 1024),
        name="modulation",
    )(cc, mod_w, mod_b.reshape(depth, 1, n))


def _inproj_kernel(x_ref, nw_ref, sc_ref, sh_ref, w_ref, o_ref, h_scr):
    @pl.when(pl.program_id(2) == 0)
    def _():
        h = _rms(x_ref[0]) * nw_ref[...] * (1.0 + sc_ref[0]) + sh_ref[0]
        h_scr[...] = h.astype(BF16)

    o_ref[0] = _dot(h_scr[...], w_ref[...]).astype(o_ref.dtype)


def _inproj(x, nw, sc, sh, w):
    b, l, d = x.shape
    n = w.shape[1]
    tm = min(1024, l)
    tn = 1792
    return pl.pallas_call(
        _inproj_kernel,
        grid=(b, l // tm, n // tn),
        in_specs=[pl.BlockSpec((1, tm, d), lambda bi, i, j: (bi, i, 0)),
                  pl.BlockSpec((1, d), lambda bi, i, j: (0, 0)),
                  pl.BlockSpec((1, 1, d), lambda bi, i, j: (bi, 0, 0)),
                  pl.BlockSpec((1, 1, d), lambda bi, i, j: (bi, 0, 0)),
                  pl.BlockSpec((d, tn), lambda bi, i, j: (0, j))],
        out_specs=pl.BlockSpec((1, tm, tn), lambda bi, i, j: (bi, i, j)),
        out_shape=jax.ShapeDtypeStruct((b, l, n), BF16),
        scratch_shapes=[pltpu.VMEM((tm, d), BF16)],
        compiler_params=_cparams(("parallel", "parallel", "arbitrary"), 40 * 1024 * 1024),
        name="inproj",
    )(x, nw, sc, sh, w)


def _prep_kernel(*refs, use_rope):
    (p_ref, nv_ref, qn_ref, wuq_ref, kvn_ref, wk_ref, wv_ref, gqn_ref, gkn_ref, gseg_ref, pm_ref, pg_ref,
     e_ref, et_ref, eye_ref) = refs[:15]
    if use_rope:
        cm_ref, sm_ref, cg_ref, sg_ref = refs[15:19]
        outs = refs[19:]
    else:
        outs = refs[15:]
    qm_o, km_o, vm_o, qg_o, kg_o, vg_o, nvt_o = outs
    nvt_o[0] = _dot_nt(eye_ref[...], nv_ref[0]).astype(BF16)

    pb = p_ref[0]
    cq = pb[:, 0:256].astype(F32)
    ckv = pb[:, 256:384].astype(F32)
    kpe = pb[:, 384:512].astype(F32)
    gq = pb[:, 512:768].astype(F32)
    gk = pb[:, 768:896].astype(F32)
    gv = pb[:, 896:1024]

    cqn = (_rms(cq) * qn_ref[...]).astype(BF16)
    qa = _dot(cqn, wuq_ref[...])
    ckn = (_rms(ckv) * kvn_ref[...]).astype(BF16)
    kn = _dot(ckn, wk_ref[...])
    vm_t = _dot_nt(wv_ref[...], ckn)
    if use_rope:
        cm, sm = cm_ref[...], sm_ref[...]
        pm = pm_ref[...]

        def rope_m(t):
            return t * cm + _dot_sel(t, pm) * sm

        qa = jnp.concatenate([rope_m(qa[:, h * 128:(h + 1) * 128]) for h in range(N_HEADS)], axis=1)
        kpe = rope_m(kpe)
    km = kn + jnp.concatenate([kpe] * N_HEADS, axis=1)
    qm_o[0] = (qa * (MLA_SCALE * LOG2E)).astype(BF16)
    km_o[0] = km.astype(BF16)
    vm_o[0] = vm_t.astype(BF16)

    gseg = gseg_ref[...]
    gqn = gq * lax.rsqrt(_dot_sel(gq * gq, gseg) * (1.0 / HEAD_W) + EPS) * gqn_ref[...]
    gkn = gk * lax.rsqrt(_dot_sel(gk * gk, gseg[:128, :128]) * (1.0 / HEAD_W) + EPS) * gkn_ref[...]
    if use_rope:
        cg, sg = cg_ref[...], sg_ref[...]
        pg = pg_ref[...]
        gqn = gqn * cg + _dot_sel(gqn, pg) * sg
        gkn = gkn * cg[:, :128] + _dot_sel(gkn, pg[:128, :128]) * sg[:, :128]
    qg_o[0] = (gqn * (HEAD_W ** -0.5 * LOG2E)).astype(BF16)
    e = e_ref[...]
    kg_o[0] = _dot(gkn.astype(BF16), e).astype(BF16)
    vg_o[0] = _dot_nt(et_ref[...], gv).astype(BF16)


def _prep(proj, consts, tables):
    b, l, _ = proj.shape
    tm = min(512, l)
    use_rope = tables is not None
    full = lambda a: pl.BlockSpec(a.shape, lambda bi, i: (0,) * a.ndim)
    in_specs = [pl.BlockSpec((1, tm, 1024), lambda bi, i: (bi, i, 0)),
                pl.BlockSpec((1, tm, BRANCH_W), lambda bi, i: (bi, i, COL_NV // BRANCH_W))] + [full(a) for a in consts]
    args = [proj, proj] + list(consts)
    if use_rope:
        in_specs += [pl.BlockSpec((tm, t.shape[1]), lambda bi, i: (i, 0)) for t in tables]
        args += list(tables)
    tok = lambda w: (pl.BlockSpec((1, tm, w), lambda bi, i: (bi, i, 0)), jax.ShapeDtypeStruct((b, l, w), BF16))
    tr = (pl.BlockSpec((1, BRANCH_W, tm), lambda bi, i: (bi, 0, i)), jax.ShapeDtypeStruct((b, BRANCH_W, l), BF16))
    outs = (tok(512), tok(512), tr, tok(256), tok(256), tr, tr)
    return pl.pallas_call(
        functools.partial(_prep_kernel, use_rope=use_rope),
        grid=(b, l // tm),
        in_specs=in_specs,
        out_specs=[o[0] for o in outs],
        out_shape=[o[1] for o in outs],
        compiler_params=_cparams(("parallel", "parallel"), 40 * 1024 * 1024),
        name="prep_rope" if use_rope else "prep",
    )(*args)


def _attn_kernel(*refs, nseg, probs, tk):
    per = 1 + 2 * nseg
    nprob = len(probs)
    gw = 256
    ones_rows = 16
    state = []
    for pi, (dq, qscale, log2_scores) in enumerate(probs):
        q_ref = refs[pi * per]
        segs = [(refs[pi * per + 1 + 2 * i], refs[pi * per + 2 + 2 * i]) for i in range(nseg)]
        tq = q_ref.shape[1]
        hpg = gw // dq
        qstacks = []
        for g in range(N_HEADS // hpg):
            qg = q_ref[0, :, g * gw:(g + 1) * gw]
            if qscale is not None:
                qg = qg * jnp.asarray(qscale, BF16)
            qstacks.append(jnp.concatenate([jnp.where(_head_mask(qg.shape, j, dq), qg, jnp.zeros_like(qg))
                                            for j in range(hpg)], axis=0))
        state.append((segs, tq, hpg, qstacks, jnp.exp2 if log2_scores else jnp.exp))
    carry = tuple(tuple((jnp.full((1, st[1]), -jnp.inf, F32), jnp.zeros((HEAD_W + ones_rows, st[1]), F32))
                        for _ in range(N_HEADS)) for st in state)

    def chunk_step(pi, carry, si, off, tkk):
        segs, tq, hpg, qstacks, exp = state[pi]
        k_ref, vt_ref = segs[si]
        st_g = [_dot_nt(k_ref[0, pl.ds(off, tkk), g * gw:(g + 1) * gw], qstacks[g])
                for g in range(N_HEADS // hpg)]
        ones = jnp.ones((ones_rows, tkk), BF16)
        new = []
        for h in range(N_HEADS):
            m, acc = carry[h]
            vt = jnp.concatenate([vt_ref[0, h * HEAD_W:(h + 1) * HEAD_W, pl.ds(off, tkk)], ones], axis=0)
            st = st_g[h // hpg][:, (h % hpg) * tq:(h % hpg + 1) * tq]
            mn = jnp.maximum(m, jnp.max(st, axis=0, keepdims=True))
            acc = exp(m - mn) * acc + _dot(vt, exp(st - mn).astype(BF16))
            new.append((mn, acc))
        return tuple(new)

    for si in range(nseg):
        lk = state[0][0][si][0].shape[1]
        tkk = min(tk, lk)
        n = lk // tkk

        def body(c, carry, si=si, tkk=tkk):
            off = c * tkk if isinstance(c, int) else pl.multiple_of(c * tkk, tkk)
            return tuple(chunk_step(pi, carry[pi], si, off, tkk) for pi in range(nprob))

        carry = body(0, carry) if n == 1 else lax.fori_loop(0, n, body, carry)
    for pi in range(nprob):
        o_ref = refs[nprob * per + pi]
        out_t = jnp.concatenate([acc[:HEAD_W] * (1.0 / acc[HEAD_W:HEAD_W + 1]) for _, acc in carry[pi]], axis=0)
        o_ref[0] = out_t.T.astype(o_ref.dtype)


def _attention(problems, name):
    b, lq, _ = problems[0]["q"][0].shape
    tq = min(512, lq)
    in_specs, args, probs = [], [], []
    for p in problems:
        (qa, qcol), dq = p["q"], p["dq"]
        wq = N_HEADS * dq
        assert qcol % wq == 0 and qa.shape[:2] == (b, lq)
        in_specs.append(pl.BlockSpec((1, tq, wq), lambda bi, i, qcol=qcol, wq=wq: (bi, i, qcol // wq)))
        args.append(qa)
        for (ka, kcol), vt in p["segs"]:
            assert kcol % wq == 0 and vt.shape[1] == BRANCH_W and vt.shape[2] == ka.shape[1]
            in_specs.append(pl.BlockSpec((1, ka.shape[1], wq), lambda bi, i, kcol=kcol, wq=wq: (bi, 0, kcol // wq)))
            in_specs.append(pl.BlockSpec((1, BRANCH_W, vt.shape[2]), lambda bi, i: (bi, 0, 0)))
            args += [ka, vt]
        probs.append((dq, p.get("qscale"), p.get("log2_scores", True)))
    nseg = len(problems[0]["segs"])
    assert all(len(p["segs"]) == nseg for p in problems)
    outs = pl.pallas_call(
        functools.partial(_attn_kernel, nseg=nseg, probs=tuple(probs), tk=ATTN_KEY_CHUNK),
        grid=(b, lq // tq),
        in_specs=in_specs,
        out_specs=[pl.BlockSpec((1, tq, BRANCH_W), lambda bi, i: (bi, i, 0))] * len(problems),
        out_shape=[jax.ShapeDtypeStruct((b, lq, BRANCH_W), BF16)] * len(problems),
        compiler_params=_cparams(("parallel", "arbitrary"), VMEM_LIMIT_V7X),
        name=name,
    )(*args)
    return outs


def _proj_cols(arr, col, width, rows):
    assert col % width == 0
    return pl.BlockSpec((1, rows, width), lambda *idx: (idx[0], 0, col // width))


def _na_kernel(pat_ref, ks_ref, q_ref, k_ref, vt_ref, kc_ref, vct_ref, m_ref, o_ref, *, kw):
    del pat_ref
    g = pl.program_id(1)
    off = pl.multiple_of(ks_ref[g] * GRID_W, 128)
    q = q_ref[0] * jnp.asarray(HEAD_W ** -0.5, BF16)
    qb = q.shape[0]
    qstack = jnp.concatenate([jnp.where(_head_mask(q.shape, h), q, jnp.zeros_like(q))
                              for h in range(N_HEADS)], axis=0)
    st_w = _dot_nt(k_ref[0, pl.ds(off, kw), :], qstack)
    st_c = _dot_nt(kc_ref[0], qstack)
    ones_w = jnp.ones((16, kw), BF16)
    ones_c = jnp.ones((16, kc_ref.shape[1]), BF16)
    outs = []
    for h in range(N_HEADS):
        sw = st_w[:, h * qb:(h + 1) * qb] + m_ref[0, h]
        sc = st_c[:, h * qb:(h + 1) * qb]
        mx = jnp.maximum(jnp.max(sw, axis=0, keepdims=True), jnp.max(sc, axis=0, keepdims=True))
        vtw = jnp.concatenate([vt_ref[0, h * HEAD_W:(h + 1) * HEAD_W, pl.ds(off, kw)], ones_w], axis=0)
        vtc = jnp.concatenate([vct_ref[0, h * HEAD_W:(h + 1) * HEAD_W, :], ones_c], axis=0)
        acc = _dot(vtw, jnp.exp(sw - mx).astype(BF16)) + _dot(vtc, jnp.exp(sc - mx).astype(BF16))
        outs.append(acc[:HEAD_W] * (1.0 / acc[HEAD_W:HEAD_W + 1]))
    o_ref[0] = jnp.concatenate(outs, axis=0).T.astype(o_ref.dtype)


def _na_plan(s):
    rows = s // GRID_W
    wr = min(NA_WIN_R, rows)
    wc = NA_WIN_C
    qr = min(NA_Q_ROWS, rows)
    kwr = min(qr + wr - 1 + (1 if qr + wr - 1 < rows else 0), rows)
    ngrp = rows // qr
    qc = np.arange(GRID_W)[:, None]
    kc = np.arange(GRID_W)[None, :]
    cs = np.clip(qc - wc // 2, 0, GRID_W - wc)
    valid_c = (kc >= cs) & (kc < cs + wc)
    rel_c = np.where(valid_c, kc - qc + (NA_WIN_C - 1), 0)
    assert (valid_c.sum(1) == wc).all()
    pats, pat_ids, ks_rows = [], [], []
    for g in range(ngrp):
        r0 = g * qr
        ks = int(np.clip(r0 - wr // 2, 0, rows - kwr))
        assert (ks * GRID_W) % 128 == 0
        r = (r0 + np.arange(qr))[:, None]
        kr = (ks + np.arange(kwr))[None, :]
        rs = np.clip(r - wr // 2, 0, rows - wr)
        valid_r = (kr >= rs) & (kr < rs + wr)
        assert (valid_r.sum(1) == wr).all()
        rel_r = np.where(valid_r, kr - r + (NA_WIN_R - 1), 0)
        key = (valid_r.tobytes(), rel_r.tobytes())
        for pi, (pk, *_rest) in enumerate(pats):
            if pk == key:
                pat_ids.append(pi)
                break
        else:
            pat_ids.append(len(pats))
            pats.append((key, valid_r, rel_r))
        ks_rows.append(ks)
    valid_r = np.stack([p[1] for p in pats])
    rel_r = np.stack([p[2] for p in pats])
    return (qr, kwr, np.asarray(pat_ids, np.int32), np.asarray(ks_rows, np.int32), valid_r, rel_r, valid_c, rel_c)


def _na_bias_masks(na_bias, valid_r, rel_r, valid_c, rel_c):
    h = na_bias.shape[0]
    npat, qr, kwr = valid_r.shape
    ncol = 2 * NA_WIN_C - 1
    brow = na_bias[:, rel_r, :].astype(F32)
    onehot_c = ((rel_c[None] == np.arange(ncol)[:, None, None]) & valid_c[None]).astype(np.float32)
    m = jnp.einsum('hpqkc,cxy->phkyqx', brow, jnp.asarray(onehot_c), precision=HIGHEST)
    valid = (valid_r.transpose(0, 2, 1)[:, None, :, None, :, None]
             & valid_c.T[None, None, None, :, None, :])
    m = jnp.where(valid, m, NEG_BIG)
    return m.reshape(npat, h, kwr * GRID_W, qr * GRID_W)


def _na_attention(projx, projc, nvx_t, nvc_t, na_bias):
    b, s, _ = projx.shape
    lc = projc.shape[1]
    qr, kwr, pat_ids, ks_rows, valid_r, rel_r, valid_c, rel_c = _na_plan(s)
    qb, kw = qr * GRID_W, kwr * GRID_W
    assert kw % 128 == 0
    mb = _na_bias_masks(na_bias, valid_r, rel_r, valid_c, rel_c)
    grid_spec = pltpu.PrefetchScalarGridSpec(
        num_scalar_prefetch=2,
        grid=(b, s // qb),
        in_specs=[pl.BlockSpec((1, qb, 256), lambda bi, g, pat, ks: (bi, g, COL_NQ // 256)),
                  pl.BlockSpec((1, s, 256), lambda bi, g, pat, ks: (bi, 0, COL_NK // 256)),
                  pl.BlockSpec((1, BRANCH_W, s), lambda bi, g, pat, ks: (bi, 0, 0)),
                  pl.BlockSpec((1, lc, 256), lambda bi, g, pat, ks: (bi, 0, COL_NK // 256)),
                  pl.BlockSpec((1, BRANCH_W, lc), lambda bi, g, pat, ks: (bi, 0, 0)),
                  pl.BlockSpec((1, N_HEADS, kw, qb), lambda bi, g, pat, ks: (pat[g], 0, 0, 0))],
        out_specs=pl.BlockSpec((1, qb, BRANCH_W), lambda bi, g, pat, ks: (bi, g, 0)),
    )
    return pl.pallas_call(
        functools.partial(_na_kernel, kw=kw),
        grid_spec=grid_spec,
        out_shape=jax.ShapeDtypeStruct((b, s, BRANCH_W), BF16),
        compiler_params=_cparams(("parallel", "arbitrary"), 48 * 1024 * 1024),
        name="na_attention",
    )(jnp.asarray(pat_ids), jnp.asarray(ks_rows), projx, projx, nvx_t, projc, nvc_t, mb)


def _ret_kernel(lgs_ref, lgl_ref, gnw_ref, gseg_ref,
                qx, kx, vx, gfx, gbx, qc, kc, vc, gfc, gbc,
                yx_o, yc_o, of_s, ob_s, st_s, dec_s, qk_s, *, need_ctx):
    c = RET_CHUNK
    lc = qc.shape[1]
    sx = qx.shape[1]
    n_col = lax.broadcasted_iota(jnp.int32, (c, c), 0).astype(F32)
    m_row = lax.broadcasted_iota(jnp.int32, (c, c), 1).astype(F32)
    diff = n_col - m_row
    for h in range(N_HEADS):
        dec_s[0, :, h * c:(h + 1) * c] = jnp.where(diff >= 0, jnp.exp(lgs_ref[h] * jnp.maximum(diff, 0.0)), 0.0)
        dec_s[1, :, h * c:(h + 1) * c] = jnp.where(diff <= 0,
                                                   jnp.exp(lgs_ref[N_HEADS + h] * jnp.maximum(-diff, 0.0)), 0.0)
    pos = lax.broadcasted_iota(jnp.int32, (c, BRANCH_W), 0).astype(F32)
    lgf, lgb = lgl_ref[0], lgl_ref[1]
    qk_s[0] = jnp.exp(lgf * (pos + 1.0))
    qk_s[1] = jnp.exp(lgf * (c - 1.0 - pos))
    qk_s[2] = jnp.exp(lgb * (c - pos))
    qk_s[3] = jnp.exp(lgb * pos)
    cd_f = jnp.exp(lgf * float(c))
    cd_b = jnp.exp(lgb * float(c))
    st_s[...] = jnp.zeros_like(st_s)
    rowb = lax.broadcasted_iota(jnp.int32, (BRANCH_W, BRANCH_W), 0) // HEAD_W
    colb = lax.broadcasted_iota(jnp.int32, (BRANCH_W, BRANCH_W), 1) // HEAD_W
    bd_mask = rowb == colb

    def chunk_step(q, k, v, state, d, cd):
        kk = k * jnp.asarray(HEAD_W ** -0.5, BF16)
        o = _dot(q, state.astype(BF16)) * qk_s[2 * d]
        kstack = jnp.concatenate([jnp.where(_head_mask(kk.shape, h), kk, jnp.zeros_like(kk))
                                  for h in range(N_HEADS)], axis=0)
        vstack = jnp.concatenate([jnp.where(_head_mask(v.shape, h), v, jnp.zeros_like(v))
                                  for h in range(N_HEADS)], axis=0)
        inner = _dot_nt(q, kstack) * dec_s[d]
        o = o + _dot(inner.astype(BF16), vstack)
        kd = (kk.astype(F32) * qk_s[2 * d + 1]).astype(BF16)
        upd = lax.dot_general(kd, v, (((0,), (0,)), ((), ())), preferred_element_type=F32)
        return o, state * cd + jnp.where(bd_mask, upd, 0.0)

    gseg = gseg_ref[...]
    gnw = gnw_ref[...]

    def gnorm(o):
        mu = _dot_sel(o, gseg) * (1.0 / HEAD_W)
        dlt = o - mu
        var = _dot_sel(dlt * dlt, gseg) * (1.0 / HEAD_W)
        return dlt * lax.rsqrt(var + EPS) * gnw

    def combine(gf_ref, gb_ref, y_ref, base, ro):
        y = (gnorm(of_s[pl.ds(base + ro, c), :]) * _silu(gf_ref[0, pl.ds(ro, c), :].astype(F32))
             + gnorm(ob_s[pl.ds(base + ro, c), :]) * _silu(gb_ref[0, pl.ds(ro, c), :].astype(F32)))
        y_ref[0, pl.ds(ro, c), :] = y.astype(y_ref.dtype)

    def scan(q_ref, k_ref, v_ref, base, n, gated):
        def load(off):
            return q_ref[0, pl.ds(off, c), :], k_ref[0, pl.ds(off, c), :], v_ref[0, pl.ds(off, c), :]

        def body(j, _, emit):
            offs_f = [pl.multiple_of((2 * j + u) * c, c) for u in range(2)]
            offs_b = [pl.multiple_of((n - 1 - 2 * j - u) * c, c) for u in range(2)]
            ins_f = [load(o) for o in offs_f]
            ins_b = [load(o) for o in offs_b]
            sf, sb = st_s[0], st_s[1]
            outs_f, outs_b = [], []
            for u in range(2):
                o, sf = chunk_step(*ins_f[u], sf, 0, cd_f)
                outs_f.append(o)
                o, sb = chunk_step(*ins_b[u], sb, 1, cd_b)
                outs_b.append(o)
            st_s[0], st_s[1] = sf, sb
            for u in range(2):
                of_s[pl.ds(base + offs_f[u], c), :] = outs_f[u]
                ob_s[pl.ds(base + offs_b[u], c), :] = outs_b[u]
            for off in emit(offs_f, offs_b):
                combine(*gated, base, off)
            return 0

        none = lambda offs_f, offs_b: []
        both = none if gated is None else (lambda offs_f, offs_b: offs_f + offs_b)
        if n == 2:
            lax.fori_loop(0, 1, functools.partial(body, emit=none if gated is None else (lambda f, b: f)), 0)
        else:
            assert n % 4 == 0
            lax.fori_loop(0, n // 4, functools.partial(body, emit=none), 0)
            lax.fori_loop(n // 4, n // 2, functools.partial(body, emit=both), 0)

    scan(qc, kc, vc, 0, lc // c, (gfc, gbc, yc_o) if need_ctx else None)
    scan(qx, kx, vx, lc, sx // c, (gfx, gbx, yx_o))
    if not need_ctx:
        yc_o[...] = jnp.zeros_like(yc_o)


def _retention(projx, projc, log_g, gn_w, gseg, need_ctx):
    b, s, _ = projx.shape
    lc = projc.shape[1]
    lgs = log_g.reshape(2 * N_HEADS)
    lgl = jnp.repeat(log_g, HEAD_W, axis=1).reshape(2, 1, BRANCH_W)
    xs = [_proj_cols(projx, COL_RET + 256 * i, 256, s) for i in range(5)]
    cs = [_proj_cols(projc, COL_RET + 256 * i, 256, lc) for i in range(5)]
    c = RET_CHUNK
    yx, yc = pl.pallas_call(
        functools.partial(_ret_kernel, need_ctx=need_ctx),
        grid=(b,),
        in_specs=[pl.BlockSpec(memory_space=pltpu.SMEM),
                  pl.BlockSpec((2, 1, BRANCH_W), lambda bi: (0, 0, 0)),
                  pl.BlockSpec((1, BRANCH_W), lambda bi: (0, 0)),
                  pl.BlockSpec((BRANCH_W, BRANCH_W), lambda bi: (0, 0))] + xs + cs,
        out_specs=[pl.BlockSpec((1, s, BRANCH_W), lambda bi: (bi, 0, 0)),
                   pl.BlockSpec((1, lc, BRANCH_W), lambda bi: (bi, 0, 0))],
        out_shape=[jax.ShapeDtypeStruct((b, s, BRANCH_W), BF16),
                   jax.ShapeDtypeStruct((b, lc, BRANCH_W), BF16)],
        scratch_shapes=[pltpu.VMEM((lc + s, BRANCH_W), F32),
                        pltpu.VMEM((lc + s, BRANCH_W), F32),
                        pltpu.VMEM((2, BRANCH_W, BRANCH_W), F32),
                        pltpu.VMEM((2, c, N_HEADS * c), F32),
                        pltpu.VMEM((4, c, BRANCH_W), F32)],
        compiler_params=_cparams(("parallel",), 48 * 1024 * 1024),
        name="retention",
    )(lgs, lgl, gn_w.reshape(1, BRANCH_W), gseg, *([projx] * 5), *([projc] * 5))
    return yx, yc


def _merge_kernel(oa, ob, oc, od, g0, g1, g2, g3, x_ref, gate_ref, sc_ref, sh_ref, nw_ref, wb_ref, wo_ref,
                  xn_o, h2_o):
    acc = None
    for i, (o, g) in enumerate(((oa, g0), (ob, g1), (oc, g2), (od, g3))):
        t = (jnp.tanh(g[0].astype(F32)) + 1.0) * _dot(o[0], wb_ref[i])
        acc = t if acc is None else acc + t
    y = _dot(acc.astype(BF16), wo_ref[...])
    xn = x_ref[0] + gate_ref[0] * y
    xn_o[0] = xn
    h2_o[0] = _rms(xn) * nw_ref[...] * (1.0 + sc_ref[0]) + sh_ref[0]


def _merge(outs, proj, x, gate, sc2, sh2, n2w, wb, wo):
    b, l, d = x.shape
    tm = min(512, l)
    tok = lambda w: pl.BlockSpec((1, tm, w), lambda bi, i: (bi, i, 0))
    vec = pl.BlockSpec((1, 1, d), lambda bi, i: (bi, 0, 0))
    gates = [pl.BlockSpec((1, tm, d), lambda bi, i, k=k: (bi, i, COL_GATES // d + k)) for k in range(N_BRANCH)]
    return pl.pallas_call(
        _merge_kernel,
        grid=(b, l // tm),
        in_specs=[tok(BRANCH_W)] * 4 + gates + [tok(d), vec, vec, vec,
                                                pl.BlockSpec((1, d), lambda bi, i: (0, 0)),
                                                pl.BlockSpec(wb.shape, lambda bi, i: (0, 0, 0)),
                                                pl.BlockSpec(wo.shape, lambda bi, i: (0, 0))],
        out_specs=[tok(d), tok(d)],
        out_shape=[jax.ShapeDtypeStruct((b, l, d), F32), jax.ShapeDtypeStruct((b, l, d), F32)],
        compiler_params=_cparams(("parallel", "parallel"), 48 * 1024 * 1024),
        name="merge",
    )(*outs, proj, proj, proj, proj, x, gate, sc2, sh2, n2w, wb, wo)


def _sorting_network(n):
    pairs = []
    p = 1
    while p < n:
        k = p
        while k >= 1:
            for j in range(k % p, n - k, 2 * k):
                for i in range(min(k, n - j - k)):
                    if (i + j) // (2 * p) == (i + j + k) // (2 * p):
                        pairs.append((i + j, i + j + k))
            k //= 2
        p *= 2
    return tuple(pairs)


def _topk_rows_slabs(s, k):
    r, t = s.shape
    assert r == 8 * k
    sub = lax.broadcasted_iota(jnp.int32, (8, t), 0).astype(F32)
    vs = [s[8 * j:8 * j + 8] for j in range(k)]
    ix = [sub + float(8 * j) for j in range(k)]
    for p, q in _sorting_network(k):
        va, ia, vb, ib = vs[p], ix[p], vs[q], ix[q]
        swap = (vb > va) | ((vb == va) & (ib < ia))
        vs[p], vs[q] = jnp.maximum(va, vb), jnp.minimum(va, vb)
        ix[p], ix[q] = jnp.where(swap, ib, ia), jnp.where(swap, ia, ib)
    out_row = lax.broadcasted_iota(jnp.int32, (k, t), 0)
    vals = jnp.zeros((k, t), F32)
    idxs = jnp.zeros((k, t), F32)
    for rnd in range(k):
        hv, hi = vs[0], ix[0]
        m = jnp.max(hv, axis=0, keepdims=True)
        idx = jnp.min(jnp.where(hv == m, hi, float(r)), axis=0, keepdims=True)
        vals = jnp.where(out_row == rnd, m, vals)
        idxs = jnp.where(out_row == rnd, idx, idxs)
        won = hi == idx
        for d in range(k - 1 - rnd):
            vs[d] = jnp.where(won, vs[d + 1], vs[d])
            ix[d] = jnp.where(won, ix[d + 1], ix[d])
    return vals, idxs.astype(jnp.int32)


def _select_rows(table, sel, k):
    out = jnp.zeros_like(table)
    for r in range(k):
        out = jnp.where(sel == r, table[r:r + 1, :], out)
    return out


def _split_bf16(x):
    hi = x.astype(BF16)
    return hi, (x - hi.astype(F32)).astype(BF16)


def _peer_joint_topk(s1, s2):
    k = PEER_TOPK
    t = s1.shape[1]
    sub = lax.broadcasted_iota(jnp.int32, (8, t), 0)
    depth = jnp.full((8, t), k // 8, jnp.int32)
    for i in range(6, -1, -1):
        depth = jnp.where(sub == i, k // (i + 1), depth)
    subf = sub.astype(F32)
    lo = [jnp.where(depth > d, s1[0:8] + s2[d:d + 1], -jnp.inf) for d in range(k)]
    hi0 = s1[8:16] + s2[0:1]
    pos_hi = (subf + 8.0) * float(k)
    cnt = jnp.zeros((8, t), F32)
    out_row = lax.broadcasted_iota(jnp.int32, (k, t), 0)
    vals = jnp.zeros((k, t), F32)
    poss = jnp.zeros((k, t), F32)
    big = float(k * k)
    for rnd in range(k):
        pos_lo = subf * float(k) + cnt
        m = jnp.max(jnp.maximum(lo[0], hi0), axis=0, keepdims=True)
        cand = jnp.minimum(jnp.where(lo[0] == m, pos_lo, big), jnp.where(hi0 == m, pos_hi, big))
        pos = jnp.min(cand, axis=0, keepdims=True)
        vals = jnp.where(out_row == rnd, m, vals)
        poss = jnp.where(out_row == rnd, pos, poss)
        won_lo = pos_lo == pos
        for d in range(k - 1 - rnd):
            lo[d] = jnp.where(won_lo, lo[d + 1], lo[d])
        cnt = cnt + jnp.where(won_lo, 1.0, 0.0)
        hi0 = jnp.where(pos_hi == pos, -jnp.inf, hi0)
    p = poss.astype(jnp.int32)
    return vals, p >> 4, p & (k - 1)


def _peer_route_head(q, kh, kl):
    k = PEER_TOPK
    assert k == 16
    qh, ql = _split_bf16(q)
    s = _dot_nt(kh, qh) + (_dot_nt(kh, ql) + _dot_nt(kl, qh))
    s1, i1 = _topk_rows_slabs(s[:PEER_N_KEYS], k)
    s2, i2 = _topk_rows_slabs(s[PEER_N_KEYS:], k)
    ts, ri, rj = _peer_joint_topk(s1, s2)
    e = jnp.exp(ts - ts[0:1, :])
    return _select_rows(i1, ri, k), _select_rows(i2, rj, k), e / jnp.sum(e, axis=0, keepdims=True)


def _peer_query(h, wqh, wql):
    hh, hl = _split_bf16(h)
    return _dot(hh, wqh) + (_dot(hl, wqh) + _dot(hh, wql))


def _peer_route_kernel(h_ref, wqh_ref, wql_ref, kh_ref, kl_ref, a_o, b_o, g_o, q_s, a_s, b_s, g_s):
    k = PEER_TOPK
    q_s[...] = _peer_query(h_ref[...], wqh_ref[...], wql_ref[...])

    def head(h):
        lo = pl.multiple_of(h * PEER_DK, PEER_DK)
        ro = pl.multiple_of(h * k, k)
        a_s[pl.ds(ro, k), :], b_s[pl.ds(ro, k), :], g_s[pl.ds(ro, k), :] = _peer_route_head(
            q_s[:, pl.ds(lo, PEER_DK)], kh_ref[h], kl_ref[h])

    def head_group(i, _):
        for j in range(PEER_HEADS_PER_TRIP):
            head(PEER_HEADS_PER_TRIP * i + j)
        return 0

    lax.fori_loop(0, PEER_HEADS // PEER_HEADS_PER_TRIP, head_group, 0)
    a_o[...] = a_s[...].T
    b_o[...] = b_s[...].T
    g_o[...] = g_s[...].T


def _peer_route(h2, wq_hl, keys_hl):
    n, d = h2.shape
    t = 256
    wq_hi, wq_lo = wq_hl
    k_hi, k_lo = keys_hl
    return pl.pallas_call(
        _peer_route_kernel,
        grid=(n // t,),
        in_specs=[pl.BlockSpec((t, d), lambda i: (i, 0)),
                  pl.BlockSpec(wq_hi.shape, lambda i: (0, 0)),
                  pl.BlockSpec(wq_lo.shape, lambda i: (0, 0)),
                  pl.BlockSpec(k_hi.shape, lambda i: (0, 0, 0)),
                  pl.BlockSpec(k_lo.shape, lambda i: (0, 0, 0))],
        out_specs=[pl.BlockSpec((t, PEER_SLOTS), lambda i: (i, 0))] * 3,
        out_shape=[jax.ShapeDtypeStruct((n, PEER_SLOTS), jnp.int32),
                   jax.ShapeDtypeStruct((n, PEER_SLOTS), jnp.int32),
                   jax.ShapeDtypeStruct((n, PEER_SLOTS), F32)],
        scratch_shapes=[pltpu.VMEM((t, PEER_HEADS * PEER_DK), F32),
                        pltpu.VMEM((PEER_SLOTS, t), jnp.int32),
                        pltpu.VMEM((PEER_SLOTS, t), jnp.int32),
                        pltpu.VMEM((PEER_SLOTS, t), F32)],
        compiler_params=_cparams(("parallel",), 48 * 1024 * 1024),
        name="peer_route",
    )(h2, wq_hi, wq_lo, k_hi, k_lo)


_HI16 = -65536


def _bf16_bits(w):
    return lax.bitcast_convert_type(w, jnp.int32) & _HI16


def _peer_ffn_kernel(*refs, ec, unroll, final_norm, route_next):
    if route_next:
        (h_ref, a_ref, b_ref, g_ref, x_ref, gate_ref, u_ref, v_ref, fw_ref, qn_ref, kh_ref, kl_ref,
         o_ref, hb_s, w_s, a_c, b_c, g_c, a_n, b_n, g_n) = refs
    else:
        h_ref, a_ref, b_ref, g_ref, x_ref, gate_ref, u_ref, v_ref, fw_ref, o_ref, hb_s, w_s = refs
        a_c, b_c, g_c = a_ref, b_ref, g_ref
    e = pl.program_id(1)
    t = h_ref.shape[0]
    half = t // 2
    nk = PEER_N_KEYS

    @pl.when(e == 0)
    def _():
        if route_next:
            @pl.when(pl.program_id(0) == 0)
            def _():
                a_c[...] = a_ref[...]
                b_c[...] = b_ref[...]
                g_c[...] = g_ref[...]

        hb_s[...] = h_ref[...].astype(BF16)
        o_ref[...] = jnp.zeros_like(o_ref)
        jio = lax.broadcasted_iota(jnp.int32, (nk, PEER_SLOTS), 0)

        def tile(tt):
            arow = jnp.broadcast_to(a_c[pl.ds(tt, 1), :], (nk, PEER_SLOTS))
            brow = jnp.broadcast_to(b_c[pl.ds(tt, 1), :], (nk, PEER_SLOTS))
            grow = jnp.broadcast_to(g_c[pl.ds(tt, 1), :], (nk, PEER_SLOTS))
            cm = jnp.where(jio == arow, grow, 0.0).astype(BF16)
            bm_t = jnp.where(jio == brow, 1.0, 0.0).T.astype(BF16)
            return _dot(cm, bm_t)

        def build(tb, _):
            for u in range(unroll):
                tt = tb * unroll + u
                word = _bf16_bits(tile(tt + half)) | lax.shift_right_logical(_bf16_bits(tile(tt)), 16)
                w_s[pl.ds(pl.multiple_of(tt * PEER_W_PITCH, 8), nk), :] = word
            return 0

        lax.fori_loop(0, half // unroll, build, 0)

    if route_next:
        nsub = t // PEER_ROUTE_TOKENS
        hd = e // nsub
        co = pl.multiple_of((e % nsub) * PEER_ROUTE_TOKENS, PEER_ROUTE_TOKENS)
        ro = pl.multiple_of(hd * PEER_TOPK, PEER_TOPK)
        routed = _peer_route_head(
            qn_ref[pl.ds(co, PEER_ROUTE_TOKENS), pl.ds(pl.multiple_of(hd * PEER_DK, PEER_DK), PEER_DK)],
            kh_ref[hd], kl_ref[hd])

    hid = _dot(hb_s[...], u_ref[0])
    j0 = e * (ec // nk)
    words = jnp.concatenate([w_s[pl.ds(j0 + j, half, stride=PEER_W_PITCH), :] for j in range(ec // nk)], axis=1)
    w_lo = lax.bitcast_convert_type(lax.shift_left(words, 16), F32)
    w_hi = lax.bitcast_convert_type(words & _HI16, F32)
    wc = jnp.concatenate([w_lo, w_hi], axis=0)
    act = 0.5 * hid * (1.0 + lax.erf(hid * SQRT_HALF))
    o_ref[...] += _dot((wc * act).astype(BF16), v_ref[...])

    if route_next:
        a_n[pl.ds(ro, PEER_TOPK), pl.ds(co, PEER_ROUTE_TOKENS)] = routed[0]
        b_n[pl.ds(ro, PEER_TOPK), pl.ds(co, PEER_ROUTE_TOKENS)] = routed[1]
        g_n[pl.ds(ro, PEER_TOPK), pl.ds(co, PEER_ROUTE_TOKENS)] = routed[2]

    @pl.when(e == pl.num_programs(1) - 1)
    def _():
        y = x_ref[...] + gate_ref[0] * o_ref[...]
        o_ref[...] = _rms(y) * fw_ref[...] if final_norm else y
        if route_next:
            a_c[...] = a_n[...].T
            b_c[...] = b_n[...].T
            g_c[...] = g_n[...].T


def _peer_ffn(h2, a, b_idx, g, x, gate, u_blk, v, l, final_w, final_norm, route_next=None):
    n, d = h2.shape
    ne = v.shape[0]
    neb, _, ec = u_blk.shape
    t = min(512, l)
    nb = n // t
    unroll = 16
    assert l % t == 0 and ne == PEER_N_KEYS * PEER_N_KEYS and neb * ec == ne and (t // 2) % unroll == 0
    tok = lambda w: pl.BlockSpec((t, w), lambda i, e: (i, 0))
    in_specs = [tok(d), tok(PEER_SLOTS), tok(PEER_SLOTS), tok(PEER_SLOTS), tok(d),
                pl.BlockSpec((1, 1, d), lambda i, e: ((i * t) // l, 0, 0)),
                pl.BlockSpec((1, d, ec), lambda i, e: (e, 0, 0)),
                pl.BlockSpec((ec, d), lambda i, e: (e, 0)),
                pl.BlockSpec((1, d), lambda i, e: (0, 0))]
    args = [h2, a, b_idx, g, x, gate, u_blk, v, final_w.reshape(1, d)]
    scratch = [pltpu.VMEM((t, d), BF16), pltpu.VMEM((t // 2 * PEER_W_PITCH, PEER_N_KEYS), jnp.int32)]
    if route_next is not None:
        q_all, k_hi, k_lo = route_next
        assert neb == PEER_HEADS * (t // PEER_ROUTE_TOKENS) and a.shape[0] == t
        first = pl.BlockSpec((t, PEER_SLOTS), lambda i, e: (0, 0))
        in_specs[1:4] = [first, first, first]
        in_specs += [pl.BlockSpec((t, q_all.shape[1]), lambda i, e: (jnp.minimum(i + 1, nb - 1), 0)),
                     pl.BlockSpec(k_hi.shape, lambda i, e: (0, 0, 0)),
                     pl.BlockSpec(k_lo.shape, lambda i, e: (0, 0, 0))]
        args += [q_all, k_hi, k_lo]
        scratch += [pltpu.VMEM((t, PEER_SLOTS), jnp.int32), pltpu.VMEM((t, PEER_SLOTS), jnp.int32),
                    pltpu.VMEM((t, PEER_SLOTS), F32),
                    pltpu.VMEM((PEER_SLOTS, t), jnp.int32), pltpu.VMEM((PEER_SLOTS, t), jnp.int32),
                    pltpu.VMEM((PEER_SLOTS, t), F32)]
    return pl.pallas_call(
        functools.partial(_peer_ffn_kernel, ec=ec, unroll=unroll, final_norm=final_norm,
                          route_next=route_next is not None),
        grid=(nb, neb),
        in_specs=in_specs,
        out_specs=tok(d),
        out_shape=jax.ShapeDtypeStruct((n, d), F32),
        scratch_shapes=scratch,
        compiler_params=_cparams(("arbitrary", "arbitrary"), VMEM_LIMIT_V7X),
        name="peer_ffn_routed" if route_next is not None else "peer_ffn",
    )(*args)


def _peer_query_kernel(h_ref, wqh_ref, wql_ref, q_o):
    q_o[...] = _peer_query(h_ref[...], wqh_ref[...], wql_ref[...])


def _peer_queries(h2, wq_hl):
    n, d = h2.shape
    wq_hi, wq_lo = wq_hl
    t = min(512, n)
    return pl.pallas_call(
        _peer_query_kernel,
        grid=(n // t,),
        in_specs=[pl.BlockSpec((t, d), lambda i: (i, 0)),
                  pl.BlockSpec(wq_hi.shape, lambda i: (0, 0)),
                  pl.BlockSpec(wq_lo.shape, lambda i: (0, 0))],
        out_specs=pl.BlockSpec((t, wq_hi.shape[1]), lambda i: (i, 0)),
        out_shape=jax.ShapeDtypeStruct((n, wq_hi.shape[1]), F32),
        compiler_params=_cparams(("parallel",), 40 * 1024 * 1024),
        name="peer_queries",
    )(h2, wq_hi, wq_lo)


def _layout_w_in(w):
    parts = jnp.split(w, IN_OFFSETS, axis=1)
    z = lambda n: jnp.zeros((w.shape[0], n), w.dtype)
    kpe_blk = jnp.concatenate([z(MLA_NOPE), parts[2], z(MLA_HEAD_PAD - MLA_NOPE - MLA_ROPE)], axis=1)
    half_gates = parts[-1] * 0.5
    return jnp.concatenate([parts[0], parts[1], kpe_blk] + list(parts[3:-1]) + [half_gates], axis=1).astype(BF16)


def _layout_mla(w_uq, w_ukv):
    qh = w_uq.reshape(MLA_Q_LORA, N_HEADS, MLA_NOPE + MLA_ROPE)
    qh = jnp.pad(qh, ((0, 0), (0, 0), (0, MLA_HEAD_PAD - MLA_NOPE - MLA_ROPE)))
    kv = w_ukv.reshape(MLA_KV_LORA, N_HEADS, MLA_NOPE + MLA_V)
    kh = jnp.pad(kv[:, :, :MLA_NOPE], ((0, 0), (0, 0), (0, MLA_HEAD_PAD - MLA_NOPE)))
    vh = kv[:, :, MLA_NOPE:]
    return (qh.reshape(MLA_Q_LORA, -1).astype(BF16), kh.reshape(MLA_KV_LORA, -1).astype(BF16),
            vh.reshape(MLA_KV_LORA, -1).T.astype(BF16))


def _static_mats():
    gseg = np.kron(np.eye(N_HEADS), np.ones((HEAD_W, HEAD_W))).astype(np.float32)
    pm = np.zeros((MLA_HEAD_PAD, MLA_HEAD_PAD), np.float32)
    for dd in range(MLA_ROPE):
        blk, j = dd // 16, dd % 16
        pm[MLA_NOPE + blk * 16 + (j + 8) % 16, MLA_NOPE + dd] = 1.0
    pg = np.zeros((BRANCH_W, BRANCH_W), np.float32)
    for i in range(BRANCH_W):
        off, dd = (i // HEAD_W) * HEAD_W, i % HEAD_W
        blk, j = dd // 32, dd % 32
        pg[off + blk * 32 + (j + 16) % 32, i] = 1.0
    ex = np.zeros((GQA_KV_HEADS * HEAD_W, BRANCH_W), np.float32)
    for i in range(BRANCH_W):
        ex[((i // HEAD_W) // (N_HEADS // GQA_KV_HEADS)) * HEAD_W + i % HEAD_W, i] = 1.0
    return tuple(jnp.asarray(m, dtype=BF16) for m in (gseg, pm, pg, ex, ex.T, np.eye(BRANCH_W)))


def _rope_half_tables(pos, hf):
    freqs = ROPE_THETA ** (-jnp.arange(hf, dtype=F32) / hf)
    ang = pos[:, None] * freqs[None, :]
    c, s = jnp.cos(ang), jnp.sin(ang)
    return jnp.concatenate([c, c], axis=1), jnp.concatenate([-s, s], axis=1)


def _axial_tables(row, col, dims):
    cr, sr = _rope_half_tables(row, dims // 4)
    cc, sc = _rope_half_tables(col, dims // 4)
    return jnp.concatenate([cr, cc], axis=1), jnp.concatenate([sr, sc], axis=1)


def _rope_tables(s):
    t = jnp.arange(s)
    row, col = (t // GRID_W).astype(F32), (t % GRID_W).astype(F32)
    c32, s32 = _axial_tables(row, col, MLA_ROPE)
    pad = MLA_HEAD_PAD - MLA_NOPE - MLA_ROPE
    cm = jnp.concatenate([jnp.ones((s, MLA_NOPE), F32), c32, jnp.ones((s, pad), F32)], axis=1)
    sm = jnp.concatenate([jnp.zeros((s, MLA_NOPE), F32), s32, jnp.zeros((s, pad), F32)], axis=1)
    c64, s64 = _axial_tables(row, col, HEAD_W)
    return cm, sm, jnp.tile(c64, (1, N_HEADS)), jnp.tile(s64, (1, N_HEADS))


def _split_f32(w):
    hi = w.astype(BF16)
    return hi, (w - hi.astype(F32)).astype(BF16)


def _layout_peer_keys(keys):
    h, _, nk, dh = keys.shape
    z = jnp.zeros((h, nk, dh), keys.dtype)
    top = jnp.concatenate([keys[:, 0], z], axis=2)
    bot = jnp.concatenate([z, keys[:, 1]], axis=2)
    return jnp.concatenate([top, bot], axis=1)


def kernel(x, c, ctx, c_ctx, mod_w, mod_b, norm1_w, norm2_w, w_in, mla_q_norm, mla_w_uq, mla_kv_norm, mla_w_ukv, gqa_q_norm, gqa_k_norm, na_bias, ret_decay_logit, ret_gn_w, w_branch, w_out, peer_w_q, peer_keys, peer_u, peer_v, final_norm_w):
    b, s, d = x.shape
    lc = ctx.shape[1]
    depth = mod_w.shape[0]
    assert d == D_MODEL and s % (GRID_W * NA_Q_ROWS) == 0 and s % 256 == 0 and lc % 256 == 0

    rows = -(-(b + 1) // 8) * 8
    cc = jnp.zeros((rows, d), F32).at[:b].set(c).at[b].set(c_ctx)
    mod = _modulation(cc, mod_w, mod_b)

    gseg, pm, pg, ex, ex_t, eye = _static_mats()
    tables = _rope_tables(s)

    for l in range(depth):
        need_ctx = l < depth - 1
        mx = mod[l, :b].reshape(b, 1, 6, d)
        mc = jnp.broadcast_to(mod[l, b].reshape(1, 1, 6, d), (b, 1, 6, d))
        sh1x, sc1x, g1x, sh2x, sc2x, g2x = (mx[:, :, i] for i in range(6))
        sh1c, sc1c, g1c, sh2c, sc2c, g2c = (mc[:, :, i] for i in range(6))

        w_in_l = _layout_w_in(w_in[l])
        wuq, wk, wv = _layout_mla(mla_w_uq[l], mla_w_ukv[l])
        consts = (mla_q_norm[l].reshape(1, -1), wuq, mla_kv_norm[l].reshape(1, -1), wk, wv,
                  jnp.tile(gqa_q_norm[l], N_HEADS).reshape(1, -1),
                  jnp.tile(gqa_k_norm[l], GQA_KV_HEADS).reshape(1, -1), gseg, pm, pg, ex, ex_t, eye)
        n1w = norm1_w[l].reshape(1, d)
        n2w = norm2_w[l].reshape(1, d)
        wb = (w_branch[l] * 0.5).astype(BF16)
        wo = w_out[l].astype(BF16)
        keys_hl = _split_f32(_layout_peer_keys(peer_keys[l]))
        wq_hl = _split_f32(peer_w_q[l])
        ne = peer_u.shape[1]
        u_blk = jnp.swapaxes(peer_u[l].astype(BF16).reshape(ne // PEER_EXPERT_CHUNK, PEER_EXPERT_CHUNK, d), 1, 2)
        v_b = peer_v[l].astype(BF16)
        log_g = jax.nn.log_sigmoid(ret_decay_logit[l].astype(F32))

        projx = _inproj(x, n1w, sc1x, sh1x, w_in_l)
        projc = _inproj(ctx, n1w, sc1c, sh1c, w_in_l)
        qmx, kmx, vmx, qgx, kgx, vgx, nvx = _prep(projx, consts, tables)
        qmc, kmc, vmc, qgc, kgc, vgc, nvc = _prep(projc, consts, None)

        oa, ob = _attention([dict(q=(qmx, 0), segs=[((kmc, 0), vmc), ((kmx, 0), vmx)], dq=MLA_HEAD_PAD),
                             dict(q=(qgx, 0), segs=[((kgc, 0), vgc), ((kgx, 0), vgx)], dq=HEAD_W)], "attn_mla_gqa")
        oc = _na_attention(projx, projc, nvx, nvc, na_bias[l])
        od, od_c = _retention(projx, projc, log_g, ret_gn_w[l], gseg, need_ctx)

        def peer(h2f, xf, gate, l_blk, final):
            t = min(512, l_blk)
            if (t // PEER_ROUTE_TOKENS) * PEER_HEADS == u_blk.shape[0]:
                a0, b0, g0 = _peer_route(h2f[:t], wq_hl, keys_hl)
                return _peer_ffn(h2f, a0, b0, g0, xf, gate, u_blk, v_b, l_blk, final_norm_w, final_norm=final,
                                 route_next=(_peer_queries(h2f, wq_hl),) + keys_hl)
            a0, b0, g0 = _peer_route(h2f, wq_hl, keys_hl)
            return _peer_ffn(h2f, a0, b0, g0, xf, gate, u_blk, v_b, l_blk, final_norm_w, final_norm=final)

        x, h2x = _merge((oa, ob, oc, od), projx, x, g1x, sc2x, sh2x, n2w, wb, wo)
        x = peer(h2x.reshape(b * s, d), x.reshape(b * s, d), g2x, s, l == depth - 1).reshape(b, s, d)

        if need_ctx:
            ca, cb, ccx = _attention(
                [dict(q=(qmc, 0), segs=[((kmc, 0), vmc)], dq=MLA_HEAD_PAD),
                 dict(q=(qgc, 0), segs=[((kgc, 0), vgc)], dq=HEAD_W),
                 dict(q=(projc, COL_NQ), segs=[((projc, COL_NK), nvc)], dq=HEAD_W, qscale=HEAD_W ** -0.5,
                      log2_scores=False)], "attn_ctx")
            ctx, h2c = _merge((ca, cb, ccx, od_c), projc, ctx, g1c, sc2c, sh2c, n2w, wb, wo)
            ctx = peer(h2c.reshape(b * lc, d), ctx.reshape(b * lc, d), g2c[:1], b * lc, False).reshape(b, lc, d)

    return x
```

```python
import functools

import numpy as np
import jax
import jax.numpy as jnp
from jax import lax
from jax.experimental import pallas as pl
from jax.experimental.pallas import tpu as pltpu

F32 = jnp.float32
BF16 = jnp.bfloat16
HIGHEST = lax.Precision.HIGHEST

D_MODEL = 1024
GRID_W = 64
ROPE_THETA = 10000.0
EPS = 1e-6
N_HEADS = 4
HEAD_W = 64
BRANCH_W = N_HEADS * HEAD_W
MLA_NOPE, MLA_ROPE, MLA_V = 64, 32, 64
MLA_Q_LORA, MLA_KV_LORA = 256, 128
MLA_SCALE = (MLA_NOPE + MLA_ROPE) ** -0.5
MLA_HEAD_PAD = 128
GQA_KV_HEADS = 2
NA_WIN_R, NA_WIN_C = 8, 16
NA_Q_ROWS = 4
ATTN_KEY_CHUNK = 1024
RET_CHUNK = 128
N_BRANCH = 4
PEER_HEADS, PEER_N_KEYS, PEER_TOPK, PEER_DK = 8, 128, 16, 128
PEER_SLOTS = PEER_HEADS * PEER_TOPK
PEER_W_PITCH = PEER_N_KEYS + 8
PEER_HEADS_PER_TRIP = 4
PEER_ROUTE_TOKENS = 256
PEER_EXPERT_CHUNK = 1024
SQRT_HALF = 0.7071067811865476
LOG2E = 1.4426950408889634
NEG_BIG = -1e30

IN_SIZES = (256, 128, 32, 256, 128, 128, 256, 256, 256, 256, 256, 256, 256, 256, 4096)
IN_OFFSETS = tuple(int(v) for v in np.cumsum(IN_SIZES)[:-1])
PROJ_COLS = 7168
COL_NQ, COL_NK, COL_NV = 1024, 1280, 1536
COL_RET = 1792
COL_GATES = 3072

VMEM_LIMIT_V7X = 56 * 1024 * 1024


def _cparams(sem, vmem=None):
    return pltpu.CompilerParams(dimension_semantics=sem, vmem_limit_bytes=vmem)


def _dot(a, b):
    return jnp.dot(a, b, preferred_element_type=F32)


def _dot_hi(a, b):
    return jnp.dot(a, b, preferred_element_type=F32, precision=HIGHEST)


def _dot_nt(a, b):
    return lax.dot_general(a, b, (((1,), (1,)), ((), ())), preferred_element_type=F32)


def _dot_sel(x, sel):
    hi = x.astype(BF16)
    r1 = x - hi.astype(F32)
    mid = r1.astype(BF16)
    lo = (r1 - mid.astype(F32)).astype(BF16)
    return _dot(hi, sel) + (_dot(mid, sel) + _dot(lo, sel))


def _rms(x):
    return x * lax.rsqrt(jnp.mean(x * x, axis=-1, keepdims=True) + EPS)


def _silu(x):
    return x * jax.nn.sigmoid(x)


def _head_mask(shape, h, width=HEAD_W):
    lane = lax.broadcasted_iota(jnp.int32, shape, len(shape) - 1)
    lo = h * width
    return (lane >= lo) & (lane < lo + width)


def _mod_kernel(c_ref, w_ref, b_ref, o_ref):
    o_ref[0] = _dot_hi(_silu(c_ref[...]), w_ref[0]) + b_ref[0]


def _modulation(cc, mod_w, mod_b):
    depth, d, n = mod_w.shape
    rows = cc.shape[0]
    tn = 1536
    return pl.pallas_call(
        _mod_kernel,
        grid=(depth, n // tn),
        in_specs=[pl.BlockSpec((rows, d), lambda l, j: (0, 0)),
                  pl.BlockSpec((1, d, tn), lambda l, j: (l, 0, j)),
                  pl.BlockSpec((1, 1, tn), lambda l, j: (l, 0, j))],
        out_specs=pl.BlockSpec((1, rows, tn), lambda l, j: (l, 0, j)),
        out_shape=jax.ShapeDtypeStruct((depth, rows, n), F32),
        compiler_params=_cparams(("parallel", "parallel"), 40 * 1024 * 1024),
        name="modulation",
    )(cc, mod_w, mod_b.reshape(depth, 1, n))


def _inproj_kernel(x_ref, nw_ref, sc_ref, sh_ref, w_ref, o_ref, h_scr):
    @pl.when(pl.program_id(2) == 0)
    def _():
        h = _rms(x_ref[0]) * nw_ref[...] * (1.0 + sc_ref[0]) + sh_ref[0]
        h_scr[...] = h.astype(BF16)

    o_ref[0] = _dot(h_scr[...], w_ref[...]).astype(o_ref.dtype)


def _inproj(x, nw, sc, sh, w):
    b, l, d = x.shape
    n = w.shape[1]
    tm = min(1024, l)
    tn = 1792
    return pl.pallas_call(
        _inproj_kernel,
        grid=(b, l // tm, n // tn),
        in_specs=[pl.BlockSpec((1, tm, d), lambda bi, i, j: (bi, i, 0)),
                  pl.BlockSpec((1, d), lambda bi, i, j: (0, 0)),
                  pl.BlockSpec((1, 1, d), lambda bi, i, j: (bi, 0, 0)),
                  pl.BlockSpec((1, 1, d), lambda bi, i, j: (bi, 0, 0)),
                  pl.BlockSpec((d, tn), lambda bi, i, j: (0, j))],
        out_specs=pl.BlockSpec((1, tm, tn), lambda bi, i, j: (bi, i, j)),
        out_shape=jax.ShapeDtypeStruct((b, l, n), BF16),
        scratch_shapes=[pltpu.VMEM((tm, d), BF16)],
        compiler_params=_cparams(("parallel", "parallel", "arbitrary"), 40 * 1024 * 1024),
        name="inproj",
    )(x, nw, sc, sh, w)


def _prep_kernel(*refs, use_rope):
    (p_ref, nv_ref, qn_ref, wuq_ref, kvn_ref, wk_ref, wv_ref, gqn_ref, gkn_ref, gseg_ref, pm_ref, pg_ref,
     e_ref, et_ref, eye_ref) = refs[:15]
    if use_rope:
        cm_ref, sm_ref, cg_ref, sg_ref = refs[15:19]
        outs = refs[19:]
    else:
        outs = refs[15:]
    qm_o, km_o, vm_o, qg_o, kg_o, vg_o, nvt_o = outs
    nvt_o[0] = _dot_nt(eye_ref[...], nv_ref[0]).astype(BF16)

    pb = p_ref[0]
    cq = pb[:, 0:256].astype(F32)
    ckv = pb[:, 256:384].astype(F32)
    kpe = pb[:, 384:512].astype(F32)
    gq = pb[:, 512:768].astype(F32)
    gk = pb[:, 768:896].astype(F32)
    gv = pb[:, 896:1024]

    cqn = (_rms(cq) * qn_ref[...]).astype(BF16)
    qa = _dot(cqn, wuq_ref[...])
    ckn = (_rms(ckv) * kvn_ref[...]).astype(BF16)
    kn = _dot(ckn, wk_ref[...])
    vm_t = _dot_nt(wv_ref[...], ckn)
    if use_rope:
        cm, sm = cm_ref[...], sm_ref[...]
        pm = pm_ref[...]

        def rope_m(t):
            return t * cm + _dot_sel(t, pm) * sm

        qa = jnp.concatenate([rope_m(qa[:, h * 128:(h + 1) * 128]) for h in range(N_HEADS)], axis=1)
        kpe = rope_m(kpe)
    km = kn + jnp.concatenate([kpe] * N_HEADS, axis=1)
    qm_o[0] = (qa * (MLA_SCALE * LOG2E)).astype(BF16)
    km_o[0] = km.astype(BF16)
    vm_o[0] = vm_t.astype(BF16)

    gseg = gseg_ref[...]
    gqn = gq * lax.rsqrt(_dot_sel(gq * gq, gseg) * (1.0 / HEAD_W) + EPS) * gqn_ref[...]
    gkn = gk * lax.rsqrt(_dot_sel(gk * gk, gseg[:128, :128]) * (1.0 / HEAD_W) + EPS) * gkn_ref[...]
    if use_rope:
        cg, sg = cg_ref[...], sg_ref[...]
        pg = pg_ref[...]
        gqn = gqn * cg + _dot_sel(gqn, pg) * sg
        gkn = gkn * cg[:, :128] + _dot_sel(gkn, pg[:128, :128]) * sg[:, :128]
    qg_o[0] = (gqn * (HEAD_W ** -0.5 * LOG2E)).astype(BF16)
    e = e_ref[...]
    kg_o[0] = _dot(gkn.astype(BF16), e).astype(BF16)
    vg_o[0] = _dot_nt(et_ref[...], gv).astype(BF16)


def _prep(proj, consts, tables):
    b, l, _ = proj.shape
    tm = min(512, l)
    use_rope = tables is not None
    full = lambda a: pl.BlockSpec(a.shape, lambda bi, i: (0,) * a.ndim)
    in_specs = [pl.BlockSpec((1, tm, 1024), lambda bi, i: (bi, i, 0)),
                pl.BlockSpec((1, tm, BRANCH_W), lambda bi, i: (bi, i, COL_NV // BRANCH_W))] + [full(a) for a in consts]
    args = [proj, proj] + list(consts)
    if use_rope:
        in_specs += [pl.BlockSpec((tm, t.shape[1]), lambda bi, i: (i, 0)) for t in tables]
        args += list(tables)
    tok = lambda w: (pl.BlockSpec((1, tm, w), lambda bi, i: (bi, i, 0)), jax.ShapeDtypeStruct((b, l, w), BF16))
    tr = (pl.BlockSpec((1, BRANCH_W, tm), lambda bi, i: (bi, 0, i)), jax.ShapeDtypeStruct((b, BRANCH_W, l), BF16))
    outs = (tok(512), tok(512), tr, tok(256), tok(256), tr, tr)
    return pl.pallas_call(
        functools.partial(_prep_kernel, use_rope=use_rope),
        grid=(b, l // tm),
        in_specs=in_specs,
        out_specs=[o[0] for o in outs],
        out_shape=[o[1] for o in outs],
        compiler_params=_cparams(("parallel", "parallel"), 40 * 1024 * 1024),
        name="prep_rope" if use_rope else "prep",
    )(*args)


def _attn_kernel(*refs, nseg, probs, tk):
    per = 1 + 2 * nseg
    nprob = len(probs)
    gw = 256
    ones_rows = 16
    state = []
    for pi, (dq, qscale, log2_scores) in enumerate(probs):
        q_ref = refs[pi * per]
        segs = [(refs[pi * per + 1 + 2 * i], refs[pi * per + 2 + 2 * i]) for i in range(nseg)]
        tq = q_ref.shape[1]
        hpg = gw // dq
        qstacks = []
        for g in range(N_HEADS // hpg):
            qg = q_ref[0, :, g * gw:(g + 1) * gw]
            if qscale is not None:
                qg = qg * jnp.asarray(qscale, BF16)
            qstacks.append(jnp.concatenate([jnp.where(_head_mask(qg.shape, j, dq), qg, jnp.zeros_like(qg))
                                            for j in range(hpg)], axis=0))
        state.append((segs, tq, hpg, qstacks, jnp.exp2 if log2_scores else jnp.exp))
    carry = tuple(tuple((jnp.full((1, st[1]), -jnp.inf, F32), jnp.zeros((HEAD_W + ones_rows, st[1]), F32))
                        for _ in range(N_HEADS)) for st in state)

    def chunk_step(pi, carry, si, off, tkk):
        segs, tq, hpg, qstacks, exp = state[pi]
        k_ref, vt_ref = segs[si]
        st_g = [_dot_nt(k_ref[0, pl.ds(off, tkk), g * gw:(g + 1) * gw], qstacks[g])
                for g in range(N_HEADS // hpg)]
        ones = jnp.ones((ones_rows, tkk), BF16)
        new = []
        for h in range(N_HEADS):
            m, acc = carry[h]
            vt = jnp.concatenate([vt_ref[0, h * HEAD_W:(h + 1) * HEAD_W, pl.ds(off, tkk)], ones], axis=0)
            st = st_g[h // hpg][:, (h % hpg) * tq:(h % hpg + 1) * tq]
            mn = jnp.maximum(m, jnp.max(st, axis=0, keepdims=True))
            acc = exp(m - mn) * acc + _dot(vt, exp(st - mn).astype(BF16))
            new.append((mn, acc))
        return tuple(new)

    for si in range(nseg):
        lk = state[0][0][si][0].shape[1]
        tkk = min(tk, lk)
        n = lk // tkk

        def body(c, carry, si=si, tkk=tkk):
            off = c * tkk if isinstance(c, int) else pl.multiple_of(c * tkk, tkk)
            return tuple(chunk_step(pi, carry[pi], si, off, tkk) for pi in range(nprob))

        carry = body(0, carry) if n == 1 else lax.fori_loop(0, n, body, carry)
    for pi in range(nprob):
        o_ref = refs[nprob * per + pi]
        out_t = jnp.concatenate([acc[:HEAD_W] * (1.0 / acc[HEAD_W:HEAD_W + 1]) for _, acc in carry[pi]], axis=0)
        o_ref[0] = out_t.T.astype(o_ref.dtype)


def _attention(problems, name):
    b, lq, _ = problems[0]["q"][0].shape
    tq = min(512, lq)
    in_specs, args, probs = [], [], []
    for p in problems:
        (qa, qcol), dq = p["q"], p["dq"]
        wq = N_HEADS * dq
        assert qcol % wq == 0 and qa.shape[:2] == (b, lq)
        in_specs.append(pl.BlockSpec((1, tq, wq), lambda bi, i, qcol=qcol, wq=wq: (bi, i, qcol // wq)))
        args.append(qa)
        for (ka, kcol), vt in p["segs"]:
            assert kcol % wq == 0 and vt.shape[1] == BRANCH_W and vt.shape[2] == ka.shape[1]
            in_specs.append(pl.BlockSpec((1, ka.shape[1], wq), lambda bi, i, kcol=kcol, wq=wq: (bi, 0, kcol // wq)))
            in_specs.append(pl.BlockSpec((1, BRANCH_W, vt.shape[2]), lambda bi, i: (bi, 0, 0)))
            args += [ka, vt]
        probs.append((dq, p.get("qscale"), p.get("log2_scores", True)))
    nseg = len(problems[0]["segs"])
    assert all(len(p["segs"]) == nseg for p in problems)
    outs = pl.pallas_call(
        functools.partial(_attn_kernel, nseg=nseg, probs=tuple(probs), tk=ATTN_KEY_CHUNK),
        grid=(b, lq // tq),
        in_specs=in_specs,
        out_specs=[pl.BlockSpec((1, tq, BRANCH_W), lambda bi, i: (bi, i, 0))] * len(problems),
        out_shape=[jax.ShapeDtypeStruct((b, lq, BRANCH_W), BF16)] * len(problems),
        compiler_params=_cparams(("parallel", "arbitrary"), VMEM_LIMIT_V7X),
        name=name,
    )(*args)
    return outs


def _proj_cols(arr, col, width, rows):
    assert col % width == 0
    return pl.BlockSpec((1, rows, width), lambda *idx: (idx[0], 0, col // width))


def _na_kernel(pat_ref, ks_ref, q_ref, k_ref, vt_ref, kc_ref, vct_ref, m_ref, o_ref, *, kw):
    del pat_ref
    g = pl.program_id(1)
    off = pl.multiple_of(ks_ref[g] * GRID_W, 128)
    q = q_ref[0] * jnp.asarray(HEAD_W ** -0.5, BF16)
    qb = q.shape[0]
    qstack = jnp.concatenate([jnp.where(_head_mask(q.shape, h), q, jnp.zeros_like(q))
                              for h in range(N_HEADS)], axis=0)
    st_w = _dot_nt(k_ref[0, pl.ds(off, kw), :], qstack)
    st_c = _dot_nt(kc_ref[0], qstack)
    ones_w = jnp.ones((16, kw), BF16)
    ones_c = jnp.ones((16, kc_ref.shape[1]), BF16)
    outs = []
    for h in range(N_HEADS):
        sw = st_w[:, h * qb:(h + 1) * qb] + m_ref[0, h]
        sc = st_c[:, h * qb:(h + 1) * qb]
        mx = jnp.maximum(jnp.max(sw, axis=0, keepdims=True), jnp.max(sc, axis=0, keepdims=True))
        vtw = jnp.concatenate([vt_ref[0, h * HEAD_W:(h + 1) * HEAD_W, pl.ds(off, kw)], ones_w], axis=0)
        vtc = jnp.concatenate([vct_ref[0, h * HEAD_W:(h + 1) * HEAD_W, :], ones_c], axis=0)
        acc = _dot(vtw, jnp.exp(sw - mx).astype(BF16)) + _dot(vtc, jnp.exp(sc - mx).astype(BF16))
        outs.append(acc[:HEAD_W] * (1.0 / acc[HEAD_W:HEAD_W + 1]))
    o_ref[0] = jnp.concatenate(outs, axis=0).T.astype(o_ref.dtype)


def _na_plan(s):
    rows = s // GRID_W
    wr = min(NA_WIN_R, rows)
    wc = NA_WIN_C
    qr = min(NA_Q_ROWS, rows)
    kwr = min(qr + wr - 1 + (1 if qr + wr - 1 < rows else 0), rows)
    ngrp = rows // qr
    qc = np.arange(GRID_W)[:, None]
    kc = np.arange(GRID_W)[None, :]
    cs = np.clip(qc - wc // 2, 0, GRID_W - wc)
    valid_c = (kc >= cs) & (kc < cs + wc)
    rel_c = np.where(valid_c, kc - qc + (NA_WIN_C - 1), 0)
    assert (valid_c.sum(1) == wc).all()
    pats, pat_ids, ks_rows = [], [], []
    for g in range(ngrp):
        r0 = g * qr
        ks = int(np.clip(r0 - wr // 2, 0, rows - kwr))
        assert (ks * GRID_W) % 128 == 0
        r = (r0 + np.arange(qr))[:, None]
        kr = (ks + np.arange(kwr))[None, :]
        rs = np.clip(r - wr // 2, 0, rows - wr)
        valid_r = (kr >= rs) & (kr < rs + wr)
        assert (valid_r.sum(1) == wr).all()
        rel_r = np.where(valid_r, kr - r + (NA_WIN_R - 1), 0)
        key = (valid_r.tobytes(), rel_r.tobytes())
        for pi, (pk, *_rest) in enumerate(pats):
            if pk == key:
                pat_ids.append(pi)
                break
        else:
            pat_ids.append(len(pats))
            pats.append((key, valid_r, rel_r))
        ks_rows.append(ks)
    valid_r = np.stack([p[1] for p in pats])
    rel_r = np.stack([p[2] for p in pats])
    return (qr, kwr, np.asarray(pat_ids, np.int32), np.asarray(ks_rows, np.int32), valid_r, rel_r, valid_c, rel_c)


def _na_bias_masks(na_bias, valid_r, rel_r, valid_c, rel_c):
    h = na_bias.shape[0]
    npat, qr, kwr = valid_r.shape
    ncol = 2 * NA_WIN_C - 1
    brow = na_bias[:, rel_r, :].astype(F32)
    onehot_c = ((rel_c[None] == np.arange(ncol)[:, None, None]) & valid_c[None]).astype(np.float32)
    m = jnp.einsum('hpqkc,cxy->phkyqx', brow, jnp.asarray(onehot_c), precision=HIGHEST)
    valid = (valid_r.transpose(0, 2, 1)[:, None, :, None, :, None]
             & valid_c.T[None, None, None, :, None, :])
    m = jnp.where(valid, m, NEG_BIG)
    return m.reshape(npat, h, kwr * GRID_W, qr * GRID_W)


def _na_attention(projx, projc, nvx_t, nvc_t, na_bias):
    b, s, _ = projx.shape
    lc = projc.shape[1]
    qr, kwr, pat_ids, ks_rows, valid_r, rel_r, valid_c, rel_c = _na_plan(s)
    qb, kw = qr * GRID_W, kwr * GRID_W
    assert kw % 128 == 0
    mb = _na_bias_masks(na_bias, valid_r, rel_r, valid_c, rel_c)
    grid_spec = pltpu.PrefetchScalarGridSpec(
        num_scalar_prefetch=2,
        grid=(b, s // qb),
        in_specs=[pl.BlockSpec((1, qb, 256), lambda bi, g, pat, ks: (bi, g, COL_NQ // 256)),
                  pl.BlockSpec((1, s, 256), lambda bi, g, pat, ks: (bi, 0, COL_NK // 256)),
                  pl.BlockSpec((1, BRANCH_W, s), lambda bi, g, pat, ks: (bi, 0, 0)),
                  pl.BlockSpec((1, lc, 256), lambda bi, g, pat, ks: (bi, 0, COL_NK // 256)),
                  pl.BlockSpec((1, BRANCH_W, lc), lambda bi, g, pat, ks: (bi, 0, 0)),
                  pl.BlockSpec((1, N_HEADS, kw, qb), lambda bi, g, pat, ks: (pat[g], 0, 0, 0))],
        out_specs=pl.BlockSpec((1, qb, BRANCH_W), lambda bi, g, pat, ks: (bi, g, 0)),
    )
    return pl.pallas_call(
        functools.partial(_na_kernel, kw=kw),
        grid_spec=grid_spec,
        out_shape=jax.ShapeDtypeStruct((b, s, BRANCH_W), BF16),
        compiler_params=_cparams(("parallel", "arbitrary"), 48 * 1024 * 1024),
        name="na_attention",
    )(jnp.asarray(pat_ids), jnp.asarray(ks_rows), projx, projx, nvx_t, projc, nvc_t, mb)


def _ret_kernel(lgs_ref, lgl_ref, gnw_ref, gseg_ref,
                qx, kx, vx, gfx, gbx, qc, kc, vc, gfc, gbc,
                yx_o, yc_o, of_s, ob_s, st_s, dec_s, qk_s, *, need_ctx):
    c = RET_CHUNK
    lc = qc.shape[1]
    sx = qx.shape[1]
    n_col = lax.broadcasted_iota(jnp.int32, (c, c), 0).astype(F32)
    m_row = lax.broadcasted_iota(jnp.int32, (c, c), 1).astype(F32)
    diff = n_col - m_row
    for h in range(N_HEADS):
        dec_s[0, :, h * c:(h + 1) * c] = jnp.where(diff >= 0, jnp.exp(lgs_ref[h] * jnp.maximum(diff, 0.0)), 0.0)
        dec_s[1, :, h * c:(h + 1) * c] = jnp.where(diff <= 0,
                                                   jnp.exp(lgs_ref[N_HEADS + h] * jnp.maximum(-diff, 0.0)), 0.0)
    pos = lax.broadcasted_iota(jnp.int32, (c, BRANCH_W), 0).astype(F32)
    lgf, lgb = lgl_ref[0], lgl_ref[1]
    qk_s[0] = jnp.exp(lgf * (pos + 1.0))
    qk_s[1] = jnp.exp(lgf * (c - 1.0 - pos))
    qk_s[2] = jnp.exp(lgb * (c - pos))
    qk_s[3] = jnp.exp(lgb * pos)
    cd_f = jnp.exp(lgf * float(c))
    cd_b = jnp.exp(lgb * float(c))
    st_s[...] = jnp.zeros_like(st_s)
    rowb = lax.broadcasted_iota(jnp.int32, (BRANCH_W, BRANCH_W), 0) // HEAD_W
    colb = lax.broadcasted_iota(jnp.int32, (BRANCH_W, BRANCH_W), 1) // HEAD_W
    bd_mask = rowb == colb

    def chunk_step(q, k, v, state, d, cd):
        kk = k * jnp.asarray(HEAD_W ** -0.5, BF16)
        o = _dot(q, state.astype(BF16)) * qk_s[2 * d]
        kstack = jnp.concatenate([jnp.where(_head_mask(kk.shape, h), kk, jnp.zeros_like(kk))
                                  for h in range(N_HEADS)], axis=0)
        vstack = jnp.concatenate([jnp.where(_head_mask(v.shape, h), v, jnp.zeros_like(v))
                                  for h in range(N_HEADS)], axis=0)
        inner = _dot_nt(q, kstack) * dec_s[d]
        o = o + _dot(inner.astype(BF16), vstack)
        kd = (kk.astype(F32) * qk_s[2 * d + 1]).astype(BF16)
        upd = lax.dot_general(kd, v, (((0,), (0,)), ((), ())), preferred_element_type=F32)
        return o, state * cd + jnp.where(bd_mask, upd, 0.0)

    gseg = gseg_ref[...]
    gnw = gnw_ref[...]

    def gnorm(o):
        mu = _dot_sel(o, gseg) * (1.0 / HEAD_W)
        dlt = o - mu
        var = _dot_sel(dlt * dlt, gseg) * (1.0 / HEAD_W)
        return dlt * lax.rsqrt(var + EPS) * gnw

    def combine(gf_ref, gb_ref, y_ref, base, ro):
        y = (gnorm(of_s[pl.ds(base + ro, c), :]) * _silu(gf_ref[0, pl.ds(ro, c), :].astype(F32))
             + gnorm(ob_s[pl.ds(base + ro, c), :]) * _silu(gb_ref[0, pl.ds(ro, c), :].astype(F32)))
        y_ref[0, pl.ds(ro, c), :] = y.astype(y_ref.dtype)

    def scan(q_ref, k_ref, v_ref, base, n, gated):
        def load(off):
            return q_ref[0, pl.ds(off, c), :], k_ref[0, pl.ds(off, c), :], v_ref[0, pl.ds(off, c), :]

        def body(j, _, emit):
            offs_f = [pl.multiple_of((2 * j + u) * c, c) for u in range(2)]
            offs_b = [pl.multiple_of((n - 1 - 2 * j - u) * c, c) for u in range(2)]
            ins_f = [load(o) for o in offs_f]
            ins_b = [load(o) for o in offs_b]
            sf, sb = st_s[0], st_s[1]
            outs_f, outs_b = [], []
            for u in range(2):
                o, sf = chunk_step(*ins_f[u], sf, 0, cd_f)
                outs_f.append(o)
                o, sb = chunk_step(*ins_b[u], sb, 1, cd_b)
                outs_b.append(o)
            st_s[0], st_s[1] = sf, sb
            for u in range(2):
                of_s[pl.ds(base + offs_f[u], c), :] = outs_f[u]
                ob_s[pl.ds(base + offs_b[u], c), :] = outs_b[u]
            for off in emit(offs_f, offs_b):
                combine(*gated, base, off)
            return 0

        none = lambda offs_f, offs_b: []
        both = none if gated is None else (lambda offs_f, offs_b: offs_f + offs_b)
        if n == 2:
            lax.fori_loop(0, 1, functools.partial(body, emit=none if gated is None else (lambda f, b: f)), 0)
        else:
            assert n % 4 == 0
            lax.fori_loop(0, n // 4, functools.partial(body, emit=none), 0)
            lax.fori_loop(n // 4, n // 2, functools.partial(body, emit=both), 0)

    scan(qc, kc, vc, 0, lc // c, (gfc, gbc, yc_o) if need_ctx else None)
    scan(qx, kx, vx, lc, sx // c, (gfx, gbx, yx_o))
    if not need_ctx:
        yc_o[...] = jnp.zeros_like(yc_o)


def _retention(projx, projc, log_g, gn_w, gseg, need_ctx):
    b, s, _ = projx.shape
    lc = projc.shape[1]
    lgs = log_g.reshape(2 * N_HEADS)
    lgl = jnp.repeat(log_g, HEAD_W, axis=1).reshape(2, 1, BRANCH_W)
    xs = [_proj_cols(projx, COL_RET + 256 * i, 256, s) for i in range(5)]
    cs = [_proj_cols(projc, COL_RET + 256 * i, 256, lc) for i in range(5)]
    c = RET_CHUNK
    yx, yc = pl.pallas_call(
        functools.partial(_ret_kernel, need_ctx=need_ctx),
        grid=(b,),
        in_specs=[pl.BlockSpec(memory_space=pltpu.SMEM),
                  pl.BlockSpec((2, 1, BRANCH_W), lambda bi: (0, 0, 0)),
                  pl.BlockSpec((1, BRANCH_W), lambda bi: (0, 0)),
                  pl.BlockSpec((BRANCH_W, BRANCH_W), lambda bi: (0, 0))] + xs + cs,
        out_specs=[pl.BlockSpec((1, s, BRANCH_W), lambda bi: (bi, 0, 0)),
                   pl.BlockSpec((1, lc, BRANCH_W), lambda bi: (bi, 0, 0))],
        out_shape=[jax.ShapeDtypeStruct((b, s, BRANCH_W), BF16),
                   jax.ShapeDtypeStruct((b, lc, BRANCH_W), BF16)],
        scratch_shapes=[pltpu.VMEM((lc + s, BRANCH_W), F32),
                        pltpu.VMEM((lc + s, BRANCH_W), F32),
                        pltpu.VMEM((2, BRANCH_W, BRANCH_W), F32),
                        pltpu.VMEM((2, c, N_HEADS * c), F32),
                        pltpu.VMEM((4, c, BRANCH_W), F32)],
        compiler_params=_cparams(("parallel",), 48 * 1024 * 1024),
        name="retention",
    )(lgs, lgl, gn_w.reshape(1, BRANCH_W), gseg, *([projx] * 5), *([projc] * 5))
    return yx, yc


def _merge_kernel(oa, ob, oc, od, g0, g1, g2, g3, x_ref, gate_ref, sc_ref, sh_ref, nw_ref, wb_ref, wo_ref,
                  xn_o, h2_o):
    acc = None
    for i, (o, g) in enumerate(((oa, g0), (ob, g1), (oc, g2), (od, g3))):
        t = (0.5 * jnp.tanh(0.5 * g[0].astype(F32)) + 0.5) * _dot(o[0], wb_ref[i])
        acc = t if acc is None else acc + t
    y = _dot(acc.astype(BF16), wo_ref[...])
    xn = x_ref[0] + gate_ref[0] * y
    xn_o[0] = xn
    h2_o[0] = _rms(xn) * nw_ref[...] * (1.0 + sc_ref[0]) + sh_ref[0]


def _merge(outs, proj, x, gate, sc2, sh2, n2w, wb, wo):
    b, l, d = x.shape
    tm = min(512, l)
    tok = lambda w: pl.BlockSpec((1, tm, w), lambda bi, i: (bi, i, 0))
    vec = pl.BlockSpec((1, 1, d), lambda bi, i: (bi, 0, 0))
    gates = [pl.BlockSpec((1, tm, d), lambda bi, i, k=k: (bi, i, COL_GATES // d + k)) for k in range(N_BRANCH)]
    return pl.pallas_call(
        _merge_kernel,
        grid=(b, l // tm),
        in_specs=[tok(BRANCH_W)] * 4 + gates + [tok(d), vec, vec, vec,
                                                pl.BlockSpec((1, d), lambda bi, i: (0, 0)),
                                                pl.BlockSpec(wb.shape, lambda bi, i: (0, 0, 0)),
                                                pl.BlockSpec(wo.shape, lambda bi, i: (0, 0))],
        out_specs=[tok(d), tok(d)],
        out_shape=[jax.ShapeDtypeStruct((b, l, d), F32), jax.ShapeDtypeStruct((b, l, d), F32)],
        compiler_params=_cparams(("parallel", "parallel"), 48 * 1024 * 1024),
        name="merge",
    )(*outs, proj, proj, proj, proj, x, gate, sc2, sh2, n2w, wb, wo)


def _sorting_network(n):
    pairs = []
    p = 1
    while p < n:
        k = p
        while k >= 1:
            for j in range(k % p, n - k, 2 * k):
                for i in range(min(k, n - j - k)):
                    if (i + j) // (2 * p) == (i + j + k) // (2 * p):
                        pairs.append((i + j, i + j + k))
            k //= 2
        p *= 2
    return tuple(pairs)


def _topk_rows_slabs(s, k):
    r, t = s.shape
    assert r == 8 * k
    sub = lax.broadcasted_iota(jnp.int32, (8, t), 0).astype(F32)
    vs = [s[8 * j:8 * j + 8] for j in range(k)]
    ix = [sub + float(8 * j) for j in range(k)]
    for p, q in _sorting_network(k):
        va, ia, vb, ib = vs[p], ix[p], vs[q], ix[q]
        swap = (vb > va) | ((vb == va) & (ib < ia))
        vs[p], vs[q] = jnp.maximum(va, vb), jnp.minimum(va, vb)
        ix[p], ix[q] = jnp.where(swap, ib, ia), jnp.where(swap, ia, ib)
    out_row = lax.broadcasted_iota(jnp.int32, (k, t), 0)
    vals = jnp.zeros((k, t), F32)
    idxs = jnp.zeros((k, t), F32)
    for rnd in range(k):
        hv, hi = vs[0], ix[0]
        m = jnp.max(hv, axis=0, keepdims=True)
        idx = jnp.min(jnp.where(hv == m, hi, float(r)), axis=0, keepdims=True)
        vals = jnp.where(out_row == rnd, m, vals)
        idxs = jnp.where(out_row == rnd, idx, idxs)
        won = hi == idx
        for d in range(k - 1 - rnd):
            vs[d] = jnp.where(won, vs[d + 1], vs[d])
            ix[d] = jnp.where(won, ix[d + 1], ix[d])
    return vals, idxs.astype(jnp.int32)


def _select_rows(table, sel, k):
    out = jnp.zeros_like(table)
    for r in range(k):
        out = jnp.where(sel == r, table[r:r + 1, :], out)
    return out


def _split_bf16(x):
    hi = x.astype(BF16)
    return hi, (x - hi.astype(F32)).astype(BF16)


def _peer_joint_topk(s1, s2):
    k = PEER_TOPK
    t = s1.shape[1]
    sub = lax.broadcasted_iota(jnp.int32, (8, t), 0)
    depth = jnp.full((8, t), k // 8, jnp.int32)
    for i in range(6, -1, -1):
        depth = jnp.where(sub == i, k // (i + 1), depth)
    subf = sub.astype(F32)
    lo = [jnp.where(depth > d, s1[0:8] + s2[d:d + 1], -jnp.inf) for d in range(k)]
    hi0 = s1[8:16] + s2[0:1]
    pos_hi = (subf + 8.0) * float(k)
    cnt = jnp.zeros((8, t), F32)
    out_row = lax.broadcasted_iota(jnp.int32, (k, t), 0)
    vals = jnp.zeros((k, t), F32)
    poss = jnp.zeros((k, t), F32)
    big = float(k * k)
    for rnd in range(k):
        pos_lo = subf * float(k) + cnt
        m = jnp.max(jnp.maximum(lo[0], hi0), axis=0, keepdims=True)
        cand = jnp.minimum(jnp.where(lo[0] == m, pos_lo, big), jnp.where(hi0 == m, pos_hi, big))
        pos = jnp.min(cand, axis=0, keepdims=True)
        vals = jnp.where(out_row == rnd, m, vals)
        poss = jnp.where(out_row == rnd, pos, poss)
        won_lo = pos_lo == pos
        for d in range(k - 1 - rnd):
            lo[d] = jnp.where(won_lo, lo[d + 1], lo[d])
        cnt = cnt + jnp.where(won_lo, 1.0, 0.0)
        hi0 = jnp.where(pos_hi == pos, -jnp.inf, hi0)
    p = poss.astype(jnp.int32)
    return vals, p >> 4, p & (k - 1)


def _peer_route_head(q, kh, kl):
    k = PEER_TOPK
    assert k == 16
    qh, ql = _split_bf16(q)
    s = _dot_nt(kh, qh) + (_dot_nt(kh, ql) + _dot_nt(kl, qh))
    s1, i1 = _topk_rows_slabs(s[:PEER_N_KEYS], k)
    s2, i2 = _topk_rows_slabs(s[PEER_N_KEYS:], k)
    ts, ri, rj = _peer_joint_topk(s1, s2)
    e = jnp.exp(ts - ts[0:1, :])
    return _select_rows(i1, ri, k), _select_rows(i2, rj, k), e / jnp.sum(e, axis=0, keepdims=True)


def _peer_query(h, wqh, wql):
    hh, hl = _split_bf16(h)
    return _dot(hh, wqh) + (_dot(hl, wqh) + _dot(hh, wql))


def _peer_route_kernel(h_ref, wqh_ref, wql_ref, kh_ref, kl_ref, a_o, b_o, g_o, q_s, a_s, b_s, g_s):
    k = PEER_TOPK
    q_s[...] = _peer_query(h_ref[...], wqh_ref[...], wql_ref[...])

    def head(h):
        lo = pl.multiple_of(h * PEER_DK, PEER_DK)
        ro = pl.multiple_of(h * k, k)
        a_s[pl.ds(ro, k), :], b_s[pl.ds(ro, k), :], g_s[pl.ds(ro, k), :] = _peer_route_head(
            q_s[:, pl.ds(lo, PEER_DK)], kh_ref[h], kl_ref[h])

    def head_group(i, _):
        for j in range(PEER_HEADS_PER_TRIP):
            head(PEER_HEADS_PER_TRIP * i + j)
        return 0

    lax.fori_loop(0, PEER_HEADS // PEER_HEADS_PER_TRIP, head_group, 0)
    a_o[...] = a_s[...].T
    b_o[...] = b_s[...].T
    g_o[...] = g_s[...].T


def _peer_route(h2, wq_hl, keys_hl):
    n, d = h2.shape
    t = 256
    wq_hi, wq_lo = wq_hl
    k_hi, k_lo = keys_hl
    return pl.pallas_call(
        _peer_route_kernel,
        grid=(n // t,),
        in_specs=[pl.BlockSpec((t, d), lambda i: (i, 0)),
                  pl.BlockSpec(wq_hi.shape, lambda i: (0, 0)),
                  pl.BlockSpec(wq_lo.shape, lambda i: (0, 0)),
                  pl.BlockSpec(k_hi.shape, lambda i: (0, 0, 0)),
                  pl.BlockSpec(k_lo.shape, lambda i: (0, 0, 0))],
        out_specs=[pl.BlockSpec((t, PEER_SLOTS), lambda i: (i, 0))] * 3,
        out_shape=[jax.ShapeDtypeStruct((n, PEER_SLOTS), jnp.int32),
                   jax.ShapeDtypeStruct((n, PEER_SLOTS), jnp.int32),
                   jax.ShapeDtypeStruct((n, PEER_SLOTS), F32)],
        scratch_shapes=[pltpu.VMEM((t, PEER_HEADS * PEER_DK), F32),
                        pltpu.VMEM((PEER_SLOTS, t), jnp.int32),
                        pltpu.VMEM((PEER_SLOTS, t), jnp.int32),
                        pltpu.VMEM((PEER_SLOTS, t), F32)],
        compiler_params=_cparams(("parallel",), 48 * 1024 * 1024),
        name="peer_route",
    )(h2, wq_hi, wq_lo, k_hi, k_lo)


_HI16 = -65536


def _bf16_bits(w):
    return lax.bitcast_convert_type(w, jnp.int32) & _HI16


def _peer_ffn_kernel(*refs, ec, unroll, final_norm, route_next):
    if route_next:
        (h_ref, a_ref, b_ref, g_ref, x_ref, gate_ref, u_ref, v_ref, fw_ref, qn_ref, kh_ref, kl_ref,
         o_ref, hb_s, w_s, a_c, b_c, g_c, a_n, b_n, g_n) = refs
    else:
        h_ref, a_ref, b_ref, g_ref, x_ref, gate_ref, u_ref, v_ref, fw_ref, o_ref, hb_s, w_s = refs
        a_c, b_c, g_c = a_ref, b_ref, g_ref
    e = pl.program_id(1)
    t = h_ref.shape[0]
    half = t // 2
    nk = PEER_N_KEYS

    @pl.when(e == 0)
    def _():
        if route_next:
            @pl.when(pl.program_id(0) == 0)
            def _():
                a_c[...] = a_ref[...]
                b_c[...] = b_ref[...]
                g_c[...] = g_ref[...]

        hb_s[...] = h_ref[...].astype(BF16)
        o_ref[...] = jnp.zeros_like(o_ref)
        jio = lax.broadcasted_iota(jnp.int32, (nk, PEER_SLOTS), 0)

        def tile(tt):
            arow = jnp.broadcast_to(a_c[pl.ds(tt, 1), :], (nk, PEER_SLOTS))
            brow = jnp.broadcast_to(b_c[pl.ds(tt, 1), :], (nk, PEER_SLOTS))
            grow = jnp.broadcast_to(g_c[pl.ds(tt, 1), :], (nk, PEER_SLOTS))
            cm = jnp.where(jio == arow, grow, 0.0).astype(BF16)
            bm_t = jnp.where(jio == brow, 1.0, 0.0).T.astype(BF16)
            return _dot(cm, bm_t)

        def build(tb, _):
            for u in range(unroll):
                tt = tb * unroll + u
                word = _bf16_bits(tile(tt + half)) | lax.shift_right_logical(_bf16_bits(tile(tt)), 16)
                w_s[pl.ds(pl.multiple_of(tt * PEER_W_PITCH, 8), nk), :] = word
            return 0

        lax.fori_loop(0, half // unroll, build, 0)

    if route_next:
        nsub = t // PEER_ROUTE_TOKENS
        hd = e // nsub
        co = pl.multiple_of((e % nsub) * PEER_ROUTE_TOKENS, PEER_ROUTE_TOKENS)
        ro = pl.multiple_of(hd * PEER_TOPK, PEER_TOPK)
        routed = _peer_route_head(
            qn_ref[pl.ds(co, PEER_ROUTE_TOKENS), pl.ds(pl.multiple_of(hd * PEER_DK, PEER_DK), PEER_DK)],
            kh_ref[hd], kl_ref[hd])

    hid = _dot(hb_s[...], u_ref[0])
    j0 = e * (ec // nk)
    words = jnp.concatenate([w_s[pl.ds(j0 + j, half, stride=PEER_W_PITCH), :] for j in range(ec // nk)], axis=1)
    w_lo = lax.bitcast_convert_type(lax.shift_left(words, 16), F32)
    w_hi = lax.bitcast_convert_type(words & _HI16, F32)
    wc = jnp.concatenate([w_lo, w_hi], axis=0)
    act = 0.5 * hid * (1.0 + lax.erf(hid * SQRT_HALF))
    o_ref[...] += _dot((wc * act).astype(BF16), v_ref[...])

    if route_next:
        a_n[pl.ds(ro, PEER_TOPK), pl.ds(co, PEER_ROUTE_TOKENS)] = routed[0]
        b_n[pl.ds(ro, PEER_TOPK), pl.ds(co, PEER_ROUTE_TOKENS)] = routed[1]
        g_n[pl.ds(ro, PEER_TOPK), pl.ds(co, PEER_ROUTE_TOKENS)] = routed[2]

    @pl.when(e == pl.num_programs(1) - 1)
    def _():
        y = x_ref[...] + gate_ref[0] * o_ref[...]
        o_ref[...] = _rms(y) * fw_ref[...] if final_norm else y
        if route_next:
            a_c[...] = a_n[...].T
            b_c[...] = b_n[...].T
            g_c[...] = g_n[...].T


def _peer_ffn(h2, a, b_idx, g, x, gate, u_blk, v, l, final_w, final_norm, route_next=None):
    n, d = h2.shape
    ne = v.shape[0]
    neb, _, ec = u_blk.shape
    t = min(512, l)
    nb = n // t
    unroll = 16
    assert l % t == 0 and ne == PEER_N_KEYS * PEER_N_KEYS and neb * ec == ne and (t // 2) % unroll == 0
    tok = lambda w: pl.BlockSpec((t, w), lambda i, e: (i, 0))
    in_specs = [tok(d), tok(PEER_SLOTS), tok(PEER_SLOTS), tok(PEER_SLOTS), tok(d),
                pl.BlockSpec((1, 1, d), lambda i, e: ((i * t) // l, 0, 0)),
                pl.BlockSpec((1, d, ec), lambda i, e: (e, 0, 0)),
                pl.BlockSpec((ec, d), lambda i, e: (e, 0)),
                pl.BlockSpec((1, d), lambda i, e: (0, 0))]
    args = [h2, a, b_idx, g, x, gate, u_blk, v, final_w.reshape(1, d)]
    scratch = [pltpu.VMEM((t, d), BF16), pltpu.VMEM((t // 2 * PEER_W_PITCH, PEER_N_KEYS), jnp.int32)]
    if route_next is not None:
        q_all, k_hi, k_lo = route_next
        assert neb == PEER_HEADS * (t // PEER_ROUTE_TOKENS) and a.shape[0] == t
        first = pl.BlockSpec((t, PEER_SLOTS), lambda i, e: (0, 0))
        in_specs[1:4] = [first, first, first]
        in_specs += [pl.BlockSpec((t, q_all.shape[1]), lambda i, e: (jnp.minimum(i + 1, nb - 1), 0)),
                     pl.BlockSpec(k_hi.shape, lambda i, e: (0, 0, 0)),
                     pl.BlockSpec(k_lo.shape, lambda i, e: (0, 0, 0))]
        args += [q_all, k_hi, k_lo]
        scratch += [pltpu.VMEM((t, PEER_SLOTS), jnp.int32), pltpu.VMEM((t, PEER_SLOTS), jnp.int32),
                    pltpu.VMEM((t, PEER_SLOTS), F32),
                    pltpu.VMEM((PEER_SLOTS, t), jnp.int32), pltpu.VMEM((PEER_SLOTS, t), jnp.int32),
                    pltpu.VMEM((PEER_SLOTS, t), F32)]
    return pl.pallas_call(
        functools.partial(_peer_ffn_kernel, ec=ec, unroll=unroll, final_norm=final_norm,
                          route_next=route_next is not None),
        grid=(nb, neb),
        in_specs=in_specs,
        out_specs=tok(d),
        out_shape=jax.ShapeDtypeStruct((n, d), F32),
        scratch_shapes=scratch,
        compiler_params=_cparams(("arbitrary", "arbitrary"), VMEM_LIMIT_V7X),
        name="peer_ffn_routed" if route_next is not None else "peer_ffn",
    )(*args)


def _peer_query_kernel(h_ref, wqh_ref, wql_ref, q_o):
    q_o[...] = _peer_query(h_ref[...], wqh_ref[...], wql_ref[...])


def _peer_queries(h2, wq_hl):
    n, d = h2.shape
    wq_hi, wq_lo = wq_hl
    t = min(512, n)
    return pl.pallas_call(
        _peer_query_kernel,
        grid=(n // t,),
        in_specs=[pl.BlockSpec((t, d), lambda i: (i, 0)),
                  pl.BlockSpec(wq_hi.shape, lambda i: (0, 0)),
                  pl.BlockSpec(wq_lo.shape, lambda i: (0, 0))],
        out_specs=pl.BlockSpec((t, wq_hi.shape[1]), lambda i: (i, 0)),
        out_shape=jax.ShapeDtypeStruct((n, wq_hi.shape[1]), F32),
        compiler_params=_cparams(("parallel",), 40 * 1024 * 1024),
        name="peer_queries",
    )(h2, wq_hi, wq_lo)


def _layout_w_in(w):
    parts = jnp.split(w, IN_OFFSETS, axis=1)
    z = lambda n: jnp.zeros((w.shape[0], n), w.dtype)
    kpe_blk = jnp.concatenate([z(MLA_NOPE), parts[2], z(MLA_HEAD_PAD - MLA_NOPE - MLA_ROPE)], axis=1)
    return jnp.concatenate([parts[0], parts[1], kpe_blk] + list(parts[3:]), axis=1).astype(BF16)


def _layout_mla(w_uq, w_ukv):
    qh = w_uq.reshape(MLA_Q_LORA, N_HEADS, MLA_NOPE + MLA_ROPE)
    qh = jnp.pad(qh, ((0, 0), (0, 0), (0, MLA_HEAD_PAD - MLA_NOPE - MLA_ROPE)))
    kv = w_ukv.reshape(MLA_KV_LORA, N_HEADS, MLA_NOPE + MLA_V)
    kh = jnp.pad(kv[:, :, :MLA_NOPE], ((0, 0), (0, 0), (0, MLA_HEAD_PAD - MLA_NOPE)))
    vh = kv[:, :, MLA_NOPE:]
    return (qh.reshape(MLA_Q_LORA, -1).astype(BF16), kh.reshape(MLA_KV_LORA, -1).astype(BF16),
            vh.reshape(MLA_KV_LORA, -1).T.astype(BF16))


def _static_mats():
    gseg = np.kron(np.eye(N_HEADS), np.ones((HEAD_W, HEAD_W))).astype(np.float32)
    pm = np.zeros((MLA_HEAD_PAD, MLA_HEAD_PAD), np.float32)
    for dd in range(MLA_ROPE):
        blk, j = dd // 16, dd % 16
        pm[MLA_NOPE + blk * 16 + (j + 8) % 16, MLA_NOPE + dd] = 1.0
    pg = np.zeros((BRANCH_W, BRANCH_W), np.float32)
    for i in range(BRANCH_W):
        off, dd = (i // HEAD_W) * HEAD_W, i % HEAD_W
        blk, j = dd // 32, dd % 32
        pg[off + blk * 32 + (j + 16) % 32, i] = 1.0
    ex = np.zeros((GQA_KV_HEADS * HEAD_W, BRANCH_W), np.float32)
    for i in range(BRANCH_W):
        ex[((i // HEAD_W) // (N_HEADS // GQA_KV_HEADS)) * HEAD_W + i % HEAD_W, i] = 1.0
    return tuple(jnp.asarray(m, dtype=BF16) for m in (gseg, pm, pg, ex, ex.T, np.eye(BRANCH_W)))


def _rope_half_tables(pos, hf):
    freqs = ROPE_THETA ** (-jnp.arange(hf, dtype=F32) / hf)
    ang = pos[:, None] * freqs[None, :]
    c, s = jnp.cos(ang), jnp.sin(ang)
    return jnp.concatenate([c, c], axis=1), jnp.concatenate([-s, s], axis=1)


def _axial_tables(row, col, dims):
    cr, sr = _rope_half_tables(row, dims // 4)
    cc, sc = _rope_half_tables(col, dims // 4)
    return jnp.concatenate([cr, cc], axis=1), jnp.concatenate([sr, sc], axis=1)


def _rope_tables(s):
    t = jnp.arange(s)
    row, col = (t // GRID_W).astype(F32), (t % GRID_W).astype(F32)
    c32, s32 = _axial_tables(row, col, MLA_ROPE)
    pad = MLA_HEAD_PAD - MLA_NOPE - MLA_ROPE
    cm = jnp.concatenate([jnp.ones((s, MLA_NOPE), F32), c32, jnp.ones((s, pad), F32)], axis=1)
    sm = jnp.concatenate([jnp.zeros((s, MLA_NOPE), F32), s32, jnp.zeros((s, pad), F32)], axis=1)
    c64, s64 = _axial_tables(row, col, HEAD_W)
    return cm, sm, jnp.tile(c64, (1, N_HEADS)), jnp.tile(s64, (1, N_HEADS))


def _split_f32(w):
    hi = w.astype(BF16)
    return hi, (w - hi.astype(F32)).astype(BF16)


def _layout_peer_keys(keys):
    h, _, nk, dh = keys.shape
    z = jnp.zeros((h, nk, dh), keys.dtype)
    top = jnp.concatenate([keys[:, 0], z], axis=2)
    bot = jnp.concatenate([z, keys[:, 1]], axis=2)
    return jnp.concatenate([top, bot], axis=1)


def kernel(x, c, ctx, c_ctx, mod_w, mod_b, norm1_w, norm2_w, w_in, mla_q_norm, mla_w_uq, mla_kv_norm, mla_w_ukv, gqa_q_norm, gqa_k_norm, na_bias, ret_decay_logit, ret_gn_w, w_branch, w_out, peer_w_q, peer_keys, peer_u, peer_v, final_norm_w):
    b, s, d = x.shape
    lc = ctx.shape[1]
    depth = mod_w.shape[0]
    assert d == D_MODEL and s % (GRID_W * NA_Q_ROWS) == 0 and s % 256 == 0 and lc % 256 == 0

    rows = -(-(b + 1) // 8) * 8
    cc = jnp.zeros((rows, d), F32).at[:b].set(c).at[b].set(c_ctx)
    mod = _modulation(cc, mod_w, mod_b)

    gseg, pm, pg, ex, ex_t, eye = _static_mats()
    tables = _rope_tables(s)

    for l in range(depth):
        need_ctx = l < depth - 1
        mx = mod[l, :b].reshape(b, 1, 6, d)
        mc = jnp.broadcast_to(mod[l, b].reshape(1, 1, 6, d), (b, 1, 6, d))
        sh1x, sc1x, g1x, sh2x, sc2x, g2x = (mx[:, :, i] for i in range(6))
        sh1c, sc1c, g1c, sh2c, sc2c, g2c = (mc[:, :, i] for i in range(6))

        w_in_l = _layout_w_in(w_in[l])
        wuq, wk, wv = _layout_mla(mla_w_uq[l], mla_w_ukv[l])
        consts = (mla_q_norm[l].reshape(1, -1), wuq, mla_kv_norm[l].reshape(1, -1), wk, wv,
                  jnp.tile(gqa_q_norm[l], N_HEADS).reshape(1, -1),
                  jnp.tile(gqa_k_norm[l], GQA_KV_HEADS).reshape(1, -1), gseg, pm, pg, ex, ex_t, eye)
        n1w = norm1_w[l].reshape(1, d)
        n2w = norm2_w[l].reshape(1, d)
        wb = w_branch[l].astype(BF16)
        wo = w_out[l].astype(BF16)
        keys_hl = _split_f32(_layout_peer_keys(peer_keys[l]))
        wq_hl = _split_f32(peer_w_q[l])
        ne = peer_u.shape[1]
        u_blk = jnp.swapaxes(peer_u[l].astype(BF16).reshape(ne // PEER_EXPERT_CHUNK, PEER_EXPERT_CHUNK, d), 1, 2)
        v_b = peer_v[l].astype(BF16)
        log_g = jax.nn.log_sigmoid(ret_decay_logit[l].astype(F32))

        projx = _inproj(x, n1w, sc1x, sh1x, w_in_l)
        ctx_flat = (b * lc) % 1024 == 0
        if ctx_flat:
            projc = _inproj(ctx.reshape(1, b * lc, d), n1w, sc1c[:1], sh1c[:1], w_in_l).reshape(b, lc, -1)
        else:
            projc = _inproj(ctx, n1w, sc1c, sh1c, w_in_l)
        qmx, kmx, vmx, qgx, kgx, vgx, nvx = _prep(projx, consts, tables)
        qmc, kmc, vmc, qgc, kgc, vgc, nvc = _prep(projc, consts, None)

        oa, ob = _attention([dict(q=(qmx, 0), segs=[((kmc, 0), vmc), ((kmx, 0), vmx)], dq=MLA_HEAD_PAD),
                             dict(q=(qgx, 0), segs=[((kgc, 0), vgc), ((kgx, 0), vgx)], dq=HEAD_W)], "attn_mla_gqa")
        oc = _na_attention(projx, projc, nvx, nvc, na_bias[l])
        od, od_c = _retention(projx, projc, log_g, ret_gn_w[l], gseg, need_ctx)

        def peer(h2f, xf, gate, l_blk, final):
            t = min(512, l_blk)
            if (t // PEER_ROUTE_TOKENS) * PEER_HEADS == u_blk.shape[0]:
                a0, b0, g0 = _peer_route(h2f[:t], wq_hl, keys_hl)
                return _peer_ffn(h2f, a0, b0, g0, xf, gate, u_blk, v_b, l_blk, final_norm_w, final_norm=final,
                                 route_next=(_peer_queries(h2f, wq_hl),) + keys_hl)
            a0, b0, g0 = _peer_route(h2f, wq_hl, keys_hl)
            return _peer_ffn(h2f, a0, b0, g0, xf, gate, u_blk, v_b, l_blk, final_norm_w, final_norm=final)

        x, h2x = _merge((oa, ob, oc, od), projx, x, g1x, sc2x, sh2x, n2w, wb, wo)
        x = peer(h2x.reshape(b * s, d), x.reshape(b * s, d), g2x, s, l == depth - 1).reshape(b, s, d)

        if need_ctx:
            ca, cb, ccx = _attention(
                [dict(q=(qmc, 0), segs=[((kmc, 0), vmc)], dq=MLA_HEAD_PAD),
                 dict(q=(qgc, 0), segs=[((kgc, 0), vgc)], dq=HEAD_W),
                 dict(q=(projc, COL_NQ), segs=[((projc, COL_NK), nvc)], dq=HEAD_W, qscale=HEAD_W ** -0.5,
                      log2_scores=False)], "attn_ctx")
            if ctx_flat:
                one = lambda a: a.reshape(1, b * lc, a.shape[-1])
                ctx, h2c = _merge(tuple(one(a) for a in (ca, cb, ccx, od_c)), one(projc), one(ctx),
                                  g1c[:1], sc2c[:1], sh2c[:1], n2w, wb, wo)
            else:
                ctx, h2c = _merge((ca, cb, ccx, od_c), projc, ctx, g1c, sc2c, sh2c, n2w, wb, wo)
            ctx = peer(h2c.reshape(b * lc, d), ctx.reshape(b * lc, d), g2c[:1], b * lc, False).reshape(b, lc, d)

    return x
```
